```python
import jax, jax.numpy as jnp
from jax import lax
import numpy as np

D_MODEL = 2048
BATCH = 4
SEQ = 4096
DEPTH = 2

GRID_W = 64
CTX_LEN = 256

SWA_HEADS = 6
SWA_KV_HEADS = 2
SWA_HEAD_DIM = 128
SWA_WINDOW = 128
SWA_BLOCK = 128
RET_HEADS = 6
RET_QK_DIM = 64
RET_V_DIM = 128
RET_CHUNK = 128
RET_DECAY_BASE = 5.0
MLA_HEADS = 4
MLA_NOPE_DIM = 128
MLA_ROPE_DIM = 64
MLA_V_DIM = 128
MLA_KV_RANK = 256
MLA_Q_BLOCK = 128

SWA_WIDTH = SWA_HEADS * SWA_HEAD_DIM
RET_WIDTH = RET_HEADS * RET_V_DIM
MLA_WIDTH = MLA_HEADS * MLA_V_DIM
MIX_WIDTH = SWA_WIDTH + RET_WIDTH + MLA_WIDTH

IN_SPLITS = (SWA_HEADS * SWA_HEAD_DIM, SWA_KV_HEADS * SWA_HEAD_DIM, SWA_KV_HEADS * SWA_HEAD_DIM,
             RET_HEADS * RET_QK_DIM, RET_HEADS * RET_QK_DIM, RET_WIDTH, RET_WIDTH, RET_WIDTH,
             MLA_HEADS * (MLA_NOPE_DIM + MLA_ROPE_DIM), MLA_KV_RANK, MLA_ROPE_DIM)
IN_WIDTH = sum(IN_SPLITS)

N_GROUPS = 4
EXPERTS_PER_GROUP = 8
N_EXPERTS = N_GROUPS * EXPERTS_PER_GROUP
TOP_K = 2
EXPERT_HIDDEN = 1024
MOE_BLOCK = 128

ALPHA = (2 * DEPTH) ** 0.25
BETA = (8 * DEPTH) ** -0.25
ROPE_BASE = 10000.0
NORM_EPS = 1e-6
NEG_INF = -1e30

kernel_name = 'hybrid_dit_swa_retnet_mla_hmoe'


def layer_norm(x, eps=NORM_EPS):
    xf = x.astype(jnp.float32)
    mu = jnp.mean(xf, axis=-1, keepdims=True)
    xc = xf - mu
    var = jnp.mean(xc * xc, axis=-1, keepdims=True)
    return (xc * lax.rsqrt(var + eps)).astype(x.dtype)


def affine_layer_norm(x, g, b):
    return layer_norm(x) * g + b


def rms_norm(x, g, eps=NORM_EPS):
    xf = x.astype(jnp.float32)
    y = xf * lax.rsqrt(jnp.mean(xf * xf, axis=-1, keepdims=True) + eps)
    return y.astype(x.dtype) * g


def modulate(x, shift, scale):
    return layer_norm(x) * (1.0 + scale) + shift


def rope_1d(x, pos):
    half = x.shape[-1] // 2
    inv_freq = ROPE_BASE ** (-jnp.arange(half, dtype=jnp.float32) / half)
    ang = pos.astype(jnp.float32)[:, None] * inv_freq[None, :]
    cos = jnp.cos(ang)[:, None, :].astype(x.dtype)
    sin = jnp.sin(ang)[:, None, :].astype(x.dtype)
    x1, x2 = x[..., :half], x[..., half:]
    return jnp.concatenate([x1 * cos - x2 * sin, x2 * cos + x1 * sin], axis=-1)


def rope_2d(x, rows, cols):
    d = x.shape[-1] // 2
    return jnp.concatenate([rope_1d(x[..., :d], rows), rope_1d(x[..., d:], cols)], axis=-1)


def split_projection(t):
    pts, acc = [], 0
    for s in IN_SPLITS[:-1]:
        acc += s
        pts.append(acc)
    return jnp.split(t, pts, axis=-1)


def swa_mixer(q, k, v, qc, kc, vc, sink, rows, cols, compute_ctx):
    B, L = q.shape[0], q.shape[1]
    n_ctx = kc.shape[1]
    nb = L // SWA_BLOCK
    G = SWA_HEADS // SWA_KV_HEADS
    scale = SWA_HEAD_DIM ** -0.5
    qb = rope_2d(q, rows, cols).reshape(B, nb, SWA_BLOCK, SWA_KV_HEADS, G, SWA_HEAD_DIM)
    k = rope_2d(k, rows, cols)

    def band(t):
        pad = jnp.zeros((B, SWA_BLOCK) + t.shape[2:], t.dtype)
        tb = jnp.concatenate([pad, t, pad], axis=1).reshape((B, nb + 2, SWA_BLOCK) + t.shape[2:])
        return jnp.concatenate([tb[:, :-2], tb[:, 1:-1], tb[:, 2:]], axis=2)

    kw, vw = band(k), band(v)
    q_pos = jnp.arange(nb)[:, None] * SWA_BLOCK + jnp.arange(SWA_BLOCK)[None, :]
    k_pos = jnp.arange(nb)[:, None] * SWA_BLOCK - SWA_BLOCK + jnp.arange(3 * SWA_BLOCK)[None, :]
    valid = ((jnp.abs(q_pos[:, :, None] - k_pos[:, None, :]) <= SWA_WINDOW)
             & (k_pos >= 0)[:, None, :] & (k_pos < L)[:, None, :])
    s_loc = jnp.einsum('bnqhgd,bnkhd->bnhgqk', qb, kw).astype(jnp.float32) * scale
    s_loc = jnp.where(valid[None, :, None, None], s_loc, NEG_INF)
    s_ctx = jnp.einsum('bnqhgd,bkhd->bnhgqk', qb, kc).astype(jnp.float32) * scale
    sink_f = sink.astype(jnp.float32)
    sink_l = jnp.broadcast_to(sink_f.reshape(1, 1, SWA_KV_HEADS, G, 1, 1), s_loc.shape[:-1] + (1,))
    p = jax.nn.softmax(jnp.concatenate([sink_l, s_ctx, s_loc], axis=-1), axis=-1).astype(v.dtype)
    out = (jnp.einsum('bnhgqk,bkhd->bnqhgd', p[..., 1:1 + n_ctx], vc)
           + jnp.einsum('bnhgqk,bnkhd->bnqhgd', p[..., 1 + n_ctx:], vw)).reshape(B, L, SWA_WIDTH)
    out_c = None
    if compute_ctx:
        qcg = qc.reshape(B, n_ctx, SWA_KV_HEADS, G, SWA_HEAD_DIM)
        s = jnp.einsum('bqhgd,bkhd->bhgqk', qcg, kc).astype(jnp.float32) * scale
        sink_c = jnp.broadcast_to(sink_f.reshape(1, SWA_KV_HEADS, G, 1, 1), s.shape[:-1] + (1,))
        pc = jax.nn.softmax(jnp.concatenate([sink_c, s], axis=-1), axis=-1).astype(vc.dtype)
        out_c = jnp.einsum('bhgqk,bkhd->bqhgd', pc[..., 1:], vc).reshape(B, n_ctx, SWA_WIDTH)
    return out, out_c


def retention_chunks(q, k, v, log_gamma, s0):
    B, H, T, dk = q.shape
    dv = v.shape[-1]
    C = RET_CHUNK
    n = T // C
    qc = q.reshape(B, H, n, C, dk)
    kc = k.reshape(B, H, n, C, dk)
    vc = v.reshape(B, H, n, C, dv)
    idx = jnp.arange(C, dtype=jnp.float32)
    diff = idx[:, None] - idx[None, :]
    dmat = jnp.where(diff >= 0, jnp.exp(log_gamma[:, None, None] * jnp.maximum(diff, 0.0)), 0.0)
    q_decay = jnp.exp(log_gamma[:, None] * (idx + 1.0))
    k_decay = jnp.exp(log_gamma[:, None] * (C - 1.0 - idx))
    c_decay = jnp.exp(log_gamma * C)
    scores = jnp.einsum('bhncd,bhnsd->bhncs', qc, kc) * dmat[None, :, None]
    y = jnp.einsum('bhncs,bhnse->bhnce', scores, vc)
    kv = jnp.einsum('bhnsd,bhnse->nbhde', kc * k_decay[None, :, None, :, None], vc)

    def step(s, kv_i):
        return c_decay[None, :, None, None] * s + kv_i, s

    s_last, s_prev = lax.scan(step, s0, kv)
    y = y + jnp.einsum('bhncd,nbhde->bhnce', qc * q_decay[None, :, None, :, None], s_prev)
    return y.reshape(B, H, T, dv), s_last


def retention_mixer(q, k, v, gf, gb, qc, kc, vc, gfc, gbc, decay_exp, compute_ctx):
    B, L = q.shape[0], q.shape[1]
    n_ctx = qc.shape[1]
    pos_c = jnp.arange(n_ctx, dtype=jnp.float32)
    pos_l = n_ctx + jnp.arange(L, dtype=jnp.float32)
    qk_scale = RET_QK_DIM ** -0.5

    def heads(t):
        return jnp.transpose(t, (0, 2, 1, 3)).astype(jnp.float32)

    q_l, k_l, v_l = heads(rope_1d(q, pos_l) * qk_scale), heads(rope_1d(k, pos_l)), heads(v)
    q_c, k_c, v_c = heads(rope_1d(qc, pos_c) * qk_scale), heads(rope_1d(kc, pos_c)), heads(vc)
    log_gamma = jnp.log1p(-jnp.exp2(-decay_exp.astype(jnp.float32)))
    s0 = jnp.zeros((B, RET_HEADS, RET_QK_DIM, RET_V_DIM), jnp.float32)

    def rev(t):
        return jnp.flip(t, axis=2)

    yc_f, sc_f = retention_chunks(q_c, k_c, v_c, log_gamma[0], s0)
    yc_b, sc_b = retention_chunks(rev(q_c), rev(k_c), rev(v_c), log_gamma[1], s0)
    y_f, _ = retention_chunks(q_l, k_l, v_l, log_gamma[0], sc_f)
    y_b, _ = retention_chunks(rev(q_l), rev(k_l), rev(v_l), log_gamma[1], sc_b)

    def merge(yf, yb, g_f, g_b):
        def norm(y):
            yn = layer_norm(y)
            return jnp.transpose(yn, (0, 2, 1, 3)).reshape(B, -1, RET_WIDTH).astype(g_f.dtype)
        return jax.nn.silu(g_f) * norm(yf) + jax.nn.silu(g_b) * norm(yb)

    out = merge(y_f, rev(y_b), gf, gb)
    out_c = merge(yc_f, rev(yc_b), gfc, gbc) if compute_ctx else None
    return out, out_c


def mla_mixer(q, ckv, k_rope, qc, ckv_c, k_rope_c, kv_norm_g, w_uk, w_uv, rows, cols, compute_ctx):
    B, L = q.shape[0], q.shape[1]
    n_ctx = qc.shape[1]
    scale = (MLA_NOPE_DIM + MLA_ROPE_DIM) ** -0.5
    w_uk = w_uk.reshape(MLA_KV_RANK, MLA_HEADS, MLA_NOPE_DIM)
    w_uv = w_uv.reshape(MLA_KV_RANK, MLA_HEADS, MLA_V_DIM)

    def expand(c):
        c = rms_norm(c, kv_norm_g)
        return jnp.einsum('btr,rhd->bthd', c, w_uk), jnp.einsum('btr,rhd->bthd', c, w_uv)

    k_nope, v = expand(ckv)
    kc_nope, vc = expand(ckv_c)
    k_rope = rope_2d(k_rope[:, :, None, :], rows, cols)[:, :, 0]
    q_nope = q[..., :MLA_NOPE_DIM]
    q_rope = rope_2d(q[..., MLA_NOPE_DIM:], rows, cols)
    nb = L // MLA_Q_BLOCK

    def to_blocks(t):
        return jnp.moveaxis(t.reshape((B, nb, MLA_Q_BLOCK) + t.shape[2:]), 1, 0)

    def attend(args):
        qn, qr = args
        s_c = jnp.einsum('bqhd,bkhd->bhqk', qn, kc_nope) + jnp.einsum('bqhd,bkd->bhqk', qr, k_rope_c)
        s_l = jnp.einsum('bqhd,bkhd->bhqk', qn, k_nope) + jnp.einsum('bqhd,bkd->bhqk', qr, k_rope)
        p = jax.nn.softmax(jnp.concatenate([s_c, s_l], axis=-1).astype(jnp.float32) * scale, axis=-1).astype(v.dtype)
        return jnp.einsum('bhqk,bkhd->bqhd', p[..., :n_ctx], vc) + jnp.einsum('bhqk,bkhd->bqhd', p[..., n_ctx:], v)

    out = lax.map(attend, (to_blocks(q_nope), to_blocks(q_rope)))
    out = jnp.moveaxis(out, 0, 1).reshape(B, L, MLA_WIDTH)
    out_c = None
    if compute_ctx:
        s = (jnp.einsum('bqhd,bkhd->bhqk', qc[..., :MLA_NOPE_DIM], kc_nope)
             + jnp.einsum('bqhd,bkd->bhqk', qc[..., MLA_NOPE_DIM:], k_rope_c))
        pc = jax.nn.softmax(s.astype(jnp.float32) * scale, axis=-1).astype(vc.dtype)
        out_c = jnp.einsum('bhqk,bkhd->bqhd', pc, vc).reshape(B, n_ctx, MLA_WIDTH)
    return out, out_c


def token_mixer(h, hc, w_in, w_out, swa_sink, ret_decay, mla_kv_norm, mla_w_uk, mla_w_uv, rows, cols, compute_ctx):
    B, L, _ = h.shape
    n_ctx = hc.shape[1]

    def hd(t, n):
        return t.reshape(t.shape[0], t.shape[1], n, -1)

    sq, sk, sv, rq, rk, rv, rgf, rgb, mq, mckv, mkr = split_projection(h @ w_in)
    csq, csk, csv, crq, crk, crv, crgf, crgb, cmq, cmckv, cmkr = split_projection(hc @ w_in)
    a, a_c = swa_mixer(hd(sq, SWA_HEADS), hd(sk, SWA_KV_HEADS), hd(sv, SWA_KV_HEADS),
                       hd(csq, SWA_HEADS), hd(csk, SWA_KV_HEADS), hd(csv, SWA_KV_HEADS),
                       swa_sink, rows, cols, compute_ctx)
    r, r_c = retention_mixer(hd(rq, RET_HEADS), hd(rk, RET_HEADS), hd(rv, RET_HEADS), rgf, rgb,
                             hd(crq, RET_HEADS), hd(crk, RET_HEADS), hd(crv, RET_HEADS), crgf, crgb,
                             ret_decay, compute_ctx)
    m, m_c = mla_mixer(hd(mq, MLA_HEADS), mckv, mkr, hd(cmq, MLA_HEADS), cmckv, cmkr,
                       mla_kv_norm, mla_w_uk, mla_w_uv, rows, cols, compute_ctx)
    y = jnp.concatenate([a, r, m], axis=-1) @ w_out
    y_c = jnp.concatenate([a_c, r_c, m_c], axis=-1) @ w_out if compute_ctx else None
    return y, y_c


def routed_experts(h, expert_id, weight, w_gate, w_up, w_down):
    N, D = h.shape
    A = expert_id.shape[0]
    n_blocks = (A + MOE_BLOCK - 1) // MOE_BLOCK + N_EXPERTS
    rows_total = n_blocks * MOE_BLOCK
    tok = jnp.arange(A, dtype=jnp.int32) // TOP_K
    order = jnp.argsort(expert_id)
    e_sorted = expert_id[order]
    counts = jnp.bincount(expert_id, length=N_EXPERTS)
    starts = jnp.cumsum(counts) - counts
    padded = (counts + MOE_BLOCK - 1) // MOE_BLOCK * MOE_BLOCK
    ends = jnp.cumsum(padded)
    pstarts = ends - padded
    dest = pstarts[e_sorted] + jnp.arange(A, dtype=jnp.int32) - starts[e_sorted]
    buf_tok = jnp.full((rows_total,), N, jnp.int32).at[dest].set(tok[order])
    buf_w = jnp.zeros((rows_total,), jnp.float32).at[dest].set(weight[order].astype(jnp.float32))
    block_expert = jnp.minimum(jnp.searchsorted(ends, jnp.arange(n_blocks, dtype=jnp.int32) * MOE_BLOCK, side='right'),
                               N_EXPERTS - 1)
    h_pad = jnp.concatenate([h, jnp.zeros((1, D), h.dtype)], axis=0)

    def expert_block(args):
        tok_blk, e = args
        xb = h_pad[tok_blk]
        return (jax.nn.silu(xb @ w_gate[e]) * (xb @ w_up[e])) @ w_down[e]

    yb = lax.map(expert_block, (buf_tok.reshape(n_blocks, MOE_BLOCK), block_expert)).reshape(rows_total, D)
    yb = yb * buf_w[:, None].astype(yb.dtype)
    return jax.ops.segment_sum(yb, buf_tok, num_segments=N + 1)[:N]


def hier_moe(h, w_group, b_group, w_expert, b_expert, w_gate, w_up, w_down):
    N = h.shape[0]
    g_logits = (h @ w_group).astype(jnp.float32) + b_group.astype(jnp.float32)
    g_prob = jax.nn.softmax(g_logits, axis=-1)
    g_idx = jnp.argmax(g_logits, axis=-1)
    rows = jnp.arange(N)
    p_group = g_prob[rows, g_idx][:, None]
    e_logits = ((h @ w_expert).astype(jnp.float32) + b_expert.astype(jnp.float32)).reshape(N, N_GROUPS, EXPERTS_PER_GROUP)
    e_in = e_logits[rows, g_idx]
    e_top, e_idx = lax.top_k(e_in, TOP_K)
    gate = jax.nn.softmax(e_top, axis=-1) * p_group
    expert_id = (g_idx[:, None] * EXPERTS_PER_GROUP + e_idx).astype(jnp.int32).reshape(-1)
    return routed_experts(h, expert_id, gate.reshape(-1), w_gate, w_up, w_down)


def setup_inputs(seed: int = 0) -> dict:
    key = jax.random.key(seed)
    ks = jax.random.split(key, 26)
    f32 = jnp.float32
    Dm = D_MODEL

    def nrm(k, shape, scale):
        return jax.random.normal(k, shape, f32) * scale

    return {
        'x': nrm(ks[0], (BATCH, SEQ, Dm), 1.0),
        'c': nrm(ks[1], (BATCH, Dm), 1.0),
        'ctx': nrm(ks[2], (BATCH, CTX_LEN, Dm), 1.0),
        'c_ctx': nrm(ks[3], (Dm,), 1.0),
        'w_ada': nrm(ks[4], (DEPTH, Dm, 6 * Dm), Dm ** -0.5),
        'b_ada': nrm(ks[5], (DEPTH, 6 * Dm), 0.02),
        'w_in': nrm(ks[6], (DEPTH, Dm, IN_WIDTH), Dm ** -0.5),
        'swa_sink': nrm(ks[7], (DEPTH, SWA_HEADS), 0.5),
        'ret_decay': RET_DECAY_BASE + jnp.arange(RET_HEADS, dtype=f32) + nrm(ks[8], (DEPTH, 2, RET_HEADS), 0.1),
        'mla_kv_norm': 1.0 + nrm(ks[9], (DEPTH, MLA_KV_RANK), 0.02),
        'mla_w_uk': nrm(ks[10], (DEPTH, MLA_KV_RANK, MLA_HEADS * MLA_NOPE_DIM), MLA_KV_RANK ** -0.5),
        'mla_w_uv': nrm(ks[11], (DEPTH, MLA_KV_RANK, MLA_HEADS * MLA_V_DIM), MLA_KV_RANK ** -0.5),
        'w_out': nrm(ks[12], (DEPTH, MIX_WIDTH, Dm), BETA * MIX_WIDTH ** -0.5),
        'ln1_g': 1.0 + nrm(ks[13], (DEPTH, Dm), 0.02),
        'ln1_b': nrm(ks[14], (DEPTH, Dm), 0.02),
        'ln2_g': 1.0 + nrm(ks[15], (DEPTH, Dm), 0.02),
        'ln2_b': nrm(ks[16], (DEPTH, Dm), 0.02),
        'moe_w_group': nrm(ks[17], (DEPTH, Dm, N_GROUPS), Dm ** -0.5),
        'moe_b_group': nrm(ks[18], (DEPTH, N_GROUPS), 0.01),
        'moe_w_expert': nrm(ks[19], (DEPTH, Dm, N_EXPERTS), Dm ** -0.5),
        'moe_b_expert': nrm(ks[20], (DEPTH, N_EXPERTS), 0.01),
        'moe_w_gate': nrm(ks[21], (DEPTH, N_EXPERTS, Dm, EXPERT_HIDDEN), Dm ** -0.5),
        'moe_w_up': nrm(ks[22], (DEPTH, N_EXPERTS, Dm, EXPERT_HIDDEN), Dm ** -0.5),
        'moe_w_down': nrm(ks[23], (DEPTH, N_EXPERTS, EXPERT_HIDDEN, Dm), BETA * EXPERT_HIDDEN ** -0.5),
    }


def reference(x, c, ctx, c_ctx, w_ada, b_ada, w_in, swa_sink, ret_decay, mla_kv_norm, mla_w_uk, mla_w_uv,
              w_out, ln1_g, ln1_b, ln2_g, ln2_b, moe_w_group, moe_b_group, moe_w_expert, moe_b_expert,
              moe_w_gate, moe_w_up, moe_w_down):
    B, L, D = x.shape
    n_rows = L // GRID_W
    rows = jnp.repeat(jnp.arange(n_rows, dtype=jnp.float32), GRID_W)
    cols = jnp.tile(jnp.arange(GRID_W, dtype=jnp.float32), n_rows)
    for l in range(DEPTH):
        ctx_out = l < DEPTH - 1
        mod = (jax.nn.silu(c) @ w_ada[l] + b_ada[l]).reshape(B, 6, D)
        mod_c = (jax.nn.silu(c_ctx) @ w_ada[l] + b_ada[l]).reshape(6, D)
        h = modulate(x, mod[:, 0, None], mod[:, 1, None])
        hc = modulate(ctx, mod_c[0], mod_c[1])
        y, y_c = token_mixer(h, hc, w_in[l], w_out[l], swa_sink[l], ret_decay[l], mla_kv_norm[l],
                             mla_w_uk[l], mla_w_uv[l], rows, cols, ctx_out)
        x = affine_layer_norm(ALPHA * x + mod[:, 2, None] * y, ln1_g[l], ln1_b[l])
        h = modulate(x, mod[:, 3, None], mod[:, 4, None]).reshape(B * L, D)
        if ctx_out:
            ctx = affine_layer_norm(ALPHA * ctx + mod_c[2] * y_c, ln1_g[l], ln1_b[l])
            hc = modulate(ctx, mod_c[3], mod_c[4]).reshape(-1, D)
            tokens = jnp.concatenate([h, hc], axis=0)
        else:
            tokens = h
        f = hier_moe(tokens, moe_w_group[l], moe_b_group[l], moe_w_expert[l], moe_b_expert[l],
                     moe_w_gate[l], moe_w_up[l], moe_w_down[l])
        x = affine_layer_norm(ALPHA * x + mod[:, 5, None] * f[:B * L].reshape(B, L, D), ln2_g[l], ln2_b[l])
        if ctx_out:
            ctx = affine_layer_norm(ALPHA * ctx + mod_c[5] * f[B * L:].reshape(B, -1, D), ln2_g[l], ln2_b[l])
    return x
```

```python
import functools

import jax
import jax.numpy as jnp
from jax import lax
from jax.experimental import pallas as pl
from jax.experimental.pallas import tpu as pltpu

f32 = jnp.float32
bf16 = jnp.bfloat16
i32 = jnp.int32

GRID_W = 64
SWA_HEADS, SWA_KV_HEADS, SWA_HEAD_DIM, SWA_WINDOW = 6, 2, 128, 128
RET_HEADS, RET_QK_DIM, RET_V_DIM, RET_CHUNK = 6, 64, 128, 128
MLA_HEADS, MLA_NOPE_DIM, MLA_ROPE_DIM, MLA_V_DIM, MLA_KV_RANK = 4, 128, 64, 128, 256
N_GROUPS, EXPERTS_PER_GROUP, TOP_K = 4, 8, 2
N_EXPERTS = N_GROUPS * EXPERTS_PER_GROUP
ROPE_BASE = 10000.0
NORM_EPS = 1e-6
NEG_INF = -1e30
LANES = 128

_SQ, _RV, _GF, _GB, _MQN, _SK, _SV, _MQR, _CKV, _RQ, _RK, _KR = (
    0, 768, 1536, 2304, 3072, 3584, 3840, 4096, 4352, 4608, 4992, 5376)
_NP = 5632
_ROPE_SEGS = ((_SQ // 128, 6, 0), (_SK // 128, 2, 0), (_MQR // 128, 2, 2),
              (_RQ // 128, 3, 1), (_RK // 128, 3, 1), (_KR // 128, 1, 2))

_VMEM_LIMIT = 48 * 1024 * 1024


def _cparams(*sem):
    return pltpu.CompilerParams(dimension_semantics=sem, vmem_limit_bytes=_VMEM_LIMIT)


def _tile(n, pref):
    t = min(n, pref)
    while n % t:
        t //= 2
    return t


def _ln(x):
    mu = jnp.mean(x, axis=-1, keepdims=True)
    xc = x - mu
    var = jnp.mean(xc * xc, axis=-1, keepdims=True)
    return xc * lax.rsqrt(var + NORM_EPS)


def _silu(x):
    return x / (1.0 + jnp.exp(-x))


def _dot(a, b):
    return jnp.dot(a, b, preferred_element_type=f32)


def _dot_nt(a, b):
    return lax.dot_general(a, b, (((1,), (1,)), ((), ())), preferred_element_type=f32)


def _dot_tn(a, b):
    return lax.dot_general(a, b, (((0,), (0,)), ((), ())), preferred_element_type=f32)


def _ada_kernel(c_ref, w_ref, b_ref, o_ref):
    s = _silu(c_ref[...]).astype(bf16)
    o_ref[0] = _dot(s, w_ref[0].astype(bf16)) + b_ref[0]


def _ada(cc, w_ada, b_ada):
    depth, d, n = w_ada.shape
    tn = _tile(n, 1024)
    return pl.pallas_call(
        _ada_kernel,
        grid=(depth, n // tn),
        in_specs=[pl.BlockSpec((8, d), lambda l, j: (0, 0)),
                  pl.BlockSpec((1, d, tn), lambda l, j: (l, 0, j)),
                  pl.BlockSpec((1, 1, tn), lambda l, j: (l, 0, j))],
        out_specs=pl.BlockSpec((1, 8, tn), lambda l, j: (l, 0, j)),
        out_shape=jax.ShapeDtypeStruct((depth, 8, n), f32),
        compiler_params=_cparams("parallel", "parallel"),
        name="ada",
    )(cc, w_ada, b_ada.reshape(depth, 1, n))


def _inproj_kernel(x_ref, mod_ref, w_ref, o_ref, h_scr):
    @pl.when(pl.program_id(1) == 0)
    def _():
        y = _ln(x_ref[...])
        h_scr[...] = (y * (1.0 + mod_ref[0, 1:2, :]) + mod_ref[0, 0:1, :]).astype(bf16)

    o_ref[...] = _dot(h_scr[...], w_ref[...]).astype(bf16)


def _inproj(xs, mod, w, n_lat, lat_len, n_batch):
    t, d = xs.shape
    n = w.shape[1]
    tm = _tile(lat_len, 512)
    while (t - n_lat) % tm:
        tm //= 2
    tn = _tile(n, 512)
    nlb, bpb = n_lat // tm, lat_len // tm

    def grp(i):
        return jnp.where(i < nlb, i // bpb, n_batch)

    return pl.pallas_call(
        _inproj_kernel,
        grid=(t // tm, n // tn),
        in_specs=[pl.BlockSpec((tm, d), lambda i, j: (i, 0)),
                  pl.BlockSpec((1, 6, d), lambda i, j: (grp(i), 0, 0)),
                  pl.BlockSpec((d, tn), lambda i, j: (0, j))],
        out_specs=pl.BlockSpec((tm, tn), lambda i, j: (i, j)),
        out_shape=jax.ShapeDtypeStruct((t, n), bf16),
        scratch_shapes=[pltpu.VMEM((tm, d), bf16)],
        compiler_params=_cparams("parallel", "arbitrary"),
        name="inproj",
    )(xs, mod, w)


def _rope_tables(lat_len, ctx_len):
    lane = jnp.arange(LANES)
    t = jnp.arange(lat_len)
    rows = (t // GRID_W).astype(f32)
    cols = (t % GRID_W).astype(f32)

    def freq(half):
        return ROPE_BASE ** (-jnp.arange(half, dtype=f32) / half)

    def table(pos, inv, sign):
        ang = pos * inv[None, :]
        return jnp.stack([jnp.cos(ang), jnp.sin(ang) * sign[None, :]])

    def ident(n):
        return jnp.stack([jnp.ones((n, LANES), f32), jnp.zeros((n, LANES), f32)])

    sign32 = jnp.where((lane % 64) < 32, -1.0, 1.0).astype(f32)
    sign16 = jnp.where((lane % 32) < 16, -1.0, 1.0).astype(f32)
    inv32 = freq(32)[lane % 32]
    inv16 = freq(16)[lane % 16]
    pos_swa = jnp.where((lane // 64)[None, :] == 0, rows[:, None], cols[:, None])
    pos_mla = jnp.where(((lane % 64) // 32)[None, :] == 0, rows[:, None], cols[:, None])
    pos_ret_l = jnp.broadcast_to((ctx_len + t).astype(f32)[:, None], (lat_len, LANES))
    pos_ret_c = jnp.broadcast_to(jnp.arange(ctx_len).astype(f32)[:, None], (ctx_len, LANES))
    t0 = jnp.concatenate([table(pos_swa, inv32, sign32), ident(ctx_len)], axis=1)
    t1 = jnp.concatenate([table(pos_ret_l, inv32, sign32), table(pos_ret_c, inv32, sign32)], axis=1)
    t2 = jnp.concatenate([table(pos_mla, inv16, sign16), ident(ctx_len)], axis=1)
    return jnp.stack([t0, t1, t2])


def _rope_kernel(col_ref, typ_ref, p_ref, tab_ref, o_ref):
    s = pl.program_id(0)
    x = p_ref[...].astype(f32)
    cos = tab_ref[0, 0]
    sin = tab_ref[0, 1]
    lane = lax.broadcasted_iota(i32, x.shape, 1)

    def apply(half):
        first = (lane & (2 * half - 1)) < half
        partner = jnp.where(first, pltpu.roll(x, LANES - half, 1), pltpu.roll(x, half, 1))
        o_ref[...] = (x * cos + partner * sin).astype(bf16)

    @pl.when(typ_ref[s] != 2)
    def _():
        apply(32)

    @pl.when(typ_ref[s] == 2)
    def _():
        apply(16)


def _rope(p, tab, n_batch, lat_len, ctx_len):
    tr = _tile(ctx_len, 256)
    nlb, ncb = lat_len // tr, ctx_len // tr
    cols, typs = [], []
    for first, count, typ in _ROPE_SEGS:
        cols += list(range(first, first + count))
        typs += [typ] * count

    def rowblk(pb, b):
        return jnp.where(pb < nlb, b * nlb + pb, n_batch * nlb + b * ncb + (pb - nlb))

    grid_spec = pltpu.PrefetchScalarGridSpec(
        num_scalar_prefetch=2,
        grid=(len(cols), nlb + ncb, n_batch),
        in_specs=[pl.BlockSpec((tr, LANES), lambda s, pb, b, c, ty: (rowblk(pb, b), c[s])),
                  pl.BlockSpec((1, 2, tr, LANES), lambda s, pb, b, c, ty: (ty[s], 0, pb, 0))],
        out_specs=pl.BlockSpec((tr, LANES), lambda s, pb, b, c, ty: (rowblk(pb, b), c[s])),
    )
    return pl.pallas_call(
        _rope_kernel,
        grid_spec=grid_spec,
        out_shape=jax.ShapeDtypeStruct(p.shape, p.dtype),
        input_output_aliases={2: 0},
        compiler_params=_cparams("arbitrary", "arbitrary", "arbitrary"),
        name="rope",
    )(jnp.array(cols, i32), jnp.array(typs, i32), p, tab)


def _mla_expand_kernel(ckv_ref, kr_ref, g_ref, wuk_ref, wuv_ref, k_ref, v_ref):
    c = ckv_ref[...].astype(f32)
    cn = (c * lax.rsqrt(jnp.mean(c * c, axis=-1, keepdims=True) + NORM_EPS) * g_ref[...]).astype(bf16)
    kn = _dot(cn, wuk_ref[...]).astype(bf16)
    v_ref[...] = _dot(cn, wuv_ref[...]).astype(bf16)
    kr = kr_ref[...]
    for h in range(MLA_HEADS):
        k_ref[:, h * 256:h * 256 + 128] = kn[:, h * 128:(h + 1) * 128]
        k_ref[:, h * 256 + 128:(h + 1) * 256] = kr


def _mla_expand(p, g, wuk, wuv):
    t = p.shape[0]
    tm = _tile(t, 512)
    return pl.pallas_call(
        _mla_expand_kernel,
        grid=(t // tm,),
        in_specs=[pl.BlockSpec((tm, 256), lambda i: (i, _CKV // 256)),
                  pl.BlockSpec((tm, 128), lambda i: (i, _KR // 128)),
                  pl.BlockSpec((1, 256), lambda i: (0, 0)),
                  pl.BlockSpec((256, 512), lambda i: (0, 0)),
                  pl.BlockSpec((256, 512), lambda i: (0, 0))],
        out_specs=[pl.BlockSpec((tm, 1024), lambda i: (i, 0)),
                   pl.BlockSpec((tm, 512), lambda i: (i, 0))],
        out_shape=[jax.ShapeDtypeStruct((t, 1024), bf16), jax.ShapeDtypeStruct((t, 512), bf16)],
        compiler_params=_cparams("parallel"),
        name="mla_expand",
    )(p, p, g, wuk, wuv)


def _mla_kernel(qn_ref, qr_ref, k_ref, v_ref, o_ref, qx_scr, m_scr, l_scr, acc_scr, *, scale):
    kb = pl.program_id(2)

    @pl.when(kb == 0)
    def _():
        m_scr[...] = jnp.full(m_scr.shape, -jnp.inf, f32)
        l_scr[...] = jnp.zeros(l_scr.shape, f32)
        acc_scr[...] = jnp.zeros(acc_scr.shape, f32)
        lane = lax.broadcasted_iota(i32, (qn_ref.shape[0], LANES), 1)
        for h in range(MLA_HEADS):
            qn = qn_ref[:, h * 128:(h + 1) * 128].astype(f32) * scale
            slab = qr_ref[:, (h // 2) * 128:(h // 2 + 1) * 128].astype(f32)
            if h % 2:
                slab = pltpu.roll(slab, 64, 1)
            qr = jnp.where(lane < 64, slab, 0.0) * scale
            qx_scr[h, :, 0:128] = qn.astype(bf16)
            qx_scr[h, :, 128:256] = qr.astype(bf16)

    for h in range(MLA_HEADS):
        s = _dot_nt(qx_scr[h], k_ref[:, h * 256:(h + 1) * 256])
        m_prev = m_scr[h]
        m_new = jnp.maximum(m_prev, jnp.max(s, axis=1, keepdims=True))
        alpha = jnp.exp(m_prev - m_new)
        p = jnp.exp(s - m_new[:, 0:1])
        l_scr[h] = alpha * l_scr[h] + jnp.sum(p, axis=1, keepdims=True)
        acc_scr[h] = alpha * acc_scr[h] + _dot(p.astype(bf16), v_ref[:, h * 128:(h + 1) * 128])
        m_scr[h] = m_new

    @pl.when(kb == pl.num_programs(2) - 1)
    def _():
        for h in range(MLA_HEADS):
            o_ref[:, h * 128:(h + 1) * 128] = (acc_scr[h] / l_scr[h]).astype(bf16)


def _mla_call(p, kx, v, prev, grid, tq, tk, qrow, krow):
    t = p.shape[0]
    scale = (MLA_NOPE_DIM + MLA_ROPE_DIM) ** -0.5
    in_specs = [pl.BlockSpec((tq, 512), lambda b, qi, kb: (qrow(b, qi), _MQN // 512)),
                pl.BlockSpec((tq, 256), lambda b, qi, kb: (qrow(b, qi), _MQR // 256)),
                pl.BlockSpec((tk, 1024), lambda b, qi, kb: (krow(b, kb), 0)),
                pl.BlockSpec((tk, 512), lambda b, qi, kb: (krow(b, kb), 0))]
    args = [p, p, kx, v]
    aliases = {}
    kern = functools.partial(_mla_kernel, scale=scale)
    if prev is not None:
        in_specs.append(pl.BlockSpec(memory_space=pl.ANY))
        args.append(prev)
        aliases = {4: 0}
        kern = functools.partial(_mla_kernel_aliased, scale=scale)
    return pl.pallas_call(
        kern,
        grid=grid,
        in_specs=in_specs,
        out_specs=pl.BlockSpec((tq, 512), lambda b, qi, kb: (qrow(b, qi), 0)),
        out_shape=jax.ShapeDtypeStruct((t, 512), bf16),
        scratch_shapes=[pltpu.VMEM((MLA_HEADS, tq, 256), bf16), pltpu.VMEM((MLA_HEADS, tq, LANES), f32),
                        pltpu.VMEM((MLA_HEADS, tq, LANES), f32), pltpu.VMEM((MLA_HEADS, tq, LANES), f32)],
        input_output_aliases=aliases,
        compiler_params=_cparams("parallel", "parallel", "arbitrary"),
        name="mla_attn",
    )(*args)


def _mla_kernel_aliased(qn_ref, qr_ref, k_ref, v_ref, prev_ref, o_ref, *scr, scale):
    del prev_ref
    _mla_kernel(qn_ref, qr_ref, k_ref, v_ref, o_ref, *scr, scale=scale)


def _mla_attention(p, kx, v, n_batch, lat_len, ctx_len, compute_ctx):
    tk = ctx_len
    tq = _tile(lat_len, 512)
    nlq, nlk = lat_len // tq, lat_len // tk
    out = _mla_call(p, kx, v, None, (n_batch, nlq, nlk + 1), tq, tk,
                    lambda b, qi: b * nlq + qi,
                    lambda b, kb: jnp.where(kb < nlk, b * nlk + kb, n_batch * nlk + b))
    if compute_ctx:
        base = n_batch * lat_len // ctx_len
        out = _mla_call(p, kx, v, out, (n_batch, 1, 1), ctx_len, ctx_len,
                        lambda b, qi: base + b, lambda b, kb: base + b)
    return out


def _swa_softmax_out(parts, sink_col, o_ref, g):
    m = sink_col
    for s, _ in parts:
        m = jnp.maximum(m, jnp.max(s, axis=1, keepdims=True))
    den = jnp.exp(sink_col - m)
    acc = None
    for s, vv in parts:
        pr = jnp.exp(s - m)
        den = den + jnp.sum(pr, axis=1, keepdims=True)
        pv = _dot(pr.astype(bf16), vv)
        acc = pv if acc is None else acc + pv
    o = acc / den
    nq = o.shape[0] // 3
    for j in range(3):
        o_ref[:, (3 * g + j) * 128:(3 * g + j + 1) * 128] = o[j * nq:(j + 1) * nq].astype(bf16)


def _sink_col(sink_ref, g, nq):
    row = lax.broadcasted_iota(i32, (3 * nq, 1), 0)
    return jnp.where(row < nq, sink_ref[3 * g], jnp.where(row < 2 * nq, sink_ref[3 * g + 1], sink_ref[3 * g + 2]))


def _swa_kernel(sink_ref, q_ref, kp_ref, ko_ref, kn_ref, vp_ref, vo_ref, vn_ref, kc_ref, vc_ref, o_ref, *, scale):
    i = pl.program_id(1)
    nb = pl.num_programs(1)
    blk = q_ref.shape[0]
    q = q_ref[...]
    r = lax.broadcasted_iota(i32, (3 * blk, 3 * blk), 0) & (blk - 1)
    c = lax.broadcasted_iota(i32, (3 * blk, 3 * blk), 1)
    lo = jnp.where(i == 0, blk, 0)
    hi = jnp.where(i == nb - 1, 2 * blk, 3 * blk)
    d = c - r
    valid = (d >= 0) & (d <= 2 * SWA_WINDOW) & (c >= lo) & (c < hi)
    for g in range(SWA_KV_HEADS):
        hs = slice(g * 128, (g + 1) * 128)
        qg = jnp.concatenate([q[:, (3 * g + j) * 128:(3 * g + j + 1) * 128] for j in range(3)], axis=0)
        kloc = jnp.concatenate([kp_ref[:, hs], ko_ref[:, hs], kn_ref[:, hs]], axis=0)
        vloc = jnp.concatenate([vp_ref[:, hs], vo_ref[:, hs], vn_ref[:, hs]], axis=0)
        s_loc = jnp.where(valid, _dot_nt(qg, kloc) * scale, NEG_INF)
        s_ctx = _dot_nt(qg, kc_ref[:, hs]) * scale
        _swa_softmax_out([(s_ctx, vc_ref[:, hs]), (s_loc, vloc)], _sink_col(sink_ref, g, blk), o_ref, g)


def _swa_ctx_kernel(sink_ref, q_ref, kc_ref, vc_ref, prev_ref, o_ref, *, scale):
    del prev_ref
    q = q_ref[...]
    nq = q.shape[0]
    for g in range(SWA_KV_HEADS):
        hs = slice(g * 128, (g + 1) * 128)
        qg = jnp.concatenate([q[:, (3 * g + j) * 128:(3 * g + j + 1) * 128] for j in range(3)], axis=0)
        s_ctx = _dot_nt(qg, kc_ref[:, hs]) * scale
        _swa_softmax_out([(s_ctx, vc_ref[:, hs])], _sink_col(sink_ref, g, nq), o_ref, g)


def _swa_attention(p, sink, n_batch, lat_len, ctx_len, compute_ctx):
    t = p.shape[0]
    blk = SWA_WINDOW
    nb = lat_len // blk
    cbase = n_batch * lat_len // ctx_len
    scale = SWA_HEAD_DIM ** -0.5
    smem = pl.BlockSpec(memory_space=pltpu.SMEM)

    def kspec(col, off):
        return pl.BlockSpec((blk, 256), lambda b, i: (b * nb + jnp.clip(i + off, 0, nb - 1), col // 256))

    out = pl.pallas_call(
        functools.partial(_swa_kernel, scale=scale),
        grid=(n_batch, nb),
        in_specs=[smem,
                  pl.BlockSpec((blk, 768), lambda b, i: (b * nb + i, _SQ // 768)),
                  kspec(_SK, -1), kspec(_SK, 0), kspec(_SK, 1),
                  kspec(_SV, -1), kspec(_SV, 0), kspec(_SV, 1),
                  pl.BlockSpec((ctx_len, 256), lambda b, i: (cbase + b, _SK // 256)),
                  pl.BlockSpec((ctx_len, 256), lambda b, i: (cbase + b, _SV // 256))],
        out_specs=pl.BlockSpec((blk, 768), lambda b, i: (b * nb + i, 0)),
        out_shape=jax.ShapeDtypeStruct((t, 768), bf16),
        compiler_params=_cparams("parallel", "parallel"),
        name="swa_attn",
    )(sink, p, p, p, p, p, p, p, p, p)
    if compute_ctx:
        out = pl.pallas_call(
            functools.partial(_swa_ctx_kernel, scale=scale),
            grid=(n_batch,),
            in_specs=[smem,
                      pl.BlockSpec((ctx_len, 768), lambda b: (cbase + b, _SQ // 768)),
                      pl.BlockSpec((ctx_len, 256), lambda b: (cbase + b, _SK // 256)),
                      pl.BlockSpec((ctx_len, 256), lambda b: (cbase + b, _SV // 256)),
                      pl.BlockSpec(memory_space=pl.ANY)],
            out_specs=pl.BlockSpec((ctx_len, 768), lambda b: (cbase + b, 0)),
            out_shape=jax.ShapeDtypeStruct((t, 768), bf16),
            input_output_aliases={4: 0},
            compiler_params=_cparams("parallel"),
            name="swa_ctx",
        )(sink, p, p, p, out)
    return out


def _ret_kernel(lg_ref, q_ref, k_ref, v_ref, *rest, backward):
    if backward:
        yf_ref, gf_ref, gb_ref, o_ref, s_scr, d_scr, qd_scr, kd_scr, gc_scr = rest
    else:
        o_ref, s_scr, d_scr, qd_scr, kd_scr, gc_scr = rest
    i = pl.program_id(1)
    cc = RET_CHUNK
    direction = 1 if backward else 0

    @pl.when(i == 0)
    def _():
        s_scr[...] = jnp.zeros(s_scr.shape, f32)
        r = lax.broadcasted_iota(i32, (cc, cc), 0).astype(f32)
        c = lax.broadcasted_iota(i32, (cc, cc), 1).astype(f32)
        diff = (c - r) if backward else (r - c)
        for h in range(RET_HEADS):
            lgh = lg_ref[direction, h]
            d_scr[h] = jnp.where(diff >= 0, jnp.exp(lgh * jnp.maximum(diff, 0.0)), 0.0)
            qd_scr[h] = jnp.exp(lgh * ((cc - r) if backward else (r + 1.0))) * (RET_QK_DIM ** -0.5)
            kd_scr[h] = jnp.exp(lgh * (r if backward else (cc - 1.0 - r)))
            gc_scr[h] = jnp.exp(lgh * cc + jnp.zeros((cc, cc), f32))

    lane = lax.broadcasted_iota(i32, (cc, LANES), 1)
    for pair in range(RET_HEADS // 2):
        q2 = q_ref[:, pair * 128:(pair + 1) * 128].astype(f32)
        k2b = k_ref[:, pair * 128:(pair + 1) * 128]
        k2 = k2b.astype(f32)
        for sub in range(2):
            h = 2 * pair + sub
            hs = slice(h * 128, (h + 1) * 128)
            qa = jnp.where((lane < 64) if sub == 0 else (lane >= 64), q2, 0.0)
            s = _dot_nt((qa * (RET_QK_DIM ** -0.5)).astype(bf16), k2b)
            vh = v_ref[:, hs]
            y = _dot((s * d_scr[h]).astype(bf16), vh) + _dot((qa * qd_scr[h]).astype(bf16), s_scr[h].astype(bf16))
            s_scr[h] = gc_scr[h] * s_scr[h] + _dot_tn((k2 * kd_scr[h]).astype(bf16), vh)
            yn = _ln(y)
            if backward:
                o_ref[:, hs] = (_silu(gf_ref[:, hs].astype(f32)) * yf_ref[:, hs].astype(f32)
                                + _silu(gb_ref[:, hs].astype(f32)) * yn).astype(bf16)
            else:
                o_ref[:, hs] = yn.astype(bf16)


def _retention(p, lg, n_batch, lat_len, ctx_len):
    t = p.shape[0]
    cc = RET_CHUNK
    ncc, nlc = ctx_len // cc, lat_len // cc
    cbase = n_batch * nlc
    smem = pl.BlockSpec(memory_space=pltpu.SMEM)
    scratch = [pltpu.VMEM((RET_HEADS, cc, cc), f32) for _ in range(5)]

    def fwd_row(b, i):
        return jnp.where(i < ncc, cbase + b * ncc + i, b * nlc + (i - ncc))

    def bwd_row(b, i):
        return jnp.where(i < ncc, cbase + b * ncc + (ncc - 1 - i), b * nlc + (nlc - 1 - (i - ncc)))

    def specs(row):
        return [pl.BlockSpec((cc, 384), lambda b, i: (row(b, i), _RQ // 384)),
                pl.BlockSpec((cc, 384), lambda b, i: (row(b, i), _RK // 384)),
                pl.BlockSpec((cc, 768), lambda b, i: (row(b, i), _RV // 768))]

    yf = pl.pallas_call(
        functools.partial(_ret_kernel, backward=False),
        grid=(n_batch, ncc + nlc),
        in_specs=[smem] + specs(fwd_row),
        out_specs=pl.BlockSpec((cc, 768), lambda b, i: (fwd_row(b, i), 0)),
        out_shape=jax.ShapeDtypeStruct((t, 768), bf16),
        scratch_shapes=scratch,
        compiler_params=_cparams("parallel", "arbitrary"),
        name="ret_fwd",
    )(lg, p, p, p)
    return pl.pallas_call(
        functools.partial(_ret_kernel, backward=True),
        grid=(n_batch, ncc + nlc),
        in_specs=[smem] + specs(bwd_row) + [
            pl.BlockSpec((cc, 768), lambda b, i: (bwd_row(b, i), 0)),
            pl.BlockSpec((cc, 768), lambda b, i: (bwd_row(b, i), _GF // 768)),
            pl.BlockSpec((cc, 768), lambda b, i: (bwd_row(b, i), _GB // 768))],
        out_specs=pl.BlockSpec((cc, 768), lambda b, i: (bwd_row(b, i), 0)),
        out_shape=jax.ShapeDtypeStruct((t, 768), bf16),
        scratch_shapes=scratch,
        compiler_params=_cparams("parallel", "arbitrary"),
        name="ret_bwd",
    )(lg, p, p, p, yf, p, p)


def _outproj_kernel(a_ref, r_ref, m_ref, x_ref, mod_ref, w_ref, g_ref, b_ref, xo_ref, h_ref, *, alpha):
    y = (_dot(a_ref[...], w_ref[0:768, :]) + _dot(r_ref[...], w_ref[768:1536, :])
         + _dot(m_ref[...], w_ref[1536:2048, :]))
    xn = _ln(alpha * x_ref[...] + mod_ref[0, 2:3, :] * y) * g_ref[...] + b_ref[...]
    xo_ref[...] = xn
    h_ref[...] = _ln(xn) * (1.0 + mod_ref[0, 4:5, :]) + mod_ref[0, 3:4, :]


def _outproj(a, r, m, xs, mod, w, g, b, n_rows, n_lat, lat_len, n_batch, alpha):
    d = xs.shape[1]
    tm = _tile(lat_len, 256)
    while (n_rows - n_lat) % tm:
        tm //= 2
    nlb, bpb = n_lat // tm, lat_len // tm

    def grp(i):
        return jnp.where(i < nlb, i // bpb, n_batch)

    row = lambda i: (i, 0)
    return pl.pallas_call(
        functools.partial(_outproj_kernel, alpha=alpha),
        grid=(n_rows // tm,),
        in_specs=[pl.BlockSpec((tm, 768), row), pl.BlockSpec((tm, 768), row), pl.BlockSpec((tm, 512), row),
                  pl.BlockSpec((tm, d), row),
                  pl.BlockSpec((1, 6, d), lambda i: (grp(i), 0, 0)),
                  pl.BlockSpec(w.shape, lambda i: (0, 0)),
                  pl.BlockSpec((1, d), lambda i: (0, 0)), pl.BlockSpec((1, d), lambda i: (0, 0))],
        out_specs=[pl.BlockSpec((tm, d), row), pl.BlockSpec((tm, d), row)],
        out_shape=[jax.ShapeDtypeStruct((n_rows, d), f32), jax.ShapeDtypeStruct((n_rows, d), f32)],
        compiler_params=_cparams("parallel"),
        name="outproj",
    )(a, r, m, xs, mod, w, g, b)


_ROUTE_LANE0 = N_GROUPS


def _router_kernel(h_ref, w_ref, b_ref, o_ref, cnt_ref, carry_scr):
    i = pl.program_id(0)

    @pl.when(i == 0)
    def _():
        carry_scr[...] = jnp.zeros(carry_scr.shape, f32)

    tm = h_ref.shape[0]
    logits = _dot(h_ref[...].astype(bf16), w_ref[...]) + b_ref[...]
    lane = lax.broadcasted_iota(i32, (tm, LANES), 1)
    lane_f = lane.astype(f32)
    big = float(2 * LANES)
    gl = jnp.where(lane < N_GROUPS, logits, -jnp.inf)
    gmax = jnp.max(gl, axis=1, keepdims=True)
    gidx = jnp.min(jnp.where(gl == gmax, lane_f, big), axis=1, keepdims=True)
    p_group = 1.0 / jnp.sum(jnp.exp(gl - gmax), axis=1, keepdims=True)
    egroup = ((lane - _ROUTE_LANE0) >> 3).astype(f32)
    in_group = (lane >= _ROUTE_LANE0) & (lane < _ROUTE_LANE0 + N_EXPERTS) & (egroup == gidx)
    ev = jnp.where(in_group, logits, -jnp.inf)
    e1 = jnp.max(ev, axis=1, keepdims=True)
    i1 = jnp.min(jnp.where(ev == e1, lane_f, big), axis=1, keepdims=True)
    ev2 = jnp.where(lane_f == i1, -jnp.inf, ev)
    e2 = jnp.max(ev2, axis=1, keepdims=True)
    i2 = jnp.min(jnp.where(ev2 == e2, lane_f, big), axis=1, keepdims=True)
    tt = jnp.exp(e2 - e1)
    w1 = p_group / (1.0 + tt)
    w2 = p_group * tt / (1.0 + tt)

    hit1 = lane_f == i1
    hit2 = lane_f == i2
    onehot = jnp.where(hit1, 1.0, jnp.where(hit2, 1.0, 0.0))
    rr = lax.broadcasted_iota(i32, (tm, tm), 0)
    cc = lax.broadcasted_iota(i32, (tm, tm), 1)
    lower = jnp.where(cc < rr, 1.0, 0.0).astype(bf16)
    before = _dot(lower, onehot.astype(bf16)) + carry_scr[0:1, :]
    rank1 = jnp.sum(jnp.where(hit1, before, 0.0), axis=1, keepdims=True)
    rank2 = jnp.sum(jnp.where(hit2, before, 0.0), axis=1, keepdims=True)
    carry_scr[0:1, :] = carry_scr[0:1, :] + jnp.sum(onehot, axis=0, keepdims=True)

    o_ref[...] = jnp.where(lane == 0, i1 - _ROUTE_LANE0, jnp.where(lane == 1, i2 - _ROUTE_LANE0, jnp.where(
        lane == 2, rank1, jnp.where(lane == 3, rank2, jnp.where(lane == 4, w1, jnp.where(lane == 5, w2, 0.0))))))
    cnt_ref[...] = carry_scr[...]


def _router(h, w, b):
    n, d = h.shape
    tm = _tile(n, 256)
    return pl.pallas_call(
        _router_kernel,
        grid=(n // tm,),
        in_specs=[pl.BlockSpec((tm, d), lambda i: (i, 0)),
                  pl.BlockSpec((d, LANES), lambda i: (0, 0)),
                  pl.BlockSpec((1, LANES), lambda i: (0, 0))],
        out_specs=[pl.BlockSpec((tm, LANES), lambda i: (i, 0)), pl.BlockSpec((8, LANES), lambda i: (0, 0))],
        out_shape=[jax.ShapeDtypeStruct((n, LANES), f32), jax.ShapeDtypeStruct((8, LANES), f32)],
        scratch_shapes=[pltpu.VMEM((8, LANES), f32)],
        compiler_params=_cparams("arbitrary"),
        name="router",
    )(h, w, b)


def _dispatch_kernel(nused_ref, tok_ref, h_hbm, o_ref, sem):
    i = pl.program_id(0)
    rb = o_ref.shape[0]

    @pl.when(i < nused_ref[0])
    def _():
        def issue(r, carry):
            pltpu.make_async_copy(h_hbm.at[pl.ds(tok_ref[0, 0, r], 1), :], o_ref.at[pl.ds(r, 1), :], sem).start()
            return carry

        lax.fori_loop(0, rb, issue, 0)
        pltpu.make_async_copy(h_hbm.at[pl.ds(0, rb), :], o_ref, sem).wait()


def _dispatch(h, buf_tok, n_used, n_blocks, rb):
    d = h.shape[1]

    def blk(i, nu):
        return jnp.minimum(i, nu[0] - 1)

    grid_spec = pltpu.PrefetchScalarGridSpec(
        num_scalar_prefetch=1,
        grid=(n_blocks,),
        in_specs=[pl.BlockSpec((1, 1, rb), lambda i, nu: (blk(i, nu), 0, 0), memory_space=pltpu.SMEM),
                  pl.BlockSpec(memory_space=pl.ANY)],
        out_specs=pl.BlockSpec((rb, d), lambda i, nu: (blk(i, nu), 0)),
        scratch_shapes=[pltpu.SemaphoreType.DMA(())],
    )
    return pl.pallas_call(
        _dispatch_kernel,
        grid_spec=grid_spec,
        out_shape=jax.ShapeDtypeStruct((n_blocks * rb, d), f32),
        compiler_params=_cparams("arbitrary"),
        name="moe_dispatch",
    )(n_used, buf_tok.reshape(n_blocks, 1, rb), h)


def _expert_kernel(nused_ref, be_ref, x_ref, wg_ref, wu_ref, wd_ref, y_ref):
    del be_ref

    @pl.when(pl.program_id(0) < nused_ref[0])
    def _():
        x = x_ref[...].astype(bf16)
        act = (_silu(_dot(x, wg_ref[0])) * _dot(x, wu_ref[0])).astype(bf16)
        y_ref[...] = _dot(act, wd_ref[0])


def _experts(xs, n_used, block_expert, wg, wu, wd, rb):
    r, d = xs.shape
    hid = wg.shape[2]

    def blk(i, nu, be):
        return jnp.minimum(i, nu[0] - 1)

    grid_spec = pltpu.PrefetchScalarGridSpec(
        num_scalar_prefetch=2,
        grid=(r // rb,),
        in_specs=[pl.BlockSpec((rb, d), lambda i, nu, be: (blk(i, nu, be), 0)),
                  pl.BlockSpec((1, d, hid), lambda i, nu, be: (be[i], 0, 0)),
                  pl.BlockSpec((1, d, hid), lambda i, nu, be: (be[i], 0, 0)),
                  pl.BlockSpec((1, hid, d), lambda i, nu, be: (be[i], 0, 0))],
        out_specs=pl.BlockSpec((rb, d), lambda i, nu, be: (blk(i, nu, be), 0)),
    )
    return pl.pallas_call(
        _expert_kernel,
        grid_spec=grid_spec,
        out_shape=jax.ShapeDtypeStruct((r, d), f32),
        compiler_params=_cparams("arbitrary"),
        name="moe_experts",
    )(n_used, block_expert, xs, wg, wu, wd)


def _combine_kernel(dest_ref, y_hbm, x_ref, gate_ref, mod_ref, g_ref, b_ref, o_ref, ybuf, sem, *, alpha):
    tm = x_ref.shape[0]

    def issue(r, carry):
        for k in range(TOP_K):
            pltpu.make_async_copy(y_hbm.at[pl.ds(dest_ref[0, 0, TOP_K * r + k], 1), :],
                                  ybuf.at[k, pl.ds(r, 1), :], sem).start()
        return carry

    lax.fori_loop(0, tm, issue, 0)
    for k in range(TOP_K):
        pltpu.make_async_copy(y_hbm.at[pl.ds(0, tm), :], ybuf.at[k], sem).wait()
    gates = gate_ref[...]
    f = gates[:, 4:5] * ybuf[0] + gates[:, 5:6] * ybuf[1]
    o_ref[...] = _ln(alpha * x_ref[...] + mod_ref[0, 5:6, :] * f) * g_ref[...] + b_ref[...]


def _combine(y, dest, xs, rout, mod, g, b, n_lat, lat_len, n_batch, alpha):
    n, d = xs.shape
    tm = _tile(lat_len, 256)
    while (n - n_lat) % tm:
        tm //= 2
    nlb, bpb = n_lat // tm, lat_len // tm

    def grp(i):
        return jnp.where(i < nlb, i // bpb, n_batch)

    return pl.pallas_call(
        functools.partial(_combine_kernel, alpha=alpha),
        grid=(n // tm,),
        in_specs=[pl.BlockSpec((1, 1, TOP_K * tm), lambda i: (i, 0, 0), memory_space=pltpu.SMEM),
                  pl.BlockSpec(memory_space=pl.ANY),
                  pl.BlockSpec((tm, d), lambda i: (i, 0)),
                  pl.BlockSpec((tm, LANES), lambda i: (i, 0)),
                  pl.BlockSpec((1, 6, d), lambda i: (grp(i), 0, 0)),
                  pl.BlockSpec((1, d), lambda i: (0, 0)), pl.BlockSpec((1, d), lambda i: (0, 0))],
        out_specs=pl.BlockSpec((tm, d), lambda i: (i, 0)),
        out_shape=jax.ShapeDtypeStruct((n, d), f32),
        scratch_shapes=[pltpu.VMEM((TOP_K, tm, d), f32), pltpu.SemaphoreType.DMA(())],
        compiler_params=_cparams("arbitrary"),
        name="moe_combine",
    )(dest.reshape(n // tm, 1, TOP_K * tm), y, xs, rout, mod, g, b)


def _moe(h, xs, rout_w, rout_b, wg, wu, wd, mod, g, b, n_lat, lat_len, n_batch, alpha):
    n = h.shape[0]
    rb = 256 if n % 256 == 0 else 128
    rout, cnt = _router(h, rout_w, rout_b)
    eid = rout[:, 0:2].astype(i32)
    rank = rout[:, 2:4].astype(i32)
    counts = cnt[0, _ROUTE_LANE0:_ROUTE_LANE0 + N_EXPERTS].astype(i32)
    padded = (counts + rb - 1) // rb * rb
    pend = jnp.cumsum(padded)
    pstart = pend - padded
    dest = (pstart[eid] + rank).reshape(-1)
    n_blocks = TOP_K * n // rb + N_EXPERTS
    n_used = (pend[-1:] // rb).astype(i32)
    block_expert = jnp.minimum(
        jnp.searchsorted(pend, jnp.arange(n_blocks, dtype=i32) * rb, side="right"), N_EXPERTS - 1).astype(i32)
    block_expert = jnp.where(jnp.arange(n_blocks) < n_used[0], block_expert, block_expert[jnp.maximum(n_used[0] - 1, 0)])
    buf_tok = jnp.zeros((n_blocks * rb,), i32).at[dest].set(jnp.arange(TOP_K * n, dtype=i32) // TOP_K)
    xs_sorted = _dispatch(h, buf_tok, n_used, n_blocks, rb)
    y = _experts(xs_sorted, n_used, block_expert, wg, wu, wd, rb)
    return _combine(y, dest, xs, rout, mod, g, b, n_lat, lat_len, n_batch, alpha)


def _permute_w_in(w):
    d = w.shape[0]
    o_sq, o_sk, o_sv, o_rq, o_rk, o_rv, o_gf, o_gb, o_mq, o_ckv, o_kr = (
        0, 768, 1024, 1280, 1664, 2048, 2816, 3584, 4352, 5120, 5376)
    mq = w[:, o_mq:o_mq + 768].reshape(d, MLA_HEADS, MLA_NOPE_DIM + MLA_ROPE_DIM)
    parts = [w[:, o_sq:o_sq + 768], w[:, o_rv:o_rv + 768], w[:, o_gf:o_gf + 768], w[:, o_gb:o_gb + 768],
             mq[:, :, :MLA_NOPE_DIM].reshape(d, 512), w[:, o_sk:o_sk + 256], w[:, o_sv:o_sv + 256],
             mq[:, :, MLA_NOPE_DIM:].reshape(d, 256), w[:, o_ckv:o_ckv + 256], w[:, o_rq:o_rq + 384],
             w[:, o_rk:o_rk + 384], w[:, o_kr:o_kr + 64], jnp.zeros((d, _NP - _KR - 64), w.dtype)]
    return jnp.concatenate(parts, axis=1).astype(bf16)


def kernel(x, c, ctx, c_ctx, w_ada, b_ada, w_in, swa_sink, ret_decay, mla_kv_norm, mla_w_uk, mla_w_uv, w_out, ln1_g, ln1_b, ln2_g, ln2_b, moe_w_group, moe_b_group, moe_w_expert, moe_b_expert, moe_w_gate, moe_w_up, moe_w_down):
    n_batch, lat_len, d = x.shape
    ctx_len = ctx.shape[1]
    depth = w_ada.shape[0]
    n_lat, n_ctx = n_batch * lat_len, n_batch * ctx_len
    alpha = (2 * depth) ** 0.25

    cc = jnp.zeros((8, d), f32).at[:n_batch].set(c).at[n_batch].set(c_ctx)
    mod_all = _ada(cc, w_ada, b_ada).reshape(depth, 8, 6, d)
    tab = _rope_tables(lat_len, ctx_len)
    xs = jnp.concatenate([x.reshape(n_lat, d), ctx.reshape(n_ctx, d)], axis=0)

    for l in range(depth):
        ctx_out = l < depth - 1
        mod = mod_all[l]
        p = _inproj(xs, mod, _permute_w_in(w_in[l]), n_lat, lat_len, n_batch)
        p = _rope(p, tab, n_batch, lat_len, ctx_len)
        kx, v = _mla_expand(p, mla_kv_norm[l][None, :], mla_w_uk[l].astype(bf16), mla_w_uv[l].astype(bf16))
        m = _mla_attention(p, kx, v, n_batch, lat_len, ctx_len, ctx_out)
        a = _swa_attention(p, swa_sink[l], n_batch, lat_len, ctx_len, ctx_out)
        lg = jnp.log1p(-jnp.exp2(-ret_decay[l].astype(f32)))
        r = _retention(p, lg, n_batch, lat_len, ctx_len)
        n_rows = n_lat + n_ctx if ctx_out else n_lat
        xs, h = _outproj(a, r, m, xs, mod, w_out[l].astype(bf16), ln1_g[l][None, :], ln1_b[l][None, :],
                         n_rows, n_lat, lat_len, n_batch, alpha)
        rout_w = jnp.zeros((d, LANES), f32).at[:, :N_GROUPS].set(moe_w_group[l]).at[
            :, _ROUTE_LANE0:_ROUTE_LANE0 + N_EXPERTS].set(moe_w_expert[l]).astype(bf16)
        rout_b = jnp.zeros((1, LANES), f32).at[0, :N_GROUPS].set(moe_b_group[l]).at[
            0, _ROUTE_LANE0:_ROUTE_LANE0 + N_EXPERTS].set(moe_b_expert[l])
        xs = _moe(h, xs, rout_w, rout_b, moe_w_gate[l].astype(bf16), moe_w_up[l].astype(bf16),
                  moe_w_down[l].astype(bf16), mod, ln2_g[l][None, :], ln2_b[l][None, :],
                  n_lat, lat_len, n_batch, alpha)
    return xs[:n_lat].reshape(n_batch, lat_len, d)
```

```python
import functools

import jax
import jax.numpy as jnp
from jax import lax
from jax.experimental import pallas as pl
from jax.experimental.pallas import tpu as pltpu

f32 = jnp.float32
bf16 = jnp.bfloat16
i32 = jnp.int32

GRID_W = 64
SWA_HEADS, SWA_KV_HEADS, SWA_HEAD_DIM, SWA_WINDOW = 6, 2, 128, 128
RET_HEADS, RET_QK_DIM, RET_V_DIM, RET_CHUNK = 6, 64, 128, 128
MLA_HEADS, MLA_NOPE_DIM, MLA_ROPE_DIM, MLA_V_DIM, MLA_KV_RANK = 4, 128, 64, 128, 256
N_GROUPS, EXPERTS_PER_GROUP, TOP_K = 4, 8, 2
N_EXPERTS = N_GROUPS * EXPERTS_PER_GROUP
ROPE_BASE = 10000.0
NORM_EPS = 1e-6
NEG_INF = -1e30
LANES = 128

_SQ, _RV, _GF, _GB, _MQN, _SK, _SV, _MQR, _CKV, _RQ, _RK, _KR = (
    0, 768, 1536, 2304, 3072, 3584, 3840, 4096, 4352, 4608, 4992, 5376)
_NP = 5504
_LOG2E = 1.4426950408889634
_SWA_SCALE = SWA_HEAD_DIM ** -0.5
_RET_SCALE = RET_QK_DIM ** -0.5
_MLA_SCALE = (MLA_NOPE_DIM + MLA_ROPE_DIM) ** -0.5 * _LOG2E
_SEGMENTS = ((_SQ, 768, 0, _SWA_SCALE), (_RV, 768, None, 1.0), (_GF, 768, None, 1.0), (_GB, 768, None, 1.0),
             (_MQN, 512, None, _MLA_SCALE), (_SK, 256, 0, 1.0), (_SV, 256, None, 1.0), (_MQR, 256, 2, _MLA_SCALE),
             (_CKV, 256, None, 1.0), (_RQ, 384, 1, _RET_SCALE), (_RK, 384, 1, 1.0), (_KR, 128, 2, 1.0))

_VMEM_LIMIT = 48 * 1024 * 1024


def _cparams(*sem):
    return pltpu.CompilerParams(dimension_semantics=sem, vmem_limit_bytes=_VMEM_LIMIT)


def _tile(n, pref):
    t = min(n, pref)
    while n % t:
        t //= 2
    return t


def _ln(x):
    mu = jnp.mean(x, axis=-1, keepdims=True)
    xc = x - mu
    var = jnp.mean(xc * xc, axis=-1, keepdims=True)
    return xc * lax.rsqrt(var + NORM_EPS)


def _silu(x):
    return x / (1.0 + jnp.exp(-x))


def _dot(a, b):
    return jnp.dot(a, b, preferred_element_type=f32)


def _dot_nt(a, b):
    return lax.dot_general(a, b, (((1,), (1,)), ((), ())), preferred_element_type=f32)


def _dot_tn(a, b):
    return lax.dot_general(a, b, (((0,), (0,)), ((), ())), preferred_element_type=f32)


def _ada_kernel(c_ref, w_ref, b_ref, o_ref):
    s = _silu(c_ref[...]).astype(bf16)
    o_ref[0] = _dot(s, w_ref[0].astype(bf16)) + b_ref[0]


def _ada(cc, w_ada, b_ada):
    depth, d, n = w_ada.shape
    tn = _tile(n, 1024)
    return pl.pallas_call(
        _ada_kernel,
        grid=(depth, n // tn),
        in_specs=[pl.BlockSpec((8, d), lambda l, j: (0, 0)),
                  pl.BlockSpec((1, d, tn), lambda l, j: (l, 0, j)),
                  pl.BlockSpec((1, 1, tn), lambda l, j: (l, 0, j))],
        out_specs=pl.BlockSpec((1, 8, tn), lambda l, j: (l, 0, j)),
        out_shape=jax.ShapeDtypeStruct((depth, 8, n), f32),
        compiler_params=_cparams("parallel", "parallel"),
        name="ada",
    )(cc, w_ada, b_ada.reshape(depth, 1, n))


_INPROJ_CHUNK = 512


def _inproj_kernel(x_ref, mod_ref, tab_ref, w_ref, o_ref):
    tm = x_ref.shape[0]
    h = (_ln(x_ref[...]) * (1.0 + mod_ref[0, 1:2, :]) + mod_ref[0, 0:1, :]).astype(bf16)
    lane = lax.broadcasted_iota(i32, (tm, LANES), 1)
    first = {32: (lane & 63) < 32, 16: (lane & 31) < 16}
    slab_kind = {}
    for off, width, typ, scale in _SEGMENTS:
        for k in range(off // LANES, (off + width) // LANES):
            slab_kind[k] = (typ, scale)
    for c0 in range(0, _NP, _INPROJ_CHUNK):
        c1 = min(c0 + _INPROJ_CHUNK, _NP)
        acc = _dot(h, w_ref[:, c0:c1])
        for k in range(c0 // LANES, c1 // LANES):
            typ, scale = slab_kind[k]
            xk = acc[:, k * LANES - c0:(k + 1) * LANES - c0]
            if scale != 1.0:
                xk = xk * scale
            if typ is not None:
                half = 16 if typ == 2 else 32
                partner = jnp.where(first[half], pltpu.roll(xk, LANES - half, 1), pltpu.roll(xk, half, 1))
                xk = xk * tab_ref[typ, 0] + partner * tab_ref[typ, 1]
            o_ref[:, k * LANES:(k + 1) * LANES] = xk.astype(bf16)


def _inproj_rows(lat_len, n_ctx):
    tm = _tile(lat_len, 256)
    while n_ctx % tm:
        tm //= 2
    return tm


def _inproj(xs, mod, tab, w, n_lat, lat_len, ctx_len, n_batch):
    t, d = xs.shape
    tm = _inproj_rows(lat_len, t - n_lat)
    nlb, bpb = n_lat // tm, lat_len // tm
    ctx_blocks = max(1, ctx_len // tm)

    def grp(i):
        return jnp.where(i < nlb, i // bpb, n_batch)

    def posblk(i):
        return jnp.where(i < nlb, i % bpb, bpb + (i - nlb) % ctx_blocks)

    return pl.pallas_call(
        _inproj_kernel,
        grid=(t // tm,),
        in_specs=[pl.BlockSpec((tm, d), lambda i: (i, 0)),
                  pl.BlockSpec((1, 6, d), lambda i: (grp(i), 0, 0)),
                  pl.BlockSpec((3, 2, tm, LANES), lambda i: (0, 0, posblk(i), 0)),
                  pl.BlockSpec((d, _NP), lambda i: (0, 0), pipeline_mode=pl.Buffered(1))],
        out_specs=pl.BlockSpec((tm, _NP), lambda i: (i, 0)),
        out_shape=jax.ShapeDtypeStruct((t, _NP), bf16),
        compiler_params=_cparams("parallel"),
        name="inproj",
    )(xs, mod, tab, w)


def _rope_tables(lat_len, ctx_len, tm):
    lane = jnp.arange(LANES)
    t = jnp.arange(lat_len)
    rows = (t // GRID_W).astype(f32)
    cols = (t % GRID_W).astype(f32)

    def freq(half):
        return ROPE_BASE ** (-jnp.arange(half, dtype=f32) / half)

    def table(pos, inv, sign):
        ang = pos * inv[None, :]
        return jnp.stack([jnp.cos(ang), jnp.sin(ang) * sign[None, :]])

    def ident(n):
        return jnp.stack([jnp.ones((n, LANES), f32), jnp.zeros((n, LANES), f32)])

    sign32 = jnp.where((lane % 64) < 32, -1.0, 1.0).astype(f32)
    sign16 = jnp.where((lane % 32) < 16, -1.0, 1.0).astype(f32)
    inv32 = freq(32)[lane % 32]
    inv16 = freq(16)[lane % 16]
    pos_swa = jnp.where((lane // 64)[None, :] == 0, rows[:, None], cols[:, None])
    pos_mla = jnp.where(((lane % 64) // 32)[None, :] == 0, rows[:, None], cols[:, None])
    rep = max(1, tm // ctx_len)
    nc = rep * ctx_len
    pos_ret_l = jnp.broadcast_to((ctx_len + t).astype(f32)[:, None], (lat_len, LANES))
    pos_ret_c = jnp.broadcast_to(jnp.tile(jnp.arange(ctx_len), rep).astype(f32)[:, None], (nc, LANES))
    t0 = jnp.concatenate([table(pos_swa, inv32, sign32), ident(nc)], axis=1)
    t1 = jnp.concatenate([table(pos_ret_l, inv32, sign32), table(pos_ret_c, inv32, sign32)], axis=1)
    t2 = jnp.concatenate([table(pos_mla, inv16, sign16), ident(nc)], axis=1)
    return jnp.stack([t0, t1, t2])


def _mla_expand_kernel(ckv_ref, kr_ref, g_ref, wuk_ref, wuv_ref, k_ref, v_ref):
    c = ckv_ref[...].astype(f32)
    cn = (c * lax.rsqrt(jnp.mean(c * c, axis=-1, keepdims=True) + NORM_EPS) * g_ref[...]).astype(bf16)
    kn = _dot(cn, wuk_ref[...]).astype(bf16)
    vv = _dot(cn, wuv_ref[...]).astype(bf16)
    kr = kr_ref[...]
    for h in range(MLA_HEADS):
        k_ref[h, :, 0:128] = kn[:, h * 128:(h + 1) * 128]
        k_ref[h, :, 128:256] = kr
        v_ref[h] = vv[:, h * 128:(h + 1) * 128]


def _mla_expand(p, g, wuk, wuv):
    t = p.shape[0]
    tm = _tile(t, 512)
    return pl.pallas_call(
        _mla_expand_kernel,
        grid=(t // tm,),
        in_specs=[pl.BlockSpec((tm, 256), lambda i: (i, _CKV // 256)),
                  pl.BlockSpec((tm, 128), lambda i: (i, _KR // 128)),
                  pl.BlockSpec((1, 256), lambda i: (0, 0)),
                  pl.BlockSpec((256, 512), lambda i: (0, 0)),
                  pl.BlockSpec((256, 512), lambda i: (0, 0))],
        out_specs=[pl.BlockSpec((MLA_HEADS, tm, 256), lambda i: (0, i, 0)),
                   pl.BlockSpec((MLA_HEADS, tm, 128), lambda i: (0, i, 0))],
        out_shape=[jax.ShapeDtypeStruct((MLA_HEADS, t, 256), bf16), jax.ShapeDtypeStruct((MLA_HEADS, t, 128), bf16)],
        compiler_params=_cparams("parallel"),
        name="mla_expand",
    )(p, p, g, wuk, wuv)


def _mla_kernel(*refs, with_lat):
    if with_lat:
        qn_ref, qr_ref, kc_ref, vc_ref, kl_ref, vl_ref, o_ref, qx_scr = refs
    else:
        qn_ref, qr_ref, kc_ref, vc_ref, _, o_ref, qx_scr = refs
    lane = lax.broadcasted_iota(i32, (qn_ref.shape[0], LANES), 1)
    for h in range(MLA_HEADS):
        slab = qr_ref[:, (h // 2) * 128:(h // 2 + 1) * 128].astype(f32)
        if h % 2:
            slab = pltpu.roll(slab, 64, 1)
        qx_scr[h, :, 0:128] = qn_ref[:, h * 128:(h + 1) * 128]
        qx_scr[h, :, 128:256] = jnp.where(lane < 64, slab, 0.0).astype(bf16)
    for h in range(MLA_HEADS):
        q = qx_scr[h]
        parts = [(_dot_nt(q, kc_ref[h]), vc_ref[h])]
        if with_lat:
            parts.append((_dot_nt(q, kl_ref[h]), vl_ref[h]))
        m = None
        for s, _ in parts:
            mx = jnp.max(s, axis=1, keepdims=True)
            m = mx if m is None else jnp.maximum(m, mx)
        den, acc = None, None
        for s, vv in parts:
            pr = jnp.exp2(s - m)
            sm = jnp.sum(pr, axis=1, keepdims=True)
            pv = _dot(pr.astype(bf16), vv)
            den = sm if den is None else den + sm
            acc = pv if acc is None else acc + pv
        o_ref[:, h * 128:(h + 1) * 128] = (acc / den).astype(bf16)


def _mla_attention(p, kx, v, n_batch, lat_len, ctx_len, compute_ctx):
    t = p.shape[0]
    tq = _tile(lat_len, 256)
    nlq = lat_len // tq
    cbase = n_batch * lat_len // ctx_len
    once = pl.Buffered(1)
    out = pl.pallas_call(
        functools.partial(_mla_kernel, with_lat=True),
        grid=(n_batch, nlq),
        in_specs=[pl.BlockSpec((tq, 512), lambda b, qi: (b * nlq + qi, _MQN // 512)),
                  pl.BlockSpec((tq, 256), lambda b, qi: (b * nlq + qi, _MQR // 256)),
                  pl.BlockSpec((MLA_HEADS, ctx_len, 256), lambda b, qi: (0, cbase + b, 0)),
                  pl.BlockSpec((MLA_HEADS, ctx_len, 128), lambda b, qi: (0, cbase + b, 0)),
                  pl.BlockSpec((MLA_HEADS, lat_len, 256), lambda b, qi: (0, b, 0), pipeline_mode=once),
                  pl.BlockSpec((MLA_HEADS, lat_len, 128), lambda b, qi: (0, b, 0), pipeline_mode=once)],
        out_specs=pl.BlockSpec((tq, 512), lambda b, qi: (b * nlq + qi, 0)),
        out_shape=jax.ShapeDtypeStruct((t, 512), bf16),
        scratch_shapes=[pltpu.VMEM((MLA_HEADS, tq, 256), bf16)],
        compiler_params=_cparams("parallel", "arbitrary"),
        name="mla_attn",
    )(p, p, kx, v, kx, v)
    if compute_ctx:
        out = pl.pallas_call(
            functools.partial(_mla_kernel, with_lat=False),
            grid=(n_batch,),
            in_specs=[pl.BlockSpec((ctx_len, 512), lambda b: (cbase + b, _MQN // 512)),
                      pl.BlockSpec((ctx_len, 256), lambda b: (cbase + b, _MQR // 256)),
                      pl.BlockSpec((MLA_HEADS, ctx_len, 256), lambda b: (0, cbase + b, 0)),
                      pl.BlockSpec((MLA_HEADS, ctx_len, 128), lambda b: (0, cbase + b, 0)),
                      pl.BlockSpec(memory_space=pl.ANY)],
            out_specs=pl.BlockSpec((ctx_len, 512), lambda b: (cbase + b, 0)),
            out_shape=jax.ShapeDtypeStruct((t, 512), bf16),
            scratch_shapes=[pltpu.VMEM((MLA_HEADS, ctx_len, 256), bf16)],
            input_output_aliases={4: 0},
            compiler_params=_cparams("parallel"),
            name="mla_ctx",
        )(p, p, kx, v, out)
    return out


def _swa_softmax_out(parts, sink_col, o_ref, g):
    m = sink_col
    for s, _ in parts:
        m = jnp.maximum(m, jnp.max(s, axis=1, keepdims=True))
    den = jnp.exp(sink_col - m)
    acc = None
    for s, vv in parts:
        pr = jnp.exp(s - m)
        den = den + jnp.sum(pr, axis=1, keepdims=True)
        pv = _dot(pr.astype(bf16), vv)
        acc = pv if acc is None else acc + pv
    o = acc / den
    nq = o.shape[0] // 3
    for j in range(3):
        o_ref[:, (3 * g + j) * 128:(3 * g + j + 1) * 128] = o[j * nq:(j + 1) * nq].astype(bf16)


def _sink_col(sink_ref, g, nq):
    row = lax.broadcasted_iota(i32, (3 * nq, 1), 0)
    return jnp.where(row < nq, sink_ref[3 * g], jnp.where(row < 2 * nq, sink_ref[3 * g + 1], sink_ref[3 * g + 2]))


def _swa_kernel(sink_ref, q_ref, kp_ref, ko_ref, kn_ref, vp_ref, vo_ref, vn_ref, kc_ref, vc_ref, o_ref):
    i = pl.program_id(1)
    nb = pl.num_programs(1)
    blk = q_ref.shape[0]
    q = q_ref[...]
    r = lax.broadcasted_iota(i32, (3 * blk, 3 * blk), 0) & (blk - 1)
    c = lax.broadcasted_iota(i32, (3 * blk, 3 * blk), 1)
    lo = jnp.where(i == 0, blk, 0)
    hi = jnp.where(i == nb - 1, 2 * blk, 3 * blk)
    d = c - r
    valid = (d >= 0) & (d <= 2 * SWA_WINDOW) & (c >= lo) & (c < hi)
    for g in range(SWA_KV_HEADS):
        hs = slice(g * 128, (g + 1) * 128)
        qg = jnp.concatenate([q[:, (3 * g + j) * 128:(3 * g + j + 1) * 128] for j in range(3)], axis=0)
        kloc = jnp.concatenate([kp_ref[:, hs], ko_ref[:, hs], kn_ref[:, hs]], axis=0)
        vloc = jnp.concatenate([vp_ref[:, hs], vo_ref[:, hs], vn_ref[:, hs]], axis=0)
        s_loc = jnp.where(valid, _dot_nt(qg, kloc), NEG_INF)
        s_ctx = _dot_nt(qg, kc_ref[:, hs])
        _swa_softmax_out([(s_ctx, vc_ref[:, hs]), (s_loc, vloc)], _sink_col(sink_ref, g, blk), o_ref, g)


def _swa_ctx_kernel(sink_ref, q_ref, kc_ref, vc_ref, prev_ref, o_ref):
    del prev_ref
    q = q_ref[...]
    nq = q.shape[0]
    for g in range(SWA_KV_HEADS):
        hs = slice(g * 128, (g + 1) * 128)
        qg = jnp.concatenate([q[:, (3 * g + j) * 128:(3 * g + j + 1) * 128] for j in range(3)], axis=0)
        s_ctx = _dot_nt(qg, kc_ref[:, hs])
        _swa_softmax_out([(s_ctx, vc_ref[:, hs])], _sink_col(sink_ref, g, nq), o_ref, g)


def _swa_attention(p, sink, n_batch, lat_len, ctx_len, compute_ctx):
    t = p.shape[0]
    blk = SWA_WINDOW
    nb = lat_len // blk
    cbase = n_batch * lat_len // ctx_len
    smem = pl.BlockSpec(memory_space=pltpu.SMEM)

    def kspec(col, off):
        return pl.BlockSpec((blk, 256), lambda b, i: (b * nb + jnp.clip(i + off, 0, nb - 1), col // 256))

    out = pl.pallas_call(
        _swa_kernel,
        grid=(n_batch, nb),
        in_specs=[smem,
                  pl.BlockSpec((blk, 768), lambda b, i: (b * nb + i, _SQ // 768)),
                  kspec(_SK, -1), kspec(_SK, 0), kspec(_SK, 1),
                  kspec(_SV, -1), kspec(_SV, 0), kspec(_SV, 1),
                  pl.BlockSpec((ctx_len, 256), lambda b, i: (cbase + b, _SK // 256)),
                  pl.BlockSpec((ctx_len, 256), lambda b, i: (cbase + b, _SV // 256))],
        out_specs=pl.BlockSpec((blk, 768), lambda b, i: (b * nb + i, 0)),
        out_shape=jax.ShapeDtypeStruct((t, 768), bf16),
        compiler_params=_cparams("parallel", "parallel"),
        name="swa_attn",
    )(sink, p, p, p, p, p, p, p, p, p)
    if compute_ctx:
        out = pl.pallas_call(
            _swa_ctx_kernel,
            grid=(n_batch,),
            in_specs=[smem,
                      pl.BlockSpec((ctx_len, 768), lambda b: (cbase + b, _SQ // 768)),
                      pl.BlockSpec((ctx_len, 256), lambda b: (cbase + b, _SK // 256)),
                      pl.BlockSpec((ctx_len, 256), lambda b: (cbase + b, _SV // 256)),
                      pl.BlockSpec(memory_space=pl.ANY)],
            out_specs=pl.BlockSpec((ctx_len, 768), lambda b: (cbase + b, 0)),
            out_shape=jax.ShapeDtypeStruct((t, 768), bf16),
            input_output_aliases={4: 0},
            compiler_params=_cparams("parallel"),
            name="swa_ctx",
        )(sink, p, p, p, out)
    return out


def _ret_kernel(lg_ref, q_ref, k_ref, v_ref, *rest, backward):
    if backward:
        yf_ref, gf_ref, gb_ref, o_ref, s_scr, d_scr, qd_scr, kd_scr, gc_scr = rest
    else:
        o_ref, s_scr, d_scr, qd_scr, kd_scr, gc_scr = rest
    i = pl.program_id(1)
    cc = RET_CHUNK
    direction = 1 if backward else 0

    @pl.when(i == 0)
    def _():
        s_scr[...] = jnp.zeros(s_scr.shape, f32)
        r = lax.broadcasted_iota(i32, (cc, cc), 0).astype(f32)
        c = lax.broadcasted_iota(i32, (cc, cc), 1).astype(f32)
        diff = (c - r) if backward else (r - c)
        for h in range(RET_HEADS):
            lgh = lg_ref[direction, h]
            d_scr[h] = jnp.where(diff >= 0, jnp.exp(lgh * jnp.maximum(diff, 0.0)), 0.0)
            qd_scr[h] = jnp.exp(lgh * ((cc - r) if backward else (r + 1.0)))
            kd_scr[h] = jnp.exp(lgh * (r if backward else (cc - 1.0 - r)))
            gc_scr[h] = jnp.exp(lgh * cc + jnp.zeros((cc, cc), f32))

    lane = lax.broadcasted_iota(i32, (cc, LANES), 1)
    for pair in range(RET_HEADS // 2):
        q2 = q_ref[:, pair * 128:(pair + 1) * 128].astype(f32)
        k2b = k_ref[:, pair * 128:(pair + 1) * 128]
        k2 = k2b.astype(f32)
        for sub in range(2):
            h = 2 * pair + sub
            hs = slice(h * 128, (h + 1) * 128)
            qa = jnp.where((lane < 64) if sub == 0 else (lane >= 64), q2, 0.0)
            s = _dot_nt(qa.astype(bf16), k2b)
            vh = v_ref[:, hs]
            y = _dot((s * d_scr[h]).astype(bf16), vh) + _dot((qa * qd_scr[h]).astype(bf16), s_scr[h].astype(bf16))
            s_scr[h] = gc_scr[h] * s_scr[h] + _dot_tn((k2 * kd_scr[h]).astype(bf16), vh)
            yn = _ln(y)
            if backward:
                o_ref[:, hs] = (_silu(gf_ref[:, hs].astype(f32)) * yf_ref[:, hs].astype(f32)
                                + _silu(gb_ref[:, hs].astype(f32)) * yn).astype(bf16)
            else:
                o_ref[:, hs] = yn.astype(bf16)


def _retention(p, lg, n_batch, lat_len, ctx_len):
    t = p.shape[0]
    cc = RET_CHUNK
    ncc, nlc = ctx_len // cc, lat_len // cc
    cbase = n_batch * nlc
    smem = pl.BlockSpec(memory_space=pltpu.SMEM)
    scratch = [pltpu.VMEM((RET_HEADS, cc, cc), f32) for _ in range(5)]

    def fwd_row(b, i):
        return jnp.where(i < ncc, cbase + b * ncc + i, b * nlc + (i - ncc))

    def bwd_row(b, i):
        return jnp.where(i < ncc, cbase + b * ncc + (ncc - 1 - i), b * nlc + (nlc - 1 - (i - ncc)))

    def specs(row):
        return [pl.BlockSpec((cc, 384), lambda b, i: (row(b, i), _RQ // 384)),
                pl.BlockSpec((cc, 384), lambda b, i: (row(b, i), _RK // 384)),
                pl.BlockSpec((cc, 768), lambda b, i: (row(b, i), _RV // 768))]

    yf = pl.pallas_call(
        functools.partial(_ret_kernel, backward=False),
        grid=(n_batch, ncc + nlc),
        in_specs=[smem] + specs(fwd_row),
        out_specs=pl.BlockSpec((cc, 768), lambda b, i: (fwd_row(b, i), 0)),
        out_shape=jax.ShapeDtypeStruct((t, 768), bf16),
        scratch_shapes=scratch,
        compiler_params=_cparams("parallel", "arbitrary"),
        name="ret_fwd",
    )(lg, p, p, p)
    return pl.pallas_call(
        functools.partial(_ret_kernel, backward=True),
        grid=(n_batch, ncc + nlc),
        in_specs=[smem] + specs(bwd_row) + [
            pl.BlockSpec((cc, 768), lambda b, i: (bwd_row(b, i), 0)),
            pl.BlockSpec((cc, 768), lambda b, i: (bwd_row(b, i), _GF // 768)),
            pl.BlockSpec((cc, 768), lambda b, i: (bwd_row(b, i), _GB // 768))],
        out_specs=pl.BlockSpec((cc, 768), lambda b, i: (bwd_row(b, i), 0)),
        out_shape=jax.ShapeDtypeStruct((t, 768), bf16),
        scratch_shapes=scratch,
        compiler_params=_cparams("parallel", "arbitrary"),
        name="ret_bwd",
    )(lg, p, p, p, yf, p, p)


def _outproj_kernel(a_ref, r_ref, m_ref, x_ref, mod_ref, w_ref, g_ref, b_ref, xo_ref, h_ref, *, alpha):
    y = (_dot(a_ref[...], w_ref[0:768, :]) + _dot(r_ref[...], w_ref[768:1536, :])
         + _dot(m_ref[...], w_ref[1536:2048, :]))
    xn = _ln(alpha * x_ref[...] + mod_ref[0, 2:3, :] * y) * g_ref[...] + b_ref[...]
    xo_ref[...] = xn
    h_ref[...] = _ln(xn) * (1.0 + mod_ref[0, 4:5, :]) + mod_ref[0, 3:4, :]


def _outproj(a, r, m, xs, mod, w, g, b, n_rows, n_lat, lat_len, n_batch, alpha):
    d = xs.shape[1]
    tm = _tile(lat_len, 256)
    while (n_rows - n_lat) % tm:
        tm //= 2
    nlb, bpb = n_lat // tm, lat_len // tm

    def grp(i):
        return jnp.where(i < nlb, i // bpb, n_batch)

    row = lambda i: (i, 0)
    return pl.pallas_call(
        functools.partial(_outproj_kernel, alpha=alpha),
        grid=(n_rows // tm,),
        in_specs=[pl.BlockSpec((tm, 768), row), pl.BlockSpec((tm, 768), row), pl.BlockSpec((tm, 512), row),
                  pl.BlockSpec((tm, d), row),
                  pl.BlockSpec((1, 6, d), lambda i: (grp(i), 0, 0)),
                  pl.BlockSpec(w.shape, lambda i: (0, 0)),
                  pl.BlockSpec((1, d), lambda i: (0, 0)), pl.BlockSpec((1, d), lambda i: (0, 0))],
        out_specs=[pl.BlockSpec((tm, d), row), pl.BlockSpec((tm, d), row)],
        out_shape=[jax.ShapeDtypeStruct((n_rows, d), f32), jax.ShapeDtypeStruct((n_rows, d), f32)],
        compiler_params=_cparams("parallel"),
        name="outproj",
    )(a, r, m, xs, mod, w, g, b)


_ROUTE_LANE0 = N_GROUPS


def _router_kernel(h_ref, w_ref, b_ref, o_ref, cnt_ref, carry_scr):
    i = pl.program_id(0)

    @pl.when(i == 0)
    def _():
        carry_scr[...] = jnp.zeros(carry_scr.shape, f32)

    tm = h_ref.shape[0]
    logits = _dot(h_ref[...].astype(bf16), w_ref[...]) + b_ref[...]
    lane = lax.broadcasted_iota(i32, (tm, LANES), 1)
    lane_f = lane.astype(f32)
    big = float(2 * LANES)
    gl = jnp.where(lane < N_GROUPS, logits, -jnp.inf)
    gmax = jnp.max(gl, axis=1, keepdims=True)
    gidx = jnp.min(jnp.where(gl == gmax, lane_f, big), axis=1, keepdims=True)
    p_group = 1.0 / jnp.sum(jnp.exp(gl - gmax), axis=1, keepdims=True)
    egroup = ((lane - _ROUTE_LANE0) >> 3).astype(f32)
    in_group = (lane >= _ROUTE_LANE0) & (lane < _ROUTE_LANE0 + N_EXPERTS) & (egroup == gidx)
    ev = jnp.where(in_group, logits, -jnp.inf)
    e1 = jnp.max(ev, axis=1, keepdims=True)
    i1 = jnp.min(jnp.where(ev == e1, lane_f, big), axis=1, keepdims=True)
    ev2 = jnp.where(lane_f == i1, -jnp.inf, ev)
    e2 = jnp.max(ev2, axis=1, keepdims=True)
    i2 = jnp.min(jnp.where(ev2 == e2, lane_f, big), axis=1, keepdims=True)
    tt = jnp.exp(e2 - e1)
    w1 = p_group / (1.0 + tt)
    w2 = p_group * tt / (1.0 + tt)

    hit1 = lane_f == i1
    hit2 = lane_f == i2
    onehot = jnp.where(hit1, 1.0, jnp.where(hit2, 1.0, 0.0))
    rr = lax.broadcasted_iota(i32, (tm, tm), 0)
    cc = lax.broadcasted_iota(i32, (tm, tm), 1)
    lower = jnp.where(cc < rr, 1.0, 0.0).astype(bf16)
    before = _dot(lower, onehot.astype(bf16)) + carry_scr[0:1, :]
    rank1 = jnp.sum(jnp.where(hit1, before, 0.0), axis=1, keepdims=True)
    rank2 = jnp.sum(jnp.where(hit2, before, 0.0), axis=1, keepdims=True)
    carry_scr[0:1, :] = carry_scr[0:1, :] + jnp.sum(onehot, axis=0, keepdims=True)

    o_ref[...] = jnp.where(lane == 0, i1 - _ROUTE_LANE0, jnp.where(lane == 1, i2 - _ROUTE_LANE0, jnp.where(
        lane == 2, rank1, jnp.where(lane == 3, rank2, jnp.where(lane == 4, w1, jnp.where(lane == 5, w2, 0.0))))))
    cnt_ref[...] = carry_scr[...]


def _router(h, w, b):
    n, d = h.shape
    tm = _tile(n, 256)
    return pl.pallas_call(
        _router_kernel,
        grid=(n // tm,),
        in_specs=[pl.BlockSpec((tm, d), lambda i: (i, 0)),
                  pl.BlockSpec((d, LANES), lambda i: (0, 0)),
                  pl.BlockSpec((1, LANES), lambda i: (0, 0))],
        out_specs=[pl.BlockSpec((tm, LANES), lambda i: (i, 0)), pl.BlockSpec((8, LANES), lambda i: (0, 0))],
        out_shape=[jax.ShapeDtypeStruct((n, LANES), f32), jax.ShapeDtypeStruct((8, LANES), f32)],
        scratch_shapes=[pltpu.VMEM((8, LANES), f32)],
        compiler_params=_cparams("arbitrary"),
        name="router",
    )(h, w, b)


def _dispatch_kernel(nused_ref, tok_ref, h_hbm, o_ref, sem):
    i = pl.program_id(0)
    rb = o_ref.shape[0]

    @pl.when(i < nused_ref[0])
    def _():
        def issue(r, carry):
            pltpu.make_async_copy(h_hbm.at[pl.ds(tok_ref[0, 0, r], 1), :], o_ref.at[pl.ds(r, 1), :], sem).start()
            return carry

        lax.fori_loop(0, rb, issue, 0)
        pltpu.make_async_copy(h_hbm.at[pl.ds(0, rb), :], o_ref, sem).wait()


def _dispatch(h, buf_tok, n_used, n_blocks, rb):
    d = h.shape[1]

    def blk(i, nu):
        return jnp.minimum(i, nu[0] - 1)

    grid_spec = pltpu.PrefetchScalarGridSpec(
        num_scalar_prefetch=1,
        grid=(n_blocks,),
        in_specs=[pl.BlockSpec((1, 1, rb), lambda i, nu: (blk(i, nu), 0, 0), memory_space=pltpu.SMEM),
                  pl.BlockSpec(memory_space=pl.ANY)],
        out_specs=pl.BlockSpec((rb, d), lambda i, nu: (blk(i, nu), 0)),
        scratch_shapes=[pltpu.SemaphoreType.DMA(())],
    )
    return pl.pallas_call(
        _dispatch_kernel,
        grid_spec=grid_spec,
        out_shape=jax.ShapeDtypeStruct((n_blocks * rb, d), f32),
        compiler_params=_cparams("arbitrary"),
        name="moe_dispatch",
    )(n_used, buf_tok.reshape(n_blocks, 1, rb), h)


def _expert_kernel(nused_ref, be_ref, x_ref, wg_ref, wu_ref, wd_ref, y_ref):
    del be_ref

    @pl.when(pl.program_id(0) < nused_ref[0])
    def _():
        x = x_ref[...].astype(bf16)
        act = (_silu(_dot(x, wg_ref[0])) * _dot(x, wu_ref[0])).astype(bf16)
        y_ref[...] = _dot(act, wd_ref[0])


def _experts(xs, n_used, block_expert, wg, wu, wd, rb):
    r, d = xs.shape
    hid = wg.shape[2]

    def blk(i, nu, be):
        return jnp.minimum(i, nu[0] - 1)

    grid_spec = pltpu.PrefetchScalarGridSpec(
        num_scalar_prefetch=2,
        grid=(r // rb,),
        in_specs=[pl.BlockSpec((rb, d), lambda i, nu, be: (blk(i, nu, be), 0)),
                  pl.BlockSpec((1, d, hid), lambda i, nu, be: (be[i], 0, 0)),
                  pl.BlockSpec((1, d, hid), lambda i, nu, be: (be[i], 0, 0)),
                  pl.BlockSpec((1, hid, d), lambda i, nu, be: (be[i], 0, 0))],
        out_specs=pl.BlockSpec((rb, d), lambda i, nu, be: (blk(i, nu, be), 0)),
    )
    return pl.pallas_call(
        _expert_kernel,
        grid_spec=grid_spec,
        out_shape=jax.ShapeDtypeStruct((r, d), f32),
        compiler_params=_cparams("arbitrary"),
        name="moe_experts",
    )(n_used, block_expert, xs, wg, wu, wd)


def _combine_kernel(dest_ref, y_hbm, x_ref, gate_ref, mod_ref, g_ref, b_ref, o_ref, ybuf, sem, *, alpha):
    tm = x_ref.shape[0]

    def issue(r, carry):
        for k in range(TOP_K):
            pltpu.make_async_copy(y_hbm.at[pl.ds(dest_ref[0, 0, TOP_K * r + k], 1), :],
                                  ybuf.at[k, pl.ds(r, 1), :], sem).start()
        return carry

    lax.fori_loop(0, tm, issue, 0)
    for k in range(TOP_K):
        pltpu.make_async_copy(y_hbm.at[pl.ds(0, tm), :], ybuf.at[k], sem).wait()
    gates = gate_ref[...]
    f = gates[:, 4:5] * ybuf[0] + gates[:, 5:6] * ybuf[1]
    o_ref[...] = _ln(alpha * x_ref[...] + mod_ref[0, 5:6, :] * f) * g_ref[...] + b_ref[...]


def _combine(y, dest, xs, rout, mod, g, b, n_lat, lat_len, n_batch, alpha):
    n, d = xs.shape
    tm = _tile(lat_len, 256)
    while (n - n_lat) % tm:
        tm //= 2
    nlb, bpb = n_lat // tm, lat_len // tm

    def grp(i):
        return jnp.where(i < nlb, i // bpb, n_batch)

    return pl.pallas_call(
        functools.partial(_combine_kernel, alpha=alpha),
        grid=(n // tm,),
        in_specs=[pl.BlockSpec((1, 1, TOP_K * tm), lambda i: (i, 0, 0), memory_space=pltpu.SMEM),
                  pl.BlockSpec(memory_space=pl.ANY),
                  pl.BlockSpec((tm, d), lambda i: (i, 0)),
                  pl.BlockSpec((tm, LANES), lambda i: (i, 0)),
                  pl.BlockSpec((1, 6, d), lambda i: (grp(i), 0, 0)),
                  pl.BlockSpec((1, d), lambda i: (0, 0)), pl.BlockSpec((1, d), lambda i: (0, 0))],
        out_specs=pl.BlockSpec((tm, d), lambda i: (i, 0)),
        out_shape=jax.ShapeDtypeStruct((n, d), f32),
        scratch_shapes=[pltpu.VMEM((TOP_K, tm, d), f32), pltpu.SemaphoreType.DMA(())],
        compiler_params=_cparams("arbitrary"),
        name="moe_combine",
    )(dest.reshape(n // tm, 1, TOP_K * tm), y, xs, rout, mod, g, b)


def _moe(h, xs, rout_w, rout_b, wg, wu, wd, mod, g, b, n_lat, lat_len, n_batch, alpha):
    n = h.shape[0]
    rb = 256 if n % 256 == 0 else 128
    rout, cnt = _router(h, rout_w, rout_b)
    eid = rout[:, 0:2].astype(i32)
    rank = rout[:, 2:4].astype(i32)
    counts = cnt[0, _ROUTE_LANE0:_ROUTE_LANE0 + N_EXPERTS].astype(i32)
    padded = (counts + rb - 1) // rb * rb
    pend = jnp.cumsum(padded)
    pstart = pend - padded
    dest = (pstart[eid] + rank).reshape(-1)
    n_blocks = TOP_K * n // rb + N_EXPERTS
    n_used = (pend[-1:] // rb).astype(i32)
    block_expert = jnp.minimum(
        jnp.searchsorted(pend, jnp.arange(n_blocks, dtype=i32) * rb, side="right"), N_EXPERTS - 1).astype(i32)
    block_expert = jnp.where(jnp.arange(n_blocks) < n_used[0], block_expert, block_expert[jnp.maximum(n_used[0] - 1, 0)])
    buf_tok = jnp.zeros((n_blocks * rb,), i32).at[dest].set(jnp.arange(TOP_K * n, dtype=i32) // TOP_K)
    xs_sorted = _dispatch(h, buf_tok, n_used, n_blocks, rb)
    y = _experts(xs_sorted, n_used, block_expert, wg, wu, wd, rb)
    return _combine(y, dest, xs, rout, mod, g, b, n_lat, lat_len, n_batch, alpha)


def _permute_w_in(w):
    d = w.shape[0]
    o_sq, o_sk, o_sv, o_rq, o_rk, o_rv, o_gf, o_gb, o_mq, o_ckv, o_kr = (
        0, 768, 1024, 1280, 1664, 2048, 2816, 3584, 4352, 5120, 5376)
    mq = w[:, o_mq:o_mq + 768].reshape(d, MLA_HEADS, MLA_NOPE_DIM + MLA_ROPE_DIM)
    parts = [w[:, o_sq:o_sq + 768], w[:, o_rv:o_rv + 768], w[:, o_gf:o_gf + 768], w[:, o_gb:o_gb + 768],
             mq[:, :, :MLA_NOPE_DIM].reshape(d, 512), w[:, o_sk:o_sk + 256], w[:, o_sv:o_sv + 256],
             mq[:, :, MLA_NOPE_DIM:].reshape(d, 256), w[:, o_ckv:o_ckv + 256], w[:, o_rq:o_rq + 384],
             w[:, o_rk:o_rk + 384], w[:, o_kr:o_kr + 64], jnp.zeros((d, _NP - _KR - 64), w.dtype)]
    return jnp.concatenate(parts, axis=1).astype(bf16)


def kernel(x, c, ctx, c_ctx, w_ada, b_ada, w_in, swa_sink, ret_decay, mla_kv_norm, mla_w_uk, mla_w_uv, w_out, ln1_g, ln1_b, ln2_g, ln2_b, moe_w_group, moe_b_group, moe_w_expert, moe_b_expert, moe_w_gate, moe_w_up, moe_w_down):
    n_batch, lat_len, d = x.shape
    ctx_len = ctx.shape[1]
    depth = w_ada.shape[0]
    n_lat, n_ctx = n_batch * lat_len, n_batch * ctx_len
    alpha = (2 * depth) ** 0.25

    cc = jnp.zeros((8, d), f32).at[:n_batch].set(c).at[n_batch].set(c_ctx)
    mod_all = _ada(cc, w_ada, b_ada).reshape(depth, 8, 6, d)
    tab = _rope_tables(lat_len, ctx_len, _inproj_rows(lat_len, n_ctx))
    xs = jnp.concatenate([x.reshape(n_lat, d), ctx.reshape(n_ctx, d)], axis=0)

    for l in range(depth):
        ctx_out = l < depth - 1
        mod = mod_all[l]
        p = _inproj(xs, mod, tab, _permute_w_in(w_in[l]), n_lat, lat_len, ctx_len, n_batch)
        kx, v = _mla_expand(p, mla_kv_norm[l][None, :], mla_w_uk[l].astype(bf16), mla_w_uv[l].astype(bf16))
        m = _mla_attention(p, kx, v, n_batch, lat_len, ctx_len, ctx_out)
        a = _swa_attention(p, swa_sink[l], n_batch, lat_len, ctx_len, ctx_out)
        lg = jnp.log1p(-jnp.exp2(-ret_decay[l].astype(f32)))
        r = _retention(p, lg, n_batch, lat_len, ctx_len)
        n_rows = n_lat + n_ctx if ctx_out else n_lat
        xs, h = _outproj(a, r, m, xs, mod, w_out[l].astype(bf16), ln1_g[l][None, :], ln1_b[l][None, :],
                         n_rows, n_lat, lat_len, n_batch, alpha)
        rout_w = jnp.zeros((d, LANES), f32).at[:, :N_GROUPS].set(moe_w_group[l]).at[
            :, _ROUTE_LANE0:_ROUTE_LANE0 + N_EXPERTS].set(moe_w_expert[l]).astype(bf16)
        rout_b = jnp.zeros((1, LANES), f32).at[0, :N_GROUPS].set(moe_b_group[l]).at[
            0, _ROUTE_LANE0:_ROUTE_LANE0 + N_EXPERTS].set(moe_b_expert[l])
        xs = _moe(h, xs, rout_w, rout_b, moe_w_gate[l].astype(bf16), moe_w_up[l].astype(bf16),
                  moe_w_down[l].astype(bf16), mod, ln2_g[l][None, :], ln2_b[l][None, :],
                  n_lat, lat_len, n_batch, alpha)
    return xs[:n_lat].reshape(n_batch, lat_len, d)
```

```python
import functools

import jax
import jax.numpy as jnp
from jax import lax
from jax.experimental import pallas as pl
from jax.experimental.pallas import tpu as pltpu

f32 = jnp.float32
bf16 = jnp.bfloat16
i32 = jnp.int32

GRID_W = 64
SWA_HEADS, SWA_KV_HEADS, SWA_HEAD_DIM, SWA_WINDOW = 6, 2, 128, 128
RET_HEADS, RET_QK_DIM, RET_V_DIM, RET_CHUNK = 6, 64, 128, 128
MLA_HEADS, MLA_NOPE_DIM, MLA_ROPE_DIM, MLA_V_DIM, MLA_KV_RANK = 4, 128, 64, 128, 256
N_GROUPS, EXPERTS_PER_GROUP, TOP_K = 4, 8, 2
N_EXPERTS = N_GROUPS * EXPERTS_PER_GROUP
ROPE_BASE = 10000.0
NORM_EPS = 1e-6
NEG_INF = -1e30
LANES = 128

_SQ, _RV, _GF, _GB, _MQN, _SK, _SV, _MQR, _CKV, _RQ, _RK, _KR = (
    0, 768, 1536, 2304, 3072, 3584, 3840, 4096, 4352, 4608, 4992, 5376)
_NP = 5504
_LOG2E = 1.4426950408889634
_SWA_SCALE = SWA_HEAD_DIM ** -0.5
_RET_SCALE = RET_QK_DIM ** -0.5
_MLA_SCALE = (MLA_NOPE_DIM + MLA_ROPE_DIM) ** -0.5 * _LOG2E
_SEGMENTS = ((_SQ, 768, 0, _SWA_SCALE), (_RV, 768, None, 1.0), (_GF, 768, None, 1.0), (_GB, 768, None, 1.0),
             (_MQN, 512, None, _MLA_SCALE), (_SK, 256, 0, 1.0), (_SV, 256, None, 1.0), (_MQR, 256, 2, _MLA_SCALE),
             (_CKV, 256, None, 1.0), (_RQ, 384, 1, _RET_SCALE), (_RK, 384, 1, 1.0), (_KR, 128, 2, 1.0))

_VMEM_LIMIT = 48 * 1024 * 1024


def _cparams(*sem):
    return pltpu.CompilerParams(dimension_semantics=sem, vmem_limit_bytes=_VMEM_LIMIT)


def _tile(n, pref):
    t = min(n, pref)
    while n % t:
        t //= 2
    return t


def _ln(x):
    mu = jnp.mean(x, axis=-1, keepdims=True)
    xc = x - mu
    var = jnp.mean(xc * xc, axis=-1, keepdims=True)
    return xc * lax.rsqrt(var + NORM_EPS)


def _silu(x):
    return x / (1.0 + jnp.exp(-x))


def _dot(a, b):
    return jnp.dot(a, b, preferred_element_type=f32)


def _dot_nt(a, b):
    return lax.dot_general(a, b, (((1,), (1,)), ((), ())), preferred_element_type=f32)


def _dot_tn(a, b):
    return lax.dot_general(a, b, (((0,), (0,)), ((), ())), preferred_element_type=f32)


def _ada_kernel(c_ref, w_ref, b_ref, o_ref):
    s = _silu(c_ref[...]).astype(bf16)
    o_ref[0] = _dot(s, w_ref[0].astype(bf16)) + b_ref[0]


def _ada(cc, w_ada, b_ada):
    depth, d, n = w_ada.shape
    tn = _tile(n, 1024)
    return pl.pallas_call(
        _ada_kernel,
        grid=(depth, n // tn),
        in_specs=[pl.BlockSpec((8, d), lambda l, j: (0, 0)),
                  pl.BlockSpec((1, d, tn), lambda l, j: (l, 0, j)),
                  pl.BlockSpec((1, 1, tn), lambda l, j: (l, 0, j))],
        out_specs=pl.BlockSpec((1, 8, tn), lambda l, j: (l, 0, j)),
        out_shape=jax.ShapeDtypeStruct((depth, 8, n), f32),
        compiler_params=_cparams("parallel", "parallel"),
        name="ada",
    )(cc, w_ada, b_ada.reshape(depth, 1, n))


_INPROJ_CHUNK = 512


def _inproj_kernel(x_ref, mod_ref, tab_ref, w_ref, o_ref):
    tm = x_ref.shape[0]
    h = (_ln(x_ref[...]) * (1.0 + mod_ref[0, 1:2, :]) + mod_ref[0, 0:1, :]).astype(bf16)
    lane = lax.broadcasted_iota(i32, (tm, LANES), 1)
    first = {32: (lane & 63) < 32, 16: (lane & 31) < 16}
    slab_kind = {}
    for off, width, typ, scale in _SEGMENTS:
        for k in range(off // LANES, (off + width) // LANES):
            slab_kind[k] = (typ, scale)
    for c0 in range(0, _NP, _INPROJ_CHUNK):
        c1 = min(c0 + _INPROJ_CHUNK, _NP)
        acc = _dot(h, w_ref[:, c0:c1])
        for k in range(c0 // LANES, c1 // LANES):
            typ, scale = slab_kind[k]
            xk = acc[:, k * LANES - c0:(k + 1) * LANES - c0]
            if scale != 1.0:
                xk = xk * scale
            if typ is not None:
                half = 16 if typ == 2 else 32
                partner = jnp.where(first[half], pltpu.roll(xk, LANES - half, 1), pltpu.roll(xk, half, 1))
                xk = xk * tab_ref[typ, 0] + partner * tab_ref[typ, 1]
            o_ref[:, k * LANES:(k + 1) * LANES] = xk.astype(bf16)


def _inproj_rows(lat_len, n_ctx):
    tm = _tile(lat_len, 256)
    while n_ctx % tm:
        tm //= 2
    return tm


def _inproj(xs, mod, tab, w, n_lat, lat_len, ctx_len, n_batch):
    t, d = xs.shape
    tm = _inproj_rows(lat_len, t - n_lat)
    nlb, bpb = n_lat // tm, lat_len // tm
    ctx_blocks = max(1, ctx_len // tm)

    def grp(i):
        return jnp.where(i < nlb, i // bpb, n_batch)

    def posblk(i):
        return jnp.where(i < nlb, i % bpb, bpb + (i - nlb) % ctx_blocks)

    return pl.pallas_call(
        _inproj_kernel,
        grid=(t // tm,),
        in_specs=[pl.BlockSpec((tm, d), lambda i: (i, 0)),
                  pl.BlockSpec((1, 6, d), lambda i: (grp(i), 0, 0)),
                  pl.BlockSpec((3, 2, tm, LANES), lambda i: (0, 0, posblk(i), 0)),
                  pl.BlockSpec((d, _NP), lambda i: (0, 0), pipeline_mode=pl.Buffered(1))],
        out_specs=pl.BlockSpec((tm, _NP), lambda i: (i, 0)),
        out_shape=jax.ShapeDtypeStruct((t, _NP), bf16),
        compiler_params=_cparams("parallel"),
        name="inproj",
    )(xs, mod, tab, w)


def _rope_tables(lat_len, ctx_len, tm):
    lane = jnp.arange(LANES)
    t = jnp.arange(lat_len)
    rows = (t // GRID_W).astype(f32)
    cols = (t % GRID_W).astype(f32)

    def freq(half):
        return ROPE_BASE ** (-jnp.arange(half, dtype=f32) / half)

    def table(pos, inv, sign):
        ang = pos * inv[None, :]
        return jnp.stack([jnp.cos(ang), jnp.sin(ang) * sign[None, :]])

    def ident(n):
        return jnp.stack([jnp.ones((n, LANES), f32), jnp.zeros((n, LANES), f32)])

    sign32 = jnp.where((lane % 64) < 32, -1.0, 1.0).astype(f32)
    sign16 = jnp.where((lane % 32) < 16, -1.0, 1.0).astype(f32)
    inv32 = freq(32)[lane % 32]
    inv16 = freq(16)[lane % 16]
    pos_swa = jnp.where((lane // 64)[None, :] == 0, rows[:, None], cols[:, None])
    pos_mla = jnp.where(((lane % 64) // 32)[None, :] == 0, rows[:, None], cols[:, None])
    rep = max(1, tm // ctx_len)
    nc = rep * ctx_len
    pos_ret_l = jnp.broadcast_to((ctx_len + t).astype(f32)[:, None], (lat_len, LANES))
    pos_ret_c = jnp.broadcast_to(jnp.tile(jnp.arange(ctx_len), rep).astype(f32)[:, None], (nc, LANES))
    t0 = jnp.concatenate([table(pos_swa, inv32, sign32), ident(nc)], axis=1)
    t1 = jnp.concatenate([table(pos_ret_l, inv32, sign32), table(pos_ret_c, inv32, sign32)], axis=1)
    t2 = jnp.concatenate([table(pos_mla, inv16, sign16), ident(nc)], axis=1)
    return jnp.stack([t0, t1, t2])


def _mla_expand_kernel(ckv_ref, kr_ref, g_ref, wuk_ref, wuv_ref, k_ref, v_ref):
    c = ckv_ref[...].astype(f32)
    cn = (c * lax.rsqrt(jnp.mean(c * c, axis=-1, keepdims=True) + NORM_EPS) * g_ref[...]).astype(bf16)
    kn = _dot(cn, wuk_ref[...]).astype(bf16)
    vv = _dot(cn, wuv_ref[...]).astype(bf16)
    kr = kr_ref[...]
    for h in range(MLA_HEADS):
        k_ref[h, :, 0:128] = kn[:, h * 128:(h + 1) * 128]
        k_ref[h, :, 128:256] = kr
        v_ref[h] = vv[:, h * 128:(h + 1) * 128]


def _mla_expand(p, g, wuk, wuv):
    t = p.shape[0]
    tm = _tile(t, 512)
    return pl.pallas_call(
        _mla_expand_kernel,
        grid=(t // tm,),
        in_specs=[pl.BlockSpec((tm, 256), lambda i: (i, _CKV // 256)),
                  pl.BlockSpec((tm, 128), lambda i: (i, _KR // 128)),
                  pl.BlockSpec((1, 256), lambda i: (0, 0)),
                  pl.BlockSpec((256, 512), lambda i: (0, 0)),
                  pl.BlockSpec((256, 512), lambda i: (0, 0))],
        out_specs=[pl.BlockSpec((MLA_HEADS, tm, 256), lambda i: (0, i, 0)),
                   pl.BlockSpec((MLA_HEADS, tm, 128), lambda i: (0, i, 0))],
        out_shape=[jax.ShapeDtypeStruct((MLA_HEADS, t, 256), bf16), jax.ShapeDtypeStruct((MLA_HEADS, t, 128), bf16)],
        compiler_params=_cparams("parallel"),
        name="mla_expand",
    )(p, p, g, wuk, wuv)


def _mla_kernel(*refs, with_lat):
    if with_lat:
        qn_ref, qr_ref, kc_ref, vc_ref, kl_ref, vl_ref, o_ref, qx_scr = refs
    else:
        qn_ref, qr_ref, kc_ref, vc_ref, _, o_ref, qx_scr = refs
    lane = lax.broadcasted_iota(i32, (qn_ref.shape[0], LANES), 1)
    for h in range(MLA_HEADS):
        slab = qr_ref[:, (h // 2) * 128:(h // 2 + 1) * 128].astype(f32)
        if h % 2:
            slab = pltpu.roll(slab, 64, 1)
        qx_scr[h, :, 0:128] = qn_ref[:, h * 128:(h + 1) * 128]
        qx_scr[h, :, 128:256] = jnp.where(lane < 64, slab, 0.0).astype(bf16)
    for h in range(MLA_HEADS):
        q = qx_scr[h]
        parts = [(_dot_nt(q, kc_ref[h]), vc_ref[h])]
        if with_lat:
            parts.append((_dot_nt(q, kl_ref[h]), vl_ref[h]))
        m = None
        for s, _ in parts:
            mx = jnp.max(s, axis=1, keepdims=True)
            m = mx if m is None else jnp.maximum(m, mx)
        den, acc = None, None
        for s, vv in parts:
            pr = jnp.exp2(s - m)
            sm = jnp.sum(pr, axis=1, keepdims=True)
            pv = _dot(pr.astype(bf16), vv)
            den = sm if den is None else den + sm
            acc = pv if acc is None else acc + pv
        o_ref[:, h * 128:(h + 1) * 128] = (acc / den).astype(bf16)


def _mla_attention(p, kx, v, n_batch, lat_len, ctx_len, compute_ctx):
    t = p.shape[0]
    tq = _tile(lat_len, 256)
    nlq = lat_len // tq
    cbase = n_batch * lat_len // ctx_len
    once = pl.Buffered(1)
    out = pl.pallas_call(
        functools.partial(_mla_kernel, with_lat=True),
        grid=(n_batch, nlq),
        in_specs=[pl.BlockSpec((tq, 512), lambda b, qi: (b * nlq + qi, _MQN // 512)),
                  pl.BlockSpec((tq, 256), lambda b, qi: (b * nlq + qi, _MQR // 256)),
                  pl.BlockSpec((MLA_HEADS, ctx_len, 256), lambda b, qi: (0, cbase + b, 0)),
                  pl.BlockSpec((MLA_HEADS, ctx_len, 128), lambda b, qi: (0, cbase + b, 0)),
                  pl.BlockSpec((MLA_HEADS, lat_len, 256), lambda b, qi: (0, b, 0), pipeline_mode=once),
                  pl.BlockSpec((MLA_HEADS, lat_len, 128), lambda b, qi: (0, b, 0), pipeline_mode=once)],
        out_specs=pl.BlockSpec((tq, 512), lambda b, qi: (b * nlq + qi, 0)),
        out_shape=jax.ShapeDtypeStruct((t, 512), bf16),
        scratch_shapes=[pltpu.VMEM((MLA_HEADS, tq, 256), bf16)],
        compiler_params=_cparams("parallel", "arbitrary"),
        name="mla_attn",
    )(p, p, kx, v, kx, v)
    if compute_ctx:
        out = pl.pallas_call(
            functools.partial(_mla_kernel, with_lat=False),
            grid=(n_batch,),
            in_specs=[pl.BlockSpec((ctx_len, 512), lambda b: (cbase + b, _MQN // 512)),
                      pl.BlockSpec((ctx_len, 256), lambda b: (cbase + b, _MQR // 256)),
                      pl.BlockSpec((MLA_HEADS, ctx_len, 256), lambda b: (0, cbase + b, 0)),
                      pl.BlockSpec((MLA_HEADS, ctx_len, 128), lambda b: (0, cbase + b, 0)),
                      pl.BlockSpec(memory_space=pl.ANY)],
            out_specs=pl.BlockSpec((ctx_len, 512), lambda b: (cbase + b, 0)),
            out_shape=jax.ShapeDtypeStruct((t, 512), bf16),
            scratch_shapes=[pltpu.VMEM((MLA_HEADS, ctx_len, 256), bf16)],
            input_output_aliases={4: 0},
            compiler_params=_cparams("parallel"),
            name="mla_ctx",
        )(p, p, kx, v, out)
    return out


def _swa_softmax_out(parts, sink_col, o_ref, g):
    m = sink_col
    for s, _ in parts:
        m = jnp.maximum(m, jnp.max(s, axis=1, keepdims=True))
    den = jnp.exp(sink_col - m)
    acc = None
    for s, vv in parts:
        pr = jnp.exp(s - m)
        den = den + jnp.sum(pr, axis=1, keepdims=True)
        pv = _dot(pr.astype(bf16), vv)
        acc = pv if acc is None else acc + pv
    o = acc / den
    nq = o.shape[0] // 3
    for j in range(3):
        o_ref[:, (3 * g + j) * 128:(3 * g + j + 1) * 128] = o[j * nq:(j + 1) * nq].astype(bf16)


def _sink_col(sink_ref, g, nq):
    row = lax.broadcasted_iota(i32, (3 * nq, 1), 0)
    return jnp.where(row < nq, sink_ref[3 * g], jnp.where(row < 2 * nq, sink_ref[3 * g + 1], sink_ref[3 * g + 2]))


def _swa_kernel(sink_ref, q_ref, kp_ref, ko_ref, kn_ref, vp_ref, vo_ref, vn_ref, kc_ref, vc_ref, o_ref):
    i = pl.program_id(1)
    nb = pl.num_programs(1)
    blk = q_ref.shape[0]
    q = q_ref[...]
    r = lax.broadcasted_iota(i32, (3 * blk, 3 * blk), 0) & (blk - 1)
    c = lax.broadcasted_iota(i32, (3 * blk, 3 * blk), 1)
    lo = jnp.where(i == 0, blk, 0)
    hi = jnp.where(i == nb - 1, 2 * blk, 3 * blk)
    d = c - r
    valid = (d >= 0) & (d <= 2 * SWA_WINDOW) & (c >= lo) & (c < hi)
    for g in range(SWA_KV_HEADS):
        hs = slice(g * 128, (g + 1) * 128)
        qg = jnp.concatenate([q[:, (3 * g + j) * 128:(3 * g + j + 1) * 128] for j in range(3)], axis=0)
        kloc = jnp.concatenate([kp_ref[:, hs], ko_ref[:, hs], kn_ref[:, hs]], axis=0)
        vloc = jnp.concatenate([vp_ref[:, hs], vo_ref[:, hs], vn_ref[:, hs]], axis=0)
        s_loc = jnp.where(valid, _dot_nt(qg, kloc), NEG_INF)
        s_ctx = _dot_nt(qg, kc_ref[:, hs])
        _swa_softmax_out([(s_ctx, vc_ref[:, hs]), (s_loc, vloc)], _sink_col(sink_ref, g, blk), o_ref, g)


def _swa_ctx_kernel(sink_ref, q_ref, kc_ref, vc_ref, prev_ref, o_ref):
    del prev_ref
    q = q_ref[...]
    nq = q.shape[0]
    for g in range(SWA_KV_HEADS):
        hs = slice(g * 128, (g + 1) * 128)
        qg = jnp.concatenate([q[:, (3 * g + j) * 128:(3 * g + j + 1) * 128] for j in range(3)], axis=0)
        s_ctx = _dot_nt(qg, kc_ref[:, hs])
        _swa_softmax_out([(s_ctx, vc_ref[:, hs])], _sink_col(sink_ref, g, nq), o_ref, g)


def _swa_attention(p, sink, n_batch, lat_len, ctx_len, compute_ctx):
    t = p.shape[0]
    blk = SWA_WINDOW
    nb = lat_len // blk
    cbase = n_batch * lat_len // ctx_len
    smem = pl.BlockSpec(memory_space=pltpu.SMEM)

    def kspec(col, off):
        return pl.BlockSpec((blk, 256), lambda b, i: (b * nb + jnp.clip(i + off, 0, nb - 1), col // 256))

    out = pl.pallas_call(
        _swa_kernel,
        grid=(n_batch, nb),
        in_specs=[smem,
                  pl.BlockSpec((blk, 768), lambda b, i: (b * nb + i, _SQ // 768)),
                  kspec(_SK, -1), kspec(_SK, 0), kspec(_SK, 1),
                  kspec(_SV, -1), kspec(_SV, 0), kspec(_SV, 1),
                  pl.BlockSpec((ctx_len, 256), lambda b, i: (cbase + b, _SK // 256)),
                  pl.BlockSpec((ctx_len, 256), lambda b, i: (cbase + b, _SV // 256))],
        out_specs=pl.BlockSpec((blk, 768), lambda b, i: (b * nb + i, 0)),
        out_shape=jax.ShapeDtypeStruct((t, 768), bf16),
        compiler_params=_cparams("parallel", "parallel"),
        name="swa_attn",
    )(sink, p, p, p, p, p, p, p, p, p)
    if compute_ctx:
        out = pl.pallas_call(
            _swa_ctx_kernel,
            grid=(n_batch,),
            in_specs=[smem,
                      pl.BlockSpec((ctx_len, 768), lambda b: (cbase + b, _SQ // 768)),
                      pl.BlockSpec((ctx_len, 256), lambda b: (cbase + b, _SK // 256)),
                      pl.BlockSpec((ctx_len, 256), lambda b: (cbase + b, _SV // 256)),
                      pl.BlockSpec(memory_space=pl.ANY)],
            out_specs=pl.BlockSpec((ctx_len, 768), lambda b: (cbase + b, 0)),
            out_shape=jax.ShapeDtypeStruct((t, 768), bf16),
            input_output_aliases={4: 0},
            compiler_params=_cparams("parallel"),
            name="swa_ctx",
        )(sink, p, p, p, out)
    return out


def _ret_kernel(lg_ref, q_ref, k_ref, v_ref, *rest, backward):
    if backward:
        yf_ref, gf_ref, gb_ref, o_ref, s_scr, d_scr, qd_scr, kd_scr, gc_scr = rest
    else:
        o_ref, s_scr, d_scr, qd_scr, kd_scr, gc_scr = rest
    i = pl.program_id(1)
    cc = RET_CHUNK
    direction = 1 if backward else 0

    @pl.when(i == 0)
    def _():
        s_scr[...] = jnp.zeros(s_scr.shape, f32)
        r = lax.broadcasted_iota(i32, (cc, cc), 0).astype(f32)
        c = lax.broadcasted_iota(i32, (cc, cc), 1).astype(f32)
        diff = (c - r) if backward else (r - c)
        for h in range(RET_HEADS):
            lgh = lg_ref[direction, h]
            d_scr[h] = jnp.where(diff >= 0, jnp.exp(lgh * jnp.maximum(diff, 0.0)), 0.0)
            qd_scr[h] = jnp.exp(lgh * ((cc - r) if backward else (r + 1.0)))
            kd_scr[h] = jnp.exp(lgh * (r if backward else (cc - 1.0 - r)))
            gc_scr[h] = jnp.exp(lgh * cc + jnp.zeros((cc, cc), f32))

    lane = lax.broadcasted_iota(i32, (cc, LANES), 1)
    for pair in range(RET_HEADS // 2):
        q2 = q_ref[:, pair * 128:(pair + 1) * 128].astype(f32)
        k2b = k_ref[:, pair * 128:(pair + 1) * 128]
        k2 = k2b.astype(f32)
        for sub in range(2):
            h = 2 * pair + sub
            hs = slice(h * 128, (h + 1) * 128)
            qa = jnp.where((lane < 64) if sub == 0 else (lane >= 64), q2, 0.0)
            s = _dot_nt(qa.astype(bf16), k2b)
            vh = v_ref[:, hs]
            y = _dot((s * d_scr[h]).astype(bf16), vh) + _dot((qa * qd_scr[h]).astype(bf16), s_scr[h].astype(bf16))
            s_scr[h] = gc_scr[h] * s_scr[h] + _dot_tn((k2 * kd_scr[h]).astype(bf16), vh)
            yn = _ln(y)
            if backward:
                o_ref[:, hs] = (_silu(gf_ref[:, hs].astype(f32)) * yf_ref[:, hs].astype(f32)
                                + _silu(gb_ref[:, hs].astype(f32)) * yn).astype(bf16)
            else:
                o_ref[:, hs] = yn.astype(bf16)


def _retention(p, lg, n_batch, lat_len, ctx_len):
    t = p.shape[0]
    cc = RET_CHUNK
    ncc, nlc = ctx_len // cc, lat_len // cc
    cbase = n_batch * nlc
    smem = pl.BlockSpec(memory_space=pltpu.SMEM)
    scratch = [pltpu.VMEM((RET_HEADS, cc, cc), f32) for _ in range(5)]

    def fwd_row(b, i):
        return jnp.where(i < ncc, cbase + b * ncc + i, b * nlc + (i - ncc))

    def bwd_row(b, i):
        return jnp.where(i < ncc, cbase + b * ncc + (ncc - 1 - i), b * nlc + (nlc - 1 - (i - ncc)))

    def specs(row):
        return [pl.BlockSpec((cc, 384), lambda b, i: (row(b, i), _RQ // 384)),
                pl.BlockSpec((cc, 384), lambda b, i: (row(b, i), _RK // 384)),
                pl.BlockSpec((cc, 768), lambda b, i: (row(b, i), _RV // 768))]

    yf = pl.pallas_call(
        functools.partial(_ret_kernel, backward=False),
        grid=(n_batch, ncc + nlc),
        in_specs=[smem] + specs(fwd_row),
        out_specs=pl.BlockSpec((cc, 768), lambda b, i: (fwd_row(b, i), 0)),
        out_shape=jax.ShapeDtypeStruct((t, 768), bf16),
        scratch_shapes=scratch,
        compiler_params=_cparams("parallel", "arbitrary"),
        name="ret_fwd",
    )(lg, p, p, p)
    return pl.pallas_call(
        functools.partial(_ret_kernel, backward=True),
        grid=(n_batch, ncc + nlc),
        in_specs=[smem] + specs(bwd_row) + [
            pl.BlockSpec((cc, 768), lambda b, i: (bwd_row(b, i), 0)),
            pl.BlockSpec((cc, 768), lambda b, i: (bwd_row(b, i), _GF // 768)),
            pl.BlockSpec((cc, 768), lambda b, i: (bwd_row(b, i), _GB // 768))],
        out_specs=pl.BlockSpec((cc, 768), lambda b, i: (bwd_row(b, i), 0)),
        out_shape=jax.ShapeDtypeStruct((t, 768), bf16),
        scratch_shapes=scratch,
        compiler_params=_cparams("parallel", "arbitrary"),
        name="ret_bwd",
    )(lg, p, p, p, yf, p, p)


def _store_token_rows(ref, val):
    tm, d = val.shape
    ru = d // LANES
    for s in range(ru):
        ref[pl.ds(s, tm, stride=ru), :] = val[:, s * LANES:(s + 1) * LANES]


def _load_token_rows(ref, tm):
    ru = ref.shape[0] // tm
    return jnp.concatenate([ref[pl.ds(s, tm, stride=ru), :] for s in range(ru)], axis=1)


def _outproj_kernel(a_ref, r_ref, m_ref, x_ref, mod_ref, w_ref, g_ref, b_ref, rw_ref, rb_ref,
                    xo_ref, h_ref, rout_ref, cnt_ref, carry_scr, *, alpha):
    @pl.when(pl.program_id(0) == 0)
    def _():
        carry_scr[...] = jnp.zeros(carry_scr.shape, f32)

    y = (_dot(a_ref[...], w_ref[0:768, :]) + _dot(r_ref[...], w_ref[768:1536, :])
         + _dot(m_ref[...], w_ref[1536:2048, :]))
    xn = _ln(alpha * x_ref[...] + mod_ref[0, 2:3, :] * y) * g_ref[...] + b_ref[...]
    xo_ref[...] = xn
    h = _ln(xn) * (1.0 + mod_ref[0, 4:5, :]) + mod_ref[0, 3:4, :]
    _store_token_rows(h_ref, h)
    rout_ref[...] = _route(_dot(h.astype(bf16), rw_ref[...]) + rb_ref[...], carry_scr)
    cnt_ref[...] = carry_scr[...]


def _outproj(a, r, m, xs, mod, w, g, b, rw, rb, n_rows, n_lat, lat_len, n_batch, alpha):
    d = xs.shape[1]
    ru = d // LANES
    tm = _tile(lat_len, 256)
    while (n_rows - n_lat) % tm:
        tm //= 2
    nlb, bpb = n_lat // tm, lat_len // tm

    def grp(i):
        return jnp.where(i < nlb, i // bpb, n_batch)

    row = lambda i: (i, 0)
    const = lambda i: (0, 0)
    return pl.pallas_call(
        functools.partial(_outproj_kernel, alpha=alpha),
        grid=(n_rows // tm,),
        in_specs=[pl.BlockSpec((tm, 768), row), pl.BlockSpec((tm, 768), row), pl.BlockSpec((tm, 512), row),
                  pl.BlockSpec((tm, d), row),
                  pl.BlockSpec((1, 6, d), lambda i: (grp(i), 0, 0)),
                  pl.BlockSpec(w.shape, const),
                  pl.BlockSpec((1, d), const), pl.BlockSpec((1, d), const),
                  pl.BlockSpec((d, LANES), const), pl.BlockSpec((1, LANES), const)],
        out_specs=[pl.BlockSpec((tm, d), row), pl.BlockSpec((tm * ru, LANES), row),
                   pl.BlockSpec((tm, LANES), row), pl.BlockSpec((8, LANES), const)],
        out_shape=[jax.ShapeDtypeStruct((n_rows, d), f32), jax.ShapeDtypeStruct((n_rows * ru, LANES), f32),
                   jax.ShapeDtypeStruct((n_rows, LANES), f32), jax.ShapeDtypeStruct((8, LANES), f32)],
        scratch_shapes=[pltpu.VMEM((8, LANES), f32)],
        compiler_params=_cparams("arbitrary"),
        name="outproj",
    )(a, r, m, xs, mod, w, g, b, rw, rb)


_ROUTE_LANE0 = N_GROUPS


def _route(logits, carry_scr):
    tm = logits.shape[0]
    lane = lax.broadcasted_iota(i32, (tm, LANES), 1)
    lane_f = lane.astype(f32)
    big = float(2 * LANES)
    gl = jnp.where(lane < N_GROUPS, logits, -jnp.inf)
    gmax = jnp.max(gl, axis=1, keepdims=True)
    gidx = jnp.min(jnp.where(gl == gmax, lane_f, big), axis=1, keepdims=True)
    p_group = 1.0 / jnp.sum(jnp.exp(gl - gmax), axis=1, keepdims=True)
    egroup = ((lane - _ROUTE_LANE0) >> 3).astype(f32)
    in_group = (lane >= _ROUTE_LANE0) & (lane < _ROUTE_LANE0 + N_EXPERTS) & (egroup == gidx)
    ev = jnp.where(in_group, logits, -jnp.inf)
    e1 = jnp.max(ev, axis=1, keepdims=True)
    i1 = jnp.min(jnp.where(ev == e1, lane_f, big), axis=1, keepdims=True)
    ev2 = jnp.where(lane_f == i1, -jnp.inf, ev)
    e2 = jnp.max(ev2, axis=1, keepdims=True)
    i2 = jnp.min(jnp.where(ev2 == e2, lane_f, big), axis=1, keepdims=True)
    tt = jnp.exp(e2 - e1)
    w1 = p_group / (1.0 + tt)
    w2 = p_group * tt / (1.0 + tt)

    hit1 = lane_f == i1
    hit2 = lane_f == i2
    onehot = jnp.where(hit1, 1.0, jnp.where(hit2, 1.0, 0.0))
    rr = lax.broadcasted_iota(i32, (tm, tm), 0)
    cc = lax.broadcasted_iota(i32, (tm, tm), 1)
    lower = jnp.where(cc < rr, 1.0, 0.0).astype(bf16)
    before = _dot(lower, onehot.astype(bf16)) + carry_scr[0:1, :]
    rank1 = jnp.sum(jnp.where(hit1, before, 0.0), axis=1, keepdims=True)
    rank2 = jnp.sum(jnp.where(hit2, before, 0.0), axis=1, keepdims=True)
    carry_scr[0:1, :] = carry_scr[0:1, :] + jnp.sum(onehot, axis=0, keepdims=True)

    return jnp.where(lane == 0, i1 - _ROUTE_LANE0, jnp.where(lane == 1, i2 - _ROUTE_LANE0, jnp.where(
        lane == 2, rank1, jnp.where(lane == 3, rank2, jnp.where(lane == 4, w1, jnp.where(lane == 5, w2, 0.0))))))


def _dispatch_kernel(cnt_ref, pstart_ref, padded_ref, dest_ref, h_hbm, o_hbm, sems, *, ru):
    i = pl.program_id(0)
    nb = pl.num_programs(0) - 1
    tm = dest_ref.shape[2] // TOP_K

    def row_copy(src_row, dst_row, sem):
        return pltpu.make_async_copy(h_hbm.at[pl.ds(pl.multiple_of(src_row, ru), ru), :],
                                     o_hbm.at[pl.ds(pl.multiple_of(dst_row, ru), ru), :], sem)

    def wait_block(sem):
        pltpu.make_async_copy(h_hbm.at[pl.ds(0, TOP_K * tm * ru), :], o_hbm.at[pl.ds(0, TOP_K * tm * ru), :], sem).wait()

    @pl.when(i < nb)
    def _():
        base = i * tm

        def issue(r, carry):
            for k in range(TOP_K):
                row_copy((base + r) * ru, dest_ref[0, 0, TOP_K * r + k], sems.at[i % 2]).start()
            return carry

        lax.fori_loop(0, tm, issue, 0)

    @pl.when(i > 0)
    def _():
        wait_block(sems.at[(i - 1) % 2])

    @pl.when(i == nb)
    def _():
        sem = sems.at[i % 2]

        def per_expert(e, total):
            lo, hi = cnt_ref[e], padded_ref[e]

            def fill(r, carry):
                row_copy(0, (pstart_ref[e] + r) * ru, sem).start()
                return carry

            lax.fori_loop(lo, hi, fill, 0)
            return total + (hi - lo)

        total = lax.fori_loop(0, N_EXPERTS, per_expert, 0)

        def drain(r, carry):
            row_copy(0, 0, sem).wait()
            return carry

        lax.fori_loop(0, total, drain, 0)


def _dispatch(h3, dest_rows, counts, pstart, padded, n_rows, tm, ru):
    n = dest_rows.shape[0] // TOP_K
    nb = n // tm
    grid_spec = pltpu.PrefetchScalarGridSpec(
        num_scalar_prefetch=3,
        grid=(nb + 1,),
        in_specs=[pl.BlockSpec((1, 1, TOP_K * tm), lambda i, *_: (jnp.minimum(i, nb - 1), 0, 0),
                               memory_space=pltpu.SMEM),
                  pl.BlockSpec(memory_space=pl.ANY)],
        out_specs=pl.BlockSpec(memory_space=pl.ANY),
        scratch_shapes=[pltpu.SemaphoreType.DMA((2,))],
    )
    return pl.pallas_call(
        functools.partial(_dispatch_kernel, ru=ru),
        grid_spec=grid_spec,
        out_shape=jax.ShapeDtypeStruct((n_rows * ru, LANES), f32),
        compiler_params=_cparams("arbitrary"),
        name="moe_dispatch",
    )(counts, pstart, padded, dest_rows.reshape(nb, 1, TOP_K * tm), h3)


_CAST_ROWS = 256


def _expert_kernel(nused_ref, be_ref, first_ref, nxt_ref, x_ref, wg_hbm, wu_hbm, wd_hbm, y_ref,
                   wg_f, wu_f, wd_f, wg_b, wu_b, wd_b, sems, *, layer):
    i = pl.program_id(0)
    used = i < nused_ref[0]
    staged = ((wg_hbm, wg_f, wg_b), (wu_hbm, wu_f, wu_b), (wd_hbm, wd_f, wd_b))

    def fetch(e):
        return [pltpu.make_async_copy(hbm.at[layer, e], stage, sems.at[j]) for j, (hbm, stage, _) in enumerate(staged)]

    @pl.when(i == 0)
    def _():
        for cp in fetch(be_ref[0]):
            cp.start()

    @pl.when(used & (first_ref[i] == 1))
    def _():
        for cp in fetch(be_ref[i]):
            cp.wait()
        for _, stage, dst in staged:
            rows = stage.shape[0]
            step = min(_CAST_ROWS, rows)

            def cast(c, carry, stage=stage, dst=dst, step=step):
                sl = pl.ds(pl.multiple_of(c * step, step), step)
                dst[sl, :] = stage[sl, :].astype(bf16)
                return carry

            lax.fori_loop(0, rows // step, cast, 0)
        nxt = nxt_ref[be_ref[i]]

        @pl.when(nxt >= 0)
        def _():
            for cp in fetch(nxt):
                cp.start()

    @pl.when(used)
    def _():
        tm = x_ref.shape[0] // (wg_b.shape[0] // LANES)
        x = _load_token_rows(x_ref, tm).astype(bf16)
        act = (_silu(_dot(x, wg_b[...])) * _dot(x, wu_b[...])).astype(bf16)
        _store_token_rows(y_ref, _dot(act, wd_b[...]))


_EXPERT_VMEM_LIMIT = 60 * 1024 * 1024


def _experts(xs3, n_used, block_expert, first, nxt, wg, wu, wd, layer, rb, ru):
    d, hid = wg.shape[2], wg.shape[3]
    n_blocks = xs3.shape[0] // (rb * ru)

    def blk(i, nu, *_):
        return (jnp.minimum(i, nu[0] - 1), 0)

    grid_spec = pltpu.PrefetchScalarGridSpec(
        num_scalar_prefetch=4,
        grid=(n_blocks,),
        in_specs=[pl.BlockSpec((rb * ru, LANES), blk),
                  pl.BlockSpec(memory_space=pl.ANY), pl.BlockSpec(memory_space=pl.ANY),
                  pl.BlockSpec(memory_space=pl.ANY)],
        out_specs=pl.BlockSpec((rb * ru, LANES), blk),
        scratch_shapes=[pltpu.VMEM((d, hid), f32), pltpu.VMEM((d, hid), f32), pltpu.VMEM((hid, d), f32),
                        pltpu.VMEM((d, hid), bf16), pltpu.VMEM((d, hid), bf16), pltpu.VMEM((hid, d), bf16),
                        pltpu.SemaphoreType.DMA((3,))],
    )
    return pl.pallas_call(
        functools.partial(_expert_kernel, layer=layer),
        grid_spec=grid_spec,
        out_shape=jax.ShapeDtypeStruct(xs3.shape, f32),
        compiler_params=pltpu.CompilerParams(dimension_semantics=("arbitrary",), vmem_limit_bytes=_EXPERT_VMEM_LIMIT),
        name="moe_experts",
    )(n_used, block_expert, first, nxt, xs3, wg, wu, wd)


def _combine_kernel(dcur_ref, dnxt_ref, y_hbm, x_ref, gate_ref, mod_ref, g_ref, b_ref, o_ref, ybuf, sems, *, alpha, ru):
    i = pl.program_id(0)
    nb = pl.num_programs(0)
    tm = x_ref.shape[0]

    def issue(dref, slot):
        def body(r, carry):
            for k in range(TOP_K):
                pltpu.make_async_copy(y_hbm.at[pl.ds(pl.multiple_of(dref[0, 0, TOP_K * r + k], ru), ru), :],
                                      ybuf.at[slot, k, pl.ds(pl.multiple_of(r * ru, ru), ru), :], sems.at[slot]).start()
            return carry

        lax.fori_loop(0, tm, body, 0)

    @pl.when(i == 0)
    def _():
        issue(dcur_ref, 0)

    @pl.when(i + 1 < nb)
    def _():
        issue(dnxt_ref, (i + 1) % 2)

    slot = i % 2
    for k in range(TOP_K):
        pltpu.make_async_copy(y_hbm.at[pl.ds(0, tm * ru), :], ybuf.at[slot, k], sems.at[slot]).wait()
    gates = gate_ref[...]
    f = gates[:, 4:5] * _load_token_rows(ybuf.at[slot, 0], tm) + gates[:, 5:6] * _load_token_rows(ybuf.at[slot, 1], tm)
    o_ref[...] = _ln(alpha * x_ref[...] + mod_ref[0, 5:6, :] * f) * g_ref[...] + b_ref[...]


def _combine(y3, dest_rows, xs, rout, mod, g, b, n_lat, lat_len, n_batch, alpha, tm, ru):
    n, d = xs.shape
    nb = n // tm
    nlb, bpb = n_lat // tm, lat_len // tm

    def grp(i):
        return jnp.where(i < nlb, i // bpb, n_batch)

    dest3 = dest_rows.reshape(nb, 1, TOP_K * tm)
    const = lambda i: (0, 0)
    return pl.pallas_call(
        functools.partial(_combine_kernel, alpha=alpha, ru=ru),
        grid=(nb,),
        in_specs=[pl.BlockSpec((1, 1, TOP_K * tm), lambda i: (i, 0, 0), memory_space=pltpu.SMEM),
                  pl.BlockSpec((1, 1, TOP_K * tm), lambda i: (jnp.minimum(i + 1, nb - 1), 0, 0),
                               memory_space=pltpu.SMEM),
                  pl.BlockSpec(memory_space=pl.ANY),
                  pl.BlockSpec((tm, d), lambda i: (i, 0)),
                  pl.BlockSpec((tm, LANES), lambda i: (i, 0)),
                  pl.BlockSpec((1, 6, d), lambda i: (grp(i), 0, 0)),
                  pl.BlockSpec((1, d), const), pl.BlockSpec((1, d), const)],
        out_specs=pl.BlockSpec((tm, d), lambda i: (i, 0)),
        out_shape=jax.ShapeDtypeStruct((n, d), f32),
        scratch_shapes=[pltpu.VMEM((2, TOP_K, tm * ru, LANES), f32), pltpu.SemaphoreType.DMA((2,))],
        compiler_params=_cparams("arbitrary"),
        name="moe_combine",
    )(dest3, dest3, y3, xs, rout, mod, g, b)


def _moe(h3, rout, cnt, xs, wg, wu, wd, layer, mod, g, b, n_lat, lat_len, n_batch, alpha):
    n, d = xs.shape
    ru = d // LANES
    tm = _tile(lat_len, 256)
    while (n - n_lat) % tm:
        tm //= 2
    rb = tm
    eid = rout[:, 0:2].astype(i32)
    rank = rout[:, 2:4].astype(i32)
    counts = cnt[0, _ROUTE_LANE0:_ROUTE_LANE0 + N_EXPERTS].astype(i32)
    padded = (counts + rb - 1) // rb * rb
    pend = jnp.cumsum(padded)
    pstart = pend - padded
    dest_rows = ((pstart[eid] + rank) * ru).reshape(-1)
    n_blocks = TOP_K * n // rb + N_EXPERTS
    n_used = (pend[-1:] // rb).astype(i32)
    blk = jnp.arange(n_blocks, dtype=i32)
    ex = jnp.arange(N_EXPERTS, dtype=i32)
    block_expert = jnp.minimum(jnp.sum((pend[None, :] <= (blk * rb)[:, None]).astype(i32), axis=1), N_EXPERTS - 1)
    block_expert = jnp.where(blk < n_used[0], block_expert, block_expert[jnp.maximum(n_used[0] - 1, 0)])
    first = ((blk < n_used[0]) & ((blk == 0) | (block_expert != jnp.roll(block_expert, 1)))).astype(i32)
    later_active = (ex[None, :] > ex[:, None]) & (counts > 0)[None, :]
    nxt = jnp.min(jnp.where(later_active, ex[None, :], N_EXPERTS), axis=1)
    nxt = jnp.where(nxt == N_EXPERTS, -1, nxt).astype(i32)
    xs3 = _dispatch(h3, dest_rows, counts, pstart, padded, n_blocks * rb, tm, ru)
    y3 = _experts(xs3, n_used, block_expert, first, nxt, wg, wu, wd, layer, rb, ru)
    return _combine(y3, dest_rows, xs, rout, mod, g, b, n_lat, lat_len, n_batch, alpha, tm, ru)


def _permute_w_in(w):
    d = w.shape[0]
    o_sq, o_sk, o_sv, o_rq, o_rk, o_rv, o_gf, o_gb, o_mq, o_ckv, o_kr = (
        0, 768, 1024, 1280, 1664, 2048, 2816, 3584, 4352, 5120, 5376)
    mq = w[:, o_mq:o_mq + 768].reshape(d, MLA_HEADS, MLA_NOPE_DIM + MLA_ROPE_DIM)
    parts = [w[:, o_sq:o_sq + 768], w[:, o_rv:o_rv + 768], w[:, o_gf:o_gf + 768], w[:, o_gb:o_gb + 768],
             mq[:, :, :MLA_NOPE_DIM].reshape(d, 512), w[:, o_sk:o_sk + 256], w[:, o_sv:o_sv + 256],
             mq[:, :, MLA_NOPE_DIM:].reshape(d, 256), w[:, o_ckv:o_ckv + 256], w[:, o_rq:o_rq + 384],
             w[:, o_rk:o_rk + 384], w[:, o_kr:o_kr + 64], jnp.zeros((d, _NP - _KR - 64), w.dtype)]
    return jnp.concatenate(parts, axis=1).astype(bf16)


def kernel(x, c, ctx, c_ctx, w_ada, b_ada, w_in, swa_sink, ret_decay, mla_kv_norm, mla_w_uk, mla_w_uv, w_out, ln1_g, ln1_b, ln2_g, ln2_b, moe_w_group, moe_b_group, moe_w_expert, moe_b_expert, moe_w_gate, moe_w_up, moe_w_down):
    n_batch, lat_len, d = x.shape
    ctx_len = ctx.shape[1]
    depth = w_ada.shape[0]
    n_lat, n_ctx = n_batch * lat_len, n_batch * ctx_len
    alpha = (2 * depth) ** 0.25

    cc = jnp.zeros((8, d), f32).at[:n_batch].set(c).at[n_batch].set(c_ctx)
    mod_all = _ada(cc, w_ada, b_ada).reshape(depth, 8, 6, d)
    tab = _rope_tables(lat_len, ctx_len, _inproj_rows(lat_len, n_ctx))
    xs = jnp.concatenate([x.reshape(n_lat, d), ctx.reshape(n_ctx, d)], axis=0)

    for l in range(depth):
        ctx_out = l < depth - 1
        mod = mod_all[l]
        p = _inproj(xs, mod, tab, _permute_w_in(w_in[l]), n_lat, lat_len, ctx_len, n_batch)
        kx, v = _mla_expand(p, mla_kv_norm[l][None, :], mla_w_uk[l].astype(bf16), mla_w_uv[l].astype(bf16))
        m = _mla_attention(p, kx, v, n_batch, lat_len, ctx_len, ctx_out)
        a = _swa_attention(p, swa_sink[l], n_batch, lat_len, ctx_len, ctx_out)
        lg = jnp.log1p(-jnp.exp2(-ret_decay[l].astype(f32)))
        r = _retention(p, lg, n_batch, lat_len, ctx_len)
        n_rows = n_lat + n_ctx if ctx_out else n_lat
        rout_w = jnp.zeros((d, LANES), f32).at[:, :N_GROUPS].set(moe_w_group[l]).at[
            :, _ROUTE_LANE0:_ROUTE_LANE0 + N_EXPERTS].set(moe_w_expert[l]).astype(bf16)
        rout_b = jnp.zeros((1, LANES), f32).at[0, :N_GROUPS].set(moe_b_group[l]).at[
            0, _ROUTE_LANE0:_ROUTE_LANE0 + N_EXPERTS].set(moe_b_expert[l])
        xs, h3, rout, cnt = _outproj(a, r, m, xs, mod, w_out[l].astype(bf16), ln1_g[l][None, :], ln1_b[l][None, :],
                                     rout_w, rout_b, n_rows, n_lat, lat_len, n_batch, alpha)
        xs = _moe(h3, rout, cnt, xs, moe_w_gate, moe_w_up, moe_w_down, l, mod, ln2_g[l][None, :], ln2_b[l][None, :],
                  n_lat, lat_len, n_batch, alpha)
    return xs[:n_lat].reshape(n_batch, lat_len, d)
```

```python
import functools

import jax
import jax.numpy as jnp
from jax import lax
from jax.experimental import pallas as pl
from jax.experimental.pallas import tpu as pltpu

f32 = jnp.float32
bf16 = jnp.bfloat16
i32 = jnp.int32

GRID_W = 64
SWA_HEADS, SWA_KV_HEADS, SWA_HEAD_DIM, SWA_WINDOW = 6, 2, 128, 128
RET_HEADS, RET_QK_DIM, RET_V_DIM, RET_CHUNK = 6, 64, 128, 128
MLA_HEADS, MLA_NOPE_DIM, MLA_ROPE_DIM, MLA_V_DIM, MLA_KV_RANK = 4, 128, 64, 128, 256
N_GROUPS, EXPERTS_PER_GROUP, TOP_K = 4, 8, 2
N_EXPERTS = N_GROUPS * EXPERTS_PER_GROUP
ROPE_BASE = 10000.0
NORM_EPS = 1e-6
NEG_INF = -1e30
LANES = 128

_SQ, _RV, _GF, _GB, _MQN, _SK, _SV, _MQR, _CKV, _RQ, _RK, _KR = (
    0, 768, 1536, 2304, 3072, 3584, 3840, 4096, 4352, 4608, 4992, 5376)
_NP = 5504
_LOG2E = 1.4426950408889634
_SWA_SCALE = SWA_HEAD_DIM ** -0.5
_RET_SCALE = RET_QK_DIM ** -0.5
_MLA_SCALE = (MLA_NOPE_DIM + MLA_ROPE_DIM) ** -0.5 * _LOG2E
_SEGMENTS = ((_SQ, 768, 0, _SWA_SCALE), (_RV, 768, None, 1.0), (_GF, 768, None, 1.0), (_GB, 768, None, 1.0),
             (_MQN, 512, None, _MLA_SCALE), (_SK, 256, 0, 1.0), (_SV, 256, None, 1.0), (_MQR, 256, 2, _MLA_SCALE),
             (_CKV, 256, None, 1.0), (_RQ, 384, 1, _RET_SCALE), (_RK, 384, 1, 1.0), (_KR, 128, 2, 1.0))

_VMEM_LIMIT = 48 * 1024 * 1024


def _cparams(*sem):
    return pltpu.CompilerParams(dimension_semantics=sem, vmem_limit_bytes=_VMEM_LIMIT)


def _tile(n, pref):
    t = min(n, pref)
    while n % t:
        t //= 2
    return t


def _ln(x):
    mu = jnp.mean(x, axis=-1, keepdims=True)
    xc = x - mu
    var = jnp.mean(xc * xc, axis=-1, keepdims=True)
    return xc * lax.rsqrt(var + NORM_EPS)


def _silu(x):
    return x / (1.0 + jnp.exp(-x))


def _dot(a, b):
    return jnp.dot(a, b, preferred_element_type=f32)


def _dot_nt(a, b):
    return lax.dot_general(a, b, (((1,), (1,)), ((), ())), preferred_element_type=f32)


def _dot_tn(a, b):
    return lax.dot_general(a, b, (((0,), (0,)), ((), ())), preferred_element_type=f32)


def _ada_kernel(c_ref, w_ref, b_ref, o_ref):
    s = _silu(c_ref[...]).astype(bf16)
    o_ref[0] = _dot(s, w_ref[0].astype(bf16)) + b_ref[0]


def _ada(cc, w_ada, b_ada):
    depth, d, n = w_ada.shape
    tn = _tile(n, 1024)
    return pl.pallas_call(
        _ada_kernel,
        grid=(depth, n // tn),
        in_specs=[pl.BlockSpec((8, d), lambda l, j: (0, 0)),
                  pl.BlockSpec((1, d, tn), lambda l, j: (l, 0, j)),
                  pl.BlockSpec((1, 1, tn), lambda l, j: (l, 0, j))],
        out_specs=pl.BlockSpec((1, 8, tn), lambda l, j: (l, 0, j)),
        out_shape=jax.ShapeDtypeStruct((depth, 8, n), f32),
        compiler_params=_cparams("parallel", "parallel"),
        name="ada",
    )(cc, w_ada, b_ada.reshape(depth, 1, n))


_INPROJ_CHUNK = 512


def _inproj_kernel(x_ref, mod_ref, tab_ref, w_ref, o_ref):
    tm = x_ref.shape[0]
    h = (_ln(x_ref[...]) * (1.0 + mod_ref[0, 1:2, :]) + mod_ref[0, 0:1, :]).astype(bf16)
    lane = lax.broadcasted_iota(i32, (tm, LANES), 1)
    first = {32: (lane & 63) < 32, 16: (lane & 31) < 16}
    slab_kind = {}
    for off, width, typ, scale in _SEGMENTS:
        for k in range(off // LANES, (off + width) // LANES):
            slab_kind[k] = (typ, scale)
    for c0 in range(0, _NP, _INPROJ_CHUNK):
        c1 = min(c0 + _INPROJ_CHUNK, _NP)
        acc = _dot(h, w_ref[:, c0:c1])
        for k in range(c0 // LANES, c1 // LANES):
            typ, scale = slab_kind[k]
            xk = acc[:, k * LANES - c0:(k + 1) * LANES - c0]
            if scale != 1.0:
                xk = xk * scale
            if typ is not None:
                half = 16 if typ == 2 else 32
                partner = jnp.where(first[half], pltpu.roll(xk, LANES - half, 1), pltpu.roll(xk, half, 1))
                xk = xk * tab_ref[typ, 0] + partner * tab_ref[typ, 1]
            o_ref[:, k * LANES:(k + 1) * LANES] = xk.astype(bf16)


def _inproj_rows(lat_len, n_ctx):
    tm = _tile(lat_len, 256)
    while n_ctx % tm:
        tm //= 2
    return tm


def _inproj(xs, mod, tab, w, n_lat, lat_len, ctx_len, n_batch):
    t, d = xs.shape
    tm = _inproj_rows(lat_len, t - n_lat)
    nlb, bpb = n_lat // tm, lat_len // tm
    ctx_blocks = max(1, ctx_len // tm)

    def grp(i):
        return jnp.where(i < nlb, i // bpb, n_batch)

    def posblk(i):
        return jnp.where(i < nlb, i % bpb, bpb + (i - nlb) % ctx_blocks)

    return pl.pallas_call(
        _inproj_kernel,
        grid=(t // tm,),
        in_specs=[pl.BlockSpec((tm, d), lambda i: (i, 0)),
                  pl.BlockSpec((1, 6, d), lambda i: (grp(i), 0, 0)),
                  pl.BlockSpec((3, 2, tm, LANES), lambda i: (0, 0, posblk(i), 0)),
                  pl.BlockSpec((d, _NP), lambda i: (0, 0), pipeline_mode=pl.Buffered(1))],
        out_specs=pl.BlockSpec((tm, _NP), lambda i: (i, 0)),
        out_shape=jax.ShapeDtypeStruct((t, _NP), bf16),
        compiler_params=_cparams("parallel"),
        name="inproj",
    )(xs, mod, tab, w)


def _rope_tables(lat_len, ctx_len, tm):
    lane = jnp.arange(LANES)
    t = jnp.arange(lat_len)
    rows = (t // GRID_W).astype(f32)
    cols = (t % GRID_W).astype(f32)

    def freq(half):
        return ROPE_BASE ** (-jnp.arange(half, dtype=f32) / half)

    def table(pos, inv, sign):
        ang = pos * inv[None, :]
        return jnp.stack([jnp.cos(ang), jnp.sin(ang) * sign[None, :]])

    def ident(n):
        return jnp.stack([jnp.ones((n, LANES), f32), jnp.zeros((n, LANES), f32)])

    sign32 = jnp.where((lane % 64) < 32, -1.0, 1.0).astype(f32)
    sign16 = jnp.where((lane % 32) < 16, -1.0, 1.0).astype(f32)
    inv32 = freq(32)[lane % 32]
    inv16 = freq(16)[lane % 16]
    pos_swa = jnp.where((lane // 64)[None, :] == 0, rows[:, None], cols[:, None])
    pos_mla = jnp.where(((lane % 64) // 32)[None, :] == 0, rows[:, None], cols[:, None])
    rep = max(1, tm // ctx_len)
    nc = rep * ctx_len
    pos_ret_l = jnp.broadcast_to((ctx_len + t).astype(f32)[:, None], (lat_len, LANES))
    pos_ret_c = jnp.broadcast_to(jnp.tile(jnp.arange(ctx_len), rep).astype(f32)[:, None], (nc, LANES))
    t0 = jnp.concatenate([table(pos_swa, inv32, sign32), ident(nc)], axis=1)
    t1 = jnp.concatenate([table(pos_ret_l, inv32, sign32), table(pos_ret_c, inv32, sign32)], axis=1)
    t2 = jnp.concatenate([table(pos_mla, inv16, sign16), ident(nc)], axis=1)
    return jnp.stack([t0, t1, t2])


def _mla_expand_kernel(ckv_ref, kr_ref, g_ref, wuk_ref, wuv_ref, k_ref, v_ref):
    c = ckv_ref[...].astype(f32)
    cn = (c * lax.rsqrt(jnp.mean(c * c, axis=-1, keepdims=True) + NORM_EPS) * g_ref[...]).astype(bf16)
    kn = _dot(cn, wuk_ref[...]).astype(bf16)
    vv = _dot(cn, wuv_ref[...]).astype(bf16)
    kr = kr_ref[...]
    for h in range(MLA_HEADS):
        k_ref[h, :, 0:128] = kn[:, h * 128:(h + 1) * 128]
        k_ref[h, :, 128:256] = kr
        v_ref[h] = vv[:, h * 128:(h + 1) * 128]


def _mla_expand(p, g, wuk, wuv):
    t = p.shape[0]
    tm = _tile(t, 512)
    return pl.pallas_call(
        _mla_expand_kernel,
        grid=(t // tm,),
        in_specs=[pl.BlockSpec((tm, 256), lambda i: (i, _CKV // 256)),
                  pl.BlockSpec((tm, 128), lambda i: (i, _KR // 128)),
                  pl.BlockSpec((1, 256), lambda i: (0, 0)),
                  pl.BlockSpec((256, 512), lambda i: (0, 0)),
                  pl.BlockSpec((256, 512), lambda i: (0, 0))],
        out_specs=[pl.BlockSpec((MLA_HEADS, tm, 256), lambda i: (0, i, 0)),
                   pl.BlockSpec((MLA_HEADS, tm, 128), lambda i: (0, i, 0))],
        out_shape=[jax.ShapeDtypeStruct((MLA_HEADS, t, 256), bf16), jax.ShapeDtypeStruct((MLA_HEADS, t, 128), bf16)],
        compiler_params=_cparams("parallel"),
        name="mla_expand",
    )(p, p, g, wuk, wuv)


def _mla_kernel(*refs, with_lat):
    if with_lat:
        qn_ref, qr_ref, kc_ref, vc_ref, kl_ref, vl_ref, o_ref, qx_scr = refs
    else:
        qn_ref, qr_ref, kc_ref, vc_ref, _, o_ref, qx_scr = refs
    lane = lax.broadcasted_iota(i32, (qn_ref.shape[0], LANES), 1)
    for h in range(MLA_HEADS):
        slab = qr_ref[:, (h // 2) * 128:(h // 2 + 1) * 128].astype(f32)
        if h % 2:
            slab = pltpu.roll(slab, 64, 1)
        qx_scr[h, :, 0:128] = qn_ref[:, h * 128:(h + 1) * 128]
        qx_scr[h, :, 128:256] = jnp.where(lane < 64, slab, 0.0).astype(bf16)
    for h in range(MLA_HEADS):
        q = qx_scr[h]
        parts = [(_dot_nt(q, kc_ref[h]), vc_ref[h])]
        if with_lat:
            parts.append((_dot_nt(q, kl_ref[h]), vl_ref[h]))
        m = None
        for s, _ in parts:
            mx = jnp.max(s, axis=1, keepdims=True)
            m = mx if m is None else jnp.maximum(m, mx)
        den, acc = None, None
        for s, vv in parts:
            pr = jnp.exp2(s - m)
            sm = jnp.sum(pr, axis=1, keepdims=True)
            pv = _dot(pr.astype(bf16), vv)
            den = sm if den is None else den + sm
            acc = pv if acc is None else acc + pv
        o_ref[:, h * 128:(h + 1) * 128] = (acc / den).astype(bf16)


def _mla_attention(p, kx, v, n_batch, lat_len, ctx_len, compute_ctx):
    t = p.shape[0]
    tq = _tile(lat_len, 256)
    nlq = lat_len // tq
    cbase = n_batch * lat_len // ctx_len
    once = pl.Buffered(1)
    out = pl.pallas_call(
        functools.partial(_mla_kernel, with_lat=True),
        grid=(n_batch, nlq),
        in_specs=[pl.BlockSpec((tq, 512), lambda b, qi: (b * nlq + qi, _MQN // 512)),
                  pl.BlockSpec((tq, 256), lambda b, qi: (b * nlq + qi, _MQR // 256)),
                  pl.BlockSpec((MLA_HEADS, ctx_len, 256), lambda b, qi: (0, cbase + b, 0)),
                  pl.BlockSpec((MLA_HEADS, ctx_len, 128), lambda b, qi: (0, cbase + b, 0)),
                  pl.BlockSpec((MLA_HEADS, lat_len, 256), lambda b, qi: (0, b, 0), pipeline_mode=once),
                  pl.BlockSpec((MLA_HEADS, lat_len, 128), lambda b, qi: (0, b, 0), pipeline_mode=once)],
        out_specs=pl.BlockSpec((tq, 512), lambda b, qi: (b * nlq + qi, 0)),
        out_shape=jax.ShapeDtypeStruct((t, 512), bf16),
        scratch_shapes=[pltpu.VMEM((MLA_HEADS, tq, 256), bf16)],
        compiler_params=_cparams("parallel", "arbitrary"),
        name="mla_attn",
    )(p, p, kx, v, kx, v)
    if compute_ctx:
        out = pl.pallas_call(
            functools.partial(_mla_kernel, with_lat=False),
            grid=(n_batch,),
            in_specs=[pl.BlockSpec((ctx_len, 512), lambda b: (cbase + b, _MQN // 512)),
                      pl.BlockSpec((ctx_len, 256), lambda b: (cbase + b, _MQR // 256)),
                      pl.BlockSpec((MLA_HEADS, ctx_len, 256), lambda b: (0, cbase + b, 0)),
                      pl.BlockSpec((MLA_HEADS, ctx_len, 128), lambda b: (0, cbase + b, 0)),
                      pl.BlockSpec(memory_space=pl.ANY)],
            out_specs=pl.BlockSpec((ctx_len, 512), lambda b: (cbase + b, 0)),
            out_shape=jax.ShapeDtypeStruct((t, 512), bf16),
            scratch_shapes=[pltpu.VMEM((MLA_HEADS, ctx_len, 256), bf16)],
            input_output_aliases={4: 0},
            compiler_params=_cparams("parallel"),
            name="mla_ctx",
        )(p, p, kx, v, out)
    return out


def _swa_softmax_out(parts, sink_col, o_ref, g):
    m = sink_col
    for s, _ in parts:
        m = jnp.maximum(m, jnp.max(s, axis=1, keepdims=True))
    den = jnp.exp(sink_col - m)
    acc = None
    for s, vv in parts:
        pr = jnp.exp(s - m)
        den = den + jnp.sum(pr, axis=1, keepdims=True)
        pv = _dot(pr.astype(bf16), vv)
        acc = pv if acc is None else acc + pv
    o = acc / den
    nq = o.shape[0] // 3
    for j in range(3):
        o_ref[:, (3 * g + j) * 128:(3 * g + j + 1) * 128] = o[j * nq:(j + 1) * nq].astype(bf16)


def _sink_col(sink_ref, g, nq):
    row = lax.broadcasted_iota(i32, (3 * nq, 1), 0)
    return jnp.where(row < nq, sink_ref[3 * g], jnp.where(row < 2 * nq, sink_ref[3 * g + 1], sink_ref[3 * g + 2]))


def _swa_kernel(sink_ref, q_ref, kp_ref, ko_ref, kn_ref, vp_ref, vo_ref, vn_ref, kc_ref, vc_ref, o_ref):
    i = pl.program_id(1)
    nb = pl.num_programs(1)
    blk = q_ref.shape[0]
    q = q_ref[...]
    r = lax.broadcasted_iota(i32, (3 * blk, 3 * blk), 0) & (blk - 1)
    c = lax.broadcasted_iota(i32, (3 * blk, 3 * blk), 1)
    lo = jnp.where(i == 0, blk, 0)
    hi = jnp.where(i == nb - 1, 2 * blk, 3 * blk)
    d = c - r
    valid = (d >= 0) & (d <= 2 * SWA_WINDOW) & (c >= lo) & (c < hi)
    for g in range(SWA_KV_HEADS):
        hs = slice(g * 128, (g + 1) * 128)
        qg = jnp.concatenate([q[:, (3 * g + j) * 128:(3 * g + j + 1) * 128] for j in range(3)], axis=0)
        kloc = jnp.concatenate([kp_ref[:, hs], ko_ref[:, hs], kn_ref[:, hs]], axis=0)
        vloc = jnp.concatenate([vp_ref[:, hs], vo_ref[:, hs], vn_ref[:, hs]], axis=0)
        s_loc = jnp.where(valid, _dot_nt(qg, kloc), NEG_INF)
        s_ctx = _dot_nt(qg, kc_ref[:, hs])
        _swa_softmax_out([(s_ctx, vc_ref[:, hs]), (s_loc, vloc)], _sink_col(sink_ref, g, blk), o_ref, g)


def _swa_ctx_kernel(sink_ref, q_ref, kc_ref, vc_ref, prev_ref, o_ref):
    del prev_ref
    q = q_ref[...]
    nq = q.shape[0]
    for g in range(SWA_KV_HEADS):
        hs = slice(g * 128, (g + 1) * 128)
        qg = jnp.concatenate([q[:, (3 * g + j) * 128:(3 * g + j + 1) * 128] for j in range(3)], axis=0)
        s_ctx = _dot_nt(qg, kc_ref[:, hs])
        _swa_softmax_out([(s_ctx, vc_ref[:, hs])], _sink_col(sink_ref, g, nq), o_ref, g)


def _swa_attention(p, sink, n_batch, lat_len, ctx_len, compute_ctx):
    t = p.shape[0]
    blk = SWA_WINDOW
    nb = lat_len // blk
    cbase = n_batch * lat_len // ctx_len
    smem = pl.BlockSpec(memory_space=pltpu.SMEM)

    def kspec(col, off):
        return pl.BlockSpec((blk, 256), lambda b, i: (b * nb + jnp.clip(i + off, 0, nb - 1), col // 256))

    out = pl.pallas_call(
        _swa_kernel,
        grid=(n_batch, nb),
        in_specs=[smem,
                  pl.BlockSpec((blk, 768), lambda b, i: (b * nb + i, _SQ // 768)),
                  kspec(_SK, -1), kspec(_SK, 0), kspec(_SK, 1),
                  kspec(_SV, -1), kspec(_SV, 0), kspec(_SV, 1),
                  pl.BlockSpec((ctx_len, 256), lambda b, i: (cbase + b, _SK // 256)),
                  pl.BlockSpec((ctx_len, 256), lambda b, i: (cbase + b, _SV // 256))],
        out_specs=pl.BlockSpec((blk, 768), lambda b, i: (b * nb + i, 0)),
        out_shape=jax.ShapeDtypeStruct((t, 768), bf16),
        compiler_params=_cparams("parallel", "parallel"),
        name="swa_attn",
    )(sink, p, p, p, p, p, p, p, p, p)
    if compute_ctx:
        out = pl.pallas_call(
            _swa_ctx_kernel,
            grid=(n_batch,),
            in_specs=[smem,
                      pl.BlockSpec((ctx_len, 768), lambda b: (cbase + b, _SQ // 768)),
                      pl.BlockSpec((ctx_len, 256), lambda b: (cbase + b, _SK // 256)),
                      pl.BlockSpec((ctx_len, 256), lambda b: (cbase + b, _SV // 256)),
                      pl.BlockSpec(memory_space=pl.ANY)],
            out_specs=pl.BlockSpec((ctx_len, 768), lambda b: (cbase + b, 0)),
            out_shape=jax.ShapeDtypeStruct((t, 768), bf16),
            input_output_aliases={4: 0},
            compiler_params=_cparams("parallel"),
            name="swa_ctx",
        )(sink, p, p, p, out)
    return out


def _ret_kernel(lg_ref, q_ref, k_ref, v_ref, *rest, backward):
    if backward:
        yf_ref, gf_ref, gb_ref, o_ref, s_scr, d_scr, qd_scr, kd_scr, gc_scr = rest
    else:
        o_ref, s_scr, d_scr, qd_scr, kd_scr, gc_scr = rest
    i = pl.program_id(1)
    cc = RET_CHUNK
    direction = 1 if backward else 0

    @pl.when(i == 0)
    def _():
        s_scr[...] = jnp.zeros(s_scr.shape, f32)
        r = lax.broadcasted_iota(i32, (cc, cc), 0).astype(f32)
        c = lax.broadcasted_iota(i32, (cc, cc), 1).astype(f32)
        diff = (c - r) if backward else (r - c)
        for h in range(RET_HEADS):
            lgh = lg_ref[direction, h]
            d_scr[h] = jnp.where(diff >= 0, jnp.exp(lgh * jnp.maximum(diff, 0.0)), 0.0)
            qd_scr[h] = jnp.exp(lgh * ((cc - r) if backward else (r + 1.0)))
            kd_scr[h] = jnp.exp(lgh * (r if backward else (cc - 1.0 - r)))
            gc_scr[h] = jnp.exp(lgh * cc + jnp.zeros((cc, cc), f32))

    lane = lax.broadcasted_iota(i32, (cc, LANES), 1)
    for pair in range(RET_HEADS // 2):
        q2 = q_ref[:, pair * 128:(pair + 1) * 128].astype(f32)
        k2b = k_ref[:, pair * 128:(pair + 1) * 128]
        k2 = k2b.astype(f32)
        for sub in range(2):
            h = 2 * pair + sub
            hs = slice(h * 128, (h + 1) * 128)
            qa = jnp.where((lane < 64) if sub == 0 else (lane >= 64), q2, 0.0)
            s = _dot_nt(qa.astype(bf16), k2b)
            vh = v_ref[:, hs]
            y = _dot((s * d_scr[h]).astype(bf16), vh) + _dot((qa * qd_scr[h]).astype(bf16), s_scr[h].astype(bf16))
            s_scr[h] = gc_scr[h] * s_scr[h] + _dot_tn((k2 * kd_scr[h]).astype(bf16), vh)
            yn = _ln(y)
            if backward:
                o_ref[:, hs] = (_silu(gf_ref[:, hs].astype(f32)) * yf_ref[:, hs].astype(f32)
                                + _silu(gb_ref[:, hs].astype(f32)) * yn).astype(bf16)
            else:
                o_ref[:, hs] = yn.astype(bf16)


def _retention(p, lg, n_batch, lat_len, ctx_len):
    t = p.shape[0]
    cc = RET_CHUNK
    ncc, nlc = ctx_len // cc, lat_len // cc
    cbase = n_batch * nlc
    smem = pl.BlockSpec(memory_space=pltpu.SMEM)
    scratch = [pltpu.VMEM((RET_HEADS, cc, cc), f32) for _ in range(5)]

    def fwd_row(b, i):
        return jnp.where(i < ncc, cbase + b * ncc + i, b * nlc + (i - ncc))

    def bwd_row(b, i):
        return jnp.where(i < ncc, cbase + b * ncc + (ncc - 1 - i), b * nlc + (nlc - 1 - (i - ncc)))

    def specs(row):
        return [pl.BlockSpec((cc, 384), lambda b, i: (row(b, i), _RQ // 384)),
                pl.BlockSpec((cc, 384), lambda b, i: (row(b, i), _RK // 384)),
                pl.BlockSpec((cc, 768), lambda b, i: (row(b, i), _RV // 768))]

    yf = pl.pallas_call(
        functools.partial(_ret_kernel, backward=False),
        grid=(n_batch, ncc + nlc),
        in_specs=[smem] + specs(fwd_row),
        out_specs=pl.BlockSpec((cc, 768), lambda b, i: (fwd_row(b, i), 0)),
        out_shape=jax.ShapeDtypeStruct((t, 768), bf16),
        scratch_shapes=scratch,
        compiler_params=_cparams("parallel", "arbitrary"),
        name="ret_fwd",
    )(lg, p, p, p)
    return pl.pallas_call(
        functools.partial(_ret_kernel, backward=True),
        grid=(n_batch, ncc + nlc),
        in_specs=[smem] + specs(bwd_row) + [
            pl.BlockSpec((cc, 768), lambda b, i: (bwd_row(b, i), 0)),
            pl.BlockSpec((cc, 768), lambda b, i: (bwd_row(b, i), _GF // 768)),
            pl.BlockSpec((cc, 768), lambda b, i: (bwd_row(b, i), _GB // 768))],
        out_specs=pl.BlockSpec((cc, 768), lambda b, i: (bwd_row(b, i), 0)),
        out_shape=jax.ShapeDtypeStruct((t, 768), bf16),
        scratch_shapes=scratch,
        compiler_params=_cparams("parallel", "arbitrary"),
        name="ret_bwd",
    )(lg, p, p, p, yf, p, p)


def _store_token_rows(ref, val):
    tm, d = val.shape
    ru = d // LANES
    for s in range(ru):
        ref[pl.ds(s, tm, stride=ru), :] = val[:, s * LANES:(s + 1) * LANES]


def _load_token_rows(ref, tm):
    ru = ref.shape[0] // tm
    return jnp.concatenate([ref[pl.ds(s, tm, stride=ru), :] for s in range(ru)], axis=1)


def _outproj_kernel(a_ref, r_ref, m_ref, x_ref, mod_ref, w_ref, g_ref, b_ref, rw_ref, rb_ref,
                    xo_ref, h_ref, rout_ref, cnt_ref, carry_scr, *, alpha):
    @pl.when(pl.program_id(0) == 0)
    def _():
        carry_scr[...] = jnp.zeros(carry_scr.shape, f32)

    y = (_dot(a_ref[...], w_ref[0:768, :]) + _dot(r_ref[...], w_ref[768:1536, :])
         + _dot(m_ref[...], w_ref[1536:2048, :]))
    xn = _ln(alpha * x_ref[...] + mod_ref[0, 2:3, :] * y) * g_ref[...] + b_ref[...]
    xo_ref[...] = xn
    h = _ln(xn) * (1.0 + mod_ref[0, 4:5, :]) + mod_ref[0, 3:4, :]
    _store_token_rows(h_ref, h)
    rout_ref[...] = _route(_dot(h.astype(bf16), rw_ref[...]) + rb_ref[...], carry_scr)
    cnt_ref[...] = carry_scr[...]


def _outproj(a, r, m, xs, mod, w, g, b, rw, rb, n_rows, n_lat, lat_len, n_batch, alpha):
    d = xs.shape[1]
    ru = d // LANES
    tm = _tile(lat_len, 256)
    while (n_rows - n_lat) % tm:
        tm //= 2
    nlb, bpb = n_lat // tm, lat_len // tm

    def grp(i):
        return jnp.where(i < nlb, i // bpb, n_batch)

    row = lambda i: (i, 0)
    const = lambda i: (0, 0)
    return pl.pallas_call(
        functools.partial(_outproj_kernel, alpha=alpha),
        grid=(n_rows // tm,),
        in_specs=[pl.BlockSpec((tm, 768), row), pl.BlockSpec((tm, 768), row), pl.BlockSpec((tm, 512), row),
                  pl.BlockSpec((tm, d), row),
                  pl.BlockSpec((1, 6, d), lambda i: (grp(i), 0, 0)),
                  pl.BlockSpec(w.shape, const),
                  pl.BlockSpec((1, d), const), pl.BlockSpec((1, d), const),
                  pl.BlockSpec((d, LANES), const), pl.BlockSpec((1, LANES), const)],
        out_specs=[pl.BlockSpec((tm, d), row), pl.BlockSpec((tm * ru, LANES), row),
                   pl.BlockSpec((tm, LANES), row), pl.BlockSpec((8, LANES), const)],
        out_shape=[jax.ShapeDtypeStruct((n_rows, d), f32), jax.ShapeDtypeStruct((n_rows * ru, LANES), f32),
                   jax.ShapeDtypeStruct((n_rows, LANES), f32), jax.ShapeDtypeStruct((8, LANES), f32)],
        scratch_shapes=[pltpu.VMEM((8, LANES), f32)],
        compiler_params=_cparams("arbitrary"),
        name="outproj",
    )(a, r, m, xs, mod, w, g, b, rw, rb)


_ROUTE_LANE0 = N_GROUPS


def _route(logits, carry_scr):
    tm = logits.shape[0]
    lane = lax.broadcasted_iota(i32, (tm, LANES), 1)
    lane_f = lane.astype(f32)
    big = float(2 * LANES)
    gl = jnp.where(lane < N_GROUPS, logits, -jnp.inf)
    gmax = jnp.max(gl, axis=1, keepdims=True)
    gidx = jnp.min(jnp.where(gl == gmax, lane_f, big), axis=1, keepdims=True)
    p_group = 1.0 / jnp.sum(jnp.exp(gl - gmax), axis=1, keepdims=True)
    egroup = ((lane - _ROUTE_LANE0) >> 3).astype(f32)
    in_group = (lane >= _ROUTE_LANE0) & (lane < _ROUTE_LANE0 + N_EXPERTS) & (egroup == gidx)
    ev = jnp.where(in_group, logits, -jnp.inf)
    e1 = jnp.max(ev, axis=1, keepdims=True)
    i1 = jnp.min(jnp.where(ev == e1, lane_f, big), axis=1, keepdims=True)
    ev2 = jnp.where(lane_f == i1, -jnp.inf, ev)
    e2 = jnp.max(ev2, axis=1, keepdims=True)
    i2 = jnp.min(jnp.where(ev2 == e2, lane_f, big), axis=1, keepdims=True)
    tt = jnp.exp(e2 - e1)
    w1 = p_group / (1.0 + tt)
    w2 = p_group * tt / (1.0 + tt)

    hit1 = lane_f == i1
    hit2 = lane_f == i2
    onehot = jnp.where(hit1, 1.0, jnp.where(hit2, 1.0, 0.0))
    rr = lax.broadcasted_iota(i32, (tm, tm), 0)
    cc = lax.broadcasted_iota(i32, (tm, tm), 1)
    lower = jnp.where(cc < rr, 1.0, 0.0).astype(bf16)
    before = _dot(lower, onehot.astype(bf16)) + carry_scr[0:1, :]
    rank1 = jnp.sum(jnp.where(hit1, before, 0.0), axis=1, keepdims=True)
    rank2 = jnp.sum(jnp.where(hit2, before, 0.0), axis=1, keepdims=True)
    carry_scr[0:1, :] = carry_scr[0:1, :] + jnp.sum(onehot, axis=0, keepdims=True)

    return jnp.where(lane == 0, i1 - _ROUTE_LANE0, jnp.where(lane == 1, i2 - _ROUTE_LANE0, jnp.where(
        lane == 2, rank1, jnp.where(lane == 3, rank2, jnp.where(lane == 4, w1, jnp.where(lane == 5, w2, 0.0))))))


def _dispatch_kernel(cnt_ref, pstart_ref, padded_ref, dest_ref, h_ref, o_hbm, sem, *, ru):
    i = pl.program_id(0)
    nb = pl.num_programs(0) - 1
    tm = dest_ref.shape[2] // TOP_K

    def row_copy(src_row, dst_row):
        return pltpu.make_async_copy(h_ref.at[pl.ds(pl.multiple_of(src_row, ru), ru), :],
                                     o_hbm.at[pl.ds(pl.multiple_of(dst_row, ru), ru), :], sem)

    @pl.when(i < nb)
    def _():
        def issue(r, carry):
            for k in range(TOP_K):
                row_copy(r * ru, dest_ref[0, 0, TOP_K * r + k]).start()
            return carry

        lax.fori_loop(0, tm, issue, 0)
        for _ in range(TOP_K):
            pltpu.make_async_copy(h_ref, o_hbm.at[pl.ds(0, tm * ru), :], sem).wait()

    @pl.when(i == nb)
    def _():
        def per_expert(e, total):
            lo, hi = cnt_ref[e], padded_ref[e]

            def fill(r, carry):
                row_copy(0, (pstart_ref[e] + r) * ru).start()
                return carry

            lax.fori_loop(lo, hi, fill, 0)
            return total + (hi - lo)

        total = lax.fori_loop(0, N_EXPERTS, per_expert, 0)

        def drain(r, carry):
            row_copy(0, 0).wait()
            return carry

        lax.fori_loop(0, total, drain, 0)


def _dispatch(h3, dest_rows, counts, pstart, padded, n_rows, tm, ru):
    n = dest_rows.shape[0] // TOP_K
    nb = n // tm
    grid_spec = pltpu.PrefetchScalarGridSpec(
        num_scalar_prefetch=3,
        grid=(nb + 1,),
        in_specs=[pl.BlockSpec((1, 1, TOP_K * tm), lambda i, *_: (jnp.minimum(i, nb - 1), 0, 0),
                               memory_space=pltpu.SMEM),
                  pl.BlockSpec((tm * ru, LANES), lambda i, *_: (jnp.minimum(i, nb - 1), 0))],
        out_specs=pl.BlockSpec(memory_space=pl.ANY),
        scratch_shapes=[pltpu.SemaphoreType.DMA(())],
    )
    return pl.pallas_call(
        functools.partial(_dispatch_kernel, ru=ru),
        grid_spec=grid_spec,
        out_shape=jax.ShapeDtypeStruct((n_rows * ru, LANES), f32),
        compiler_params=_cparams("arbitrary"),
        name="moe_dispatch",
    )(counts, pstart, padded, dest_rows.reshape(nb, 1, TOP_K * tm), h3)


_CAST_ROWS = 256


def _expert_kernel(nused_ref, be_ref, first_ref, nxt_ref, x_ref, wg_hbm, wu_hbm, wd_hbm, y_ref,
                   wg_f, wu_f, wd_f, wg_b, wu_b, wd_b, sems, *, layer):
    i = pl.program_id(0)
    used = i < nused_ref[0]
    staged = ((wg_hbm, wg_f, wg_b), (wu_hbm, wu_f, wu_b), (wd_hbm, wd_f, wd_b))

    def fetch(e):
        return [pltpu.make_async_copy(hbm.at[layer, e], stage, sems.at[j]) for j, (hbm, stage, _) in enumerate(staged)]

    @pl.when(i == 0)
    def _():
        for cp in fetch(be_ref[0]):
            cp.start()

    @pl.when(used & (first_ref[i] == 1))
    def _():
        for cp in fetch(be_ref[i]):
            cp.wait()
        for _, stage, dst in staged:
            rows = stage.shape[0]
            step = min(_CAST_ROWS, rows)

            def cast(c, carry, stage=stage, dst=dst, step=step):
                sl = pl.ds(pl.multiple_of(c * step, step), step)
                dst[sl, :] = stage[sl, :].astype(bf16)
                return carry

            lax.fori_loop(0, rows // step, cast, 0)
        nxt = nxt_ref[be_ref[i]]

        @pl.when(nxt >= 0)
        def _():
            for cp in fetch(nxt):
                cp.start()

    @pl.when(used)
    def _():
        tm = x_ref.shape[0] // (wg_b.shape[0] // LANES)
        x = _load_token_rows(x_ref, tm).astype(bf16)
        act = (_silu(_dot(x, wg_b[...])) * _dot(x, wu_b[...])).astype(bf16)
        _store_token_rows(y_ref, _dot(act, wd_b[...]))


_EXPERT_VMEM_LIMIT = 60 * 1024 * 1024


def _experts(xs3, n_used, block_expert, first, nxt, wg, wu, wd, layer, rb, ru):
    d, hid = wg.shape[2], wg.shape[3]
    n_blocks = xs3.shape[0] // (rb * ru)

    def blk(i, nu, *_):
        return (jnp.minimum(i, nu[0] - 1), 0)

    grid_spec = pltpu.PrefetchScalarGridSpec(
        num_scalar_prefetch=4,
        grid=(n_blocks,),
        in_specs=[pl.BlockSpec((rb * ru, LANES), blk),
                  pl.BlockSpec(memory_space=pl.ANY), pl.BlockSpec(memory_space=pl.ANY),
                  pl.BlockSpec(memory_space=pl.ANY)],
        out_specs=pl.BlockSpec((rb * ru, LANES), blk),
        scratch_shapes=[pltpu.VMEM((d, hid), f32), pltpu.VMEM((d, hid), f32), pltpu.VMEM((hid, d), f32),
                        pltpu.VMEM((d, hid), bf16), pltpu.VMEM((d, hid), bf16), pltpu.VMEM((hid, d), bf16),
                        pltpu.SemaphoreType.DMA((3,))],
    )
    return pl.pallas_call(
        functools.partial(_expert_kernel, layer=layer),
        grid_spec=grid_spec,
        out_shape=jax.ShapeDtypeStruct(xs3.shape, f32),
        compiler_params=pltpu.CompilerParams(dimension_semantics=("arbitrary",), vmem_limit_bytes=_EXPERT_VMEM_LIMIT),
        name="moe_experts",
    )(n_used, block_expert, first, nxt, xs3, wg, wu, wd)


def _combine_kernel(dcur_ref, dnxt_ref, y_hbm, x_ref, gate_ref, mod_ref, g_ref, b_ref, o_ref, ybuf, sems, *, alpha, ru):
    i = pl.program_id(0)
    nb = pl.num_programs(0)
    tm = x_ref.shape[0]

    def issue(dref, slot):
        def body(r, carry):
            for k in range(TOP_K):
                pltpu.make_async_copy(y_hbm.at[pl.ds(pl.multiple_of(dref[0, 0, TOP_K * r + k], ru), ru), :],
                                      ybuf.at[slot, k, pl.ds(pl.multiple_of(r * ru, ru), ru), :], sems.at[slot]).start()
            return carry

        lax.fori_loop(0, tm, body, 0)

    @pl.when(i == 0)
    def _():
        issue(dcur_ref, 0)

    @pl.when(i + 1 < nb)
    def _():
        issue(dnxt_ref, (i + 1) % 2)

    slot = i % 2
    for k in range(TOP_K):
        pltpu.make_async_copy(y_hbm.at[pl.ds(0, tm * ru), :], ybuf.at[slot, k], sems.at[slot]).wait()
    gates = gate_ref[...]
    f = gates[:, 4:5] * _load_token_rows(ybuf.at[slot, 0], tm) + gates[:, 5:6] * _load_token_rows(ybuf.at[slot, 1], tm)
    o_ref[...] = _ln(alpha * x_ref[...] + mod_ref[0, 5:6, :] * f) * g_ref[...] + b_ref[...]


def _combine(y3, dest_rows, xs, rout, mod, g, b, n_lat, lat_len, n_batch, alpha, tm, ru):
    n, d = xs.shape
    nb = n // tm
    nlb, bpb = n_lat // tm, lat_len // tm

    def grp(i):
        return jnp.where(i < nlb, i // bpb, n_batch)

    dest3 = dest_rows.reshape(nb, 1, TOP_K * tm)
    const = lambda i: (0, 0)
    return pl.pallas_call(
        functools.partial(_combine_kernel, alpha=alpha, ru=ru),
        grid=(nb,),
        in_specs=[pl.BlockSpec((1, 1, TOP_K * tm), lambda i: (i, 0, 0), memory_space=pltpu.SMEM),
                  pl.BlockSpec((1, 1, TOP_K * tm), lambda i: (jnp.minimum(i + 1, nb - 1), 0, 0),
                               memory_space=pltpu.SMEM),
                  pl.BlockSpec(memory_space=pl.ANY),
                  pl.BlockSpec((tm, d), lambda i: (i, 0)),
                  pl.BlockSpec((tm, LANES), lambda i: (i, 0)),
                  pl.BlockSpec((1, 6, d), lambda i: (grp(i), 0, 0)),
                  pl.BlockSpec((1, d), const), pl.BlockSpec((1, d), const)],
        out_specs=pl.BlockSpec((tm, d), lambda i: (i, 0)),
        out_shape=jax.ShapeDtypeStruct((n, d), f32),
        scratch_shapes=[pltpu.VMEM((2, TOP_K, tm * ru, LANES), f32), pltpu.SemaphoreType.DMA((2,))],
        compiler_params=_cparams("arbitrary"),
        name="moe_combine",
    )(dest3, dest3, y3, xs, rout, mod, g, b)


def _moe(h3, rout, cnt, xs, wg, wu, wd, layer, mod, g, b, n_lat, lat_len, n_batch, alpha):
    n, d = xs.shape
    ru = d // LANES
    tm = _tile(lat_len, 256)
    while (n - n_lat) % tm:
        tm //= 2
    rb = tm
    eid = rout[:, 0:2].astype(i32)
    rank = rout[:, 2:4].astype(i32)
    counts = cnt[0, _ROUTE_LANE0:_ROUTE_LANE0 + N_EXPERTS].astype(i32)
    padded = (counts + rb - 1) // rb * rb
    pend = jnp.cumsum(padded)
    pstart = pend - padded
    dest_rows = ((pstart[eid] + rank) * ru).reshape(-1)
    n_blocks = TOP_K * n // rb + N_EXPERTS
    n_used = (pend[-1:] // rb).astype(i32)
    blk = jnp.arange(n_blocks, dtype=i32)
    ex = jnp.arange(N_EXPERTS, dtype=i32)
    block_expert = jnp.minimum(jnp.sum((pend[None, :] <= (blk * rb)[:, None]).astype(i32), axis=1), N_EXPERTS - 1)
    block_expert = jnp.where(blk < n_used[0], block_expert, block_expert[jnp.maximum(n_used[0] - 1, 0)])
    first = ((blk < n_used[0]) & ((blk == 0) | (block_expert != jnp.roll(block_expert, 1)))).astype(i32)
    later_active = (ex[None, :] > ex[:, None]) & (counts > 0)[None, :]
    nxt = jnp.min(jnp.where(later_active, ex[None, :], N_EXPERTS), axis=1)
    nxt = jnp.where(nxt == N_EXPERTS, -1, nxt).astype(i32)
    xs3 = _dispatch(h3, dest_rows, counts, pstart, padded, n_blocks * rb, tm, ru)
    y3 = _experts(xs3, n_used, block_expert, first, nxt, wg, wu, wd, layer, rb, ru)
    return _combine(y3, dest_rows, xs, rout, mod, g, b, n_lat, lat_len, n_batch, alpha, tm, ru)


def _permute_w_in(w):
    d = w.shape[0]
    o_sq, o_sk, o_sv, o_rq, o_rk, o_rv, o_gf, o_gb, o_mq, o_ckv, o_kr = (
        0, 768, 1024, 1280, 1664, 2048, 2816, 3584, 4352, 5120, 5376)
    mq = w[:, o_mq:o_mq + 768].reshape(d, MLA_HEADS, MLA_NOPE_DIM + MLA_ROPE_DIM)
    parts = [w[:, o_sq:o_sq + 768], w[:, o_rv:o_rv + 768], w[:, o_gf:o_gf + 768], w[:, o_gb:o_gb + 768],
             mq[:, :, :MLA_NOPE_DIM].reshape(d, 512), w[:, o_sk:o_sk + 256], w[:, o_sv:o_sv + 256],
             mq[:, :, MLA_NOPE_DIM:].reshape(d, 256), w[:, o_ckv:o_ckv + 256], w[:, o_rq:o_rq + 384],
             w[:, o_rk:o_rk + 384], w[:, o_kr:o_kr + 64], jnp.zeros((d, _NP - _KR - 64), w.dtype)]
    return jnp.concatenate(parts, axis=1).astype(bf16)


def kernel(x, c, ctx, c_ctx, w_ada, b_ada, w_in, swa_sink, ret_decay, mla_kv_norm, mla_w_uk, mla_w_uv, w_out, ln1_g, ln1_b, ln2_g, ln2_b, moe_w_group, moe_b_group, moe_w_expert, moe_b_expert, moe_w_gate, moe_w_up, moe_w_down):
    n_batch, lat_len, d = x.shape
    ctx_len = ctx.shape[1]
    depth = w_ada.shape[0]
    n_lat, n_ctx = n_batch * lat_len, n_batch * ctx_len
    alpha = (2 * depth) ** 0.25

    cc = jnp.zeros((8, d), f32).at[:n_batch].set(c).at[n_batch].set(c_ctx)
    mod_all = _ada(cc, w_ada, b_ada).reshape(depth, 8, 6, d)
    tab = _rope_tables(lat_len, ctx_len, _inproj_rows(lat_len, n_ctx))
    xs = jnp.concatenate([x.reshape(n_lat, d), ctx.reshape(n_ctx, d)], axis=0)

    for l in range(depth):
        ctx_out = l < depth - 1
        mod = mod_all[l]
        p = _inproj(xs, mod, tab, _permute_w_in(w_in[l]), n_lat, lat_len, ctx_len, n_batch)
        kx, v = _mla_expand(p, mla_kv_norm[l][None, :], mla_w_uk[l].astype(bf16), mla_w_uv[l].astype(bf16))
        m = _mla_attention(p, kx, v, n_batch, lat_len, ctx_len, ctx_out)
        a = _swa_attention(p, swa_sink[l], n_batch, lat_len, ctx_len, ctx_out)
        lg = jnp.log1p(-jnp.exp2(-ret_decay[l].astype(f32)))
        r = _retention(p, lg, n_batch, lat_len, ctx_len)
        n_rows = n_lat + n_ctx if ctx_out else n_lat
        rout_w = jnp.zeros((d, LANES), f32).at[:, :N_GROUPS].set(moe_w_group[l]).at[
            :, _ROUTE_LANE0:_ROUTE_LANE0 + N_EXPERTS].set(moe_w_expert[l]).astype(bf16)
        rout_b = jnp.zeros((1, LANES), f32).at[0, :N_GROUPS].set(moe_b_group[l]).at[
            0, _ROUTE_LANE0:_ROUTE_LANE0 + N_EXPERTS].set(moe_b_expert[l])
        xs, h3, rout, cnt = _outproj(a, r, m, xs, mod, w_out[l].astype(bf16), ln1_g[l][None, :], ln1_b[l][None, :],
                                     rout_w, rout_b, n_rows, n_lat, lat_len, n_batch, alpha)
        xs = _moe(h3, rout, cnt, xs, moe_w_gate, moe_w_up, moe_w_down, l, mod, ln2_g[l][None, :], ln2_b[l][None, :],
                  n_lat, lat_len, n_batch, alpha)
    return xs[:n_lat].reshape(n_batch, lat_len, d)
```

```python
import functools

import jax
import jax.numpy as jnp
from jax import lax
from jax.experimental import pallas as pl
from jax.experimental.pallas import tpu as pltpu

f32 = jnp.float32
bf16 = jnp.bfloat16
i32 = jnp.int32

GRID_W = 64
SWA_HEADS, SWA_KV_HEADS, SWA_HEAD_DIM, SWA_WINDOW = 6, 2, 128, 128
RET_HEADS, RET_QK_DIM, RET_V_DIM, RET_CHUNK = 6, 64, 128, 128
MLA_HEADS, MLA_NOPE_DIM, MLA_ROPE_DIM, MLA_V_DIM, MLA_KV_RANK = 4, 128, 64, 128, 256
N_GROUPS, EXPERTS_PER_GROUP, TOP_K = 4, 8, 2
N_EXPERTS = N_GROUPS * EXPERTS_PER_GROUP
ROPE_BASE = 10000.0
NORM_EPS = 1e-6
NEG_INF = -1e30
LANES = 128

_SQ, _RV, _GF, _GB, _MQN, _SK, _SV, _MQR, _CKV, _RQ, _RK, _KR = (
    0, 768, 1536, 2304, 3072, 3584, 3840, 4096, 4352, 4608, 4992, 5376)
_NP = 5504
_LOG2E = 1.4426950408889634
_SWA_SCALE = SWA_HEAD_DIM ** -0.5
_RET_SCALE = RET_QK_DIM ** -0.5
_MLA_SCALE = (MLA_NOPE_DIM + MLA_ROPE_DIM) ** -0.5 * _LOG2E
_SEGMENTS = ((_SQ, 768, 0, _SWA_SCALE), (_RV, 768, None, 1.0), (_GF, 768, None, 1.0), (_GB, 768, None, 1.0),
             (_MQN, 512, None, _MLA_SCALE), (_SK, 256, 0, 1.0), (_SV, 256, None, 1.0), (_MQR, 256, 2, _MLA_SCALE),
             (_CKV, 256, None, 1.0), (_RQ, 384, 1, _RET_SCALE), (_RK, 384, 1, 1.0), (_KR, 128, 2, 1.0))

_VMEM_LIMIT = 48 * 1024 * 1024


def _cparams(*sem):
    return pltpu.CompilerParams(dimension_semantics=sem, vmem_limit_bytes=_VMEM_LIMIT)


def _tile(n, pref):
    t = min(n, pref)
    while n % t:
        t //= 2
    return t


def _ln(x):
    mu = jnp.mean(x, axis=-1, keepdims=True)
    xc = x - mu
    var = jnp.mean(xc * xc, axis=-1, keepdims=True)
    return xc * lax.rsqrt(var + NORM_EPS)


def _silu(x):
    return x / (1.0 + jnp.exp(-x))


def _dot(a, b):
    return jnp.dot(a, b, preferred_element_type=f32)


def _dot_nt(a, b):
    return lax.dot_general(a, b, (((1,), (1,)), ((), ())), preferred_element_type=f32)


def _dot_tn(a, b):
    return lax.dot_general(a, b, (((0,), (0,)), ((), ())), preferred_element_type=f32)


def _ada_kernel(c_ref, w_ref, b_ref, o_ref):
    s = _silu(c_ref[...]).astype(bf16)
    o_ref[0] = _dot(s, w_ref[0].astype(bf16)) + b_ref[0]


def _ada(cc, w_ada, b_ada):
    depth, d, n = w_ada.shape
    tn = _tile(n, 1024)
    return pl.pallas_call(
        _ada_kernel,
        grid=(depth, n // tn),
        in_specs=[pl.BlockSpec((8, d), lambda l, j: (0, 0)),
                  pl.BlockSpec((1, d, tn), lambda l, j: (l, 0, j)),
                  pl.BlockSpec((1, 1, tn), lambda l, j: (l, 0, j))],
        out_specs=pl.BlockSpec((1, 8, tn), lambda l, j: (l, 0, j)),
        out_shape=jax.ShapeDtypeStruct((depth, 8, n), f32),
        compiler_params=_cparams("parallel", "parallel"),
        name="ada",
    )(cc, w_ada, b_ada.reshape(depth, 1, n))


_INPROJ_CHUNK = 512


def _inproj_kernel(xl_ref, xc_ref, mod_ref, tab_ref, w_ref, o_ref, *, nlb):
    tm = xl_ref.shape[0]
    h = (_ln(_stream_block(xl_ref, xc_ref, nlb)) * (1.0 + mod_ref[0, 1:2, :]) + mod_ref[0, 0:1, :]).astype(bf16)
    lane = lax.broadcasted_iota(i32, (tm, LANES), 1)
    first = {32: (lane & 63) < 32, 16: (lane & 31) < 16}
    slab_kind = {}
    for off, width, typ, scale in _SEGMENTS:
        for k in range(off // LANES, (off + width) // LANES):
            slab_kind[k] = (typ, scale)
    for c0 in range(0, _NP, _INPROJ_CHUNK):
        c1 = min(c0 + _INPROJ_CHUNK, _NP)
        acc = _dot(h, w_ref[:, c0:c1])
        for k in range(c0 // LANES, c1 // LANES):
            typ, scale = slab_kind[k]
            xk = acc[:, k * LANES - c0:(k + 1) * LANES - c0]
            if scale != 1.0:
                xk = xk * scale
            if typ is not None:
                half = 16 if typ == 2 else 32
                partner = jnp.where(first[half], pltpu.roll(xk, LANES - half, 1), pltpu.roll(xk, half, 1))
                xk = xk * tab_ref[typ, 0] + partner * tab_ref[typ, 1]
            o_ref[:, k * LANES:(k + 1) * LANES] = xk.astype(bf16)


def _inproj_rows(lat_len, n_ctx):
    tm = _tile(lat_len, 256)
    while n_ctx % tm:
        tm //= 2
    return tm


def _inproj(x_lat, x_ctx, mod, tab, w, n_lat, lat_len, ctx_len, n_batch):
    d = x_lat.shape[1]
    t = n_lat + n_batch * ctx_len
    tm = _inproj_rows(lat_len, t - n_lat)
    nlb, bpb = n_lat // tm, lat_len // tm
    ctx_blocks = max(1, ctx_len // tm)

    def grp(i):
        return jnp.where(i < nlb, i // bpb, n_batch)

    def posblk(i):
        return jnp.where(i < nlb, i % bpb, bpb + (i - nlb) % ctx_blocks)

    return pl.pallas_call(
        functools.partial(_inproj_kernel, nlb=nlb),
        grid=(t // tm,),
        in_specs=_stream_specs(x_lat, x_ctx, tm, n_lat) + [
                  pl.BlockSpec((1, 6, d), lambda i: (grp(i), 0, 0)),
                  pl.BlockSpec((3, 2, tm, LANES), lambda i: (0, 0, posblk(i), 0)),
                  pl.BlockSpec((d, _NP), lambda i: (0, 0), pipeline_mode=pl.Buffered(1))],
        out_specs=pl.BlockSpec((tm, _NP), lambda i: (i, 0)),
        out_shape=jax.ShapeDtypeStruct((t, _NP), bf16),
        compiler_params=_cparams("parallel"),
        name="inproj",
    )(x_lat, x_ctx, mod, tab, w)


def _rope_tables(lat_len, ctx_len, tm):
    lane = jnp.arange(LANES)
    t = jnp.arange(lat_len)
    rows = (t // GRID_W).astype(f32)
    cols = (t % GRID_W).astype(f32)

    def freq(half):
        return ROPE_BASE ** (-jnp.arange(half, dtype=f32) / half)

    def table(pos, inv, sign):
        ang = pos * inv[None, :]
        return jnp.stack([jnp.cos(ang), jnp.sin(ang) * sign[None, :]])

    def ident(n):
        return jnp.stack([jnp.ones((n, LANES), f32), jnp.zeros((n, LANES), f32)])

    sign32 = jnp.where((lane % 64) < 32, -1.0, 1.0).astype(f32)
    sign16 = jnp.where((lane % 32) < 16, -1.0, 1.0).astype(f32)
    inv32 = freq(32)[lane % 32]
    inv16 = freq(16)[lane % 16]
    pos_swa = jnp.where((lane // 64)[None, :] == 0, rows[:, None], cols[:, None])
    pos_mla = jnp.where(((lane % 64) // 32)[None, :] == 0, rows[:, None], cols[:, None])
    rep = max(1, tm // ctx_len)
    nc = rep * ctx_len
    pos_ret_l = jnp.broadcast_to((ctx_len + t).astype(f32)[:, None], (lat_len, LANES))
    pos_ret_c = jnp.broadcast_to(jnp.tile(jnp.arange(ctx_len), rep).astype(f32)[:, None], (nc, LANES))
    t0 = jnp.concatenate([table(pos_swa, inv32, sign32), ident(nc)], axis=1)
    t1 = jnp.concatenate([table(pos_ret_l, inv32, sign32), table(pos_ret_c, inv32, sign32)], axis=1)
    t2 = jnp.concatenate([table(pos_mla, inv16, sign16), ident(nc)], axis=1)
    return jnp.stack([t0, t1, t2])


def _mla_expand_kernel(ckv_ref, kr_ref, g_ref, wuk_ref, wuv_ref, k_ref, v_ref):
    c = ckv_ref[...].astype(f32)
    cn = (c * lax.rsqrt(jnp.mean(c * c, axis=-1, keepdims=True) + NORM_EPS) * g_ref[...]).astype(bf16)
    kn = _dot(cn, wuk_ref[...]).astype(bf16)
    vv = _dot(cn, wuv_ref[...]).astype(bf16)
    kr = kr_ref[...]
    for h in range(MLA_HEADS):
        k_ref[h, :, 0:128] = kn[:, h * 128:(h + 1) * 128]
        k_ref[h, :, 128:256] = kr
        v_ref[h] = vv[:, h * 128:(h + 1) * 128]


def _mla_expand(p, g, wuk, wuv):
    t = p.shape[0]
    tm = _tile(t, 512)
    return pl.pallas_call(
        _mla_expand_kernel,
        grid=(t // tm,),
        in_specs=[pl.BlockSpec((tm, 256), lambda i: (i, _CKV // 256)),
                  pl.BlockSpec((tm, 128), lambda i: (i, _KR // 128)),
                  pl.BlockSpec((1, 256), lambda i: (0, 0)),
                  pl.BlockSpec((256, 512), lambda i: (0, 0)),
                  pl.BlockSpec((256, 512), lambda i: (0, 0))],
        out_specs=[pl.BlockSpec((MLA_HEADS, tm, 256), lambda i: (0, i, 0)),
                   pl.BlockSpec((MLA_HEADS, tm, 128), lambda i: (0, i, 0))],
        out_shape=[jax.ShapeDtypeStruct((MLA_HEADS, t, 256), bf16), jax.ShapeDtypeStruct((MLA_HEADS, t, 128), bf16)],
        compiler_params=_cparams("parallel"),
        name="mla_expand",
    )(p, p, g, wuk, wuv)


def _mla_kernel(*refs, with_lat):
    if with_lat:
        qn_ref, qr_ref, kc_ref, vc_ref, kl_ref, vl_ref, o_ref, qx_scr = refs
    else:
        qn_ref, qr_ref, kc_ref, vc_ref, _, o_ref, qx_scr = refs
    lane = lax.broadcasted_iota(i32, (qn_ref.shape[0], LANES), 1)
    for h in range(MLA_HEADS):
        slab = qr_ref[:, (h // 2) * 128:(h // 2 + 1) * 128].astype(f32)
        if h % 2:
            slab = pltpu.roll(slab, 64, 1)
        qx_scr[h, :, 0:128] = qn_ref[:, h * 128:(h + 1) * 128]
        qx_scr[h, :, 128:256] = jnp.where(lane < 64, slab, 0.0).astype(bf16)
    for h in range(MLA_HEADS):
        q = qx_scr[h]
        parts = [(_dot_nt(q, kc_ref[h]), vc_ref[h])]
        if with_lat:
            parts.append((_dot_nt(q, kl_ref[h]), vl_ref[h]))
        m = None
        for s, _ in parts:
            mx = jnp.max(s, axis=1, keepdims=True)
            m = mx if m is None else jnp.maximum(m, mx)
        den, acc = None, None
        for s, vv in parts:
            pr = jnp.exp2(s - m)
            sm = jnp.sum(pr, axis=1, keepdims=True)
            pv = _dot(pr.astype(bf16), vv)
            den = sm if den is None else den + sm
            acc = pv if acc is None else acc + pv
        o_ref[:, h * 128:(h + 1) * 128] = (acc / den).astype(bf16)


def _mla_attention(p, kx, v, n_batch, lat_len, ctx_len, compute_ctx):
    t = p.shape[0]
    tq = _tile(lat_len, 256)
    nlq = lat_len // tq
    cbase = n_batch * lat_len // ctx_len
    once = pl.Buffered(1)
    out = pl.pallas_call(
        functools.partial(_mla_kernel, with_lat=True),
        grid=(n_batch, nlq),
        in_specs=[pl.BlockSpec((tq, 512), lambda b, qi: (b * nlq + qi, _MQN // 512)),
                  pl.BlockSpec((tq, 256), lambda b, qi: (b * nlq + qi, _MQR // 256)),
                  pl.BlockSpec((MLA_HEADS, ctx_len, 256), lambda b, qi: (0, cbase + b, 0)),
                  pl.BlockSpec((MLA_HEADS, ctx_len, 128), lambda b, qi: (0, cbase + b, 0)),
                  pl.BlockSpec((MLA_HEADS, lat_len, 256), lambda b, qi: (0, b, 0), pipeline_mode=once),
                  pl.BlockSpec((MLA_HEADS, lat_len, 128), lambda b, qi: (0, b, 0), pipeline_mode=once)],
        out_specs=pl.BlockSpec((tq, 512), lambda b, qi: (b * nlq + qi, 0)),
        out_shape=jax.ShapeDtypeStruct((t, 512), bf16),
        scratch_shapes=[pltpu.VMEM((MLA_HEADS, tq, 256), bf16)],
        compiler_params=_cparams("parallel", "arbitrary"),
        name="mla_attn",
    )(p, p, kx, v, kx, v)
    if compute_ctx:
        out = pl.pallas_call(
            functools.partial(_mla_kernel, with_lat=False),
            grid=(n_batch,),
            in_specs=[pl.BlockSpec((ctx_len, 512), lambda b: (cbase + b, _MQN // 512)),
                      pl.BlockSpec((ctx_len, 256), lambda b: (cbase + b, _MQR // 256)),
                      pl.BlockSpec((MLA_HEADS, ctx_len, 256), lambda b: (0, cbase + b, 0)),
                      pl.BlockSpec((MLA_HEADS, ctx_len, 128), lambda b: (0, cbase + b, 0)),
                      pl.BlockSpec(memory_space=pl.ANY)],
            out_specs=pl.BlockSpec((ctx_len, 512), lambda b: (cbase + b, 0)),
            out_shape=jax.ShapeDtypeStruct((t, 512), bf16),
            scratch_shapes=[pltpu.VMEM((MLA_HEADS, ctx_len, 256), bf16)],
            input_output_aliases={4: 0},
            compiler_params=_cparams("parallel"),
            name="mla_ctx",
        )(p, p, kx, v, out)
    return out


def _swa_softmax_out(parts, sink_col, o_ref, g):
    m = sink_col
    for s, _ in parts:
        m = jnp.maximum(m, jnp.max(s, axis=1, keepdims=True))
    den = jnp.exp(sink_col - m)
    acc = None
    for s, vv in parts:
        pr = jnp.exp(s - m)
        den = den + jnp.sum(pr, axis=1, keepdims=True)
        pv = _dot(pr.astype(bf16), vv)
        acc = pv if acc is None else acc + pv
    o = acc / den
    nq = o.shape[0] // 3
    for j in range(3):
        o_ref[:, (3 * g + j) * 128:(3 * g + j + 1) * 128] = o[j * nq:(j + 1) * nq].astype(bf16)


def _sink_col(sink_ref, g, nq):
    row = lax.broadcasted_iota(i32, (3 * nq, 1), 0)
    return jnp.where(row < nq, sink_ref[3 * g], jnp.where(row < 2 * nq, sink_ref[3 * g + 1], sink_ref[3 * g + 2]))


def _swa_kernel(sink_ref, q_ref, kp_ref, ko_ref, kn_ref, vp_ref, vo_ref, vn_ref, kc_ref, vc_ref, o_ref):
    i = pl.program_id(1)
    nb = pl.num_programs(1)
    blk = q_ref.shape[0]
    q = q_ref[...]
    r = lax.broadcasted_iota(i32, (3 * blk, 3 * blk), 0) & (blk - 1)
    c = lax.broadcasted_iota(i32, (3 * blk, 3 * blk), 1)
    lo = jnp.where(i == 0, blk, 0)
    hi = jnp.where(i == nb - 1, 2 * blk, 3 * blk)
    d = c - r
    valid = (d >= 0) & (d <= 2 * SWA_WINDOW) & (c >= lo) & (c < hi)
    for g in range(SWA_KV_HEADS):
        hs = slice(g * 128, (g + 1) * 128)
        qg = jnp.concatenate([q[:, (3 * g + j) * 128:(3 * g + j + 1) * 128] for j in range(3)], axis=0)
        kloc = jnp.concatenate([kp_ref[:, hs], ko_ref[:, hs], kn_ref[:, hs]], axis=0)
        vloc = jnp.concatenate([vp_ref[:, hs], vo_ref[:, hs], vn_ref[:, hs]], axis=0)
        s_loc = jnp.where(valid, _dot_nt(qg, kloc), NEG_INF)
        s_ctx = _dot_nt(qg, kc_ref[:, hs])
        _swa_softmax_out([(s_ctx, vc_ref[:, hs]), (s_loc, vloc)], _sink_col(sink_ref, g, blk), o_ref, g)


def _swa_ctx_kernel(sink_ref, q_ref, kc_ref, vc_ref, prev_ref, o_ref):
    del prev_ref
    q = q_ref[...]
    nq = q.shape[0]
    for g in range(SWA_KV_HEADS):
        hs = slice(g * 128, (g + 1) * 128)
        qg = jnp.concatenate([q[:, (3 * g + j) * 128:(3 * g + j + 1) * 128] for j in range(3)], axis=0)
        s_ctx = _dot_nt(qg, kc_ref[:, hs])
        _swa_softmax_out([(s_ctx, vc_ref[:, hs])], _sink_col(sink_ref, g, nq), o_ref, g)


def _swa_attention(p, sink, n_batch, lat_len, ctx_len, compute_ctx):
    t = p.shape[0]
    blk = SWA_WINDOW
    nb = lat_len // blk
    cbase = n_batch * lat_len // ctx_len
    smem = pl.BlockSpec(memory_space=pltpu.SMEM)

    def kspec(col, off):
        return pl.BlockSpec((blk, 256), lambda b, i: (b * nb + jnp.clip(i + off, 0, nb - 1), col // 256))

    out = pl.pallas_call(
        _swa_kernel,
        grid=(n_batch, nb),
        in_specs=[smem,
                  pl.BlockSpec((blk, 768), lambda b, i: (b * nb + i, _SQ // 768)),
                  kspec(_SK, -1), kspec(_SK, 0), kspec(_SK, 1),
                  kspec(_SV, -1), kspec(_SV, 0), kspec(_SV, 1),
                  pl.BlockSpec((ctx_len, 256), lambda b, i: (cbase + b, _SK // 256)),
                  pl.BlockSpec((ctx_len, 256), lambda b, i: (cbase + b, _SV // 256))],
        out_specs=pl.BlockSpec((blk, 768), lambda b, i: (b * nb + i, 0)),
        out_shape=jax.ShapeDtypeStruct((t, 768), bf16),
        compiler_params=_cparams("parallel", "parallel"),
        name="swa_attn",
    )(sink, p, p, p, p, p, p, p, p, p)
    if compute_ctx:
        out = pl.pallas_call(
            _swa_ctx_kernel,
            grid=(n_batch,),
            in_specs=[smem,
                      pl.BlockSpec((ctx_len, 768), lambda b: (cbase + b, _SQ // 768)),
                      pl.BlockSpec((ctx_len, 256), lambda b: (cbase + b, _SK // 256)),
                      pl.BlockSpec((ctx_len, 256), lambda b: (cbase + b, _SV // 256)),
                      pl.BlockSpec(memory_space=pl.ANY)],
            out_specs=pl.BlockSpec((ctx_len, 768), lambda b: (cbase + b, 0)),
            out_shape=jax.ShapeDtypeStruct((t, 768), bf16),
            input_output_aliases={4: 0},
            compiler_params=_cparams("parallel"),
            name="swa_ctx",
        )(sink, p, p, p, out)
    return out


def _ret_kernel(lg_ref, qf_ref, kf_ref, vf_ref, qb_ref, kb_ref, vb_ref, of_ref, ob_ref,
                s_scr, d_scr, qd_scr, kd_scr, gc_scr):
    i = pl.program_id(1)
    cc = RET_CHUNK

    @pl.when(i == 0)
    def _():
        s_scr[...] = jnp.zeros(s_scr.shape, f32)
        r = lax.broadcasted_iota(i32, (cc, cc), 0).astype(f32)
        c = lax.broadcasted_iota(i32, (cc, cc), 1).astype(f32)
        for direction in range(2):
            backward = direction == 1
            diff = (c - r) if backward else (r - c)
            for h in range(RET_HEADS):
                lgh = lg_ref[direction, h]
                j = direction * RET_HEADS + h
                d_scr[j] = jnp.where(diff >= 0, jnp.exp(lgh * jnp.maximum(diff, 0.0)), 0.0)
                qd_scr[j] = jnp.exp(lgh * ((cc - r) if backward else (r + 1.0)))
                kd_scr[j] = jnp.exp(lgh * (r if backward else (cc - 1.0 - r)))
                gc_scr[j] = jnp.exp(lgh * cc + jnp.zeros((cc, cc), f32))

    lane = lax.broadcasted_iota(i32, (cc, LANES), 1)
    for pair in range(RET_HEADS // 2):
        for direction, (q_ref, k_ref, v_ref, o_ref) in enumerate(
                ((qf_ref, kf_ref, vf_ref, of_ref), (qb_ref, kb_ref, vb_ref, ob_ref))):
            q2 = q_ref[:, pair * 128:(pair + 1) * 128].astype(f32)
            k2b = k_ref[:, pair * 128:(pair + 1) * 128]
            k2 = k2b.astype(f32)
            for sub in range(2):
                h = 2 * pair + sub
                j = direction * RET_HEADS + h
                hs = slice(h * 128, (h + 1) * 128)
                qa = jnp.where((lane < 64) if sub == 0 else (lane >= 64), q2, 0.0)
                s = _dot_nt(qa.astype(bf16), k2b)
                vh = v_ref[:, hs]
                y = (_dot((s * d_scr[j]).astype(bf16), vh)
                     + _dot((qa * qd_scr[j]).astype(bf16), s_scr[j].astype(bf16)))
                s_scr[j] = gc_scr[j] * s_scr[j] + _dot_tn((k2 * kd_scr[j]).astype(bf16), vh)
                o_ref[:, hs] = _ln(y).astype(bf16)


def _retention(p, lg, n_batch, lat_len, ctx_len):
    t = p.shape[0]
    cc = RET_CHUNK
    ncc, nlc = ctx_len // cc, lat_len // cc
    cbase = n_batch * nlc
    smem = pl.BlockSpec(memory_space=pltpu.SMEM)
    scratch = [pltpu.VMEM((2 * RET_HEADS, cc, cc), f32) for _ in range(5)]

    def fwd_row(b, i):
        return jnp.where(i < ncc, cbase + b * ncc + i, b * nlc + (i - ncc))

    def bwd_row(b, i):
        return jnp.where(i < ncc, cbase + b * ncc + (ncc - 1 - i), b * nlc + (nlc - 1 - (i - ncc)))

    def specs(row):
        return [pl.BlockSpec((cc, 384), lambda b, i: (row(b, i), _RQ // 384)),
                pl.BlockSpec((cc, 384), lambda b, i: (row(b, i), _RK // 384)),
                pl.BlockSpec((cc, 768), lambda b, i: (row(b, i), _RV // 768))]

    return pl.pallas_call(
        _ret_kernel,
        grid=(n_batch, ncc + nlc),
        in_specs=[smem] + specs(fwd_row) + specs(bwd_row),
        out_specs=[pl.BlockSpec((cc, 768), lambda b, i: (fwd_row(b, i), 0)),
                   pl.BlockSpec((cc, 768), lambda b, i: (bwd_row(b, i), 0))],
        out_shape=[jax.ShapeDtypeStruct((t, 768), bf16), jax.ShapeDtypeStruct((t, 768), bf16)],
        scratch_shapes=scratch,
        compiler_params=_cparams("parallel", "arbitrary"),
        name="retention",
    )(lg, p, p, p, p, p, p)


def _stream_specs(x_lat, x_ctx, tm, n_lat):
    d = x_lat.shape[1]
    nlb = n_lat // tm
    ctx_first = nlb if x_ctx is x_lat else 0
    return [pl.BlockSpec((tm, d), lambda i: (jnp.minimum(i, nlb - 1), 0)),
            pl.BlockSpec((tm, d), lambda i: (jnp.maximum(i - nlb, 0) + ctx_first, 0))]


def _stream_block(xl_ref, xc_ref, nlb):
    return jnp.where(pl.program_id(0) < nlb, xl_ref[...], xc_ref[...])


def _outproj_kernel(a_ref, yf_ref, yb_ref, gf_ref, gb_ref, m_ref, xl_ref, xc_ref, mod_ref, w_ref, g_ref, b_ref,
                    rw_ref, rb_ref, xo_ref, h_ref, rout_ref, cnt_ref, carry_scr, *, alpha, nlb):
    @pl.when(pl.program_id(0) == 0)
    def _():
        carry_scr[...] = jnp.zeros(carry_scr.shape, f32)

    ret = (_silu(gf_ref[...].astype(f32)) * yf_ref[...].astype(f32)
           + _silu(gb_ref[...].astype(f32)) * yb_ref[...].astype(f32)).astype(bf16)
    y = (_dot(a_ref[...], w_ref[0:768, :]) + _dot(ret, w_ref[768:1536, :]) + _dot(m_ref[...], w_ref[1536:2048, :]))
    xn = _ln(alpha * _stream_block(xl_ref, xc_ref, nlb) + mod_ref[0, 2:3, :] * y) * g_ref[...] + b_ref[...]
    xo_ref[...] = xn
    h = _ln(xn) * (1.0 + mod_ref[0, 4:5, :]) + mod_ref[0, 3:4, :]
    h_ref[...] = h
    rout_ref[...] = _route(_dot(h.astype(bf16), rw_ref[...]) + rb_ref[...], carry_scr)
    cnt_ref[...] = carry_scr[...]


def _outproj(a, yf, yb, p, m, x_lat, x_ctx, mod, w, g, b, rw, rb, n_rows, n_lat, lat_len, n_batch, alpha):
    d = x_lat.shape[1]
    tm = _tile(lat_len, 256)
    while (n_rows - n_lat) % tm:
        tm //= 2
    nlb, bpb = n_lat // tm, lat_len // tm

    def grp(i):
        return jnp.where(i < nlb, i // bpb, n_batch)

    row = lambda i: (i, 0)
    const = lambda i: (0, 0)
    return pl.pallas_call(
        functools.partial(_outproj_kernel, alpha=alpha, nlb=nlb),
        grid=(n_rows // tm,),
        in_specs=[pl.BlockSpec((tm, 768), row), pl.BlockSpec((tm, 768), row), pl.BlockSpec((tm, 768), row),
                  pl.BlockSpec((tm, 768), lambda i: (i, _GF // 768)), pl.BlockSpec((tm, 768), lambda i: (i, _GB // 768)),
                  pl.BlockSpec((tm, 512), row)] + _stream_specs(x_lat, x_ctx, tm, n_lat) + [
                  pl.BlockSpec((1, 6, d), lambda i: (grp(i), 0, 0)),
                  pl.BlockSpec(w.shape, const),
                  pl.BlockSpec((1, d), const), pl.BlockSpec((1, d), const),
                  pl.BlockSpec((d, LANES), const), pl.BlockSpec((1, LANES), const)],
        out_specs=[pl.BlockSpec((tm, d), row), pl.BlockSpec((tm, d), row),
                   pl.BlockSpec((tm, LANES), row), pl.BlockSpec((8, LANES), const)],
        out_shape=[jax.ShapeDtypeStruct((n_rows, d), f32), jax.ShapeDtypeStruct((n_rows, d), f32),
                   jax.ShapeDtypeStruct((n_rows, LANES), f32), jax.ShapeDtypeStruct((8, LANES), f32)],
        scratch_shapes=[pltpu.VMEM((8, LANES), f32)],
        compiler_params=_cparams("arbitrary"),
        name="outproj",
    )(a, yf, yb, p, p, m, x_lat, x_ctx, mod, w, g, b, rw, rb)


_ROUTE_LANE0 = N_GROUPS


def _route(logits, carry_scr):
    tm = logits.shape[0]
    lane = lax.broadcasted_iota(i32, (tm, LANES), 1)
    lane_f = lane.astype(f32)
    big = float(2 * LANES)
    gl = jnp.where(lane < N_GROUPS, logits, -jnp.inf)
    gmax = jnp.max(gl, axis=1, keepdims=True)
    gidx = jnp.min(jnp.where(gl == gmax, lane_f, big), axis=1, keepdims=True)
    p_group = 1.0 / jnp.sum(jnp.exp(gl - gmax), axis=1, keepdims=True)
    egroup = ((lane - _ROUTE_LANE0) >> 3).astype(f32)
    in_group = (lane >= _ROUTE_LANE0) & (lane < _ROUTE_LANE0 + N_EXPERTS) & (egroup == gidx)
    ev = jnp.where(in_group, logits, -jnp.inf)
    e1 = jnp.max(ev, axis=1, keepdims=True)
    i1 = jnp.min(jnp.where(ev == e1, lane_f, big), axis=1, keepdims=True)
    ev2 = jnp.where(lane_f == i1, -jnp.inf, ev)
    e2 = jnp.max(ev2, axis=1, keepdims=True)
    i2 = jnp.min(jnp.where(ev2 == e2, lane_f, big), axis=1, keepdims=True)
    tt = jnp.exp(e2 - e1)
    w1 = p_group / (1.0 + tt)
    w2 = p_group * tt / (1.0 + tt)

    hit1 = lane_f == i1
    hit2 = lane_f == i2
    onehot = jnp.where(hit1, 1.0, jnp.where(hit2, 1.0, 0.0))
    rr = lax.broadcasted_iota(i32, (tm, tm), 0)
    cc = lax.broadcasted_iota(i32, (tm, tm), 1)
    lower = jnp.where(cc < rr, 1.0, 0.0).astype(bf16)
    before = _dot(lower, onehot.astype(bf16)) + carry_scr[0:1, :]
    rank1 = jnp.sum(jnp.where(hit1, before, 0.0), axis=1, keepdims=True)
    rank2 = jnp.sum(jnp.where(hit2, before, 0.0), axis=1, keepdims=True)
    carry_scr[0:1, :] = carry_scr[0:1, :] + jnp.sum(onehot, axis=0, keepdims=True)

    return jnp.where(lane == 0, i1 - _ROUTE_LANE0, jnp.where(lane == 1, i2 - _ROUTE_LANE0, jnp.where(
        lane == 2, rank1, jnp.where(lane == 3, rank2, jnp.where(lane == 4, w1, jnp.where(lane == 5, w2, 0.0))))))


def _dispatch_kernel(cnt_ref, pstart_ref, padded_ref, dest_ref, h_ref, o_hbm, sem):
    i = pl.program_id(0)
    nb = pl.num_programs(0) - 1
    tm = dest_ref.shape[2] // TOP_K

    def row_copy(src_row, dst_row):
        return pltpu.make_async_copy(h_ref.at[pl.ds(src_row, 1), :], o_hbm.at[dst_row], sem)

    @pl.when(i < nb)
    def _():
        def issue(r, carry):
            for k in range(TOP_K):
                row_copy(r, dest_ref[0, 0, TOP_K * r + k]).start()
            return carry

        lax.fori_loop(0, tm, issue, 0)
        for _ in range(TOP_K):
            pltpu.make_async_copy(h_ref, o_hbm.at[pl.ds(0, tm), 0], sem).wait()

    @pl.when(i == nb)
    def _():
        def per_expert(e, total):
            lo, hi = cnt_ref[e], padded_ref[e]

            def fill(r, carry):
                row_copy(0, pstart_ref[e] + r).start()
                return carry

            lax.fori_loop(lo, hi, fill, 0)
            return total + (hi - lo)

        total = lax.fori_loop(0, N_EXPERTS, per_expert, 0)

        def drain(r, carry):
            row_copy(0, 0).wait()
            return carry

        lax.fori_loop(0, total, drain, 0)


def _dispatch(h, dest_rows, counts, pstart, padded, n_rows, tm):
    n, d = h.shape
    nb = n // tm
    grid_spec = pltpu.PrefetchScalarGridSpec(
        num_scalar_prefetch=3,
        grid=(nb + 1,),
        in_specs=[pl.BlockSpec((1, 1, TOP_K * tm), lambda i, *_: (jnp.minimum(i, nb - 1), 0, 0),
                               memory_space=pltpu.SMEM),
                  pl.BlockSpec((tm, d), lambda i, *_: (jnp.minimum(i, nb - 1), 0))],
        out_specs=pl.BlockSpec(memory_space=pl.ANY),
        scratch_shapes=[pltpu.SemaphoreType.DMA(())],
    )
    return pl.pallas_call(
        _dispatch_kernel,
        grid_spec=grid_spec,
        out_shape=jax.ShapeDtypeStruct((n_rows, 1, d), f32),
        compiler_params=_cparams("arbitrary"),
        name="moe_dispatch",
    )(counts, pstart, padded, dest_rows.reshape(nb, 1, TOP_K * tm), h)


_CAST_ROWS = 256


def _expert_kernel(nused_ref, be_ref, first_ref, nxt_ref, half_ref, x_ref, wg_hbm, wu_hbm, wd_hbm, y_ref,
                   wg_f, wu_f, wd_f, wg_b, wu_b, wd_b, sems, *, layer):
    i = pl.program_id(0)
    used = i < nused_ref[0]
    staged = ((wg_hbm, wg_f, wg_b), (wu_hbm, wu_f, wu_b), (wd_hbm, wd_f, wd_b))

    def fetch(e):
        return [pltpu.make_async_copy(hbm.at[layer, e], stage, sems.at[j]) for j, (hbm, stage, _) in enumerate(staged)]

    @pl.when(i == 0)
    def _():
        for cp in fetch(be_ref[0]):
            cp.start()

    @pl.when(used & (first_ref[i] == 1))
    def _():
        for cp in fetch(be_ref[i]):
            cp.wait()
        for _, stage, dst in staged:
            rows = stage.shape[0]
            step = min(_CAST_ROWS, rows)

            def cast(c, carry, stage=stage, dst=dst, step=step):
                sl = pl.ds(pl.multiple_of(c * step, step), step)
                dst[sl, :] = stage[sl, :].astype(bf16)
                return carry

            lax.fori_loop(0, rows // step, cast, 0)
        nxt = nxt_ref[be_ref[i]]

        @pl.when(nxt >= 0)
        def _():
            for cp in fetch(nxt):
                cp.start()

    def ffn(rows):
        x = x_ref[0:rows, 0, :].astype(bf16)
        act = (_silu(_dot(x, wg_b[...])) * _dot(x, wu_b[...])).astype(bf16)
        y_ref[0:rows, 0, :] = _dot(act, wd_b[...])

    @pl.when(used & (half_ref[i] == 0))
    def _():
        ffn(x_ref.shape[0])

    @pl.when(used & (half_ref[i] == 1))
    def _():
        ffn(x_ref.shape[0] // 2)


_EXPERT_VMEM_LIMIT = 60 * 1024 * 1024


def _experts(xs3, n_used, block_expert, first, nxt, half, wg, wu, wd, layer, rb):
    d, hid = wg.shape[2], wg.shape[3]
    n_blocks = xs3.shape[0] // rb

    def blk(i, nu, *_):
        return (jnp.minimum(i, nu[0] - 1), 0, 0)

    grid_spec = pltpu.PrefetchScalarGridSpec(
        num_scalar_prefetch=5,
        grid=(n_blocks,),
        in_specs=[pl.BlockSpec((rb, 1, d), blk),
                  pl.BlockSpec(memory_space=pl.ANY), pl.BlockSpec(memory_space=pl.ANY),
                  pl.BlockSpec(memory_space=pl.ANY)],
        out_specs=pl.BlockSpec((rb, 1, d), blk),
        scratch_shapes=[pltpu.VMEM((d, hid), f32), pltpu.VMEM((d, hid), f32), pltpu.VMEM((hid, d), f32),
                        pltpu.VMEM((d, hid), bf16), pltpu.VMEM((d, hid), bf16), pltpu.VMEM((hid, d), bf16),
                        pltpu.SemaphoreType.DMA((3,))],
    )
    return pl.pallas_call(
        functools.partial(_expert_kernel, layer=layer),
        grid_spec=grid_spec,
        out_shape=jax.ShapeDtypeStruct(xs3.shape, f32),
        compiler_params=pltpu.CompilerParams(dimension_semantics=("arbitrary",), vmem_limit_bytes=_EXPERT_VMEM_LIMIT),
        name="moe_experts",
    )(n_used, block_expert, first, nxt, half, xs3, wg, wu, wd)


def _combine_kernel(dcur_ref, dnxt_ref, y_hbm, x_ref, gate_ref, mod_ref, g_ref, b_ref, o_ref, ybuf, sems, *, alpha):
    i = pl.program_id(0)
    nb = pl.num_programs(0)
    tm = x_ref.shape[0]

    def issue(dref, slot):
        def body(r, carry):
            for k in range(TOP_K):
                pltpu.make_async_copy(y_hbm.at[dref[0, 0, TOP_K * r + k]], ybuf.at[slot, k, pl.ds(r, 1), :],
                                      sems.at[slot]).start()
            return carry

        lax.fori_loop(0, tm, body, 0)

    @pl.when(i == 0)
    def _():
        issue(dcur_ref, 0)

    def step(slot):
        @pl.when(i + 1 < nb)
        def _():
            issue(dnxt_ref, 1 - slot)

        for k in range(TOP_K):
            pltpu.make_async_copy(y_hbm.at[pl.ds(0, tm), 0], ybuf.at[slot, k], sems.at[slot]).wait()
        gates = gate_ref[...]
        f = gates[:, 4:5] * ybuf[slot, 0] + gates[:, 5:6] * ybuf[slot, 1]
        o_ref[...] = _ln(alpha * x_ref[...] + mod_ref[0, 5:6, :] * f) * g_ref[...] + b_ref[...]

    for slot in range(2):
        pl.when(i % 2 == slot)(functools.partial(step, slot))


def _combine(y3, dest_rows, xs, rout, mod, g, b, n_lat, lat_len, n_batch, alpha, tm):
    n, d = xs.shape
    nb = n // tm
    nlb, bpb = n_lat // tm, lat_len // tm

    def grp(i):
        return jnp.where(i < nlb, i // bpb, n_batch)

    dest3 = dest_rows.reshape(nb, 1, TOP_K * tm)
    const = lambda i: (0, 0)
    return pl.pallas_call(
        functools.partial(_combine_kernel, alpha=alpha),
        grid=(nb,),
        in_specs=[pl.BlockSpec((1, 1, TOP_K * tm), lambda i: (i, 0, 0), memory_space=pltpu.SMEM),
                  pl.BlockSpec((1, 1, TOP_K * tm), lambda i: (jnp.minimum(i + 1, nb - 1), 0, 0),
                               memory_space=pltpu.SMEM),
                  pl.BlockSpec(memory_space=pl.ANY),
                  pl.BlockSpec((tm, d), lambda i: (i, 0)),
                  pl.BlockSpec((tm, LANES), lambda i: (i, 0)),
                  pl.BlockSpec((1, 6, d), lambda i: (grp(i), 0, 0)),
                  pl.BlockSpec((1, d), const), pl.BlockSpec((1, d), const)],
        out_specs=pl.BlockSpec((tm, d), lambda i: (i, 0)),
        out_shape=jax.ShapeDtypeStruct((n, d), f32),
        scratch_shapes=[pltpu.VMEM((2, TOP_K, tm, d), f32), pltpu.SemaphoreType.DMA((2,))],
        compiler_params=_cparams("arbitrary"),
        name="moe_combine",
    )(dest3, dest3, y3, xs, rout, mod, g, b)


def _moe(h, rout, cnt, xs, wg, wu, wd, layer, mod, g, b, n_lat, lat_len, n_batch, alpha):
    n, d = xs.shape
    tm = _tile(lat_len, 256)
    while (n - n_lat) % tm:
        tm //= 2
    rb = tm
    eid = rout[:, 0:2].astype(i32)
    rank = rout[:, 2:4].astype(i32)
    counts = cnt[0, _ROUTE_LANE0:_ROUTE_LANE0 + N_EXPERTS].astype(i32)
    padded = (counts + rb - 1) // rb * rb
    pend = jnp.cumsum(padded)
    pstart = pend - padded
    dest_rows = (pstart[eid] + rank).reshape(-1)
    n_blocks = TOP_K * n // rb + N_EXPERTS
    n_used = (pend[-1:] // rb).astype(i32)
    blk = jnp.arange(n_blocks, dtype=i32)
    ex = jnp.arange(N_EXPERTS, dtype=i32)
    block_expert = jnp.minimum(jnp.sum((pend[None, :] <= (blk * rb)[:, None]).astype(i32), axis=1), N_EXPERTS - 1)
    block_expert = jnp.where(blk < n_used[0], block_expert, block_expert[jnp.maximum(n_used[0] - 1, 0)])
    first = ((blk < n_used[0]) & ((blk == 0) | (block_expert != jnp.roll(block_expert, 1)))).astype(i32)
    later_active = (ex[None, :] > ex[:, None]) & (counts > 0)[None, :]
    nxt = jnp.min(jnp.where(later_active, ex[None, :], N_EXPERTS), axis=1)
    nxt = jnp.where(nxt == N_EXPERTS, -1, nxt).astype(i32)
    valid = jnp.clip(pstart[block_expert] + counts[block_expert] - blk * rb, 0, rb)
    half = ((blk < n_used[0]) & (valid <= rb // 2)).astype(i32)
    xs3 = _dispatch(h, dest_rows, counts, pstart, padded, n_blocks * rb, tm)
    y3 = _experts(xs3, n_used, block_expert, first, nxt, half, wg, wu, wd, layer, rb)
    return _combine(y3, dest_rows, xs, rout, mod, g, b, n_lat, lat_len, n_batch, alpha, tm)


def _permute_w_in(w):
    d = w.shape[0]
    o_sq, o_sk, o_sv, o_rq, o_rk, o_rv, o_gf, o_gb, o_mq, o_ckv, o_kr = (
        0, 768, 1024, 1280, 1664, 2048, 2816, 3584, 4352, 5120, 5376)
    mq = w[:, o_mq:o_mq + 768].reshape(d, MLA_HEADS, MLA_NOPE_DIM + MLA_ROPE_DIM)
    parts = [w[:, o_sq:o_sq + 768], w[:, o_rv:o_rv + 768], w[:, o_gf:o_gf + 768], w[:, o_gb:o_gb + 768],
             mq[:, :, :MLA_NOPE_DIM].reshape(d, 512), w[:, o_sk:o_sk + 256], w[:, o_sv:o_sv + 256],
             mq[:, :, MLA_NOPE_DIM:].reshape(d, 256), w[:, o_ckv:o_ckv + 256], w[:, o_rq:o_rq + 384],
             w[:, o_rk:o_rk + 384], w[:, o_kr:o_kr + 64], jnp.zeros((d, _NP - _KR - 64), w.dtype)]
    return jnp.concatenate(parts, axis=1).astype(bf16)


def kernel(x, c, ctx, c_ctx, w_ada, b_ada, w_in, swa_sink, ret_decay, mla_kv_norm, mla_w_uk, mla_w_uv, w_out, ln1_g, ln1_b, ln2_g, ln2_b, moe_w_group, moe_b_group, moe_w_expert, moe_b_expert, moe_w_gate, moe_w_up, moe_w_down):
    n_batch, lat_len, d = x.shape
    ctx_len = ctx.shape[1]
    depth = w_ada.shape[0]
    n_lat, n_ctx = n_batch * lat_len, n_batch * ctx_len
    alpha = (2 * depth) ** 0.25

    cc = jnp.zeros((8, d), f32).at[:n_batch].set(c).at[n_batch].set(c_ctx)
    mod_all = _ada(cc, w_ada, b_ada).reshape(depth, 8, 6, d)
    tab = _rope_tables(lat_len, ctx_len, _inproj_rows(lat_len, n_ctx))
    x_lat, x_ctx = x.reshape(n_lat, d), ctx.reshape(n_ctx, d)

    for l in range(depth):
        ctx_out = l < depth - 1
        mod = mod_all[l]
        p = _inproj(x_lat, x_ctx, mod, tab, _permute_w_in(w_in[l]), n_lat, lat_len, ctx_len, n_batch)
        kx, v = _mla_expand(p, mla_kv_norm[l][None, :], mla_w_uk[l].astype(bf16), mla_w_uv[l].astype(bf16))
        m = _mla_attention(p, kx, v, n_batch, lat_len, ctx_len, ctx_out)
        a = _swa_attention(p, swa_sink[l], n_batch, lat_len, ctx_len, ctx_out)
        lg = jnp.log1p(-jnp.exp2(-ret_decay[l].astype(f32)))
        yf, yb = _retention(p, lg, n_batch, lat_len, ctx_len)
        n_rows = n_lat + n_ctx if ctx_out else n_lat
        rout_w = jnp.zeros((d, LANES), f32).at[:, :N_GROUPS].set(moe_w_group[l]).at[
            :, _ROUTE_LANE0:_ROUTE_LANE0 + N_EXPERTS].set(moe_w_expert[l]).astype(bf16)
        rout_b = jnp.zeros((1, LANES), f32).at[0, :N_GROUPS].set(moe_b_group[l]).at[
            0, _ROUTE_LANE0:_ROUTE_LANE0 + N_EXPERTS].set(moe_b_expert[l])
        xs, h, rout, cnt = _outproj(a, yf, yb, p, m, x_lat, x_ctx, mod, w_out[l].astype(bf16), ln1_g[l][None, :],
                                    ln1_b[l][None, :], rout_w, rout_b, n_rows, n_lat, lat_len, n_batch, alpha)
        xs = _moe(h, rout, cnt, xs, moe_w_gate, moe_w_up, moe_w_down, l, mod, ln2_g[l][None, :], ln2_b[l][None, :],
                  n_lat, lat_len, n_batch, alpha)
        x_lat = x_ctx = xs
    return xs[:n_lat].reshape(n_batch, lat_len, d)
```

```python
import functools

import jax
import jax.numpy as jnp
from jax import lax
from jax.experimental import pallas as pl
from jax.experimental.pallas import tpu as pltpu

f32 = jnp.float32
bf16 = jnp.bfloat16
i32 = jnp.int32

GRID_W = 64
SWA_HEADS, SWA_KV_HEADS, SWA_HEAD_DIM, SWA_WINDOW = 6, 2, 128, 128
RET_HEADS, RET_QK_DIM, RET_V_DIM, RET_CHUNK = 6, 64, 128, 128
MLA_HEADS, MLA_NOPE_DIM, MLA_ROPE_DIM, MLA_V_DIM, MLA_KV_RANK = 4, 128, 64, 128, 256
N_GROUPS, EXPERTS_PER_GROUP, TOP_K = 4, 8, 2
N_EXPERTS = N_GROUPS * EXPERTS_PER_GROUP
ROPE_BASE = 10000.0
NORM_EPS = 1e-6
NEG_INF = -1e30
LANES = 128

_SQ, _RV, _GF, _GB, _MQN, _SK, _SV, _MQR, _CKV, _RQ, _RK, _KR = (
    0, 768, 1536, 2304, 3072, 3584, 3840, 4096, 4352, 4608, 4992, 5376)
_NP = 5504
_LOG2E = 1.4426950408889634
_SWA_SCALE = SWA_HEAD_DIM ** -0.5
_RET_SCALE = RET_QK_DIM ** -0.5
_MLA_SCALE = (MLA_NOPE_DIM + MLA_ROPE_DIM) ** -0.5 * _LOG2E
_SEGMENTS = ((_SQ, 768, 0, _SWA_SCALE), (_RV, 768, None, 1.0), (_GF, 768, None, 1.0), (_GB, 768, None, 1.0),
             (_MQN, 512, None, _MLA_SCALE), (_SK, 256, 0, 1.0), (_SV, 256, None, 1.0), (_MQR, 256, 2, _MLA_SCALE),
             (_CKV, 256, None, 1.0), (_RQ, 384, 1, _RET_SCALE), (_RK, 384, 1, 1.0), (_KR, 128, 2, 1.0))

_VMEM_LIMIT = 48 * 1024 * 1024


def _cparams(*sem):
    return pltpu.CompilerParams(dimension_semantics=sem, vmem_limit_bytes=_VMEM_LIMIT)


def _tile(n, pref):
    t = min(n, pref)
    while n % t:
        t //= 2
    return t


def _ln(x):
    mu = jnp.mean(x, axis=-1, keepdims=True)
    xc = x - mu
    var = jnp.mean(xc * xc, axis=-1, keepdims=True)
    return xc * lax.rsqrt(var + NORM_EPS)


def _silu(x):
    return x / (1.0 + jnp.exp(-x))


def _dot(a, b):
    return jnp.dot(a, b, preferred_element_type=f32)


def _dot_nt(a, b):
    return lax.dot_general(a, b, (((1,), (1,)), ((), ())), preferred_element_type=f32)


def _dot_tn(a, b):
    return lax.dot_general(a, b, (((0,), (0,)), ((), ())), preferred_element_type=f32)


def _ada_kernel(c_ref, w_ref, b_ref, o_ref):
    s = _silu(c_ref[...]).astype(bf16)
    o_ref[0] = _dot(s, w_ref[0].astype(bf16)) + b_ref[0]


def _ada(cc, w_ada, b_ada):
    depth, d, n = w_ada.shape
    tn = _tile(n, 1024)
    return pl.pallas_call(
        _ada_kernel,
        grid=(depth, n // tn),
        in_specs=[pl.BlockSpec((8, d), lambda l, j: (0, 0)),
                  pl.BlockSpec((1, d, tn), lambda l, j: (l, 0, j)),
                  pl.BlockSpec((1, 1, tn), lambda l, j: (l, 0, j))],
        out_specs=pl.BlockSpec((1, 8, tn), lambda l, j: (l, 0, j)),
        out_shape=jax.ShapeDtypeStruct((depth, 8, n), f32),
        compiler_params=_cparams("parallel", "parallel"),
        name="ada",
    )(cc, w_ada, b_ada.reshape(depth, 1, n))


_INPROJ_CHUNK = 512


def _inproj_kernel(xl_ref, xc_ref, mod_ref, tab_ref, w_ref, o_ref, *, nlb):
    tm = xl_ref.shape[0]
    h = (_ln(_stream_block(xl_ref, xc_ref, nlb)) * (1.0 + mod_ref[0, 1:2, :]) + mod_ref[0, 0:1, :]).astype(bf16)
    lane = lax.broadcasted_iota(i32, (tm, LANES), 1)
    first = {32: (lane & 63) < 32, 16: (lane & 31) < 16}
    slab_kind = {}
    for off, width, typ, scale in _SEGMENTS:
        for k in range(off // LANES, (off + width) // LANES):
            slab_kind[k] = (typ, scale)
    for c0 in range(0, _NP, _INPROJ_CHUNK):
        c1 = min(c0 + _INPROJ_CHUNK, _NP)
        acc = _dot(h, w_ref[:, c0:c1])
        for k in range(c0 // LANES, c1 // LANES):
            typ, scale = slab_kind[k]
            xk = acc[:, k * LANES - c0:(k + 1) * LANES - c0]
            if scale != 1.0:
                xk = xk * scale
            if typ is not None:
                half = 16 if typ == 2 else 32
                partner = jnp.where(first[half], pltpu.roll(xk, LANES - half, 1), pltpu.roll(xk, half, 1))
                xk = xk * tab_ref[typ, 0] + partner * tab_ref[typ, 1]
            o_ref[:, k * LANES:(k + 1) * LANES] = xk.astype(bf16)


def _inproj_rows(lat_len, n_ctx):
    tm = _tile(lat_len, 256)
    while n_ctx % tm:
        tm //= 2
    return tm


def _inproj(x_lat, x_ctx, mod, tab, w, n_lat, lat_len, ctx_len, n_batch):
    d = x_lat.shape[1]
    t = n_lat + n_batch * ctx_len
    tm = _inproj_rows(lat_len, t - n_lat)
    nlb, bpb = n_lat // tm, lat_len // tm
    ctx_blocks = max(1, ctx_len // tm)

    def grp(i):
        return jnp.where(i < nlb, i // bpb, n_batch)

    def posblk(i):
        return jnp.where(i < nlb, i % bpb, bpb + (i - nlb) % ctx_blocks)

    return pl.pallas_call(
        functools.partial(_inproj_kernel, nlb=nlb),
        grid=(t // tm,),
        in_specs=_stream_specs(x_lat, x_ctx, tm, n_lat) + [
                  pl.BlockSpec((1, 6, d), lambda i: (grp(i), 0, 0)),
                  pl.BlockSpec((3, 2, tm, LANES), lambda i: (0, 0, posblk(i), 0)),
                  pl.BlockSpec((d, _NP), lambda i: (0, 0), pipeline_mode=pl.Buffered(1))],
        out_specs=pl.BlockSpec((tm, _NP), lambda i: (i, 0)),
        out_shape=jax.ShapeDtypeStruct((t, _NP), bf16),
        compiler_params=_cparams("parallel"),
        name="inproj",
    )(x_lat, x_ctx, mod, tab, w)


def _rope_tables(lat_len, ctx_len, tm):
    lane = jnp.arange(LANES)
    t = jnp.arange(lat_len)
    rows = (t // GRID_W).astype(f32)
    cols = (t % GRID_W).astype(f32)

    def freq(half):
        return ROPE_BASE ** (-jnp.arange(half, dtype=f32) / half)

    def table(pos, inv, sign):
        ang = pos * inv[None, :]
        return jnp.stack([jnp.cos(ang), jnp.sin(ang) * sign[None, :]])

    def ident(n):
        return jnp.stack([jnp.ones((n, LANES), f32), jnp.zeros((n, LANES), f32)])

    sign32 = jnp.where((lane % 64) < 32, -1.0, 1.0).astype(f32)
    sign16 = jnp.where((lane % 32) < 16, -1.0, 1.0).astype(f32)
    inv32 = freq(32)[lane % 32]
    inv16 = freq(16)[lane % 16]
    pos_swa = jnp.where((lane // 64)[None, :] == 0, rows[:, None], cols[:, None])
    pos_mla = jnp.where(((lane % 64) // 32)[None, :] == 0, rows[:, None], cols[:, None])
    rep = max(1, tm // ctx_len)
    nc = rep * ctx_len
    pos_ret_l = jnp.broadcast_to((ctx_len + t).astype(f32)[:, None], (lat_len, LANES))
    pos_ret_c = jnp.broadcast_to(jnp.tile(jnp.arange(ctx_len), rep).astype(f32)[:, None], (nc, LANES))
    t0 = jnp.concatenate([table(pos_swa, inv32, sign32), ident(nc)], axis=1)
    t1 = jnp.concatenate([table(pos_ret_l, inv32, sign32), table(pos_ret_c, inv32, sign32)], axis=1)
    t2 = jnp.concatenate([table(pos_mla, inv16, sign16), ident(nc)], axis=1)
    return jnp.stack([t0, t1, t2])


def _mla_expand_kernel(ckv_ref, kr_ref, g_ref, wuk_ref, wuv_ref, k_ref, v_ref):
    c = ckv_ref[...].astype(f32)
    cn = (c * lax.rsqrt(jnp.mean(c * c, axis=-1, keepdims=True) + NORM_EPS) * g_ref[...]).astype(bf16)
    kn = _dot(cn, wuk_ref[...]).astype(bf16)
    vv = _dot(cn, wuv_ref[...]).astype(bf16)
    kr = kr_ref[...]
    for h in range(MLA_HEADS):
        k_ref[h, :, 0:128] = kn[:, h * 128:(h + 1) * 128]
        k_ref[h, :, 128:256] = kr
        v_ref[h] = vv[:, h * 128:(h + 1) * 128]


def _mla_expand(p, g, wuk, wuv):
    t = p.shape[0]
    tm = _tile(t, 512)
    return pl.pallas_call(
        _mla_expand_kernel,
        grid=(t // tm,),
        in_specs=[pl.BlockSpec((tm, 256), lambda i: (i, _CKV // 256)),
                  pl.BlockSpec((tm, 128), lambda i: (i, _KR // 128)),
                  pl.BlockSpec((1, 256), lambda i: (0, 0)),
                  pl.BlockSpec((256, 512), lambda i: (0, 0)),
                  pl.BlockSpec((256, 512), lambda i: (0, 0))],
        out_specs=[pl.BlockSpec((MLA_HEADS, tm, 256), lambda i: (0, i, 0)),
                   pl.BlockSpec((MLA_HEADS, tm, 128), lambda i: (0, i, 0))],
        out_shape=[jax.ShapeDtypeStruct((MLA_HEADS, t, 256), bf16), jax.ShapeDtypeStruct((MLA_HEADS, t, 128), bf16)],
        compiler_params=_cparams("parallel"),
        name="mla_expand",
    )(p, p, g, wuk, wuv)


def _mla_kernel(*refs, with_lat):
    if with_lat:
        qn_ref, qr_ref, kc_ref, vc_ref, kl_ref, vl_ref, o_ref, qx_scr = refs
    else:
        qn_ref, qr_ref, kc_ref, vc_ref, _, o_ref, qx_scr = refs
    lane = lax.broadcasted_iota(i32, (qn_ref.shape[0], LANES), 1)
    for h in range(MLA_HEADS):
        slab = qr_ref[:, (h // 2) * 128:(h // 2 + 1) * 128].astype(f32)
        if h % 2:
            slab = pltpu.roll(slab, 64, 1)
        qx_scr[h, :, 0:128] = qn_ref[:, h * 128:(h + 1) * 128]
        qx_scr[h, :, 128:256] = jnp.where(lane < 64, slab, 0.0).astype(bf16)
    for h in range(MLA_HEADS):
        q = qx_scr[h]
        parts = [(_dot_nt(q, kc_ref[h]), vc_ref[h])]
        if with_lat:
            parts.append((_dot_nt(q, kl_ref[h]), vl_ref[h]))
        m = None
        for s, _ in parts:
            mx = jnp.max(s, axis=1, keepdims=True)
            m = mx if m is None else jnp.maximum(m, mx)
        den, acc = None, None
        for s, vv in parts:
            pr = jnp.exp2(s - m)
            sm = jnp.sum(pr, axis=1, keepdims=True)
            pv = _dot(pr.astype(bf16), vv)
            den = sm if den is None else den + sm
            acc = pv if acc is None else acc + pv
        o_ref[:, h * 128:(h + 1) * 128] = (acc / den).astype(bf16)


def _mla_attention(p, kx, v, n_batch, lat_len, ctx_len, compute_ctx):
    t = p.shape[0]
    tq = _tile(lat_len, 256)
    nlq = lat_len // tq
    cbase = n_batch * lat_len // ctx_len
    once = pl.Buffered(1)
    out = pl.pallas_call(
        functools.partial(_mla_kernel, with_lat=True),
        grid=(n_batch, nlq),
        in_specs=[pl.BlockSpec((tq, 512), lambda b, qi: (b * nlq + qi, _MQN // 512)),
                  pl.BlockSpec((tq, 256), lambda b, qi: (b * nlq + qi, _MQR // 256)),
                  pl.BlockSpec((MLA_HEADS, ctx_len, 256), lambda b, qi: (0, cbase + b, 0)),
                  pl.BlockSpec((MLA_HEADS, ctx_len, 128), lambda b, qi: (0, cbase + b, 0)),
                  pl.BlockSpec((MLA_HEADS, lat_len, 256), lambda b, qi: (0, b, 0), pipeline_mode=once),
                  pl.BlockSpec((MLA_HEADS, lat_len, 128), lambda b, qi: (0, b, 0), pipeline_mode=once)],
        out_specs=pl.BlockSpec((tq, 512), lambda b, qi: (b * nlq + qi, 0)),
        out_shape=jax.ShapeDtypeStruct((t, 512), bf16),
        scratch_shapes=[pltpu.VMEM((MLA_HEADS, tq, 256), bf16)],
        compiler_params=_cparams("parallel", "arbitrary"),
        name="mla_attn",
    )(p, p, kx, v, kx, v)
    if compute_ctx:
        out = pl.pallas_call(
            functools.partial(_mla_kernel, with_lat=False),
            grid=(n_batch,),
            in_specs=[pl.BlockSpec((ctx_len, 512), lambda b: (cbase + b, _MQN // 512)),
                      pl.BlockSpec((ctx_len, 256), lambda b: (cbase + b, _MQR // 256)),
                      pl.BlockSpec((MLA_HEADS, ctx_len, 256), lambda b: (0, cbase + b, 0)),
                      pl.BlockSpec((MLA_HEADS, ctx_len, 128), lambda b: (0, cbase + b, 0)),
                      pl.BlockSpec(memory_space=pl.ANY)],
            out_specs=pl.BlockSpec((ctx_len, 512), lambda b: (cbase + b, 0)),
            out_shape=jax.ShapeDtypeStruct((t, 512), bf16),
            scratch_shapes=[pltpu.VMEM((MLA_HEADS, ctx_len, 256), bf16)],
            input_output_aliases={4: 0},
            compiler_params=_cparams("parallel"),
            name="mla_ctx",
        )(p, p, kx, v, out)
    return out


def _swa_softmax_out(parts, sink_col, o_ref, g):
    m = sink_col
    for s, _ in parts:
        m = jnp.maximum(m, jnp.max(s, axis=1, keepdims=True))
    den = jnp.exp(sink_col - m)
    acc = None
    for s, vv in parts:
        pr = jnp.exp(s - m)
        den = den + jnp.sum(pr, axis=1, keepdims=True)
        pv = _dot(pr.astype(bf16), vv)
        acc = pv if acc is None else acc + pv
    o = acc / den
    nq = o.shape[0] // 3
    for j in range(3):
        o_ref[:, (3 * g + j) * 128:(3 * g + j + 1) * 128] = o[j * nq:(j + 1) * nq].astype(bf16)


def _sink_col(sink_ref, g, nq):
    row = lax.broadcasted_iota(i32, (3 * nq, 1), 0)
    return jnp.where(row < nq, sink_ref[3 * g], jnp.where(row < 2 * nq, sink_ref[3 * g + 1], sink_ref[3 * g + 2]))


def _swa_kernel(sink_ref, q_ref, kp_ref, ko_ref, kn_ref, vp_ref, vo_ref, vn_ref, kc_ref, vc_ref, o_ref):
    i = pl.program_id(1)
    nb = pl.num_programs(1)
    blk = q_ref.shape[0]
    q = q_ref[...]
    r = lax.broadcasted_iota(i32, (3 * blk, 3 * blk), 0) & (blk - 1)
    c = lax.broadcasted_iota(i32, (3 * blk, 3 * blk), 1)
    lo = jnp.where(i == 0, blk, 0)
    hi = jnp.where(i == nb - 1, 2 * blk, 3 * blk)
    d = c - r
    valid = (d >= 0) & (d <= 2 * SWA_WINDOW) & (c >= lo) & (c < hi)
    for g in range(SWA_KV_HEADS):
        hs = slice(g * 128, (g + 1) * 128)
        qg = jnp.concatenate([q[:, (3 * g + j) * 128:(3 * g + j + 1) * 128] for j in range(3)], axis=0)
        kloc = jnp.concatenate([kp_ref[:, hs], ko_ref[:, hs], kn_ref[:, hs]], axis=0)
        vloc = jnp.concatenate([vp_ref[:, hs], vo_ref[:, hs], vn_ref[:, hs]], axis=0)
        s_loc = jnp.where(valid, _dot_nt(qg, kloc), NEG_INF)
        s_ctx = _dot_nt(qg, kc_ref[:, hs])
        _swa_softmax_out([(s_ctx, vc_ref[:, hs]), (s_loc, vloc)], _sink_col(sink_ref, g, blk), o_ref, g)


def _swa_ctx_kernel(sink_ref, q_ref, kc_ref, vc_ref, prev_ref, o_ref):
    del prev_ref
    q = q_ref[...]
    nq = q.shape[0]
    for g in range(SWA_KV_HEADS):
        hs = slice(g * 128, (g + 1) * 128)
        qg = jnp.concatenate([q[:, (3 * g + j) * 128:(3 * g + j + 1) * 128] for j in range(3)], axis=0)
        s_ctx = _dot_nt(qg, kc_ref[:, hs])
        _swa_softmax_out([(s_ctx, vc_ref[:, hs])], _sink_col(sink_ref, g, nq), o_ref, g)


def _swa_attention(p, sink, n_batch, lat_len, ctx_len, compute_ctx):
    t = p.shape[0]
    blk = SWA_WINDOW
    nb = lat_len // blk
    cbase = n_batch * lat_len // ctx_len
    smem = pl.BlockSpec(memory_space=pltpu.SMEM)

    def kspec(col, off):
        return pl.BlockSpec((blk, 256), lambda b, i: (b * nb + jnp.clip(i + off, 0, nb - 1), col // 256))

    out = pl.pallas_call(
        _swa_kernel,
        grid=(n_batch, nb),
        in_specs=[smem,
                  pl.BlockSpec((blk, 768), lambda b, i: (b * nb + i, _SQ // 768)),
                  kspec(_SK, -1), kspec(_SK, 0), kspec(_SK, 1),
                  kspec(_SV, -1), kspec(_SV, 0), kspec(_SV, 1),
                  pl.BlockSpec((ctx_len, 256), lambda b, i: (cbase + b, _SK // 256)),
                  pl.BlockSpec((ctx_len, 256), lambda b, i: (cbase + b, _SV // 256))],
        out_specs=pl.BlockSpec((blk, 768), lambda b, i: (b * nb + i, 0)),
        out_shape=jax.ShapeDtypeStruct((t, 768), bf16),
        compiler_params=_cparams("parallel", "parallel"),
        name="swa_attn",
    )(sink, p, p, p, p, p, p, p, p, p)
    if compute_ctx:
        out = pl.pallas_call(
            _swa_ctx_kernel,
            grid=(n_batch,),
            in_specs=[smem,
                      pl.BlockSpec((ctx_len, 768), lambda b: (cbase + b, _SQ // 768)),
                      pl.BlockSpec((ctx_len, 256), lambda b: (cbase + b, _SK // 256)),
                      pl.BlockSpec((ctx_len, 256), lambda b: (cbase + b, _SV // 256)),
                      pl.BlockSpec(memory_space=pl.ANY)],
            out_specs=pl.BlockSpec((ctx_len, 768), lambda b: (cbase + b, 0)),
            out_shape=jax.ShapeDtypeStruct((t, 768), bf16),
            input_output_aliases={4: 0},
            compiler_params=_cparams("parallel"),
            name="swa_ctx",
        )(sink, p, p, p, out)
    return out


def _ret_kernel(lg_ref, qf_ref, kf_ref, vf_ref, qb_ref, kb_ref, vb_ref, of_ref, ob_ref,
                s_scr, d_scr, qd_scr, kd_scr, gc_scr):
    i = pl.program_id(1)
    cc = RET_CHUNK

    @pl.when(i == 0)
    def _():
        s_scr[...] = jnp.zeros(s_scr.shape, f32)
        r = lax.broadcasted_iota(i32, (cc, cc), 0).astype(f32)
        c = lax.broadcasted_iota(i32, (cc, cc), 1).astype(f32)
        for direction in range(2):
            backward = direction == 1
            diff = (c - r) if backward else (r - c)
            for h in range(RET_HEADS):
                lgh = lg_ref[direction, h]
                j = direction * RET_HEADS + h
                d_scr[j] = jnp.where(diff >= 0, jnp.exp(lgh * jnp.maximum(diff, 0.0)), 0.0)
                qd_scr[j] = jnp.exp(lgh * ((cc - r) if backward else (r + 1.0)))
                kd_scr[j] = jnp.exp(lgh * (r if backward else (cc - 1.0 - r)))
                gc_scr[j] = jnp.exp(lgh * cc + jnp.zeros((cc, cc), f32))

    lane = lax.broadcasted_iota(i32, (cc, LANES), 1)
    for pair in range(RET_HEADS // 2):
        for direction, (q_ref, k_ref, v_ref, o_ref) in enumerate(
                ((qf_ref, kf_ref, vf_ref, of_ref), (qb_ref, kb_ref, vb_ref, ob_ref))):
            q2 = q_ref[:, pair * 128:(pair + 1) * 128].astype(f32)
            k2b = k_ref[:, pair * 128:(pair + 1) * 128]
            k2 = k2b.astype(f32)
            for sub in range(2):
                h = 2 * pair + sub
                j = direction * RET_HEADS + h
                hs = slice(h * 128, (h + 1) * 128)
                qa = jnp.where((lane < 64) if sub == 0 else (lane >= 64), q2, 0.0)
                s = _dot_nt(qa.astype(bf16), k2b)
                vh = v_ref[:, hs]
                y = (_dot((s * d_scr[j]).astype(bf16), vh)
                     + _dot((qa * qd_scr[j]).astype(bf16), s_scr[j].astype(bf16)))
                s_scr[j] = gc_scr[j] * s_scr[j] + _dot_tn((k2 * kd_scr[j]).astype(bf16), vh)
                o_ref[:, hs] = _ln(y).astype(bf16)


def _retention(p, lg, n_batch, lat_len, ctx_len):
    t = p.shape[0]
    cc = RET_CHUNK
    ncc, nlc = ctx_len // cc, lat_len // cc
    cbase = n_batch * nlc
    smem = pl.BlockSpec(memory_space=pltpu.SMEM)
    scratch = [pltpu.VMEM((2 * RET_HEADS, cc, cc), f32) for _ in range(5)]

    def fwd_row(b, i):
        return jnp.where(i < ncc, cbase + b * ncc + i, b * nlc + (i - ncc))

    def bwd_row(b, i):
        return jnp.where(i < ncc, cbase + b * ncc + (ncc - 1 - i), b * nlc + (nlc - 1 - (i - ncc)))

    def specs(row):
        return [pl.BlockSpec((cc, 384), lambda b, i: (row(b, i), _RQ // 384)),
                pl.BlockSpec((cc, 384), lambda b, i: (row(b, i), _RK // 384)),
                pl.BlockSpec((cc, 768), lambda b, i: (row(b, i), _RV // 768))]

    return pl.pallas_call(
        _ret_kernel,
        grid=(n_batch, ncc + nlc),
        in_specs=[smem] + specs(fwd_row) + specs(bwd_row),
        out_specs=[pl.BlockSpec((cc, 768), lambda b, i: (fwd_row(b, i), 0)),
                   pl.BlockSpec((cc, 768), lambda b, i: (bwd_row(b, i), 0))],
        out_shape=[jax.ShapeDtypeStruct((t, 768), bf16), jax.ShapeDtypeStruct((t, 768), bf16)],
        scratch_shapes=scratch,
        compiler_params=_cparams("parallel", "arbitrary"),
        name="retention",
    )(lg, p, p, p, p, p, p)


def _stream_specs(x_lat, x_ctx, tm, n_lat):
    d = x_lat.shape[1]
    nlb = n_lat // tm
    ctx_first = nlb if x_ctx is x_lat else 0
    return [pl.BlockSpec((tm, d), lambda i: (jnp.minimum(i, nlb - 1), 0)),
            pl.BlockSpec((tm, d), lambda i: (jnp.maximum(i - nlb, 0) + ctx_first, 0))]


def _stream_block(xl_ref, xc_ref, nlb):
    return jnp.where(pl.program_id(0) < nlb, xl_ref[...], xc_ref[...])


def _outproj_kernel(a_ref, yf_ref, yb_ref, gf_ref, gb_ref, m_ref, xl_ref, xc_ref, mod_ref, w_ref, g_ref, b_ref,
                    rw_ref, rb_ref, xo_ref, h_ref, rout_ref, cnt_ref, carry_scr, *, alpha, nlb):
    @pl.when(pl.program_id(0) == 0)
    def _():
        carry_scr[...] = jnp.zeros(carry_scr.shape, f32)

    ret = (_silu(gf_ref[...].astype(f32)) * yf_ref[...].astype(f32)
           + _silu(gb_ref[...].astype(f32)) * yb_ref[...].astype(f32)).astype(bf16)
    y = (_dot(a_ref[...], w_ref[0:768, :]) + _dot(ret, w_ref[768:1536, :]) + _dot(m_ref[...], w_ref[1536:2048, :]))
    xn = _ln(alpha * _stream_block(xl_ref, xc_ref, nlb) + mod_ref[0, 2:3, :] * y) * g_ref[...] + b_ref[...]
    xo_ref[...] = xn
    h = _ln(xn) * (1.0 + mod_ref[0, 4:5, :]) + mod_ref[0, 3:4, :]
    h_ref[...] = h
    rout_ref[...] = _route(_dot(h.astype(bf16), rw_ref[...]) + rb_ref[...], carry_scr)
    cnt_ref[...] = carry_scr[...]


def _outproj(a, yf, yb, p, m, x_lat, x_ctx, mod, w, g, b, rw, rb, n_rows, n_lat, lat_len, n_batch, alpha):
    d = x_lat.shape[1]
    tm = _tile(lat_len, 256)
    while (n_rows - n_lat) % tm:
        tm //= 2
    nlb, bpb = n_lat // tm, lat_len // tm

    def grp(i):
        return jnp.where(i < nlb, i // bpb, n_batch)

    row = lambda i: (i, 0)
    const = lambda i: (0, 0)
    return pl.pallas_call(
        functools.partial(_outproj_kernel, alpha=alpha, nlb=nlb),
        grid=(n_rows // tm,),
        in_specs=[pl.BlockSpec((tm, 768), row), pl.BlockSpec((tm, 768), row), pl.BlockSpec((tm, 768), row),
                  pl.BlockSpec((tm, 768), lambda i: (i, _GF // 768)), pl.BlockSpec((tm, 768), lambda i: (i, _GB // 768)),
                  pl.BlockSpec((tm, 512), row)] + _stream_specs(x_lat, x_ctx, tm, n_lat) + [
                  pl.BlockSpec((1, 6, d), lambda i: (grp(i), 0, 0)),
                  pl.BlockSpec(w.shape, const),
                  pl.BlockSpec((1, d), const), pl.BlockSpec((1, d), const),
                  pl.BlockSpec((d, LANES), const), pl.BlockSpec((1, LANES), const)],
        out_specs=[pl.BlockSpec((tm, d), row), pl.BlockSpec((tm, d), row),
                   pl.BlockSpec((tm, LANES), row), pl.BlockSpec((8, LANES), const)],
        out_shape=[jax.ShapeDtypeStruct((n_rows, d), f32), jax.ShapeDtypeStruct((n_rows, d), f32),
                   jax.ShapeDtypeStruct((n_rows, LANES), f32), jax.ShapeDtypeStruct((8, LANES), f32)],
        scratch_shapes=[pltpu.VMEM((8, LANES), f32)],
        compiler_params=_cparams("arbitrary"),
        name="outproj",
    )(a, yf, yb, p, p, m, x_lat, x_ctx, mod, w, g, b, rw, rb)


_ROUTE_LANE0 = N_GROUPS


def _route(logits, carry_scr):
    tm = logits.shape[0]
    lane = lax.broadcasted_iota(i32, (tm, LANES), 1)
    lane_f = lane.astype(f32)
    big = float(2 * LANES)
    gl = jnp.where(lane < N_GROUPS, logits, -jnp.inf)
    gmax = jnp.max(gl, axis=1, keepdims=True)
    gidx = jnp.min(jnp.where(gl == gmax, lane_f, big), axis=1, keepdims=True)
    p_group = 1.0 / jnp.sum(jnp.exp(gl - gmax), axis=1, keepdims=True)
    egroup = ((lane - _ROUTE_LANE0) >> 3).astype(f32)
    in_group = (lane >= _ROUTE_LANE0) & (lane < _ROUTE_LANE0 + N_EXPERTS) & (egroup == gidx)
    ev = jnp.where(in_group, logits, -jnp.inf)
    e1 = jnp.max(ev, axis=1, keepdims=True)
    i1 = jnp.min(jnp.where(ev == e1, lane_f, big), axis=1, keepdims=True)
    ev2 = jnp.where(lane_f == i1, -jnp.inf, ev)
    e2 = jnp.max(ev2, axis=1, keepdims=True)
    i2 = jnp.min(jnp.where(ev2 == e2, lane_f, big), axis=1, keepdims=True)
    tt = jnp.exp(e2 - e1)
    w1 = p_group / (1.0 + tt)
    w2 = p_group * tt / (1.0 + tt)

    hit1 = lane_f == i1
    hit2 = lane_f == i2
    onehot = jnp.where(hit1, 1.0, jnp.where(hit2, 1.0, 0.0))
    rr = lax.broadcasted_iota(i32, (tm, tm), 0)
    cc = lax.broadcasted_iota(i32, (tm, tm), 1)
    lower = jnp.where(cc < rr, 1.0, 0.0).astype(bf16)
    before = _dot(lower, onehot.astype(bf16)) + carry_scr[0:1, :]
    rank1 = jnp.sum(jnp.where(hit1, before, 0.0), axis=1, keepdims=True)
    rank2 = jnp.sum(jnp.where(hit2, before, 0.0), axis=1, keepdims=True)
    carry_scr[0:1, :] = carry_scr[0:1, :] + jnp.sum(onehot, axis=0, keepdims=True)

    return jnp.where(lane == 0, i1 - _ROUTE_LANE0, jnp.where(lane == 1, i2 - _ROUTE_LANE0, jnp.where(
        lane == 2, rank1, jnp.where(lane == 3, rank2, jnp.where(lane == 4, w1, jnp.where(lane == 5, w2, 0.0))))))


def _dispatch_kernel(cnt_ref, pstart_ref, padded_ref, dest_ref, h_ref, o_hbm, sem):
    i = pl.program_id(0)
    nb = pl.num_programs(0) - 1
    tm = dest_ref.shape[2] // TOP_K

    def row_copy(src_row, dst_row):
        return pltpu.make_async_copy(h_ref.at[pl.ds(src_row, 1), :], o_hbm.at[dst_row], sem)

    @pl.when(i < nb)
    def _():
        def issue(r, carry):
            for k in range(TOP_K):
                row_copy(r, dest_ref[0, 0, TOP_K * r + k]).start()
            return carry

        lax.fori_loop(0, tm, issue, 0)
        for _ in range(TOP_K):
            pltpu.make_async_copy(h_ref, o_hbm.at[pl.ds(0, tm), 0], sem).wait()

    @pl.when(i == nb)
    def _():
        def per_expert(e, total):
            lo, hi = cnt_ref[e], padded_ref[e]

            def fill(r, carry):
                row_copy(0, pstart_ref[e] + r).start()
                return carry

            lax.fori_loop(lo, hi, fill, 0)
            return total + (hi - lo)

        total = lax.fori_loop(0, N_EXPERTS, per_expert, 0)

        def drain(r, carry):
            row_copy(0, 0).wait()
            return carry

        lax.fori_loop(0, total, drain, 0)


def _dispatch(h, dest_rows, counts, pstart, padded, n_rows, tm):
    n, d = h.shape
    nb = n // tm
    grid_spec = pltpu.PrefetchScalarGridSpec(
        num_scalar_prefetch=3,
        grid=(nb + 1,),
        in_specs=[pl.BlockSpec((1, 1, TOP_K * tm), lambda i, *_: (jnp.minimum(i, nb - 1), 0, 0),
                               memory_space=pltpu.SMEM),
                  pl.BlockSpec((tm, d), lambda i, *_: (jnp.minimum(i, nb - 1), 0))],
        out_specs=pl.BlockSpec(memory_space=pl.ANY),
        scratch_shapes=[pltpu.SemaphoreType.DMA(())],
    )
    return pl.pallas_call(
        _dispatch_kernel,
        grid_spec=grid_spec,
        out_shape=jax.ShapeDtypeStruct((n_rows, 1, d), f32),
        compiler_params=_cparams("arbitrary"),
        name="moe_dispatch",
    )(counts, pstart, padded, dest_rows.reshape(nb, 1, TOP_K * tm), h)


_CAST_ROWS = 256


def _expert_kernel(nused_ref, be_ref, first_ref, nxt_ref, half_ref, x_hbm, wg_hbm, wu_hbm, wd_hbm, y_hbm,
                   wg_f, wu_f, wd_f, wg_b, wu_b, wd_b, xbuf, ybuf, sems, xsems, ysems, *, layer):
    i = pl.program_id(0)
    n_used = nused_ref[0]
    used = i < n_used
    rb = xbuf.shape[1]
    slot = i % 2
    staged = ((wg_hbm, wg_f, wg_b), (wu_hbm, wu_f, wu_b), (wd_hbm, wd_f, wd_b))

    def x_fetch(blk, s):
        return pltpu.make_async_copy(x_hbm.at[pl.ds(pl.multiple_of(blk * rb, rb), rb), 0], xbuf.at[s], xsems.at[s])

    def y_store(blk, s):
        return pltpu.make_async_copy(ybuf.at[s], y_hbm.at[pl.ds(pl.multiple_of(blk * rb, rb), rb), 0], ysems.at[s])

    @pl.when(i == 0)
    def _():
        x_fetch(0, 0).start()

    @pl.when(i + 1 < n_used)
    def _():
        x_fetch(i + 1, 1 - slot).start()

    def fetch(e):
        return [pltpu.make_async_copy(hbm.at[layer, e], stage, sems.at[j]) for j, (hbm, stage, _) in enumerate(staged)]

    @pl.when(i == 0)
    def _():
        for cp in fetch(be_ref[0]):
            cp.start()

    @pl.when(used & (first_ref[i] == 1))
    def _():
        for cp in fetch(be_ref[i]):
            cp.wait()
        for _, stage, dst in staged:
            rows = stage.shape[0]
            step = min(_CAST_ROWS, rows)

            def cast(c, carry, stage=stage, dst=dst, step=step):
                sl = pl.ds(pl.multiple_of(c * step, step), step)
                dst[sl, :] = stage[sl, :].astype(bf16)
                return carry

            lax.fori_loop(0, rows // step, cast, 0)
        nxt = nxt_ref[be_ref[i]]

        @pl.when(nxt >= 0)
        def _():
            for cp in fetch(nxt):
                cp.start()

    @pl.when(used)
    def _():
        x_fetch(i, slot).wait()

    @pl.when(used & (i >= 2))
    def _():
        y_store(i - 2, slot).wait()

    def ffn(rows):
        x = xbuf[slot, 0:rows, :].astype(bf16)
        act = (_silu(_dot(x, wg_b[...])) * _dot(x, wu_b[...])).astype(bf16)
        ybuf[slot, 0:rows, :] = _dot(act, wd_b[...])
        if rows < rb:
            ybuf[slot, rows:rb, :] = jnp.zeros((rb - rows, ybuf.shape[2]), f32)

    @pl.when(used & (half_ref[i] == 0))
    def _():
        ffn(rb)

    @pl.when(used & (half_ref[i] == 1))
    def _():
        ffn(rb // 2)

    @pl.when(used)
    def _():
        y_store(i, slot).start()

    @pl.when(i == n_used - 1)
    def _():
        y_store(i, slot).wait()

        @pl.when(i >= 1)
        def _():
            y_store(i - 1, 1 - slot).wait()


_EXPERT_VMEM_LIMIT = 60 * 1024 * 1024


def _experts(xs3, n_used, block_expert, first, nxt, half, wg, wu, wd, layer, rb):
    d, hid = wg.shape[2], wg.shape[3]
    n_blocks = xs3.shape[0] // rb

    anyspace = pl.BlockSpec(memory_space=pl.ANY)
    grid_spec = pltpu.PrefetchScalarGridSpec(
        num_scalar_prefetch=5,
        grid=(n_blocks,),
        in_specs=[anyspace, anyspace, anyspace, anyspace],
        out_specs=anyspace,
        scratch_shapes=[pltpu.VMEM((d, hid), f32), pltpu.VMEM((d, hid), f32), pltpu.VMEM((hid, d), f32),
                        pltpu.VMEM((d, hid), bf16), pltpu.VMEM((d, hid), bf16), pltpu.VMEM((hid, d), bf16),
                        pltpu.VMEM((2, rb, d), f32), pltpu.VMEM((2, rb, d), f32),
                        pltpu.SemaphoreType.DMA((3,)), pltpu.SemaphoreType.DMA((2,)), pltpu.SemaphoreType.DMA((2,))],
    )
    return pl.pallas_call(
        functools.partial(_expert_kernel, layer=layer),
        grid_spec=grid_spec,
        out_shape=jax.ShapeDtypeStruct(xs3.shape, f32),
        compiler_params=pltpu.CompilerParams(dimension_semantics=("arbitrary",), vmem_limit_bytes=_EXPERT_VMEM_LIMIT),
        name="moe_experts",
    )(n_used, block_expert, first, nxt, half, xs3, wg, wu, wd)


def _combine_kernel(dcur_ref, dnxt_ref, y_hbm, x_ref, gate_ref, mod_ref, g_ref, b_ref, o_ref, ybuf, sems, *, alpha):
    i = pl.program_id(0)
    nb = pl.num_programs(0)
    tm = x_ref.shape[0]

    def issue(dref, slot):
        def body(r, carry):
            for k in range(TOP_K):
                pltpu.make_async_copy(y_hbm.at[dref[0, 0, TOP_K * r + k]], ybuf.at[slot, k, pl.ds(r, 1), :],
                                      sems.at[slot]).start()
            return carry

        lax.fori_loop(0, tm, body, 0)

    @pl.when(i == 0)
    def _():
        issue(dcur_ref, 0)

    def step(slot):
        @pl.when(i + 1 < nb)
        def _():
            issue(dnxt_ref, 1 - slot)

        for k in range(TOP_K):
            pltpu.make_async_copy(y_hbm.at[pl.ds(0, tm), 0], ybuf.at[slot, k], sems.at[slot]).wait()
        gates = gate_ref[...]
        f = gates[:, 4:5] * ybuf[slot, 0] + gates[:, 5:6] * ybuf[slot, 1]
        o_ref[...] = _ln(alpha * x_ref[...] + mod_ref[0, 5:6, :] * f) * g_ref[...] + b_ref[...]

    for slot in range(2):
        pl.when(i % 2 == slot)(functools.partial(step, slot))


def _combine(y3, dest_rows, xs, rout, mod, g, b, n_lat, lat_len, n_batch, alpha, tm):
    n, d = xs.shape
    nb = n // tm
    nlb, bpb = n_lat // tm, lat_len // tm

    def grp(i):
        return jnp.where(i < nlb, i // bpb, n_batch)

    dest3 = dest_rows.reshape(nb, 1, TOP_K * tm)
    const = lambda i: (0, 0)
    return pl.pallas_call(
        functools.partial(_combine_kernel, alpha=alpha),
        grid=(nb,),
        in_specs=[pl.BlockSpec((1, 1, TOP_K * tm), lambda i: (i, 0, 0), memory_space=pltpu.SMEM),
                  pl.BlockSpec((1, 1, TOP_K * tm), lambda i: (jnp.minimum(i + 1, nb - 1), 0, 0),
                               memory_space=pltpu.SMEM),
                  pl.BlockSpec(memory_space=pl.ANY),
                  pl.BlockSpec((tm, d), lambda i: (i, 0)),
                  pl.BlockSpec((tm, LANES), lambda i: (i, 0)),
                  pl.BlockSpec((1, 6, d), lambda i: (grp(i), 0, 0)),
                  pl.BlockSpec((1, d), const), pl.BlockSpec((1, d), const)],
        out_specs=pl.BlockSpec((tm, d), lambda i: (i, 0)),
        out_shape=jax.ShapeDtypeStruct((n, d), f32),
        scratch_shapes=[pltpu.VMEM((2, TOP_K, tm, d), f32), pltpu.SemaphoreType.DMA((2,))],
        compiler_params=_cparams("arbitrary"),
        name="moe_combine",
    )(dest3, dest3, y3, xs, rout, mod, g, b)


def _moe(h, rout, cnt, xs, wg, wu, wd, layer, mod, g, b, n_lat, lat_len, n_batch, alpha):
    n, d = xs.shape
    tm = _tile(lat_len, 256)
    while (n - n_lat) % tm:
        tm //= 2
    rb = tm
    eid = rout[:, 0:2].astype(i32)
    rank = rout[:, 2:4].astype(i32)
    counts = cnt[0, _ROUTE_LANE0:_ROUTE_LANE0 + N_EXPERTS].astype(i32)
    padded = (counts + rb - 1) // rb * rb
    pend = jnp.cumsum(padded)
    pstart = pend - padded
    dest_rows = (pstart[eid] + rank).reshape(-1)
    n_blocks = TOP_K * n // rb + N_EXPERTS
    n_used = (pend[-1:] // rb).astype(i32)
    blk = jnp.arange(n_blocks, dtype=i32)
    ex = jnp.arange(N_EXPERTS, dtype=i32)
    block_expert = jnp.minimum(jnp.sum((pend[None, :] <= (blk * rb)[:, None]).astype(i32), axis=1), N_EXPERTS - 1)
    block_expert = jnp.where(blk < n_used[0], block_expert, block_expert[jnp.maximum(n_used[0] - 1, 0)])
    first = ((blk < n_used[0]) & ((blk == 0) | (block_expert != jnp.roll(block_expert, 1)))).astype(i32)
    later_active = (ex[None, :] > ex[:, None]) & (counts > 0)[None, :]
    nxt = jnp.min(jnp.where(later_active, ex[None, :], N_EXPERTS), axis=1)
    nxt = jnp.where(nxt == N_EXPERTS, -1, nxt).astype(i32)
    valid = jnp.clip(pstart[block_expert] + counts[block_expert] - blk * rb, 0, rb)
    half = ((blk < n_used[0]) & (valid <= rb // 2)).astype(i32)
    xs3 = _dispatch(h, dest_rows, counts, pstart, padded, n_blocks * rb, tm)
    y3 = _experts(xs3, n_used, block_expert, first, nxt, half, wg, wu, wd, layer, rb)
    return _combine(y3, dest_rows, xs, rout, mod, g, b, n_lat, lat_len, n_batch, alpha, tm)


def _permute_w_in(w):
    d = w.shape[0]
    o_sq, o_sk, o_sv, o_rq, o_rk, o_rv, o_gf, o_gb, o_mq, o_ckv, o_kr = (
        0, 768, 1024, 1280, 1664, 2048, 2816, 3584, 4352, 5120, 5376)
    mq = w[:, o_mq:o_mq + 768].reshape(d, MLA_HEADS, MLA_NOPE_DIM + MLA_ROPE_DIM)
    parts = [w[:, o_sq:o_sq + 768], w[:, o_rv:o_rv + 768], w[:, o_gf:o_gf + 768], w[:, o_gb:o_gb + 768],
             mq[:, :, :MLA_NOPE_DIM].reshape(d, 512), w[:, o_sk:o_sk + 256], w[:, o_sv:o_sv + 256],
             mq[:, :, MLA_NOPE_DIM:].reshape(d, 256), w[:, o_ckv:o_ckv + 256], w[:, o_rq:o_rq + 384],
             w[:, o_rk:o_rk + 384], w[:, o_kr:o_kr + 64], jnp.zeros((d, _NP - _KR - 64), w.dtype)]
    return jnp.concatenate(parts, axis=1).astype(bf16)


def kernel(x, c, ctx, c_ctx, w_ada, b_ada, w_in, swa_sink, ret_decay, mla_kv_norm, mla_w_uk, mla_w_uv, w_out, ln1_g, ln1_b, ln2_g, ln2_b, moe_w_group, moe_b_group, moe_w_expert, moe_b_expert, moe_w_gate, moe_w_up, moe_w_down):
    n_batch, lat_len, d = x.shape
    ctx_len = ctx.shape[1]
    depth = w_ada.shape[0]
    n_lat, n_ctx = n_batch * lat_len, n_batch * ctx_len
    alpha = (2 * depth) ** 0.25

    cc = jnp.zeros((8, d), f32).at[:n_batch].set(c).at[n_batch].set(c_ctx)
    mod_all = _ada(cc, w_ada, b_ada).reshape(depth, 8, 6, d)
    tab = _rope_tables(lat_len, ctx_len, _inproj_rows(lat_len, n_ctx))
    x_lat, x_ctx = x.reshape(n_lat, d), ctx.reshape(n_ctx, d)

    for l in range(depth):
        ctx_out = l < depth - 1
        mod = mod_all[l]
        p = _inproj(x_lat, x_ctx, mod, tab, _permute_w_in(w_in[l]), n_lat, lat_len, ctx_len, n_batch)
        kx, v = _mla_expand(p, mla_kv_norm[l][None, :], mla_w_uk[l].astype(bf16), mla_w_uv[l].astype(bf16))
        m = _mla_attention(p, kx, v, n_batch, lat_len, ctx_len, ctx_out)
        a = _swa_attention(p, swa_sink[l], n_batch, lat_len, ctx_len, ctx_out)
        lg = jnp.log1p(-jnp.exp2(-ret_decay[l].astype(f32)))
        yf, yb = _retention(p, lg, n_batch, lat_len, ctx_len)
        n_rows = n_lat + n_ctx if ctx_out else n_lat
        rout_w = jnp.zeros((d, LANES), f32).at[:, :N_GROUPS].set(moe_w_group[l]).at[
            :, _ROUTE_LANE0:_ROUTE_LANE0 + N_EXPERTS].set(moe_w_expert[l]).astype(bf16)
        rout_b = jnp.zeros((1, LANES), f32).at[0, :N_GROUPS].set(moe_b_group[l]).at[
            0, _ROUTE_LANE0:_ROUTE_LANE0 + N_EXPERTS].set(moe_b_expert[l])
        xs, h, rout, cnt = _outproj(a, yf, yb, p, m, x_lat, x_ctx, mod, w_out[l].astype(bf16), ln1_g[l][None, :],
                                    ln1_b[l][None, :], rout_w, rout_b, n_rows, n_lat, lat_len, n_batch, alpha)
        xs = _moe(h, rout, cnt, xs, moe_w_gate, moe_w_up, moe_w_down, l, mod, ln2_g[l][None, :], ln2_b[l][None, :],
                  n_lat, lat_len, n_batch, alpha)
        x_lat = x_ctx = xs
    return xs[:n_lat].reshape(n_batch, lat_len, d)
```

```python
import functools

import jax
import jax.numpy as jnp
from jax import lax
from jax.experimental import pallas as pl
from jax.experimental.pallas import tpu as pltpu

f32 = jnp.float32
bf16 = jnp.bfloat16
i32 = jnp.int32

GRID_W = 64
SWA_HEADS, SWA_KV_HEADS, SWA_HEAD_DIM, SWA_WINDOW = 6, 2, 128, 128
RET_HEADS, RET_QK_DIM, RET_V_DIM, RET_CHUNK = 6, 64, 128, 128
MLA_HEADS, MLA_NOPE_DIM, MLA_ROPE_DIM, MLA_V_DIM, MLA_KV_RANK = 4, 128, 64, 128, 256
N_GROUPS, EXPERTS_PER_GROUP, TOP_K = 4, 8, 2
N_EXPERTS = N_GROUPS * EXPERTS_PER_GROUP
ROPE_BASE = 10000.0
NORM_EPS = 1e-6
NEG_INF = -1e30
LANES = 128

_SQ, _RV, _GF, _GB, _MQN, _SK, _SV, _MQR, _CKV, _RQ, _RK, _KR = (
    0, 768, 1536, 2304, 3072, 3584, 3840, 4096, 4352, 4608, 4992, 5376)
_NP = 5504
_LOG2E = 1.4426950408889634
_SWA_SCALE = SWA_HEAD_DIM ** -0.5
_RET_SCALE = RET_QK_DIM ** -0.5
_MLA_SCALE = (MLA_NOPE_DIM + MLA_ROPE_DIM) ** -0.5 * _LOG2E
_SEGMENTS = ((_SQ, 768, 0, _SWA_SCALE), (_RV, 768, None, 1.0), (_GF, 768, None, 1.0), (_GB, 768, None, 1.0),
             (_MQN, 512, None, _MLA_SCALE), (_SK, 256, 0, 1.0), (_SV, 256, None, 1.0), (_MQR, 256, 2, _MLA_SCALE),
             (_CKV, 256, None, 1.0), (_RQ, 384, 1, _RET_SCALE), (_RK, 384, 1, 1.0), (_KR, 128, 2, 1.0))

_VMEM_LIMIT = 48 * 1024 * 1024


def _cparams(*sem):
    return pltpu.CompilerParams(dimension_semantics=sem, vmem_limit_bytes=_VMEM_LIMIT)


def _tile(n, pref):
    t = min(n, pref)
    while n % t:
        t //= 2
    return t


def _ln(x):
    mu = jnp.mean(x, axis=-1, keepdims=True)
    xc = x - mu
    var = jnp.mean(xc * xc, axis=-1, keepdims=True)
    return xc * lax.rsqrt(var + NORM_EPS)


def _silu(x):
    return x / (1.0 + jnp.exp(-x))


def _dot(a, b):
    return jnp.dot(a, b, preferred_element_type=f32)


def _dot_nt(a, b):
    return lax.dot_general(a, b, (((1,), (1,)), ((), ())), preferred_element_type=f32)


def _dot_tn(a, b):
    return lax.dot_general(a, b, (((0,), (0,)), ((), ())), preferred_element_type=f32)


def _ada_kernel(c_ref, w_ref, b_ref, o_ref):
    s = _silu(c_ref[...]).astype(bf16)
    o_ref[0] = _dot(s, w_ref[0].astype(bf16)) + b_ref[0]


def _ada(cc, w_ada, b_ada):
    depth, d, n = w_ada.shape
    tn = _tile(n, 1024)
    return pl.pallas_call(
        _ada_kernel,
        grid=(depth, n // tn),
        in_specs=[pl.BlockSpec((8, d), lambda l, j: (0, 0)),
                  pl.BlockSpec((1, d, tn), lambda l, j: (l, 0, j)),
                  pl.BlockSpec((1, 1, tn), lambda l, j: (l, 0, j))],
        out_specs=pl.BlockSpec((1, 8, tn), lambda l, j: (l, 0, j)),
        out_shape=jax.ShapeDtypeStruct((depth, 8, n), f32),
        compiler_params=_cparams("parallel", "parallel"),
        name="ada",
    )(cc, w_ada, b_ada.reshape(depth, 1, n))


_INPROJ_CHUNK = 512


def _inproj_kernel(xl_ref, xc_ref, mod_ref, tab_ref, w_ref, o_ref, *, nlb):
    tm = xl_ref.shape[0]
    h = (_ln(_stream_block(xl_ref, xc_ref, nlb)) * (1.0 + mod_ref[0, 1:2, :]) + mod_ref[0, 0:1, :]).astype(bf16)
    lane = lax.broadcasted_iota(i32, (tm, LANES), 1)
    first = {32: (lane & 63) < 32, 16: (lane & 31) < 16}
    slab_kind = {}
    for off, width, typ, scale in _SEGMENTS:
        for k in range(off // LANES, (off + width) // LANES):
            slab_kind[k] = (typ, scale)
    for c0 in range(0, _NP, _INPROJ_CHUNK):
        c1 = min(c0 + _INPROJ_CHUNK, _NP)
        acc = _dot(h, w_ref[:, c0:c1])
        for k in range(c0 // LANES, c1 // LANES):
            typ, scale = slab_kind[k]
            xk = acc[:, k * LANES - c0:(k + 1) * LANES - c0]
            if scale != 1.0:
                xk = xk * scale
            if typ is not None:
                half = 16 if typ == 2 else 32
                partner = jnp.where(first[half], pltpu.roll(xk, LANES - half, 1), pltpu.roll(xk, half, 1))
                xk = xk * tab_ref[typ, 0] + partner * tab_ref[typ, 1]
            o_ref[:, k * LANES:(k + 1) * LANES] = xk.astype(bf16)


def _inproj_rows(lat_len, n_ctx):
    tm = _tile(lat_len, 256)
    while n_ctx % tm:
        tm //= 2
    return tm


def _inproj(x_lat, x_ctx, mod, tab, w, n_lat, lat_len, ctx_len, n_batch):
    d = x_lat.shape[1]
    t = n_lat + n_batch * ctx_len
    tm = _inproj_rows(lat_len, t - n_lat)
    nlb, bpb = n_lat // tm, lat_len // tm
    ctx_blocks = max(1, ctx_len // tm)

    def grp(i):
        return jnp.where(i < nlb, i // bpb, n_batch)

    def posblk(i):
        return jnp.where(i < nlb, i % bpb, bpb + (i - nlb) % ctx_blocks)

    return pl.pallas_call(
        functools.partial(_inproj_kernel, nlb=nlb),
        grid=(t // tm,),
        in_specs=_stream_specs(x_lat, x_ctx, tm, n_lat) + [
                  pl.BlockSpec((1, 6, d), lambda i: (grp(i), 0, 0)),
                  pl.BlockSpec((3, 2, tm, LANES), lambda i: (0, 0, posblk(i), 0)),
                  pl.BlockSpec((d, _NP), lambda i: (0, 0), pipeline_mode=pl.Buffered(1))],
        out_specs=pl.BlockSpec((tm, _NP), lambda i: (i, 0)),
        out_shape=jax.ShapeDtypeStruct((t, _NP), bf16),
        compiler_params=_cparams("parallel"),
        name="inproj",
    )(x_lat, x_ctx, mod, tab, w)


def _rope_tables(lat_len, ctx_len, tm):
    lane = jnp.arange(LANES)
    t = jnp.arange(lat_len)
    rows = (t // GRID_W).astype(f32)
    cols = (t % GRID_W).astype(f32)

    def freq(half):
        return ROPE_BASE ** (-jnp.arange(half, dtype=f32) / half)

    def table(pos, inv, sign):
        ang = pos * inv[None, :]
        return jnp.stack([jnp.cos(ang), jnp.sin(ang) * sign[None, :]])

    def ident(n):
        return jnp.stack([jnp.ones((n, LANES), f32), jnp.zeros((n, LANES), f32)])

    sign32 = jnp.where((lane % 64) < 32, -1.0, 1.0).astype(f32)
    sign16 = jnp.where((lane % 32) < 16, -1.0, 1.0).astype(f32)
    inv32 = freq(32)[lane % 32]
    inv16 = freq(16)[lane % 16]
    pos_swa = jnp.where((lane // 64)[None, :] == 0, rows[:, None], cols[:, None])
    pos_mla = jnp.where(((lane % 64) // 32)[None, :] == 0, rows[:, None], cols[:, None])
    rep = max(1, tm // ctx_len)
    nc = rep * ctx_len
    pos_ret_l = jnp.broadcast_to((ctx_len + t).astype(f32)[:, None], (lat_len, LANES))
    pos_ret_c = jnp.broadcast_to(jnp.tile(jnp.arange(ctx_len), rep).astype(f32)[:, None], (nc, LANES))
    t0 = jnp.concatenate([table(pos_swa, inv32, sign32), ident(nc)], axis=1)
    t1 = jnp.concatenate([table(pos_ret_l, inv32, sign32), table(pos_ret_c, inv32, sign32)], axis=1)
    t2 = jnp.concatenate([table(pos_mla, inv16, sign16), ident(nc)], axis=1)
    return jnp.stack([t0, t1, t2])


def _mla_expand_kernel(ckv_ref, kr_ref, g_ref, wuk_ref, wuv_ref, k_ref, v_ref):
    c = ckv_ref[...].astype(f32)
    cn = (c * lax.rsqrt(jnp.mean(c * c, axis=-1, keepdims=True) + NORM_EPS) * g_ref[...]).astype(bf16)
    kn = _dot(cn, wuk_ref[...]).astype(bf16)
    vv = _dot(cn, wuv_ref[...]).astype(bf16)
    kr = kr_ref[...]
    for h in range(MLA_HEADS):
        k_ref[h, :, 0:128] = kn[:, h * 128:(h + 1) * 128]
        k_ref[h, :, 128:256] = kr
        v_ref[h] = vv[:, h * 128:(h + 1) * 128]


def _mla_expand(p, g, wuk, wuv):
    t = p.shape[0]
    tm = _tile(t, 512)
    return pl.pallas_call(
        _mla_expand_kernel,
        grid=(t // tm,),
        in_specs=[pl.BlockSpec((tm, 256), lambda i: (i, _CKV // 256)),
                  pl.BlockSpec((tm, 128), lambda i: (i, _KR // 128)),
                  pl.BlockSpec((1, 256), lambda i: (0, 0)),
                  pl.BlockSpec((256, 512), lambda i: (0, 0)),
                  pl.BlockSpec((256, 512), lambda i: (0, 0))],
        out_specs=[pl.BlockSpec((MLA_HEADS, tm, 256), lambda i: (0, i, 0)),
                   pl.BlockSpec((MLA_HEADS, tm, 128), lambda i: (0, i, 0))],
        out_shape=[jax.ShapeDtypeStruct((MLA_HEADS, t, 256), bf16), jax.ShapeDtypeStruct((MLA_HEADS, t, 128), bf16)],
        compiler_params=_cparams("parallel"),
        name="mla_expand",
    )(p, p, g, wuk, wuv)


def _mla_kernel(*refs, with_lat):
    if with_lat:
        qn_ref, qr_ref, kc_ref, vc_ref, kl_ref, vl_ref, o_ref, qx_scr = refs
    else:
        qn_ref, qr_ref, kc_ref, vc_ref, _, o_ref, qx_scr = refs
    lane = lax.broadcasted_iota(i32, (qn_ref.shape[0], LANES), 1)
    for h in range(MLA_HEADS):
        slab = qr_ref[:, (h // 2) * 128:(h // 2 + 1) * 128].astype(f32)
        if h % 2:
            slab = pltpu.roll(slab, 64, 1)
        qx_scr[h, :, 0:128] = qn_ref[:, h * 128:(h + 1) * 128]
        qx_scr[h, :, 128:256] = jnp.where(lane < 64, slab, 0.0).astype(bf16)
    for h in range(MLA_HEADS):
        q = qx_scr[h]
        parts = [(_dot_nt(q, kc_ref[h]), vc_ref[h])]
        if with_lat:
            parts.append((_dot_nt(q, kl_ref[h]), vl_ref[h]))
        m = None
        for s, _ in parts:
            mx = jnp.max(s, axis=1, keepdims=True)
            m = mx if m is None else jnp.maximum(m, mx)
        den, acc = None, None
        for s, vv in parts:
            pr = jnp.exp2(s - m)
            sm = jnp.sum(pr, axis=1, keepdims=True)
            pv = _dot(pr.astype(bf16), vv)
            den = sm if den is None else den + sm
            acc = pv if acc is None else acc + pv
        o_ref[:, h * 128:(h + 1) * 128] = (acc / den).astype(bf16)


def _mla_attention(p, kx, v, n_batch, lat_len, ctx_len, compute_ctx):
    t = p.shape[0]
    tq = _tile(lat_len, 256)
    nlq = lat_len // tq
    cbase = n_batch * lat_len // ctx_len
    once = pl.Buffered(1)
    out = pl.pallas_call(
        functools.partial(_mla_kernel, with_lat=True),
        grid=(n_batch, nlq),
        in_specs=[pl.BlockSpec((tq, 512), lambda b, qi: (b * nlq + qi, _MQN // 512)),
                  pl.BlockSpec((tq, 256), lambda b, qi: (b * nlq + qi, _MQR // 256)),
                  pl.BlockSpec((MLA_HEADS, ctx_len, 256), lambda b, qi: (0, cbase + b, 0)),
                  pl.BlockSpec((MLA_HEADS, ctx_len, 128), lambda b, qi: (0, cbase + b, 0)),
                  pl.BlockSpec((MLA_HEADS, lat_len, 256), lambda b, qi: (0, b, 0), pipeline_mode=once),
                  pl.BlockSpec((MLA_HEADS, lat_len, 128), lambda b, qi: (0, b, 0), pipeline_mode=once)],
        out_specs=pl.BlockSpec((tq, 512), lambda b, qi: (b * nlq + qi, 0)),
        out_shape=jax.ShapeDtypeStruct((t, 512), bf16),
        scratch_shapes=[pltpu.VMEM((MLA_HEADS, tq, 256), bf16)],
        compiler_params=_cparams("parallel", "arbitrary"),
        name="mla_attn",
    )(p, p, kx, v, kx, v)
    if compute_ctx:
        out = pl.pallas_call(
            functools.partial(_mla_kernel, with_lat=False),
            grid=(n_batch,),
            in_specs=[pl.BlockSpec((ctx_len, 512), lambda b: (cbase + b, _MQN // 512)),
                      pl.BlockSpec((ctx_len, 256), lambda b: (cbase + b, _MQR // 256)),
                      pl.BlockSpec((MLA_HEADS, ctx_len, 256), lambda b: (0, cbase + b, 0)),
                      pl.BlockSpec((MLA_HEADS, ctx_len, 128), lambda b: (0, cbase + b, 0)),
                      pl.BlockSpec(memory_space=pl.ANY)],
            out_specs=pl.BlockSpec((ctx_len, 512), lambda b: (cbase + b, 0)),
            out_shape=jax.ShapeDtypeStruct((t, 512), bf16),
            scratch_shapes=[pltpu.VMEM((MLA_HEADS, ctx_len, 256), bf16)],
            input_output_aliases={4: 0},
            compiler_params=_cparams("parallel"),
            name="mla_ctx",
        )(p, p, kx, v, out)
    return out


def _swa_softmax_out(parts, sink_col, o_ref, g):
    m = sink_col
    for s, _ in parts:
        m = jnp.maximum(m, jnp.max(s, axis=1, keepdims=True))
    den = jnp.exp(sink_col - m)
    acc = None
    for s, vv in parts:
        pr = jnp.exp(s - m)
        den = den + jnp.sum(pr, axis=1, keepdims=True)
        pv = _dot(pr.astype(bf16), vv)
        acc = pv if acc is None else acc + pv
    o = acc / den
    nq = o.shape[0] // 3
    for j in range(3):
        o_ref[:, (3 * g + j) * 128:(3 * g + j + 1) * 128] = o[j * nq:(j + 1) * nq].astype(bf16)


def _sink_col(sink_ref, g, nq):
    row = lax.broadcasted_iota(i32, (3 * nq, 1), 0)
    return jnp.where(row < nq, sink_ref[3 * g], jnp.where(row < 2 * nq, sink_ref[3 * g + 1], sink_ref[3 * g + 2]))


def _swa_kernel(sink_ref, q_ref, kp_ref, ko_ref, kn_ref, vp_ref, vo_ref, vn_ref, kc_ref, vc_ref, o_ref):
    i = pl.program_id(1)
    nb = pl.num_programs(1)
    blk = q_ref.shape[0]
    q = q_ref[...]
    r = lax.broadcasted_iota(i32, (3 * blk, 3 * blk), 0) & (blk - 1)
    c = lax.broadcasted_iota(i32, (3 * blk, 3 * blk), 1)
    lo = jnp.where(i == 0, blk, 0)
    hi = jnp.where(i == nb - 1, 2 * blk, 3 * blk)
    d = c - r
    valid = (d >= 0) & (d <= 2 * SWA_WINDOW) & (c >= lo) & (c < hi)
    for g in range(SWA_KV_HEADS):
        hs = slice(g * 128, (g + 1) * 128)
        qg = jnp.concatenate([q[:, (3 * g + j) * 128:(3 * g + j + 1) * 128] for j in range(3)], axis=0)
        kloc = jnp.concatenate([kp_ref[:, hs], ko_ref[:, hs], kn_ref[:, hs]], axis=0)
        vloc = jnp.concatenate([vp_ref[:, hs], vo_ref[:, hs], vn_ref[:, hs]], axis=0)
        s_loc = jnp.where(valid, _dot_nt(qg, kloc), NEG_INF)
        s_ctx = _dot_nt(qg, kc_ref[:, hs])
        _swa_softmax_out([(s_ctx, vc_ref[:, hs]), (s_loc, vloc)], _sink_col(sink_ref, g, blk), o_ref, g)


def _swa_ctx_kernel(sink_ref, q_ref, kc_ref, vc_ref, prev_ref, o_ref):
    del prev_ref
    q = q_ref[...]
    nq = q.shape[0]
    for g in range(SWA_KV_HEADS):
        hs = slice(g * 128, (g + 1) * 128)
        qg = jnp.concatenate([q[:, (3 * g + j) * 128:(3 * g + j + 1) * 128] for j in range(3)], axis=0)
        s_ctx = _dot_nt(qg, kc_ref[:, hs])
        _swa_softmax_out([(s_ctx, vc_ref[:, hs])], _sink_col(sink_ref, g, nq), o_ref, g)


def _swa_attention(p, sink, n_batch, lat_len, ctx_len, compute_ctx):
    t = p.shape[0]
    blk = SWA_WINDOW
    nb = lat_len // blk
    cbase = n_batch * lat_len // ctx_len
    smem = pl.BlockSpec(memory_space=pltpu.SMEM)

    def kspec(col, off):
        return pl.BlockSpec((blk, 256), lambda b, i: (b * nb + jnp.clip(i + off, 0, nb - 1), col // 256))

    out = pl.pallas_call(
        _swa_kernel,
        grid=(n_batch, nb),
        in_specs=[smem,
                  pl.BlockSpec((blk, 768), lambda b, i: (b * nb + i, _SQ // 768)),
                  kspec(_SK, -1), kspec(_SK, 0), kspec(_SK, 1),
                  kspec(_SV, -1), kspec(_SV, 0), kspec(_SV, 1),
                  pl.BlockSpec((ctx_len, 256), lambda b, i: (cbase + b, _SK // 256)),
                  pl.BlockSpec((ctx_len, 256), lambda b, i: (cbase + b, _SV // 256))],
        out_specs=pl.BlockSpec((blk, 768), lambda b, i: (b * nb + i, 0)),
        out_shape=jax.ShapeDtypeStruct((t, 768), bf16),
        compiler_params=_cparams("parallel", "parallel"),
        name="swa_attn",
    )(sink, p, p, p, p, p, p, p, p, p)
    if compute_ctx:
        out = pl.pallas_call(
            _swa_ctx_kernel,
            grid=(n_batch,),
            in_specs=[smem,
                      pl.BlockSpec((ctx_len, 768), lambda b: (cbase + b, _SQ // 768)),
                      pl.BlockSpec((ctx_len, 256), lambda b: (cbase + b, _SK // 256)),
                      pl.BlockSpec((ctx_len, 256), lambda b: (cbase + b, _SV // 256)),
                      pl.BlockSpec(memory_space=pl.ANY)],
            out_specs=pl.BlockSpec((ctx_len, 768), lambda b: (cbase + b, 0)),
            out_shape=jax.ShapeDtypeStruct((t, 768), bf16),
            input_output_aliases={4: 0},
            compiler_params=_cparams("parallel"),
            name="swa_ctx",
        )(sink, p, p, p, out)
    return out


def _ret_kernel(lg_ref, qf_ref, kf_ref, vf_ref, qb_ref, kb_ref, vb_ref, of_ref, ob_ref,
                s_scr, d_scr, qd_scr, kd_scr, gc_scr):
    i = pl.program_id(1)
    cc = RET_CHUNK

    @pl.when(i == 0)
    def _():
        s_scr[...] = jnp.zeros(s_scr.shape, f32)
        r = lax.broadcasted_iota(i32, (cc, cc), 0).astype(f32)
        c = lax.broadcasted_iota(i32, (cc, cc), 1).astype(f32)
        for direction in range(2):
            backward = direction == 1
            diff = (c - r) if backward else (r - c)
            for h in range(RET_HEADS):
                lgh = lg_ref[direction, h]
                j = direction * RET_HEADS + h
                d_scr[j] = jnp.where(diff >= 0, jnp.exp(lgh * jnp.maximum(diff, 0.0)), 0.0)
                qd_scr[j] = jnp.exp(lgh * ((cc - r) if backward else (r + 1.0)))
                kd_scr[j] = jnp.exp(lgh * (r if backward else (cc - 1.0 - r)))
                gc_scr[j] = jnp.exp(lgh * cc + jnp.zeros((cc, cc), f32))

    lane = lax.broadcasted_iota(i32, (cc, LANES), 1)
    for pair in range(RET_HEADS // 2):
        for direction, (q_ref, k_ref, v_ref, o_ref) in enumerate(
                ((qf_ref, kf_ref, vf_ref, of_ref), (qb_ref, kb_ref, vb_ref, ob_ref))):
            q2 = q_ref[:, pair * 128:(pair + 1) * 128].astype(f32)
            k2b = k_ref[:, pair * 128:(pair + 1) * 128]
            k2 = k2b.astype(f32)
            for sub in range(2):
                h = 2 * pair + sub
                j = direction * RET_HEADS + h
                hs = slice(h * 128, (h + 1) * 128)
                qa = jnp.where((lane < 64) if sub == 0 else (lane >= 64), q2, 0.0)
                s = _dot_nt(qa.astype(bf16), k2b)
                vh = v_ref[:, hs]
                y = (_dot((s * d_scr[j]).astype(bf16), vh)
                     + _dot((qa * qd_scr[j]).astype(bf16), s_scr[j].astype(bf16)))
                s_scr[j] = gc_scr[j] * s_scr[j] + _dot_tn((k2 * kd_scr[j]).astype(bf16), vh)
                o_ref[:, hs] = _ln(y).astype(bf16)


def _retention(p, lg, n_batch, lat_len, ctx_len):
    t = p.shape[0]
    cc = RET_CHUNK
    ncc, nlc = ctx_len // cc, lat_len // cc
    cbase = n_batch * nlc
    smem = pl.BlockSpec(memory_space=pltpu.SMEM)
    scratch = [pltpu.VMEM((2 * RET_HEADS, cc, cc), f32) for _ in range(5)]

    def fwd_row(b, i):
        return jnp.where(i < ncc, cbase + b * ncc + i, b * nlc + (i - ncc))

    def bwd_row(b, i):
        return jnp.where(i < ncc, cbase + b * ncc + (ncc - 1 - i), b * nlc + (nlc - 1 - (i - ncc)))

    def specs(row):
        return [pl.BlockSpec((cc, 384), lambda b, i: (row(b, i), _RQ // 384)),
                pl.BlockSpec((cc, 384), lambda b, i: (row(b, i), _RK // 384)),
                pl.BlockSpec((cc, 768), lambda b, i: (row(b, i), _RV // 768))]

    return pl.pallas_call(
        _ret_kernel,
        grid=(n_batch, ncc + nlc),
        in_specs=[smem] + specs(fwd_row) + specs(bwd_row),
        out_specs=[pl.BlockSpec((cc, 768), lambda b, i: (fwd_row(b, i), 0)),
                   pl.BlockSpec((cc, 768), lambda b, i: (bwd_row(b, i), 0))],
        out_shape=[jax.ShapeDtypeStruct((t, 768), bf16), jax.ShapeDtypeStruct((t, 768), bf16)],
        scratch_shapes=scratch,
        compiler_params=_cparams("parallel", "arbitrary"),
        name="retention",
    )(lg, p, p, p, p, p, p)


def _stream_specs(x_lat, x_ctx, tm, n_lat):
    d = x_lat.shape[1]
    nlb = n_lat // tm
    ctx_first = nlb if x_ctx is x_lat else 0
    return [pl.BlockSpec((tm, d), lambda i: (jnp.minimum(i, nlb - 1), 0)),
            pl.BlockSpec((tm, d), lambda i: (jnp.maximum(i - nlb, 0) + ctx_first, 0))]


def _stream_block(xl_ref, xc_ref, nlb):
    return jnp.where(pl.program_id(0) < nlb, xl_ref[...], xc_ref[...])


def _outproj_kernel(a_ref, yf_ref, yb_ref, gf_ref, gb_ref, m_ref, xl_ref, xc_ref, mod_ref, w_ref, g_ref, b_ref,
                    rw_ref, rb_ref, xo_ref, h_ref, rout_ref, routt_ref, cnt_ref, carry_scr, *, alpha, nlb):
    @pl.when(pl.program_id(0) == 0)
    def _():
        carry_scr[...] = jnp.zeros(carry_scr.shape, f32)

    ret = (_silu(gf_ref[...].astype(f32)) * yf_ref[...].astype(f32)
           + _silu(gb_ref[...].astype(f32)) * yb_ref[...].astype(f32)).astype(bf16)
    y = (_dot(a_ref[...], w_ref[0:768, :]) + _dot(ret, w_ref[768:1536, :]) + _dot(m_ref[...], w_ref[1536:2048, :]))
    xn = _ln(alpha * _stream_block(xl_ref, xc_ref, nlb) + mod_ref[0, 2:3, :] * y) * g_ref[...] + b_ref[...]
    xo_ref[...] = xn
    h = _ln(xn) * (1.0 + mod_ref[0, 4:5, :]) + mod_ref[0, 3:4, :]
    h_ref[...] = h
    table = _route(_dot(h.astype(bf16), rw_ref[...]) + rb_ref[...], carry_scr)
    rout_ref[...] = table
    routt_ref[...] = jnp.transpose(table)[0:8, :]
    cnt_ref[...] = carry_scr[...]


def _outproj(a, yf, yb, p, m, x_lat, x_ctx, mod, w, g, b, rw, rb, n_rows, n_lat, lat_len, n_batch, alpha):
    d = x_lat.shape[1]
    tm = _tile(lat_len, 256)
    while (n_rows - n_lat) % tm:
        tm //= 2
    nlb, bpb = n_lat // tm, lat_len // tm

    def grp(i):
        return jnp.where(i < nlb, i // bpb, n_batch)

    row = lambda i: (i, 0)
    const = lambda i: (0, 0)
    return pl.pallas_call(
        functools.partial(_outproj_kernel, alpha=alpha, nlb=nlb),
        grid=(n_rows // tm,),
        in_specs=[pl.BlockSpec((tm, 768), row), pl.BlockSpec((tm, 768), row), pl.BlockSpec((tm, 768), row),
                  pl.BlockSpec((tm, 768), lambda i: (i, _GF // 768)), pl.BlockSpec((tm, 768), lambda i: (i, _GB // 768)),
                  pl.BlockSpec((tm, 512), row)] + _stream_specs(x_lat, x_ctx, tm, n_lat) + [
                  pl.BlockSpec((1, 6, d), lambda i: (grp(i), 0, 0)),
                  pl.BlockSpec(w.shape, const),
                  pl.BlockSpec((1, d), const), pl.BlockSpec((1, d), const),
                  pl.BlockSpec((d, LANES), const), pl.BlockSpec((1, LANES), const)],
        out_specs=[pl.BlockSpec((tm, d), row), pl.BlockSpec((tm, d), row),
                   pl.BlockSpec((tm, LANES), row), pl.BlockSpec((8, tm), row), pl.BlockSpec((8, LANES), const)],
        out_shape=[jax.ShapeDtypeStruct((n_rows, d), f32), jax.ShapeDtypeStruct((n_rows, d), f32),
                   jax.ShapeDtypeStruct((n_rows, LANES), f32), jax.ShapeDtypeStruct((n_rows // tm * 8, tm), f32),
                   jax.ShapeDtypeStruct((8, LANES), f32)],
        scratch_shapes=[pltpu.VMEM((8, LANES), f32)],
        compiler_params=_cparams("arbitrary"),
        name="outproj",
    )(a, yf, yb, p, p, m, x_lat, x_ctx, mod, w, g, b, rw, rb)


_ROUTE_LANE0 = N_GROUPS


def _route(logits, carry_scr):
    tm = logits.shape[0]
    lane = lax.broadcasted_iota(i32, (tm, LANES), 1)
    lane_f = lane.astype(f32)
    big = float(2 * LANES)
    gl = jnp.where(lane < N_GROUPS, logits, -jnp.inf)
    gmax = jnp.max(gl, axis=1, keepdims=True)
    gidx = jnp.min(jnp.where(gl == gmax, lane_f, big), axis=1, keepdims=True)
    p_group = 1.0 / jnp.sum(jnp.exp(gl - gmax), axis=1, keepdims=True)
    egroup = ((lane - _ROUTE_LANE0) >> 3).astype(f32)
    in_group = (lane >= _ROUTE_LANE0) & (lane < _ROUTE_LANE0 + N_EXPERTS) & (egroup == gidx)
    ev = jnp.where(in_group, logits, -jnp.inf)
    e1 = jnp.max(ev, axis=1, keepdims=True)
    i1 = jnp.min(jnp.where(ev == e1, lane_f, big), axis=1, keepdims=True)
    ev2 = jnp.where(lane_f == i1, -jnp.inf, ev)
    e2 = jnp.max(ev2, axis=1, keepdims=True)
    i2 = jnp.min(jnp.where(ev2 == e2, lane_f, big), axis=1, keepdims=True)
    tt = jnp.exp(e2 - e1)
    w1 = p_group / (1.0 + tt)
    w2 = p_group * tt / (1.0 + tt)

    hit1 = lane_f == i1
    hit2 = lane_f == i2
    onehot = jnp.where(hit1, 1.0, jnp.where(hit2, 1.0, 0.0))
    rr = lax.broadcasted_iota(i32, (tm, tm), 0)
    cc = lax.broadcasted_iota(i32, (tm, tm), 1)
    lower = jnp.where(cc < rr, 1.0, 0.0).astype(bf16)
    before = _dot(lower, onehot.astype(bf16)) + carry_scr[0:1, :]
    rank1 = jnp.sum(jnp.where(hit1, before, 0.0), axis=1, keepdims=True)
    rank2 = jnp.sum(jnp.where(hit2, before, 0.0), axis=1, keepdims=True)
    carry_scr[0:1, :] = carry_scr[0:1, :] + jnp.sum(onehot, axis=0, keepdims=True)

    return jnp.where(lane == 0, i1 - _ROUTE_LANE0, jnp.where(lane == 1, i2 - _ROUTE_LANE0, jnp.where(
        lane == 2, rank1, jnp.where(lane == 3, rank2, jnp.where(lane == 4, w1, jnp.where(lane == 5, w2, 0.0))))))


def _dispatch_kernel(cnt_ref, pstart_ref, padded_ref, dest_ref, h_ref, o_hbm, sem):
    i = pl.program_id(0)
    nb = pl.num_programs(0) - 1
    tm = dest_ref.shape[2] // TOP_K

    def row_copy(src_row, dst_row):
        return pltpu.make_async_copy(h_ref.at[pl.ds(src_row, 1), :], o_hbm.at[dst_row], sem)

    @pl.when(i < nb)
    def _():
        def issue(r8, carry):
            base = pl.multiple_of(r8 * 8, 8)
            for j in range(8):
                for k in range(TOP_K):
                    row_copy(base + j, dest_ref[0, 0, k * tm + base + j]).start()
            return carry

        lax.fori_loop(0, tm // 8, issue, 0)
        for _ in range(TOP_K):
            pltpu.make_async_copy(h_ref, o_hbm.at[pl.ds(0, tm), 0], sem).wait()

    @pl.when(i == nb)
    def _():
        def per_expert(e, total):
            lo, hi = cnt_ref[e], padded_ref[e]

            def fill(r, carry):
                row_copy(0, pstart_ref[e] + r).start()
                return carry

            lax.fori_loop(lo, hi, fill, 0)
            return total + (hi - lo)

        total = lax.fori_loop(0, N_EXPERTS, per_expert, 0)

        def drain(r, carry):
            row_copy(0, 0).wait()
            return carry

        lax.fori_loop(0, total, drain, 0)


def _dispatch(h, dest_rows, counts, pstart, padded, n_rows, tm):
    n, d = h.shape
    nb = n // tm
    grid_spec = pltpu.PrefetchScalarGridSpec(
        num_scalar_prefetch=3,
        grid=(nb + 1,),
        in_specs=[pl.BlockSpec((1, 1, TOP_K * tm), lambda i, *_: (jnp.minimum(i, nb - 1), 0, 0),
                               memory_space=pltpu.SMEM),
                  pl.BlockSpec((tm, d), lambda i, *_: (jnp.minimum(i, nb - 1), 0))],
        out_specs=pl.BlockSpec(memory_space=pl.ANY),
        scratch_shapes=[pltpu.SemaphoreType.DMA(())],
    )
    return pl.pallas_call(
        _dispatch_kernel,
        grid_spec=grid_spec,
        out_shape=jax.ShapeDtypeStruct((n_rows, 1, d), f32),
        compiler_params=_cparams("arbitrary"),
        name="moe_dispatch",
    )(counts, pstart, padded, dest_rows.reshape(nb, 1, TOP_K * tm), h)


_CAST_ROWS = 256


def _expert_kernel(nused_ref, be_ref, first_ref, nxt_ref, half_ref, x_hbm, wg_hbm, wu_hbm, wd_hbm, y_hbm,
                   wg_f, wu_f, wd_f, wg_b, wu_b, wd_b, xbuf, ybuf, sems, xsems, ysems, *, layer):
    i = pl.program_id(0)
    n_used = nused_ref[0]
    used = i < n_used
    rb = xbuf.shape[1]
    slot = i % 2
    staged = ((wg_hbm, wg_f, wg_b), (wu_hbm, wu_f, wu_b), (wd_hbm, wd_f, wd_b))

    def x_fetch(blk, s):
        return pltpu.make_async_copy(x_hbm.at[pl.ds(pl.multiple_of(blk * rb, rb), rb), 0], xbuf.at[s], xsems.at[s])

    def y_store(blk, s):
        return pltpu.make_async_copy(ybuf.at[s], y_hbm.at[pl.ds(pl.multiple_of(blk * rb, rb), rb), 0], ysems.at[s])

    @pl.when(i == 0)
    def _():
        x_fetch(0, 0).start()

    @pl.when(i + 1 < n_used)
    def _():
        x_fetch(i + 1, 1 - slot).start()

    def fetch(e):
        return [pltpu.make_async_copy(hbm.at[layer, e], stage, sems.at[j]) for j, (hbm, stage, _) in enumerate(staged)]

    @pl.when(i == 0)
    def _():
        for cp in fetch(be_ref[0]):
            cp.start()

    @pl.when(used & (first_ref[i] == 1))
    def _():
        for cp in fetch(be_ref[i]):
            cp.wait()
        for _, stage, dst in staged:
            rows = stage.shape[0]
            step = min(_CAST_ROWS, rows)

            def cast(c, carry, stage=stage, dst=dst, step=step):
                sl = pl.ds(pl.multiple_of(c * step, step), step)
                dst[sl, :] = stage[sl, :].astype(bf16)
                return carry

            lax.fori_loop(0, rows // step, cast, 0)
        nxt = nxt_ref[be_ref[i]]

        @pl.when(nxt >= 0)
        def _():
            for cp in fetch(nxt):
                cp.start()

    @pl.when(used)
    def _():
        x_fetch(i, slot).wait()

    @pl.when(used & (i >= 2))
    def _():
        y_store(i - 2, slot).wait()

    def ffn(rows):
        x = xbuf[slot, 0:rows, :].astype(bf16)
        act = (_silu(_dot(x, wg_b[...])) * _dot(x, wu_b[...])).astype(bf16)
        ybuf[slot, 0:rows, :] = _dot(act, wd_b[...])
        if rows < rb:
            ybuf[slot, rows:rb, :] = jnp.zeros((rb - rows, ybuf.shape[2]), f32)

    @pl.when(used & (half_ref[i] == 0))
    def _():
        ffn(rb)

    @pl.when(used & (half_ref[i] == 1))
    def _():
        ffn(rb // 2)

    @pl.when(used)
    def _():
        y_store(i, slot).start()

    @pl.when(i == n_used - 1)
    def _():
        y_store(i, slot).wait()

        @pl.when(i >= 1)
        def _():
            y_store(i - 1, 1 - slot).wait()


_EXPERT_VMEM_LIMIT = 60 * 1024 * 1024


def _experts(xs3, n_used, block_expert, first, nxt, half, wg, wu, wd, layer, rb):
    d, hid = wg.shape[2], wg.shape[3]
    n_blocks = xs3.shape[0] // rb

    anyspace = pl.BlockSpec(memory_space=pl.ANY)
    grid_spec = pltpu.PrefetchScalarGridSpec(
        num_scalar_prefetch=5,
        grid=(n_blocks,),
        in_specs=[anyspace, anyspace, anyspace, anyspace],
        out_specs=anyspace,
        scratch_shapes=[pltpu.VMEM((d, hid), f32), pltpu.VMEM((d, hid), f32), pltpu.VMEM((hid, d), f32),
                        pltpu.VMEM((d, hid), bf16), pltpu.VMEM((d, hid), bf16), pltpu.VMEM((hid, d), bf16),
                        pltpu.VMEM((2, rb, d), f32), pltpu.VMEM((2, rb, d), f32),
                        pltpu.SemaphoreType.DMA((3,)), pltpu.SemaphoreType.DMA((2,)), pltpu.SemaphoreType.DMA((2,))],
    )
    return pl.pallas_call(
        functools.partial(_expert_kernel, layer=layer),
        grid_spec=grid_spec,
        out_shape=jax.ShapeDtypeStruct(xs3.shape, f32),
        compiler_params=pltpu.CompilerParams(dimension_semantics=("arbitrary",), vmem_limit_bytes=_EXPERT_VMEM_LIMIT),
        name="moe_experts",
    )(n_used, block_expert, first, nxt, half, xs3, wg, wu, wd)


def _combine_kernel(dcur_ref, dnxt_ref, y_hbm, x_ref, gate_ref, mod_ref, g_ref, b_ref, o_ref, ybuf, sems, *, alpha):
    i = pl.program_id(0)
    nb = pl.num_programs(0)
    tm = x_ref.shape[0]

    def issue(dref, slot):
        def body(r8, carry):
            base = pl.multiple_of(r8 * 8, 8)
            for j in range(8):
                for k in range(TOP_K):
                    pltpu.make_async_copy(y_hbm.at[dref[0, 0, k * tm + base + j]],
                                          ybuf.at[slot, k, pl.ds(base + j, 1), :], sems.at[slot]).start()
            return carry

        lax.fori_loop(0, tm // 8, body, 0)

    @pl.when(i == 0)
    def _():
        issue(dcur_ref, 0)

    def step(slot):
        @pl.when(i + 1 < nb)
        def _():
            issue(dnxt_ref, 1 - slot)

        for k in range(TOP_K):
            pltpu.make_async_copy(y_hbm.at[pl.ds(0, tm), 0], ybuf.at[slot, k], sems.at[slot]).wait()
        gates = gate_ref[...]
        f = gates[:, 4:5] * ybuf[slot, 0] + gates[:, 5:6] * ybuf[slot, 1]
        o_ref[...] = _ln(alpha * x_ref[...] + mod_ref[0, 5:6, :] * f) * g_ref[...] + b_ref[...]

    for slot in range(2):
        pl.when(i % 2 == slot)(functools.partial(step, slot))


def _combine(y3, dest_rows, xs, rout, mod, g, b, n_lat, lat_len, n_batch, alpha, tm):
    n, d = xs.shape
    nb = n // tm
    nlb, bpb = n_lat // tm, lat_len // tm

    def grp(i):
        return jnp.where(i < nlb, i // bpb, n_batch)

    dest3 = dest_rows.reshape(nb, 1, TOP_K * tm)
    const = lambda i: (0, 0)
    return pl.pallas_call(
        functools.partial(_combine_kernel, alpha=alpha),
        grid=(nb,),
        in_specs=[pl.BlockSpec((1, 1, TOP_K * tm), lambda i: (i, 0, 0), memory_space=pltpu.SMEM),
                  pl.BlockSpec((1, 1, TOP_K * tm), lambda i: (jnp.minimum(i + 1, nb - 1), 0, 0),
                               memory_space=pltpu.SMEM),
                  pl.BlockSpec(memory_space=pl.ANY),
                  pl.BlockSpec((tm, d), lambda i: (i, 0)),
                  pl.BlockSpec((tm, LANES), lambda i: (i, 0)),
                  pl.BlockSpec((1, 6, d), lambda i: (grp(i), 0, 0)),
                  pl.BlockSpec((1, d), const), pl.BlockSpec((1, d), const)],
        out_specs=pl.BlockSpec((tm, d), lambda i: (i, 0)),
        out_shape=jax.ShapeDtypeStruct((n, d), f32),
        scratch_shapes=[pltpu.VMEM((2, TOP_K, tm, d), f32), pltpu.SemaphoreType.DMA((2,))],
        compiler_params=_cparams("arbitrary"),
        name="moe_combine",
    )(dest3, dest3, y3, xs, rout, mod, g, b)


def _moe(h, rout, routt, cnt, xs, wg, wu, wd, layer, mod, g, b, n_lat, lat_len, n_batch, alpha):
    n, d = xs.shape
    tm = routt.shape[1]
    rb = tm
    fields = routt.reshape(n // tm, 8, tm)
    eid = fields[:, 0:TOP_K, :].astype(i32)
    rank = fields[:, TOP_K:2 * TOP_K, :].astype(i32)
    counts = cnt[0, _ROUTE_LANE0:_ROUTE_LANE0 + N_EXPERTS].astype(i32)
    padded = (counts + rb - 1) // rb * rb
    pend = jnp.cumsum(padded)
    pstart = pend - padded
    dest_rows = rank
    for e in range(N_EXPERTS):
        dest_rows = dest_rows + jnp.where(eid == e, pstart[e], 0)
    dest_rows = dest_rows.reshape(-1)
    n_blocks = TOP_K * n // rb + N_EXPERTS
    n_used = (pend[-1:] // rb).astype(i32)
    blk = jnp.arange(n_blocks, dtype=i32)
    ex = jnp.arange(N_EXPERTS, dtype=i32)
    block_expert = jnp.minimum(jnp.sum((pend[None, :] <= (blk * rb)[:, None]).astype(i32), axis=1), N_EXPERTS - 1)
    block_expert = jnp.where(blk < n_used[0], block_expert, block_expert[jnp.maximum(n_used[0] - 1, 0)])
    first = ((blk < n_used[0]) & ((blk == 0) | (block_expert != jnp.roll(block_expert, 1)))).astype(i32)
    later_active = (ex[None, :] > ex[:, None]) & (counts > 0)[None, :]
    nxt = jnp.min(jnp.where(later_active, ex[None, :], N_EXPERTS), axis=1)
    nxt = jnp.where(nxt == N_EXPERTS, -1, nxt).astype(i32)
    valid = jnp.clip(pstart[block_expert] + counts[block_expert] - blk * rb, 0, rb)
    half = ((blk < n_used[0]) & (valid <= rb // 2)).astype(i32)
    xs3 = _dispatch(h, dest_rows, counts, pstart, padded, n_blocks * rb, tm)
    y3 = _experts(xs3, n_used, block_expert, first, nxt, half, wg, wu, wd, layer, rb)
    return _combine(y3, dest_rows, xs, rout, mod, g, b, n_lat, lat_len, n_batch, alpha, tm)


def _permute_w_in(w):
    d = w.shape[0]
    o_sq, o_sk, o_sv, o_rq, o_rk, o_rv, o_gf, o_gb, o_mq, o_ckv, o_kr = (
        0, 768, 1024, 1280, 1664, 2048, 2816, 3584, 4352, 5120, 5376)
    mq = w[:, o_mq:o_mq + 768].reshape(d, MLA_HEADS, MLA_NOPE_DIM + MLA_ROPE_DIM)
    parts = [w[:, o_sq:o_sq + 768], w[:, o_rv:o_rv + 768], w[:, o_gf:o_gf + 768], w[:, o_gb:o_gb + 768],
             mq[:, :, :MLA_NOPE_DIM].reshape(d, 512), w[:, o_sk:o_sk + 256], w[:, o_sv:o_sv + 256],
             mq[:, :, MLA_NOPE_DIM:].reshape(d, 256), w[:, o_ckv:o_ckv + 256], w[:, o_rq:o_rq + 384],
             w[:, o_rk:o_rk + 384], w[:, o_kr:o_kr + 64], jnp.zeros((d, _NP - _KR - 64), w.dtype)]
    return jnp.concatenate(parts, axis=1).astype(bf16)


def kernel(x, c, ctx, c_ctx, w_ada, b_ada, w_in, swa_sink, ret_decay, mla_kv_norm, mla_w_uk, mla_w_uv, w_out, ln1_g, ln1_b, ln2_g, ln2_b, moe_w_group, moe_b_group, moe_w_expert, moe_b_expert, moe_w_gate, moe_w_up, moe_w_down):
    n_batch, lat_len, d = x.shape
    ctx_len = ctx.shape[1]
    depth = w_ada.shape[0]
    n_lat, n_ctx = n_batch * lat_len, n_batch * ctx_len
    alpha = (2 * depth) ** 0.25

    cc = jnp.zeros((8, d), f32).at[:n_batch].set(c).at[n_batch].set(c_ctx)
    mod_all = _ada(cc, w_ada, b_ada).reshape(depth, 8, 6, d)
    tab = _rope_tables(lat_len, ctx_len, _inproj_rows(lat_len, n_ctx))
    x_lat, x_ctx = x.reshape(n_lat, d), ctx.reshape(n_ctx, d)

    for l in range(depth):
        ctx_out = l < depth - 1
        mod = mod_all[l]
        p = _inproj(x_lat, x_ctx, mod, tab, _permute_w_in(w_in[l]), n_lat, lat_len, ctx_len, n_batch)
        kx, v = _mla_expand(p, mla_kv_norm[l][None, :], mla_w_uk[l].astype(bf16), mla_w_uv[l].astype(bf16))
        m = _mla_attention(p, kx, v, n_batch, lat_len, ctx_len, ctx_out)
        a = _swa_attention(p, swa_sink[l], n_batch, lat_len, ctx_len, ctx_out)
        lg = jnp.log1p(-jnp.exp2(-ret_decay[l].astype(f32)))
        yf, yb = _retention(p, lg, n_batch, lat_len, ctx_len)
        n_rows = n_lat + n_ctx if ctx_out else n_lat
        rout_w = jnp.zeros((d, LANES), f32).at[:, :N_GROUPS].set(moe_w_group[l]).at[
            :, _ROUTE_LANE0:_ROUTE_LANE0 + N_EXPERTS].set(moe_w_expert[l]).astype(bf16)
        rout_b = jnp.zeros((1, LANES), f32).at[0, :N_GROUPS].set(moe_b_group[l]).at[
            0, _ROUTE_LANE0:_ROUTE_LANE0 + N_EXPERTS].set(moe_b_expert[l])
        xs, h, rout, routt, cnt = _outproj(a, yf, yb, p, m, x_lat, x_ctx, mod, w_out[l].astype(bf16), ln1_g[l][None, :],
                                           ln1_b[l][None, :], rout_w, rout_b, n_rows, n_lat, lat_len, n_batch, alpha)
        xs = _moe(h, rout, routt, cnt, xs, moe_w_gate, moe_w_up, moe_w_down, l, mod, ln2_g[l][None, :],
                  ln2_b[l][None, :], n_lat, lat_len, n_batch, alpha)
        x_lat = x_ctx = xs
    return xs[:n_lat].reshape(n_batch, lat_len, d)
```

```python
import functools

import jax
import jax.numpy as jnp
from jax import lax
from jax.experimental import pallas as pl
from jax.experimental.pallas import tpu as pltpu

f32 = jnp.float32
bf16 = jnp.bfloat16
i32 = jnp.int32

GRID_W = 64
SWA_HEADS, SWA_KV_HEADS, SWA_HEAD_DIM, SWA_WINDOW = 6, 2, 128, 128
RET_HEADS, RET_QK_DIM, RET_V_DIM, RET_CHUNK = 6, 64, 128, 128
MLA_HEADS, MLA_NOPE_DIM, MLA_ROPE_DIM, MLA_V_DIM, MLA_KV_RANK = 4, 128, 64, 128, 256
N_GROUPS, EXPERTS_PER_GROUP, TOP_K = 4, 8, 2
N_EXPERTS = N_GROUPS * EXPERTS_PER_GROUP
ROPE_BASE = 10000.0
NORM_EPS = 1e-6
NEG_INF = -1e30
LANES = 128

_SQ, _RV, _GF, _GB, _MQN, _SK, _SV, _MQR, _CKV, _RQ, _RK, _KR = (
    0, 768, 1536, 2304, 3072, 3584, 3840, 4096, 4352, 4608, 4992, 5376)
_NP = 5504
_LOG2E = 1.4426950408889634
_SWA_SCALE = SWA_HEAD_DIM ** -0.5
_RET_SCALE = RET_QK_DIM ** -0.5
_MLA_SCALE = (MLA_NOPE_DIM + MLA_ROPE_DIM) ** -0.5 * _LOG2E
_SEGMENTS = ((_SQ, 768, 0, _SWA_SCALE), (_RV, 768, None, 1.0), (_GF, 768, None, 1.0), (_GB, 768, None, 1.0),
             (_MQN, 512, None, _MLA_SCALE), (_SK, 256, 0, 1.0), (_SV, 256, None, 1.0), (_MQR, 256, 2, _MLA_SCALE),
             (_CKV, 256, None, 1.0), (_RQ, 384, 1, _RET_SCALE), (_RK, 384, 1, 1.0), (_KR, 128, 2, 1.0))

_VMEM_LIMIT = 48 * 1024 * 1024


def _cparams(*sem):
    return pltpu.CompilerParams(dimension_semantics=sem, vmem_limit_bytes=_VMEM_LIMIT)


def _tile(n, pref):
    t = min(n, pref)
    while n % t:
        t //= 2
    return t


def _ln(x):
    mu = jnp.mean(x, axis=-1, keepdims=True)
    xc = x - mu
    var = jnp.mean(xc * xc, axis=-1, keepdims=True)
    return xc * lax.rsqrt(var + NORM_EPS)


def _silu(x):
    return x / (1.0 + jnp.exp(-x))


def _pack_rows(x):
    half = x.shape[1] // 2
    lo = lax.bitcast_convert_type(x[:, :half].astype(bf16).astype(f32), jnp.uint32)
    hi = lax.bitcast_convert_type(x[:, half:].astype(bf16).astype(f32), jnp.uint32)
    return hi | (lo >> 16)


def _unpack_rows(u):
    lo = lax.bitcast_convert_type(u << 16, f32)
    hi = lax.bitcast_convert_type(u & jnp.uint32(0xFFFF0000), f32)
    return jnp.concatenate([lo, hi], axis=1)


def _dot(a, b):
    return jnp.dot(a, b, preferred_element_type=f32)


def _dot_nt(a, b):
    return lax.dot_general(a, b, (((1,), (1,)), ((), ())), preferred_element_type=f32)


def _dot_tn(a, b):
    return lax.dot_general(a, b, (((0,), (0,)), ((), ())), preferred_element_type=f32)


def _ada_kernel(c_ref, w_ref, b_ref, o_ref):
    s = _silu(c_ref[...]).astype(bf16)
    o_ref[0] = _dot(s, w_ref[0].astype(bf16)) + b_ref[0]


def _ada(cc, w_ada, b_ada):
    depth, d, n = w_ada.shape
    tn = _tile(n, 1024)
    return pl.pallas_call(
        _ada_kernel,
        grid=(depth, n // tn),
        in_specs=[pl.BlockSpec((8, d), lambda l, j: (0, 0)),
                  pl.BlockSpec((1, d, tn), lambda l, j: (l, 0, j)),
                  pl.BlockSpec((1, 1, tn), lambda l, j: (l, 0, j))],
        out_specs=pl.BlockSpec((1, 8, tn), lambda l, j: (l, 0, j)),
        out_shape=jax.ShapeDtypeStruct((depth, 8, n), f32),
        compiler_params=_cparams("parallel", "parallel"),
        name="ada",
    )(cc, w_ada, b_ada.reshape(depth, 1, n))


_INPROJ_CHUNK = 512


def _inproj_kernel(xl_ref, xc_ref, mod_ref, tab_ref, w_ref, o_ref, *, nlb):
    tm = xl_ref.shape[0]
    h = (_ln(_stream_block(xl_ref, xc_ref, nlb)) * (1.0 + mod_ref[0, 1:2, :]) + mod_ref[0, 0:1, :]).astype(bf16)
    lane = lax.broadcasted_iota(i32, (tm, LANES), 1)
    first = {32: (lane & 63) < 32, 16: (lane & 31) < 16}
    slab_kind = {}
    for off, width, typ, scale in _SEGMENTS:
        for k in range(off // LANES, (off + width) // LANES):
            slab_kind[k] = (typ, scale)
    for c0 in range(0, _NP, _INPROJ_CHUNK):
        c1 = min(c0 + _INPROJ_CHUNK, _NP)
        acc = _dot(h, w_ref[:, c0:c1])
        for k in range(c0 // LANES, c1 // LANES):
            typ, scale = slab_kind[k]
            xk = acc[:, k * LANES - c0:(k + 1) * LANES - c0]
            if scale != 1.0:
                xk = xk * scale
            if typ is not None:
                half = 16 if typ == 2 else 32
                partner = jnp.where(first[half], pltpu.roll(xk, LANES - half, 1), pltpu.roll(xk, half, 1))
                xk = xk * tab_ref[typ, 0] + partner * tab_ref[typ, 1]
            o_ref[:, k * LANES:(k + 1) * LANES] = xk.astype(bf16)


def _inproj_rows(lat_len, n_ctx):
    tm = _tile(lat_len, 256)
    while n_ctx % tm:
        tm //= 2
    return tm


def _inproj(x_lat, x_ctx, mod, tab, w, n_lat, lat_len, ctx_len, n_batch):
    d = x_lat.shape[1]
    t = n_lat + n_batch * ctx_len
    tm = _inproj_rows(lat_len, t - n_lat)
    nlb, bpb = n_lat // tm, lat_len // tm
    ctx_blocks = max(1, ctx_len // tm)

    def grp(i):
        return jnp.where(i < nlb, i // bpb, n_batch)

    def posblk(i):
        return jnp.where(i < nlb, i % bpb, bpb + (i - nlb) % ctx_blocks)

    return pl.pallas_call(
        functools.partial(_inproj_kernel, nlb=nlb),
        grid=(t // tm,),
        in_specs=_stream_specs(x_lat, x_ctx, tm, n_lat) + [
                  pl.BlockSpec((1, 6, d), lambda i: (grp(i), 0, 0)),
                  pl.BlockSpec((3, 2, tm, LANES), lambda i: (0, 0, posblk(i), 0)),
                  pl.BlockSpec((d, _NP), lambda i: (0, 0), pipeline_mode=pl.Buffered(1))],
        out_specs=pl.BlockSpec((tm, _NP), lambda i: (i, 0)),
        out_shape=jax.ShapeDtypeStruct((t, _NP), bf16),
        compiler_params=_cparams("parallel"),
        name="inproj",
    )(x_lat, x_ctx, mod, tab, w)


def _rope_tables(lat_len, ctx_len, tm):
    lane = jnp.arange(LANES)
    t = jnp.arange(lat_len)
    rows = (t // GRID_W).astype(f32)
    cols = (t % GRID_W).astype(f32)

    def freq(half):
        return ROPE_BASE ** (-jnp.arange(half, dtype=f32) / half)

    def table(pos, inv, sign):
        ang = pos * inv[None, :]
        return jnp.stack([jnp.cos(ang), jnp.sin(ang) * sign[None, :]])

    def ident(n):
        return jnp.stack([jnp.ones((n, LANES), f32), jnp.zeros((n, LANES), f32)])

    sign32 = jnp.where((lane % 64) < 32, -1.0, 1.0).astype(f32)
    sign16 = jnp.where((lane % 32) < 16, -1.0, 1.0).astype(f32)
    inv32 = freq(32)[lane % 32]
    inv16 = freq(16)[lane % 16]
    pos_swa = jnp.where((lane // 64)[None, :] == 0, rows[:, None], cols[:, None])
    pos_mla = jnp.where(((lane % 64) // 32)[None, :] == 0, rows[:, None], cols[:, None])
    rep = max(1, tm // ctx_len)
    nc = rep * ctx_len
    pos_ret_l = jnp.broadcast_to((ctx_len + t).astype(f32)[:, None], (lat_len, LANES))
    pos_ret_c = jnp.broadcast_to(jnp.tile(jnp.arange(ctx_len), rep).astype(f32)[:, None], (nc, LANES))
    t0 = jnp.concatenate([table(pos_swa, inv32, sign32), ident(nc)], axis=1)
    t1 = jnp.concatenate([table(pos_ret_l, inv32, sign32), table(pos_ret_c, inv32, sign32)], axis=1)
    t2 = jnp.concatenate([table(pos_mla, inv16, sign16), ident(nc)], axis=1)
    return jnp.stack([t0, t1, t2])


def _mla_expand_kernel(ckv_ref, kr_ref, g_ref, wuk_ref, wuv_ref, k_ref, v_ref):
    c = ckv_ref[...].astype(f32)
    cn = (c * lax.rsqrt(jnp.mean(c * c, axis=-1, keepdims=True) + NORM_EPS) * g_ref[...]).astype(bf16)
    kn = _dot(cn, wuk_ref[...]).astype(bf16)
    vv = _dot(cn, wuv_ref[...]).astype(bf16)
    kr = kr_ref[...]
    for h in range(MLA_HEADS):
        k_ref[h, :, 0:128] = kn[:, h * 128:(h + 1) * 128]
        k_ref[h, :, 128:256] = kr
        v_ref[h] = vv[:, h * 128:(h + 1) * 128]


def _mla_expand(p, g, wuk, wuv):
    t = p.shape[0]
    tm = _tile(t, 512)
    return pl.pallas_call(
        _mla_expand_kernel,
        grid=(t // tm,),
        in_specs=[pl.BlockSpec((tm, 256), lambda i: (i, _CKV // 256)),
                  pl.BlockSpec((tm, 128), lambda i: (i, _KR // 128)),
                  pl.BlockSpec((1, 256), lambda i: (0, 0)),
                  pl.BlockSpec((256, 512), lambda i: (0, 0)),
                  pl.BlockSpec((256, 512), lambda i: (0, 0))],
        out_specs=[pl.BlockSpec((MLA_HEADS, tm, 256), lambda i: (0, i, 0)),
                   pl.BlockSpec((MLA_HEADS, tm, 128), lambda i: (0, i, 0))],
        out_shape=[jax.ShapeDtypeStruct((MLA_HEADS, t, 256), bf16), jax.ShapeDtypeStruct((MLA_HEADS, t, 128), bf16)],
        compiler_params=_cparams("parallel"),
        name="mla_expand",
    )(p, p, g, wuk, wuv)


def _mla_kernel(*refs, with_lat):
    if with_lat:
        qn_ref, qr_ref, kc_ref, vc_ref, kl_ref, vl_ref, o_ref, qx_scr = refs
    else:
        qn_ref, qr_ref, kc_ref, vc_ref, _, o_ref, qx_scr = refs
    lane = lax.broadcasted_iota(i32, (qn_ref.shape[0], LANES), 1)
    for h in range(MLA_HEADS):
        slab = qr_ref[:, (h // 2) * 128:(h // 2 + 1) * 128].astype(f32)
        if h % 2:
            slab = pltpu.roll(slab, 64, 1)
        qx_scr[h, :, 0:128] = qn_ref[:, h * 128:(h + 1) * 128]
        qx_scr[h, :, 128:256] = jnp.where(lane < 64, slab, 0.0).astype(bf16)
    for h in range(MLA_HEADS):
        q = qx_scr[h]
        parts = [(_dot_nt(q, kc_ref[h]), vc_ref[h])]
        if with_lat:
            parts.append((_dot_nt(q, kl_ref[h]), vl_ref[h]))
        m = None
        for s, _ in parts:
            mx = jnp.max(s, axis=1, keepdims=True)
            m = mx if m is None else jnp.maximum(m, mx)
        den, acc = None, None
        for s, vv in parts:
            pr = jnp.exp2(s - m)
            sm = jnp.sum(pr, axis=1, keepdims=True)
            pv = _dot(pr.astype(bf16), vv)
            den = sm if den is None else den + sm
            acc = pv if acc is None else acc + pv
        o_ref[:, h * 128:(h + 1) * 128] = (acc / den).astype(bf16)


def _mla_attention(p, kx, v, n_batch, lat_len, ctx_len, compute_ctx):
    t = p.shape[0]
    tq = _tile(lat_len, 256)
    nlq = lat_len // tq
    cbase = n_batch * lat_len // ctx_len
    once = pl.Buffered(1)
    out = pl.pallas_call(
        functools.partial(_mla_kernel, with_lat=True),
        grid=(n_batch, nlq),
        in_specs=[pl.BlockSpec((tq, 512), lambda b, qi: (b * nlq + qi, _MQN // 512)),
                  pl.BlockSpec((tq, 256), lambda b, qi: (b * nlq + qi, _MQR // 256)),
                  pl.BlockSpec((MLA_HEADS, ctx_len, 256), lambda b, qi: (0, cbase + b, 0)),
                  pl.BlockSpec((MLA_HEADS, ctx_len, 128), lambda b, qi: (0, cbase + b, 0)),
                  pl.BlockSpec((MLA_HEADS, lat_len, 256), lambda b, qi: (0, b, 0), pipeline_mode=once),
                  pl.BlockSpec((MLA_HEADS, lat_len, 128), lambda b, qi: (0, b, 0), pipeline_mode=once)],
        out_specs=pl.BlockSpec((tq, 512), lambda b, qi: (b * nlq + qi, 0)),
        out_shape=jax.ShapeDtypeStruct((t, 512), bf16),
        scratch_shapes=[pltpu.VMEM((MLA_HEADS, tq, 256), bf16)],
        compiler_params=_cparams("parallel", "arbitrary"),
        name="mla_attn",
    )(p, p, kx, v, kx, v)
    if compute_ctx:
        out = pl.pallas_call(
            functools.partial(_mla_kernel, with_lat=False),
            grid=(n_batch,),
            in_specs=[pl.BlockSpec((ctx_len, 512), lambda b: (cbase + b, _MQN // 512)),
                      pl.BlockSpec((ctx_len, 256), lambda b: (cbase + b, _MQR // 256)),
                      pl.BlockSpec((MLA_HEADS, ctx_len, 256), lambda b: (0, cbase + b, 0)),
                      pl.BlockSpec((MLA_HEADS, ctx_len, 128), lambda b: (0, cbase + b, 0)),
                      pl.BlockSpec(memory_space=pl.ANY)],
            out_specs=pl.BlockSpec((ctx_len, 512), lambda b: (cbase + b, 0)),
            out_shape=jax.ShapeDtypeStruct((t, 512), bf16),
            scratch_shapes=[pltpu.VMEM((MLA_HEADS, ctx_len, 256), bf16)],
            input_output_aliases={4: 0},
            compiler_params=_cparams("parallel"),
            name="mla_ctx",
        )(p, p, kx, v, out)
    return out


def _swa_softmax_out(parts, sink_col, o_ref, g):
    m = sink_col
    for s, _ in parts:
        m = jnp.maximum(m, jnp.max(s, axis=1, keepdims=True))
    den = jnp.exp(sink_col - m)
    acc = None
    for s, vv in parts:
        pr = jnp.exp(s - m)
        den = den + jnp.sum(pr, axis=1, keepdims=True)
        pv = _dot(pr.astype(bf16), vv)
        acc = pv if acc is None else acc + pv
    o = acc / den
    nq = o.shape[0] // 3
    for j in range(3):
        o_ref[:, (3 * g + j) * 128:(3 * g + j + 1) * 128] = o[j * nq:(j + 1) * nq].astype(bf16)


def _sink_col(sink_ref, g, nq):
    row = lax.broadcasted_iota(i32, (3 * nq, 1), 0)
    return jnp.where(row < nq, sink_ref[3 * g], jnp.where(row < 2 * nq, sink_ref[3 * g + 1], sink_ref[3 * g + 2]))


def _swa_kernel(sink_ref, q_ref, kp_ref, ko_ref, kn_ref, vp_ref, vo_ref, vn_ref, kc_ref, vc_ref, o_ref):
    i = pl.program_id(1)
    nb = pl.num_programs(1)
    blk = q_ref.shape[0]
    q = q_ref[...]
    r = lax.broadcasted_iota(i32, (3 * blk, 3 * blk), 0) & (blk - 1)
    c = lax.broadcasted_iota(i32, (3 * blk, 3 * blk), 1)
    lo = jnp.where(i == 0, blk, 0)
    hi = jnp.where(i == nb - 1, 2 * blk, 3 * blk)
    d = c - r
    valid = (d >= 0) & (d <= 2 * SWA_WINDOW) & (c >= lo) & (c < hi)
    for g in range(SWA_KV_HEADS):
        hs = slice(g * 128, (g + 1) * 128)
        qg = jnp.concatenate([q[:, (3 * g + j) * 128:(3 * g + j + 1) * 128] for j in range(3)], axis=0)
        kloc = jnp.concatenate([kp_ref[:, hs], ko_ref[:, hs], kn_ref[:, hs]], axis=0)
        vloc = jnp.concatenate([vp_ref[:, hs], vo_ref[:, hs], vn_ref[:, hs]], axis=0)
        s_loc = jnp.where(valid, _dot_nt(qg, kloc), NEG_INF)
        s_ctx = _dot_nt(qg, kc_ref[:, hs])
        _swa_softmax_out([(s_ctx, vc_ref[:, hs]), (s_loc, vloc)], _sink_col(sink_ref, g, blk), o_ref, g)


def _swa_ctx_kernel(sink_ref, q_ref, kc_ref, vc_ref, prev_ref, o_ref):
    del prev_ref
    q = q_ref[...]
    nq = q.shape[0]
    for g in range(SWA_KV_HEADS):
        hs = slice(g * 128, (g + 1) * 128)
        qg = jnp.concatenate([q[:, (3 * g + j) * 128:(3 * g + j + 1) * 128] for j in range(3)], axis=0)
        s_ctx = _dot_nt(qg, kc_ref[:, hs])
        _swa_softmax_out([(s_ctx, vc_ref[:, hs])], _sink_col(sink_ref, g, nq), o_ref, g)


def _swa_attention(p, sink, n_batch, lat_len, ctx_len, compute_ctx):
    t = p.shape[0]
    blk = SWA_WINDOW
    nb = lat_len // blk
    cbase = n_batch * lat_len // ctx_len
    smem = pl.BlockSpec(memory_space=pltpu.SMEM)

    def kspec(col, off):
        return pl.BlockSpec((blk, 256), lambda b, i: (b * nb + jnp.clip(i + off, 0, nb - 1), col // 256))

    out = pl.pallas_call(
        _swa_kernel,
        grid=(n_batch, nb),
        in_specs=[smem,
                  pl.BlockSpec((blk, 768), lambda b, i: (b * nb + i, _SQ // 768)),
                  kspec(_SK, -1), kspec(_SK, 0), kspec(_SK, 1),
                  kspec(_SV, -1), kspec(_SV, 0), kspec(_SV, 1),
                  pl.BlockSpec((ctx_len, 256), lambda b, i: (cbase + b, _SK // 256)),
                  pl.BlockSpec((ctx_len, 256), lambda b, i: (cbase + b, _SV // 256))],
        out_specs=pl.BlockSpec((blk, 768), lambda b, i: (b * nb + i, 0)),
        out_shape=jax.ShapeDtypeStruct((t, 768), bf16),
        compiler_params=_cparams("parallel", "parallel"),
        name="swa_attn",
    )(sink, p, p, p, p, p, p, p, p, p)
    if compute_ctx:
        out = pl.pallas_call(
            _swa_ctx_kernel,
            grid=(n_batch,),
            in_specs=[smem,
                      pl.BlockSpec((ctx_len, 768), lambda b: (cbase + b, _SQ // 768)),
                      pl.BlockSpec((ctx_len, 256), lambda b: (cbase + b, _SK // 256)),
                      pl.BlockSpec((ctx_len, 256), lambda b: (cbase + b, _SV // 256)),
                      pl.BlockSpec(memory_space=pl.ANY)],
            out_specs=pl.BlockSpec((ctx_len, 768), lambda b: (cbase + b, 0)),
            out_shape=jax.ShapeDtypeStruct((t, 768), bf16),
            input_output_aliases={4: 0},
            compiler_params=_cparams("parallel"),
            name="swa_ctx",
        )(sink, p, p, p, out)
    return out


def _ret_kernel(lg_ref, qf_ref, kf_ref, vf_ref, qb_ref, kb_ref, vb_ref, of_ref, ob_ref,
                s_scr, d_scr, qd_scr, kd_scr, gc_scr):
    i = pl.program_id(1)
    cc = RET_CHUNK

    @pl.when(i == 0)
    def _():
        s_scr[...] = jnp.zeros(s_scr.shape, f32)
        r = lax.broadcasted_iota(i32, (cc, cc), 0).astype(f32)
        c = lax.broadcasted_iota(i32, (cc, cc), 1).astype(f32)
        for direction in range(2):
            backward = direction == 1
            diff = (c - r) if backward else (r - c)
            for h in range(RET_HEADS):
                lgh = lg_ref[direction, h]
                j = direction * RET_HEADS + h
                d_scr[j] = jnp.where(diff >= 0, jnp.exp(lgh * jnp.maximum(diff, 0.0)), 0.0)
                qd_scr[j] = jnp.exp(lgh * ((cc - r) if backward else (r + 1.0)))
                kd_scr[j] = jnp.exp(lgh * (r if backward else (cc - 1.0 - r)))
                gc_scr[j] = jnp.exp(lgh * cc + jnp.zeros((cc, cc), f32))

    lane = lax.broadcasted_iota(i32, (cc, LANES), 1)
    for pair in range(RET_HEADS // 2):
        for direction, (q_ref, k_ref, v_ref, o_ref) in enumerate(
                ((qf_ref, kf_ref, vf_ref, of_ref), (qb_ref, kb_ref, vb_ref, ob_ref))):
            q2 = q_ref[:, pair * 128:(pair + 1) * 128].astype(f32)
            k2b = k_ref[:, pair * 128:(pair + 1) * 128]
            k2 = k2b.astype(f32)
            for sub in range(2):
                h = 2 * pair + sub
                j = direction * RET_HEADS + h
                hs = slice(h * 128, (h + 1) * 128)
                qa = jnp.where((lane < 64) if sub == 0 else (lane >= 64), q2, 0.0)
                s = _dot_nt(qa.astype(bf16), k2b)
                vh = v_ref[:, hs]
                y = (_dot((s * d_scr[j]).astype(bf16), vh)
                     + _dot((qa * qd_scr[j]).astype(bf16), s_scr[j].astype(bf16)))
                s_scr[j] = gc_scr[j] * s_scr[j] + _dot_tn((k2 * kd_scr[j]).astype(bf16), vh)
                o_ref[:, hs] = _ln(y).astype(bf16)


def _retention(p, lg, n_batch, lat_len, ctx_len):
    t = p.shape[0]
    cc = RET_CHUNK
    ncc, nlc = ctx_len // cc, lat_len // cc
    cbase = n_batch * nlc
    smem = pl.BlockSpec(memory_space=pltpu.SMEM)
    scratch = [pltpu.VMEM((2 * RET_HEADS, cc, cc), f32) for _ in range(5)]

    def fwd_row(b, i):
        return jnp.where(i < ncc, cbase + b * ncc + i, b * nlc + (i - ncc))

    def bwd_row(b, i):
        return jnp.where(i < ncc, cbase + b * ncc + (ncc - 1 - i), b * nlc + (nlc - 1 - (i - ncc)))

    def specs(row):
        return [pl.BlockSpec((cc, 384), lambda b, i: (row(b, i), _RQ // 384)),
                pl.BlockSpec((cc, 384), lambda b, i: (row(b, i), _RK // 384)),
                pl.BlockSpec((cc, 768), lambda b, i: (row(b, i), _RV // 768))]

    return pl.pallas_call(
        _ret_kernel,
        grid=(n_batch, ncc + nlc),
        in_specs=[smem] + specs(fwd_row) + specs(bwd_row),
        out_specs=[pl.BlockSpec((cc, 768), lambda b, i: (fwd_row(b, i), 0)),
                   pl.BlockSpec((cc, 768), lambda b, i: (bwd_row(b, i), 0))],
        out_shape=[jax.ShapeDtypeStruct((t, 768), bf16), jax.ShapeDtypeStruct((t, 768), bf16)],
        scratch_shapes=scratch,
        compiler_params=_cparams("parallel", "arbitrary"),
        name="retention",
    )(lg, p, p, p, p, p, p)


def _stream_specs(x_lat, x_ctx, tm, n_lat):
    d = x_lat.shape[1]
    nlb = n_lat // tm
    ctx_first = nlb if x_ctx is x_lat else 0
    return [pl.BlockSpec((tm, d), lambda i: (jnp.minimum(i, nlb - 1), 0)),
            pl.BlockSpec((tm, d), lambda i: (jnp.maximum(i - nlb, 0) + ctx_first, 0))]


def _stream_block(xl_ref, xc_ref, nlb):
    return jnp.where(pl.program_id(0) < nlb, xl_ref[...], xc_ref[...])


def _outproj_kernel(a_ref, yf_ref, yb_ref, gf_ref, gb_ref, m_ref, xl_ref, xc_ref, mod_ref, w_ref, g_ref, b_ref,
                    rw_ref, rb_ref, xo_ref, h_ref, rout_ref, routt_ref, cnt_ref, carry_scr, *, alpha, nlb):
    @pl.when(pl.program_id(0) == 0)
    def _():
        carry_scr[...] = jnp.zeros(carry_scr.shape, f32)

    ret = (_silu(gf_ref[...].astype(f32)) * yf_ref[...].astype(f32)
           + _silu(gb_ref[...].astype(f32)) * yb_ref[...].astype(f32)).astype(bf16)
    y = _dot(jnp.concatenate([a_ref[...], ret, m_ref[...]], axis=1), w_ref[...])
    xn = _ln(alpha * _stream_block(xl_ref, xc_ref, nlb) + mod_ref[0, 2:3, :] * y) * g_ref[...] + b_ref[...]
    xo_ref[...] = xn
    h = _ln(xn) * (1.0 + mod_ref[0, 4:5, :]) + mod_ref[0, 3:4, :]
    h_ref[...] = _pack_rows(h)
    table = _route(_dot(h.astype(bf16), rw_ref[...]) + rb_ref[...], carry_scr)
    rout_ref[...] = table
    routt_ref[...] = jnp.transpose(table)[0:8, :]
    cnt_ref[...] = carry_scr[...]


def _outproj(a, yf, yb, p, m, x_lat, x_ctx, mod, w, g, b, rw, rb, n_rows, n_lat, lat_len, n_batch, alpha):
    d = x_lat.shape[1]
    tm = _tile(lat_len, 256)
    while (n_rows - n_lat) % tm:
        tm //= 2
    nlb, bpb = n_lat // tm, lat_len // tm

    def grp(i):
        return jnp.where(i < nlb, i // bpb, n_batch)

    row = lambda i: (i, 0)
    const = lambda i: (0, 0)
    return pl.pallas_call(
        functools.partial(_outproj_kernel, alpha=alpha, nlb=nlb),
        grid=(n_rows // tm,),
        in_specs=[pl.BlockSpec((tm, 768), row), pl.BlockSpec((tm, 768), row), pl.BlockSpec((tm, 768), row),
                  pl.BlockSpec((tm, 768), lambda i: (i, _GF // 768)), pl.BlockSpec((tm, 768), lambda i: (i, _GB // 768)),
                  pl.BlockSpec((tm, 512), row)] + _stream_specs(x_lat, x_ctx, tm, n_lat) + [
                  pl.BlockSpec((1, 6, d), lambda i: (grp(i), 0, 0)),
                  pl.BlockSpec(w.shape, const),
                  pl.BlockSpec((1, d), const), pl.BlockSpec((1, d), const),
                  pl.BlockSpec((d, LANES), const), pl.BlockSpec((1, LANES), const)],
        out_specs=[pl.BlockSpec((tm, d), row), pl.BlockSpec((tm, d // 2), row),
                   pl.BlockSpec((tm, LANES), row), pl.BlockSpec((8, tm), row), pl.BlockSpec((8, LANES), const)],
        out_shape=[jax.ShapeDtypeStruct((n_rows, d), f32), jax.ShapeDtypeStruct((n_rows, d // 2), jnp.uint32),
                   jax.ShapeDtypeStruct((n_rows, LANES), f32), jax.ShapeDtypeStruct((n_rows // tm * 8, tm), f32),
                   jax.ShapeDtypeStruct((8, LANES), f32)],
        scratch_shapes=[pltpu.VMEM((8, LANES), f32)],
        compiler_params=_cparams("arbitrary"),
        name="outproj",
    )(a, yf, yb, p, p, m, x_lat, x_ctx, mod, w, g, b, rw, rb)


_ROUTE_LANE0 = N_GROUPS


def _route(logits, carry_scr):
    tm = logits.shape[0]
    lane = lax.broadcasted_iota(i32, (tm, LANES), 1)
    lane_f = lane.astype(f32)
    big = float(2 * LANES)
    gl = jnp.where(lane < N_GROUPS, logits, -jnp.inf)
    gmax = jnp.max(gl, axis=1, keepdims=True)
    gidx = jnp.min(jnp.where(gl == gmax, lane_f, big), axis=1, keepdims=True)
    p_group = 1.0 / jnp.sum(jnp.exp(gl - gmax), axis=1, keepdims=True)
    egroup = ((lane - _ROUTE_LANE0) >> 3).astype(f32)
    in_group = (lane >= _ROUTE_LANE0) & (lane < _ROUTE_LANE0 + N_EXPERTS) & (egroup == gidx)
    ev = jnp.where(in_group, logits, -jnp.inf)
    e1 = jnp.max(ev, axis=1, keepdims=True)
    i1 = jnp.min(jnp.where(ev == e1, lane_f, big), axis=1, keepdims=True)
    ev2 = jnp.where(lane_f == i1, -jnp.inf, ev)
    e2 = jnp.max(ev2, axis=1, keepdims=True)
    i2 = jnp.min(jnp.where(ev2 == e2, lane_f, big), axis=1, keepdims=True)
    tt = jnp.exp(e2 - e1)
    w1 = p_group / (1.0 + tt)
    w2 = p_group * tt / (1.0 + tt)

    hit1 = lane_f == i1
    hit2 = lane_f == i2
    onehot = jnp.where(hit1, 1.0, jnp.where(hit2, 1.0, 0.0))
    rr = lax.broadcasted_iota(i32, (tm, tm), 0)
    cc = lax.broadcasted_iota(i32, (tm, tm), 1)
    lower = jnp.where(cc < rr, 1.0, 0.0).astype(bf16)
    before = _dot(lower, onehot.astype(bf16)) + carry_scr[0:1, :]
    rank1 = jnp.sum(jnp.where(hit1, before, 0.0), axis=1, keepdims=True)
    rank2 = jnp.sum(jnp.where(hit2, before, 0.0), axis=1, keepdims=True)
    carry_scr[0:1, :] = carry_scr[0:1, :] + jnp.sum(onehot, axis=0, keepdims=True)

    return jnp.where(lane == 0, i1 - _ROUTE_LANE0, jnp.where(lane == 1, i2 - _ROUTE_LANE0, jnp.where(
        lane == 2, rank1, jnp.where(lane == 3, rank2, jnp.where(lane == 4, w1, jnp.where(lane == 5, w2, 0.0))))))


def _dispatch_kernel(cnt_ref, pstart_ref, padded_ref, dest_ref, h_ref, o_hbm, sem):
    i = pl.program_id(0)
    nb = pl.num_programs(0) - 1
    tm = dest_ref.shape[2] // TOP_K

    def row_copy(src_row, dst_row):
        return pltpu.make_async_copy(h_ref.at[pl.ds(src_row, 1), :], o_hbm.at[dst_row], sem)

    @pl.when(i < nb)
    def _():
        def issue(r8, carry):
            base = pl.multiple_of(r8 * 8, 8)
            for j in range(8):
                for k in range(TOP_K):
                    row_copy(base + j, dest_ref[0, 0, k * tm + base + j]).start()
            return carry

        lax.fori_loop(0, tm // 8, issue, 0)
        for _ in range(TOP_K):
            pltpu.make_async_copy(h_ref, o_hbm.at[pl.ds(0, tm), 0], sem).wait()

    @pl.when(i == nb)
    def _():
        def per_expert(e, total):
            lo, hi = cnt_ref[e], padded_ref[e]

            def fill(r, carry):
                row_copy(0, pstart_ref[e] + r).start()
                return carry

            lax.fori_loop(lo, hi, fill, 0)
            return total + (hi - lo)

        total = lax.fori_loop(0, N_EXPERTS, per_expert, 0)

        def drain(r, carry):
            row_copy(0, 0).wait()
            return carry

        lax.fori_loop(0, total, drain, 0)


def _dispatch(h, dest_rows, counts, pstart, padded, n_rows, tm):
    n, d = h.shape
    nb = n // tm
    grid_spec = pltpu.PrefetchScalarGridSpec(
        num_scalar_prefetch=3,
        grid=(nb + 1,),
        in_specs=[pl.BlockSpec((1, 1, TOP_K * tm), lambda i, *_: (jnp.minimum(i, nb - 1), 0, 0),
                               memory_space=pltpu.SMEM),
                  pl.BlockSpec((tm, d), lambda i, *_: (jnp.minimum(i, nb - 1), 0))],
        out_specs=pl.BlockSpec(memory_space=pl.ANY),
        scratch_shapes=[pltpu.SemaphoreType.DMA(())],
    )
    return pl.pallas_call(
        _dispatch_kernel,
        grid_spec=grid_spec,
        out_shape=jax.ShapeDtypeStruct((n_rows, 1, d), h.dtype),
        compiler_params=_cparams("arbitrary"),
        name="moe_dispatch",
    )(counts, pstart, padded, dest_rows.reshape(nb, 1, TOP_K * tm), h)


_CAST_ROWS = 256


def _expert_kernel(nused_ref, be_ref, first_ref, nxt_ref, half_ref, x_hbm, wg_hbm, wu_hbm, wd_hbm, y_hbm,
                   wg_f, wu_f, wd_f, wg_b, wu_b, wd_b, xbuf, ybuf, sems, xsems, ysems, *, layer):
    i = pl.program_id(0)
    n_used = nused_ref[0]
    used = i < n_used
    rb = xbuf.shape[1]
    slot = i % 2
    staged = ((wg_hbm, wg_f, wg_b), (wu_hbm, wu_f, wu_b), (wd_hbm, wd_f, wd_b))

    def x_fetch(blk, s):
        return pltpu.make_async_copy(x_hbm.at[pl.ds(pl.multiple_of(blk * rb, rb), rb), 0], xbuf.at[s], xsems.at[s])

    def y_store(blk, s):
        return pltpu.make_async_copy(ybuf.at[s], y_hbm.at[pl.ds(pl.multiple_of(blk * rb, rb), rb), 0], ysems.at[s])

    @pl.when(i == 0)
    def _():
        x_fetch(0, 0).start()

    @pl.when(i + 1 < n_used)
    def _():
        x_fetch(i + 1, 1 - slot).start()

    def fetch(e):
        return [pltpu.make_async_copy(hbm.at[layer, e], stage, sems.at[j]) for j, (hbm, stage, _) in enumerate(staged)]

    @pl.when(i == 0)
    def _():
        for cp in fetch(be_ref[0]):
            cp.start()

    @pl.when(used & (first_ref[i] == 1))
    def _():
        for cp in fetch(be_ref[i]):
            cp.wait()
        for _, stage, dst in staged:
            rows = stage.shape[0]
            step = min(_CAST_ROWS, rows)

            def cast(c, carry, stage=stage, dst=dst, step=step):
                sl = pl.ds(pl.multiple_of(c * step, step), step)
                dst[sl, :] = stage[sl, :].astype(bf16)
                return carry

            lax.fori_loop(0, rows // step, cast, 0)
        nxt = nxt_ref[be_ref[i]]

        @pl.when(nxt >= 0)
        def _():
            for cp in fetch(nxt):
                cp.start()

    @pl.when(used)
    def _():
        x_fetch(i, slot).wait()

    @pl.when(used & (i >= 2))
    def _():
        y_store(i - 2, slot).wait()

    def ffn(rows):
        x = _unpack_rows(xbuf[slot, 0:rows, :]).astype(bf16)
        act = (_silu(_dot(x, wg_b[...])) * _dot(x, wu_b[...])).astype(bf16)
        ybuf[slot, 0:rows, :] = _pack_rows(_dot(act, wd_b[...]))
        if rows < rb:
            ybuf[slot, rows:rb, :] = jnp.zeros((rb - rows, ybuf.shape[2]), ybuf.dtype)

    @pl.when(used & (half_ref[i] == 0))
    def _():
        ffn(rb)

    @pl.when(used & (half_ref[i] == 1))
    def _():
        ffn(rb // 2)

    @pl.when(used)
    def _():
        y_store(i, slot).start()

    @pl.when(i == n_used - 1)
    def _():
        y_store(i, slot).wait()

        @pl.when(i >= 1)
        def _():
            y_store(i - 1, 1 - slot).wait()


_EXPERT_VMEM_LIMIT = 60 * 1024 * 1024


def _experts(xs3, n_used, block_expert, first, nxt, half, wg, wu, wd, layer, rb):
    d, hid = wg.shape[2], wg.shape[3]
    n_blocks = xs3.shape[0] // rb

    anyspace = pl.BlockSpec(memory_space=pl.ANY)
    grid_spec = pltpu.PrefetchScalarGridSpec(
        num_scalar_prefetch=5,
        grid=(n_blocks,),
        in_specs=[anyspace, anyspace, anyspace, anyspace],
        out_specs=anyspace,
        scratch_shapes=[pltpu.VMEM((d, hid), f32), pltpu.VMEM((d, hid), f32), pltpu.VMEM((hid, d), f32),
                        pltpu.VMEM((d, hid), bf16), pltpu.VMEM((d, hid), bf16), pltpu.VMEM((hid, d), bf16),
                        pltpu.VMEM((2, rb, d // 2), jnp.uint32), pltpu.VMEM((2, rb, d // 2), jnp.uint32),
                        pltpu.SemaphoreType.DMA((3,)), pltpu.SemaphoreType.DMA((2,)), pltpu.SemaphoreType.DMA((2,))],
    )
    return pl.pallas_call(
        functools.partial(_expert_kernel, layer=layer),
        grid_spec=grid_spec,
        out_shape=jax.ShapeDtypeStruct(xs3.shape, xs3.dtype),
        compiler_params=pltpu.CompilerParams(dimension_semantics=("arbitrary",), vmem_limit_bytes=_EXPERT_VMEM_LIMIT),
        name="moe_experts",
    )(n_used, block_expert, first, nxt, half, xs3, wg, wu, wd)


def _combine_kernel(dcur_ref, dnxt_ref, y_hbm, x_ref, gate_ref, mod_ref, g_ref, b_ref, o_ref, ybuf, sems, *, alpha):
    i = pl.program_id(0)
    nb = pl.num_programs(0)
    tm = x_ref.shape[0]

    def issue(dref, slot):
        def body(r8, carry):
            base = pl.multiple_of(r8 * 8, 8)
            for j in range(8):
                for k in range(TOP_K):
                    pltpu.make_async_copy(y_hbm.at[dref[0, 0, k * tm + base + j]],
                                          ybuf.at[slot, k, pl.ds(base + j, 1), :], sems.at[slot]).start()
            return carry

        lax.fori_loop(0, tm // 8, body, 0)

    @pl.when(i == 0)
    def _():
        issue(dcur_ref, 0)

    def step(slot):
        @pl.when(i + 1 < nb)
        def _():
            issue(dnxt_ref, 1 - slot)

        for k in range(TOP_K):
            pltpu.make_async_copy(y_hbm.at[pl.ds(0, tm), 0], ybuf.at[slot, k], sems.at[slot]).wait()
        gates = gate_ref[...]
        f = gates[:, 4:5] * _unpack_rows(ybuf[slot, 0]) + gates[:, 5:6] * _unpack_rows(ybuf[slot, 1])
        o_ref[...] = _ln(alpha * x_ref[...] + mod_ref[0, 5:6, :] * f) * g_ref[...] + b_ref[...]

    for slot in range(2):
        pl.when(i % 2 == slot)(functools.partial(step, slot))


def _combine(y3, dest_rows, xs, rout, mod, g, b, n_lat, lat_len, n_batch, alpha, tm):
    n, d = xs.shape
    nb = n // tm
    nlb, bpb = n_lat // tm, lat_len // tm

    def grp(i):
        return jnp.where(i < nlb, i // bpb, n_batch)

    dest3 = dest_rows.reshape(nb, 1, TOP_K * tm)
    const = lambda i: (0, 0)
    return pl.pallas_call(
        functools.partial(_combine_kernel, alpha=alpha),
        grid=(nb,),
        in_specs=[pl.BlockSpec((1, 1, TOP_K * tm), lambda i: (i, 0, 0), memory_space=pltpu.SMEM),
                  pl.BlockSpec((1, 1, TOP_K * tm), lambda i: (jnp.minimum(i + 1, nb - 1), 0, 0),
                               memory_space=pltpu.SMEM),
                  pl.BlockSpec(memory_space=pl.ANY),
                  pl.BlockSpec((tm, d), lambda i: (i, 0)),
                  pl.BlockSpec((tm, LANES), lambda i: (i, 0)),
                  pl.BlockSpec((1, 6, d), lambda i: (grp(i), 0, 0)),
                  pl.BlockSpec((1, d), const), pl.BlockSpec((1, d), const)],
        out_specs=pl.BlockSpec((tm, d), lambda i: (i, 0)),
        out_shape=jax.ShapeDtypeStruct((n, d), f32),
        scratch_shapes=[pltpu.VMEM((2, TOP_K, tm, d // 2), jnp.uint32), pltpu.SemaphoreType.DMA((2,))],
        compiler_params=_cparams("arbitrary"),
        name="moe_combine",
    )(dest3, dest3, y3, xs, rout, mod, g, b)


def _moe(h, rout, routt, cnt, xs, wg, wu, wd, layer, mod, g, b, n_lat, lat_len, n_batch, alpha):
    n, d = xs.shape
    tm = routt.shape[1]
    rb = tm
    fields = routt.reshape(n // tm, 8, tm)
    eid = fields[:, 0:TOP_K, :].astype(i32)
    rank = fields[:, TOP_K:2 * TOP_K, :].astype(i32)
    counts = cnt[0, _ROUTE_LANE0:_ROUTE_LANE0 + N_EXPERTS].astype(i32)
    padded = (counts + rb - 1) // rb * rb
    pend = jnp.cumsum(padded)
    pstart = pend - padded
    dest_rows = rank
    for e in range(N_EXPERTS):
        dest_rows = dest_rows + jnp.where(eid == e, pstart[e], 0)
    dest_rows = dest_rows.reshape(-1)
    n_blocks = TOP_K * n // rb + N_EXPERTS
    n_used = (pend[-1:] // rb).astype(i32)
    blk = jnp.arange(n_blocks, dtype=i32)
    ex = jnp.arange(N_EXPERTS, dtype=i32)
    block_expert = jnp.minimum(jnp.sum((pend[None, :] <= (blk * rb)[:, None]).astype(i32), axis=1), N_EXPERTS - 1)
    block_expert = jnp.where(blk < n_used[0], block_expert, block_expert[jnp.maximum(n_used[0] - 1, 0)])
    first = ((blk < n_used[0]) & ((blk == 0) | (block_expert != jnp.roll(block_expert, 1)))).astype(i32)
    later_active = (ex[None, :] > ex[:, None]) & (counts > 0)[None, :]
    nxt = jnp.min(jnp.where(later_active, ex[None, :], N_EXPERTS), axis=1)
    nxt = jnp.where(nxt == N_EXPERTS, -1, nxt).astype(i32)
    valid = jnp.clip(pstart[block_expert] + counts[block_expert] - blk * rb, 0, rb)
    half = ((blk < n_used[0]) & (valid <= rb // 2)).astype(i32)
    xs3 = _dispatch(h, dest_rows, counts, pstart, padded, n_blocks * rb, tm)
    y3 = _experts(xs3, n_used, block_expert, first, nxt, half, wg, wu, wd, layer, rb)
    return _combine(y3, dest_rows, xs, rout, mod, g, b, n_lat, lat_len, n_batch, alpha, tm)


def _permute_w_in(w):
    d = w.shape[0]
    o_sq, o_sk, o_sv, o_rq, o_rk, o_rv, o_gf, o_gb, o_mq, o_ckv, o_kr = (
        0, 768, 1024, 1280, 1664, 2048, 2816, 3584, 4352, 5120, 5376)
    mq = w[:, o_mq:o_mq + 768].reshape(d, MLA_HEADS, MLA_NOPE_DIM + MLA_ROPE_DIM)
    parts = [w[:, o_sq:o_sq + 768], w[:, o_rv:o_rv + 768], w[:, o_gf:o_gf + 768], w[:, o_gb:o_gb + 768],
             mq[:, :, :MLA_NOPE_DIM].reshape(d, 512), w[:, o_sk:o_sk + 256], w[:, o_sv:o_sv + 256],
             mq[:, :, MLA_NOPE_DIM:].reshape(d, 256), w[:, o_ckv:o_ckv + 256], w[:, o_rq:o_rq + 384],
             w[:, o_rk:o_rk + 384], w[:, o_kr:o_kr + 64], jnp.zeros((d, _NP - _KR - 64), w.dtype)]
    return jnp.concatenate(parts, axis=1).astype(bf16)


def kernel(x, c, ctx, c_ctx, w_ada, b_ada, w_in, swa_sink, ret_decay, mla_kv_norm, mla_w_uk, mla_w_uv, w_out, ln1_g, ln1_b, ln2_g, ln2_b, moe_w_group, moe_b_group, moe_w_expert, moe_b_expert, moe_w_gate, moe_w_up, moe_w_down):
    n_batch, lat_len, d = x.shape
    ctx_len = ctx.shape[1]
    depth = w_ada.shape[0]
    n_lat, n_ctx = n_batch * lat_len, n_batch * ctx_len
    alpha = (2 * depth) ** 0.25

    cc = jnp.zeros((8, d), f32).at[:n_batch].set(c).at[n_batch].set(c_ctx)
    mod_all = _ada(cc, w_ada, b_ada).reshape(depth, 8, 6, d)
    tab = _rope_tables(lat_len, ctx_len, _inproj_rows(lat_len, n_ctx))
    x_lat, x_ctx = x.reshape(n_lat, d), ctx.reshape(n_ctx, d)

    for l in range(depth):
        ctx_out = l < depth - 1
        mod = mod_all[l]
        p = _inproj(x_lat, x_ctx, mod, tab, _permute_w_in(w_in[l]), n_lat, lat_len, ctx_len, n_batch)
        kx, v = _mla_expand(p, mla_kv_norm[l][None, :], mla_w_uk[l].astype(bf16), mla_w_uv[l].astype(bf16))
        m = _mla_attention(p, kx, v, n_batch, lat_len, ctx_len, ctx_out)
        a = _swa_attention(p, swa_sink[l], n_batch, lat_len, ctx_len, ctx_out)
        lg = jnp.log1p(-jnp.exp2(-ret_decay[l].astype(f32)))
        yf, yb = _retention(p, lg, n_batch, lat_len, ctx_len)
        n_rows = n_lat + n_ctx if ctx_out else n_lat
        rout_w = jnp.zeros((d, LANES), f32).at[:, :N_GROUPS].set(moe_w_group[l]).at[
            :, _ROUTE_LANE0:_ROUTE_LANE0 + N_EXPERTS].set(moe_w_expert[l]).astype(bf16)
        rout_b = jnp.zeros((1, LANES), f32).at[0, :N_GROUPS].set(moe_b_group[l]).at[
            0, _ROUTE_LANE0:_ROUTE_LANE0 + N_EXPERTS].set(moe_b_expert[l])
        xs, h, rout, routt, cnt = _outproj(a, yf, yb, p, m, x_lat, x_ctx, mod, w_out[l].astype(bf16), ln1_g[l][None, :],
                                           ln1_b[l][None, :], rout_w, rout_b, n_rows, n_lat, lat_len, n_batch, alpha)
        xs = _moe(h, rout, routt, cnt, xs, moe_w_gate, moe_w_up, moe_w_down, l, mod, ln2_g[l][None, :],
                  ln2_b[l][None, :], n_lat, lat_len, n_batch, alpha)
        x_lat = x_ctx = xs
    return xs[:n_lat].reshape(n_batch, lat_len, d)
```

```python
import functools

import jax
import jax.numpy as jnp
from jax import lax
from jax.experimental import pallas as pl
from jax.experimental.pallas import tpu as pltpu

f32 = jnp.float32
bf16 = jnp.bfloat16
i32 = jnp.int32

GRID_W = 64
SWA_HEADS, SWA_KV_HEADS, SWA_HEAD_DIM, SWA_WINDOW = 6, 2, 128, 128
RET_HEADS, RET_QK_DIM, RET_V_DIM, RET_CHUNK = 6, 64, 128, 128
MLA_HEADS, MLA_NOPE_DIM, MLA_ROPE_DIM, MLA_V_DIM, MLA_KV_RANK = 4, 128, 64, 128, 256
N_GROUPS, EXPERTS_PER_GROUP, TOP_K = 4, 8, 2
N_EXPERTS = N_GROUPS * EXPERTS_PER_GROUP
ROPE_BASE = 10000.0
NORM_EPS = 1e-6
NEG_INF = -1e30
LANES = 128

_SQ, _RV, _GF, _GB, _MQN, _SK, _SV, _MQR, _CKV, _RQ, _RK, _KR = (
    0, 768, 1536, 2304, 3072, 3584, 3840, 4096, 4352, 4608, 4992, 5376)
_NP = 5504
_LOG2E = 1.4426950408889634
_SWA_SCALE = SWA_HEAD_DIM ** -0.5
_RET_SCALE = RET_QK_DIM ** -0.5
_MLA_SCALE = (MLA_NOPE_DIM + MLA_ROPE_DIM) ** -0.5 * _LOG2E
_SEGMENTS = ((_SQ, 768, 0, _SWA_SCALE), (_RV, 768, None, 1.0), (_GF, 768, None, 1.0), (_GB, 768, None, 1.0),
             (_MQN, 512, None, _MLA_SCALE), (_SK, 256, 0, 1.0), (_SV, 256, None, 1.0), (_MQR, 256, 2, _MLA_SCALE),
             (_CKV, 256, None, 1.0), (_RQ, 384, 1, _RET_SCALE), (_RK, 384, 1, 1.0), (_KR, 128, 2, 1.0))

_VMEM_LIMIT = 48 * 1024 * 1024


def _cparams(*sem):
    return pltpu.CompilerParams(dimension_semantics=sem, vmem_limit_bytes=_VMEM_LIMIT)


def _tile(n, pref):
    t = min(n, pref)
    while n % t:
        t //= 2
    return t


def _ln(x):
    mu = jnp.mean(x, axis=-1, keepdims=True)
    xc = x - mu
    var = jnp.mean(xc * xc, axis=-1, keepdims=True)
    return xc * lax.rsqrt(var + NORM_EPS)


def _silu(x):
    return x / (1.0 + jnp.exp(-x))


def _pack_rows(x):
    half = x.shape[1] // 2
    lo = lax.bitcast_convert_type(x[:, :half].astype(bf16).astype(f32), jnp.uint32)
    hi = lax.bitcast_convert_type(x[:, half:].astype(bf16).astype(f32), jnp.uint32)
    return hi | (lo >> 16)


def _unpack_rows(u):
    lo = lax.bitcast_convert_type(u << 16, f32)
    hi = lax.bitcast_convert_type(u & jnp.uint32(0xFFFF0000), f32)
    return jnp.concatenate([lo, hi], axis=1)


def _dot(a, b):
    return jnp.dot(a, b, preferred_element_type=f32)


def _dot_nt(a, b):
    return lax.dot_general(a, b, (((1,), (1,)), ((), ())), preferred_element_type=f32)


def _dot_tn(a, b):
    return lax.dot_general(a, b, (((0,), (0,)), ((), ())), preferred_element_type=f32)


def _ada_kernel(c_ref, w_ref, b_ref, o_ref):
    s = _silu(c_ref[...]).astype(bf16)
    o_ref[0] = _dot(s, w_ref[0].astype(bf16)) + b_ref[0]


def _ada(cc, w_ada, b_ada):
    depth, d, n = w_ada.shape
    tn = _tile(n, 2048)
    return pl.pallas_call(
        _ada_kernel,
        grid=(depth, n // tn),
        in_specs=[pl.BlockSpec((8, d), lambda l, j: (0, 0)),
                  pl.BlockSpec((1, d, tn), lambda l, j: (l, 0, j)),
                  pl.BlockSpec((1, 1, tn), lambda l, j: (l, 0, j))],
        out_specs=pl.BlockSpec((1, 8, tn), lambda l, j: (l, 0, j)),
        out_shape=jax.ShapeDtypeStruct((depth, 8, n), f32),
        compiler_params=_cparams("parallel", "parallel"),
        name="ada",
    )(cc, w_ada, b_ada.reshape(depth, 1, n))


_INPROJ_CHUNK = 512


def _inproj_kernel(xl_ref, xc_ref, mod_ref, tab_ref, w_ref, o_ref, *, nlb):
    tm = xl_ref.shape[0]
    h = (_ln(_stream_block(xl_ref, xc_ref, nlb)) * (1.0 + mod_ref[0, 1:2, :]) + mod_ref[0, 0:1, :]).astype(bf16)
    lane = lax.broadcasted_iota(i32, (tm, LANES), 1)
    first = {32: (lane & 63) < 32, 16: (lane & 31) < 16}
    slab_kind = {}
    for off, width, typ, scale in _SEGMENTS:
        for k in range(off // LANES, (off + width) // LANES):
            slab_kind[k] = (typ, scale)
    for c0 in range(0, _NP, _INPROJ_CHUNK):
        c1 = min(c0 + _INPROJ_CHUNK, _NP)
        acc = _dot(h, w_ref[:, c0:c1])
        for k in range(c0 // LANES, c1 // LANES):
            typ, scale = slab_kind[k]
            xk = acc[:, k * LANES - c0:(k + 1) * LANES - c0]
            if scale != 1.0:
                xk = xk * scale
            if typ is not None:
                half = 16 if typ == 2 else 32
                partner = jnp.where(first[half], pltpu.roll(xk, LANES - half, 1), pltpu.roll(xk, half, 1))
                xk = xk * tab_ref[typ, 0] + partner * tab_ref[typ, 1]
            o_ref[:, k * LANES:(k + 1) * LANES] = xk.astype(bf16)


def _inproj_rows(lat_len, n_ctx):
    tm = _tile(lat_len, 256)
    while n_ctx % tm:
        tm //= 2
    return tm


def _inproj(x_lat, x_ctx, mod, tab, w, n_lat, lat_len, ctx_len, n_batch):
    d = x_lat.shape[1]
    t = n_lat + n_batch * ctx_len
    tm = _inproj_rows(lat_len, t - n_lat)
    nlb, bpb = n_lat // tm, lat_len // tm
    ctx_blocks = max(1, ctx_len // tm)

    def grp(i):
        return jnp.where(i < nlb, i // bpb, n_batch)

    def posblk(i):
        return jnp.where(i < nlb, i % bpb, bpb + (i - nlb) % ctx_blocks)

    return pl.pallas_call(
        functools.partial(_inproj_kernel, nlb=None if x_ctx is x_lat else nlb),
        grid=(t // tm,),
        in_specs=_stream_specs(x_lat, x_ctx, tm, n_lat) + [
                  pl.BlockSpec((1, 6, d), lambda i: (grp(i), 0, 0)),
                  pl.BlockSpec((3, 2, tm, LANES), lambda i: (0, 0, posblk(i), 0)),
                  pl.BlockSpec((d, _NP), lambda i: (0, 0), pipeline_mode=pl.Buffered(1))],
        out_specs=pl.BlockSpec((tm, _NP), lambda i: (i, 0)),
        out_shape=jax.ShapeDtypeStruct((t, _NP), bf16),
        compiler_params=_cparams("parallel"),
        name="inproj",
    )(x_lat, x_ctx, mod, tab, w)


def _rope_tables(lat_len, ctx_len, tm):
    lane = jnp.arange(LANES)
    t = jnp.arange(lat_len)
    rows = (t // GRID_W).astype(f32)
    cols = (t % GRID_W).astype(f32)

    def freq(half):
        return ROPE_BASE ** (-jnp.arange(half, dtype=f32) / half)

    def table(pos, inv, sign):
        ang = pos * inv[None, :]
        return jnp.stack([jnp.cos(ang), jnp.sin(ang) * sign[None, :]])

    def ident(n):
        return jnp.stack([jnp.ones((n, LANES), f32), jnp.zeros((n, LANES), f32)])

    sign32 = jnp.where((lane % 64) < 32, -1.0, 1.0).astype(f32)
    sign16 = jnp.where((lane % 32) < 16, -1.0, 1.0).astype(f32)
    inv32 = freq(32)[lane % 32]
    inv16 = freq(16)[lane % 16]
    pos_swa = jnp.where((lane // 64)[None, :] == 0, rows[:, None], cols[:, None])
    pos_mla = jnp.where(((lane % 64) // 32)[None, :] == 0, rows[:, None], cols[:, None])
    rep = max(1, tm // ctx_len)
    nc = rep * ctx_len
    pos_ret_l = jnp.broadcast_to((ctx_len + t).astype(f32)[:, None], (lat_len, LANES))
    pos_ret_c = jnp.broadcast_to(jnp.tile(jnp.arange(ctx_len), rep).astype(f32)[:, None], (nc, LANES))
    t0 = jnp.concatenate([table(pos_swa, inv32, sign32), ident(nc)], axis=1)
    t1 = jnp.concatenate([table(pos_ret_l, inv32, sign32), table(pos_ret_c, inv32, sign32)], axis=1)
    t2 = jnp.concatenate([table(pos_mla, inv16, sign16), ident(nc)], axis=1)
    return jnp.stack([t0, t1, t2])


def _mla_expand_kernel(ckv_ref, kr_ref, g_ref, wuk_ref, wuv_ref, k_ref, v_ref):
    c = ckv_ref[...].astype(f32)
    cn = (c * lax.rsqrt(jnp.mean(c * c, axis=-1, keepdims=True) + NORM_EPS) * g_ref[...]).astype(bf16)
    kn = _dot(cn, wuk_ref[...]).astype(bf16)
    vv = _dot(cn, wuv_ref[...]).astype(bf16)
    kr = kr_ref[...]
    for h in range(MLA_HEADS):
        k_ref[h, :, 0:128] = kn[:, h * 128:(h + 1) * 128]
        k_ref[h, :, 128:256] = kr
        v_ref[h] = vv[:, h * 128:(h + 1) * 128]


def _mla_expand(p, g, wuk, wuv):
    t = p.shape[0]
    tm = _tile(t, 1024)
    return pl.pallas_call(
        _mla_expand_kernel,
        grid=(t // tm,),
        in_specs=[pl.BlockSpec((tm, 256), lambda i: (i, _CKV // 256)),
                  pl.BlockSpec((tm, 128), lambda i: (i, _KR // 128)),
                  pl.BlockSpec((1, 256), lambda i: (0, 0)),
                  pl.BlockSpec((256, 512), lambda i: (0, 0)),
                  pl.BlockSpec((256, 512), lambda i: (0, 0))],
        out_specs=[pl.BlockSpec((MLA_HEADS, tm, 256), lambda i: (0, i, 0)),
                   pl.BlockSpec((MLA_HEADS, tm, 128), lambda i: (0, i, 0))],
        out_shape=[jax.ShapeDtypeStruct((MLA_HEADS, t, 256), bf16), jax.ShapeDtypeStruct((MLA_HEADS, t, 128), bf16)],
        compiler_params=_cparams("parallel"),
        name="mla_expand",
    )(p, p, g, wuk, wuv)


def _mla_kernel(*refs, with_lat):
    if with_lat:
        qn_ref, qr_ref, kc_ref, vc_ref, kl_ref, vl_ref, o_ref, qx_scr = refs
    else:
        qn_ref, qr_ref, kc_ref, vc_ref, _, o_ref, qx_scr = refs
    lane = lax.broadcasted_iota(i32, (qn_ref.shape[0], LANES), 1)
    for h in range(MLA_HEADS):
        slab = qr_ref[:, (h // 2) * 128:(h // 2 + 1) * 128].astype(f32)
        if h % 2:
            slab = pltpu.roll(slab, 64, 1)
        qx_scr[h, :, 0:128] = qn_ref[:, h * 128:(h + 1) * 128]
        qx_scr[h, :, 128:256] = jnp.where(lane < 64, slab, 0.0).astype(bf16)
    for h in range(MLA_HEADS):
        q = qx_scr[h]
        parts = [(_dot_nt(q, kc_ref[h]), vc_ref[h])]
        if with_lat:
            parts.append((_dot_nt(q, kl_ref[h]), vl_ref[h]))
        m = None
        for s, _ in parts:
            mx = jnp.max(s, axis=1, keepdims=True)
            m = mx if m is None else jnp.maximum(m, mx)
        den, acc = None, None
        for s, vv in parts:
            pr = jnp.exp2(s - m)
            sm = jnp.sum(pr, axis=1, keepdims=True)
            pv = _dot(pr.astype(bf16), vv)
            den = sm if den is None else den + sm
            acc = pv if acc is None else acc + pv
        o_ref[:, h * 128:(h + 1) * 128] = (acc / den).astype(bf16)


def _mla_attention(p, kx, v, n_batch, lat_len, ctx_len, compute_ctx):
    t = p.shape[0]
    tq = _tile(lat_len, 256)
    nlq = lat_len // tq
    cbase = n_batch * lat_len // ctx_len
    once = pl.Buffered(1)
    out = pl.pallas_call(
        functools.partial(_mla_kernel, with_lat=True),
        grid=(n_batch, nlq),
        in_specs=[pl.BlockSpec((tq, 512), lambda b, qi: (b * nlq + qi, _MQN // 512)),
                  pl.BlockSpec((tq, 256), lambda b, qi: (b * nlq + qi, _MQR // 256)),
                  pl.BlockSpec((MLA_HEADS, ctx_len, 256), lambda b, qi: (0, cbase + b, 0)),
                  pl.BlockSpec((MLA_HEADS, ctx_len, 128), lambda b, qi: (0, cbase + b, 0)),
                  pl.BlockSpec((MLA_HEADS, lat_len, 256), lambda b, qi: (0, b, 0), pipeline_mode=once),
                  pl.BlockSpec((MLA_HEADS, lat_len, 128), lambda b, qi: (0, b, 0), pipeline_mode=once)],
        out_specs=pl.BlockSpec((tq, 512), lambda b, qi: (b * nlq + qi, 0)),
        out_shape=jax.ShapeDtypeStruct((t, 512), bf16),
        scratch_shapes=[pltpu.VMEM((MLA_HEADS, tq, 256), bf16)],
        compiler_params=_cparams("parallel", "arbitrary"),
        name="mla_attn",
    )(p, p, kx, v, kx, v)
    if compute_ctx:
        out = pl.pallas_call(
            functools.partial(_mla_kernel, with_lat=False),
            grid=(n_batch,),
            in_specs=[pl.BlockSpec((ctx_len, 512), lambda b: (cbase + b, _MQN // 512)),
                      pl.BlockSpec((ctx_len, 256), lambda b: (cbase + b, _MQR // 256)),
                      pl.BlockSpec((MLA_HEADS, ctx_len, 256), lambda b: (0, cbase + b, 0)),
                      pl.BlockSpec((MLA_HEADS, ctx_len, 128), lambda b: (0, cbase + b, 0)),
                      pl.BlockSpec(memory_space=pl.ANY)],
            out_specs=pl.BlockSpec((ctx_len, 512), lambda b: (cbase + b, 0)),
            out_shape=jax.ShapeDtypeStruct((t, 512), bf16),
            scratch_shapes=[pltpu.VMEM((MLA_HEADS, ctx_len, 256), bf16)],
            input_output_aliases={4: 0},
            compiler_params=_cparams("parallel"),
            name="mla_ctx",
        )(p, p, kx, v, out)
    return out


def _swa_softmax_out(parts, sink_col, o_ref, g):
    m = sink_col
    for s, _ in parts:
        m = jnp.maximum(m, jnp.max(s, axis=1, keepdims=True))
    den = jnp.exp(sink_col - m)
    acc = None
    for s, vv in parts:
        pr = jnp.exp(s - m)
        den = den + jnp.sum(pr, axis=1, keepdims=True)
        pv = _dot(pr.astype(bf16), vv)
        acc = pv if acc is None else acc + pv
    o = acc / den
    nq = o.shape[0] // 3
    for j in range(3):
        o_ref[:, (3 * g + j) * 128:(3 * g + j + 1) * 128] = o[j * nq:(j + 1) * nq].astype(bf16)


def _sink_col(sink_ref, g, nq):
    row = lax.broadcasted_iota(i32, (3 * nq, 1), 0)
    return jnp.where(row < nq, sink_ref[3 * g], jnp.where(row < 2 * nq, sink_ref[3 * g + 1], sink_ref[3 * g + 2]))


def _swa_kernel(sink_ref, q_ref, kp_ref, ko_ref, kn_ref, vp_ref, vo_ref, vn_ref, kc_ref, vc_ref, o_ref):
    i = pl.program_id(1)
    nb = pl.num_programs(1)
    blk = q_ref.shape[0]
    q = q_ref[...]
    r = lax.broadcasted_iota(i32, (3 * blk, 3 * blk), 0) & (blk - 1)
    c = lax.broadcasted_iota(i32, (3 * blk, 3 * blk), 1)
    lo = jnp.where(i == 0, blk, 0)
    hi = jnp.where(i == nb - 1, 2 * blk, 3 * blk)
    d = c - r
    valid = (d >= 0) & (d <= 2 * SWA_WINDOW) & (c >= lo) & (c < hi)
    for g in range(SWA_KV_HEADS):
        hs = slice(g * 128, (g + 1) * 128)
        qg = jnp.concatenate([q[:, (3 * g + j) * 128:(3 * g + j + 1) * 128] for j in range(3)], axis=0)
        kloc = jnp.concatenate([kp_ref[:, hs], ko_ref[:, hs], kn_ref[:, hs]], axis=0)
        vloc = jnp.concatenate([vp_ref[:, hs], vo_ref[:, hs], vn_ref[:, hs]], axis=0)
        s_loc = jnp.where(valid, _dot_nt(qg, kloc), NEG_INF)
        s_ctx = _dot_nt(qg, kc_ref[:, hs])
        _swa_softmax_out([(s_ctx, vc_ref[:, hs]), (s_loc, vloc)], _sink_col(sink_ref, g, blk), o_ref, g)


def _swa_ctx_kernel(sink_ref, q_ref, kc_ref, vc_ref, prev_ref, o_ref):
    del prev_ref
    q = q_ref[...]
    nq = q.shape[0]
    for g in range(SWA_KV_HEADS):
        hs = slice(g * 128, (g + 1) * 128)
        qg = jnp.concatenate([q[:, (3 * g + j) * 128:(3 * g + j + 1) * 128] for j in range(3)], axis=0)
        s_ctx = _dot_nt(qg, kc_ref[:, hs])
        _swa_softmax_out([(s_ctx, vc_ref[:, hs])], _sink_col(sink_ref, g, nq), o_ref, g)


def _swa_attention(p, sink, n_batch, lat_len, ctx_len, compute_ctx):
    t = p.shape[0]
    blk = SWA_WINDOW
    nb = lat_len // blk
    cbase = n_batch * lat_len // ctx_len
    smem = pl.BlockSpec(memory_space=pltpu.SMEM)

    def kspec(col, off):
        return pl.BlockSpec((blk, 256), lambda b, i: (b * nb + jnp.clip(i + off, 0, nb - 1), col // 256))

    out = pl.pallas_call(
        _swa_kernel,
        grid=(n_batch, nb),
        in_specs=[smem,
                  pl.BlockSpec((blk, 768), lambda b, i: (b * nb + i, _SQ // 768)),
                  kspec(_SK, -1), kspec(_SK, 0), kspec(_SK, 1),
                  kspec(_SV, -1), kspec(_SV, 0), kspec(_SV, 1),
                  pl.BlockSpec((ctx_len, 256), lambda b, i: (cbase + b, _SK // 256)),
                  pl.BlockSpec((ctx_len, 256), lambda b, i: (cbase + b, _SV // 256))],
        out_specs=pl.BlockSpec((blk, 768), lambda b, i: (b * nb + i, 0)),
        out_shape=jax.ShapeDtypeStruct((t, 768), bf16),
        compiler_params=_cparams("parallel", "parallel"),
        name="swa_attn",
    )(sink, p, p, p, p, p, p, p, p, p)
    if compute_ctx:
        out = pl.pallas_call(
            _swa_ctx_kernel,
            grid=(n_batch,),
            in_specs=[smem,
                      pl.BlockSpec((ctx_len, 768), lambda b: (cbase + b, _SQ // 768)),
                      pl.BlockSpec((ctx_len, 256), lambda b: (cbase + b, _SK // 256)),
                      pl.BlockSpec((ctx_len, 256), lambda b: (cbase + b, _SV // 256)),
                      pl.BlockSpec(memory_space=pl.ANY)],
            out_specs=pl.BlockSpec((ctx_len, 768), lambda b: (cbase + b, 0)),
            out_shape=jax.ShapeDtypeStruct((t, 768), bf16),
            input_output_aliases={4: 0},
            compiler_params=_cparams("parallel"),
            name="swa_ctx",
        )(sink, p, p, p, out)
    return out


def _ret_kernel(lg_ref, qf_ref, kf_ref, vf_ref, qb_ref, kb_ref, vb_ref, of_ref, ob_ref,
                s_scr, d_scr, qd_scr, kd_scr, gc_scr):
    i = pl.program_id(1)
    cc = RET_CHUNK

    @pl.when(i == 0)
    def _():
        s_scr[...] = jnp.zeros(s_scr.shape, f32)
        r = lax.broadcasted_iota(i32, (cc, cc), 0).astype(f32)
        c = lax.broadcasted_iota(i32, (cc, cc), 1).astype(f32)
        for direction in range(2):
            backward = direction == 1
            diff = (c - r) if backward else (r - c)
            for h in range(RET_HEADS):
                lgh = lg_ref[direction, h]
                j = direction * RET_HEADS + h
                d_scr[j] = jnp.where(diff >= 0, jnp.exp(lgh * jnp.maximum(diff, 0.0)), 0.0)
                qd_scr[j] = jnp.exp(lgh * ((cc - r) if backward else (r + 1.0)))
                kd_scr[j] = jnp.exp(lgh * (r if backward else (cc - 1.0 - r)))
                gc_scr[j] = jnp.exp(lgh * cc + jnp.zeros((cc, cc), f32))

    lane = lax.broadcasted_iota(i32, (cc, LANES), 1)
    cps = qf_ref.shape[0] // cc
    for step in range(cps):
        for pair in range(RET_HEADS // 2):
            for direction, (q_ref, k_ref, v_ref, o_ref) in enumerate(
                    ((qf_ref, kf_ref, vf_ref, of_ref), (qb_ref, kb_ref, vb_ref, ob_ref))):
                chunk = step if direction == 0 else cps - 1 - step
                rows = slice(chunk * cc, (chunk + 1) * cc)
                q2 = q_ref[rows, pair * 128:(pair + 1) * 128].astype(f32)
                k2b = k_ref[rows, pair * 128:(pair + 1) * 128]
                k2 = k2b.astype(f32)
                for sub in range(2):
                    h = 2 * pair + sub
                    j = direction * RET_HEADS + h
                    hs = slice(h * 128, (h + 1) * 128)
                    qa = jnp.where((lane < 64) if sub == 0 else (lane >= 64), q2, 0.0)
                    s = _dot_nt(qa.astype(bf16), k2b)
                    vh = v_ref[rows, hs]
                    y = (_dot((s * d_scr[j]).astype(bf16), vh)
                         + _dot((qa * qd_scr[j]).astype(bf16), s_scr[j].astype(bf16)))
                    s_scr[j] = gc_scr[j] * s_scr[j] + _dot_tn((k2 * kd_scr[j]).astype(bf16), vh)
                    o_ref[rows, hs] = _ln(y).astype(bf16)


def _retention(p, lg, n_batch, lat_len, ctx_len):
    t = p.shape[0]
    cc = RET_CHUNK
    cps = 2 if (ctx_len // cc) % 2 == 0 and (lat_len // cc) % 2 == 0 else 1
    blk = cps * cc
    ncc, nlc = ctx_len // blk, lat_len // blk
    cbase = n_batch * nlc
    smem = pl.BlockSpec(memory_space=pltpu.SMEM)
    scratch = [pltpu.VMEM((2 * RET_HEADS, cc, cc), f32) for _ in range(5)]

    def fwd_row(b, i):
        return jnp.where(i < ncc, cbase + b * ncc + i, b * nlc + (i - ncc))

    def bwd_row(b, i):
        return jnp.where(i < ncc, cbase + b * ncc + (ncc - 1 - i), b * nlc + (nlc - 1 - (i - ncc)))

    def specs(row):
        return [pl.BlockSpec((blk, 384), lambda b, i: (row(b, i), _RQ // 384)),
                pl.BlockSpec((blk, 384), lambda b, i: (row(b, i), _RK // 384)),
                pl.BlockSpec((blk, 768), lambda b, i: (row(b, i), _RV // 768))]

    return pl.pallas_call(
        _ret_kernel,
        grid=(n_batch, ncc + nlc),
        in_specs=[smem] + specs(fwd_row) + specs(bwd_row),
        out_specs=[pl.BlockSpec((blk, 768), lambda b, i: (fwd_row(b, i), 0)),
                   pl.BlockSpec((blk, 768), lambda b, i: (bwd_row(b, i), 0))],
        out_shape=[jax.ShapeDtypeStruct((t, 768), bf16), jax.ShapeDtypeStruct((t, 768), bf16)],
        scratch_shapes=scratch,
        compiler_params=_cparams("parallel", "arbitrary"),
        name="retention",
    )(lg, p, p, p, p, p, p)


def _stream_specs(x_lat, x_ctx, tm, n_lat):
    d = x_lat.shape[1]
    nlb = n_lat // tm
    if x_ctx is x_lat:
        return [pl.BlockSpec((tm, d), lambda i: (i, 0)), pl.BlockSpec((tm, d), lambda i: (0, 0))]
    return [pl.BlockSpec((tm, d), lambda i: (jnp.minimum(i, nlb - 1), 0)),
            pl.BlockSpec((tm, d), lambda i: (jnp.maximum(i - nlb, 0), 0))]


def _stream_block(xl_ref, xc_ref, nlb):
    if nlb is None:
        return xl_ref[...]
    return jnp.where(pl.program_id(0) < nlb, xl_ref[...], xc_ref[...])


def _outproj_kernel(a_ref, yf_ref, yb_ref, gf_ref, gb_ref, m_ref, xl_ref, xc_ref, mod_ref, w_ref, g_ref, b_ref,
                    rw_ref, rb_ref, xo_ref, h_ref, rout_ref, routt_ref, cnt_ref, carry_scr, *, alpha, nlb):
    @pl.when(pl.program_id(0) == 0)
    def _():
        carry_scr[...] = jnp.zeros(carry_scr.shape, f32)

    ret = (_silu(gf_ref[...].astype(f32)) * yf_ref[...].astype(f32)
           + _silu(gb_ref[...].astype(f32)) * yb_ref[...].astype(f32)).astype(bf16)
    y = _dot(jnp.concatenate([a_ref[...], ret, m_ref[...]], axis=1), w_ref[...])
    xn = _ln(alpha * _stream_block(xl_ref, xc_ref, nlb) + mod_ref[0, 2:3, :] * y) * g_ref[...] + b_ref[...]
    xo_ref[...] = xn
    h = _ln(xn) * (1.0 + mod_ref[0, 4:5, :]) + mod_ref[0, 3:4, :]
    h_ref[...] = _pack_rows(h)
    table = _route(_dot(h.astype(bf16), rw_ref[...]) + rb_ref[...], carry_scr)
    rout_ref[...] = table
    routt_ref[...] = jnp.transpose(table)[0:8, :]
    cnt_ref[...] = carry_scr[...]


def _outproj(a, yf, yb, p, m, x_lat, x_ctx, mod, w, g, b, rw, rb, n_rows, n_lat, lat_len, n_batch, alpha):
    d = x_lat.shape[1]
    tm = _tile(lat_len, 256)
    while (n_rows - n_lat) % tm:
        tm //= 2
    nlb, bpb = n_lat // tm, lat_len // tm

    def grp(i):
        return jnp.where(i < nlb, i // bpb, n_batch)

    row = lambda i: (i, 0)
    const = lambda i: (0, 0)
    return pl.pallas_call(
        functools.partial(_outproj_kernel, alpha=alpha, nlb=None if x_ctx is x_lat else nlb),
        grid=(n_rows // tm,),
        in_specs=[pl.BlockSpec((tm, 768), row), pl.BlockSpec((tm, 768), row), pl.BlockSpec((tm, 768), row),
                  pl.BlockSpec((tm, 768), lambda i: (i, _GF // 768)), pl.BlockSpec((tm, 768), lambda i: (i, _GB // 768)),
                  pl.BlockSpec((tm, 512), row)] + _stream_specs(x_lat, x_ctx, tm, n_lat) + [
                  pl.BlockSpec((1, 6, d), lambda i: (grp(i), 0, 0)),
                  pl.BlockSpec(w.shape, const),
                  pl.BlockSpec((1, d), const), pl.BlockSpec((1, d), const),
                  pl.BlockSpec((d, LANES), const), pl.BlockSpec((1, LANES), const)],
        out_specs=[pl.BlockSpec((tm, d), row), pl.BlockSpec((tm, d // 2), row),
                   pl.BlockSpec((tm, LANES), row), pl.BlockSpec((8, tm), row), pl.BlockSpec((8, LANES), const)],
        out_shape=[jax.ShapeDtypeStruct((n_rows, d), f32), jax.ShapeDtypeStruct((n_rows, d // 2), jnp.uint32),
                   jax.ShapeDtypeStruct((n_rows, LANES), f32), jax.ShapeDtypeStruct((n_rows // tm * 8, tm), f32),
                   jax.ShapeDtypeStruct((8, LANES), f32)],
        scratch_shapes=[pltpu.VMEM((8, LANES), f32)],
        compiler_params=_cparams("arbitrary"),
        name="outproj",
    )(a, yf, yb, p, p, m, x_lat, x_ctx, mod, w, g, b, rw, rb)


_ROUTE_LANE0 = N_GROUPS


def _route(logits, carry_scr):
    tm = logits.shape[0]
    lane = lax.broadcasted_iota(i32, (tm, LANES), 1)
    lane_f = lane.astype(f32)
    big = float(2 * LANES)
    gl = jnp.where(lane < N_GROUPS, logits, -jnp.inf)
    gmax = jnp.max(gl, axis=1, keepdims=True)
    gidx = jnp.min(jnp.where(gl == gmax, lane_f, big), axis=1, keepdims=True)
    p_group = 1.0 / jnp.sum(jnp.exp(gl - gmax), axis=1, keepdims=True)
    egroup = ((lane - _ROUTE_LANE0) >> 3).astype(f32)
    in_group = (lane >= _ROUTE_LANE0) & (lane < _ROUTE_LANE0 + N_EXPERTS) & (egroup == gidx)
    ev = jnp.where(in_group, logits, -jnp.inf)
    e1 = jnp.max(ev, axis=1, keepdims=True)
    i1 = jnp.min(jnp.where(ev == e1, lane_f, big), axis=1, keepdims=True)
    ev2 = jnp.where(lane_f == i1, -jnp.inf, ev)
    e2 = jnp.max(ev2, axis=1, keepdims=True)
    i2 = jnp.min(jnp.where(ev2 == e2, lane_f, big), axis=1, keepdims=True)
    tt = jnp.exp(e2 - e1)
    w1 = p_group / (1.0 + tt)
    w2 = p_group * tt / (1.0 + tt)

    hit1 = lane_f == i1
    hit2 = lane_f == i2
    onehot = jnp.where(hit1, 1.0, jnp.where(hit2, 1.0, 0.0))
    rr = lax.broadcasted_iota(i32, (tm, tm), 0)
    cc = lax.broadcasted_iota(i32, (tm, tm), 1)
    lower = jnp.where(cc < rr, 1.0, 0.0).astype(bf16)
    before = _dot(lower, onehot.astype(bf16)) + carry_scr[0:1, :]
    rank1 = jnp.sum(jnp.where(hit1, before, 0.0), axis=1, keepdims=True)
    rank2 = jnp.sum(jnp.where(hit2, before, 0.0), axis=1, keepdims=True)
    carry_scr[0:1, :] = carry_scr[0:1, :] + jnp.sum(onehot, axis=0, keepdims=True)

    return jnp.where(lane == 0, i1 - _ROUTE_LANE0, jnp.where(lane == 1, i2 - _ROUTE_LANE0, jnp.where(
        lane == 2, rank1, jnp.where(lane == 3, rank2, jnp.where(lane == 4, w1, jnp.where(lane == 5, w2, 0.0))))))


def _dispatch_kernel(cnt_ref, pstart_ref, padded_ref, dest_ref, h_ref, o_hbm, sem):
    i = pl.program_id(0)
    nb = pl.num_programs(0) - 1
    tm = dest_ref.shape[2] // TOP_K

    def row_copy(src_row, dst_row):
        return pltpu.make_async_copy(h_ref.at[pl.ds(src_row, 1), :], o_hbm.at[dst_row], sem)

    @pl.when(i < nb)
    def _():
        def issue(r8, carry):
            base = pl.multiple_of(r8 * 8, 8)
            for j in range(8):
                for k in range(TOP_K):
                    row_copy(base + j, dest_ref[0, 0, k * tm + base + j]).start()
            return carry

        lax.fori_loop(0, tm // 8, issue, 0)
        for _ in range(TOP_K):
            pltpu.make_async_copy(h_ref, o_hbm.at[pl.ds(0, tm), 0], sem).wait()

    @pl.when(i == nb)
    def _():
        def per_expert(e, total):
            lo, hi = cnt_ref[e], padded_ref[e]

            def fill(r, carry):
                row_copy(0, pstart_ref[e] + r).start()
                return carry

            lax.fori_loop(lo, hi, fill, 0)
            return total + (hi - lo)

        total = lax.fori_loop(0, N_EXPERTS, per_expert, 0)

        def drain(r, carry):
            row_copy(0, 0).wait()
            return carry

        lax.fori_loop(0, total, drain, 0)


def _dispatch(h, dest_rows, counts, pstart, padded, n_rows, tm):
    n, d = h.shape
    nb = n // tm
    grid_spec = pltpu.PrefetchScalarGridSpec(
        num_scalar_prefetch=3,
        grid=(nb + 1,),
        in_specs=[pl.BlockSpec((1, 1, TOP_K * tm), lambda i, *_: (jnp.minimum(i, nb - 1), 0, 0),
                               memory_space=pltpu.SMEM),
                  pl.BlockSpec((tm, d), lambda i, *_: (jnp.minimum(i, nb - 1), 0))],
        out_specs=pl.BlockSpec(memory_space=pl.ANY),
        scratch_shapes=[pltpu.SemaphoreType.DMA(())],
    )
    return pl.pallas_call(
        _dispatch_kernel,
        grid_spec=grid_spec,
        out_shape=jax.ShapeDtypeStruct((n_rows, 1, d), h.dtype),
        compiler_params=_cparams("arbitrary"),
        name="moe_dispatch",
    )(counts, pstart, padded, dest_rows.reshape(nb, 1, TOP_K * tm), h)


_CAST_ROWS = 256


def _expert_kernel(nused_ref, be_ref, first_ref, nxt_ref, half_ref, x_hbm, wg_hbm, wu_hbm, wd_hbm, y_hbm,
                   wg_f, wu_f, wd_f, wg_b, wu_b, wd_b, xbuf, ybuf, sems, xsems, ysems, *, layer):
    i = pl.program_id(0)
    n_used = nused_ref[0]
    used = i < n_used
    rb = xbuf.shape[1]
    slot = i % 2
    staged = ((wg_hbm, wg_f, wg_b), (wu_hbm, wu_f, wu_b), (wd_hbm, wd_f, wd_b))

    def x_fetch(blk, s):
        return pltpu.make_async_copy(x_hbm.at[pl.ds(pl.multiple_of(blk * rb, rb), rb), 0], xbuf.at[s], xsems.at[s])

    def y_store(blk, s):
        return pltpu.make_async_copy(ybuf.at[s], y_hbm.at[pl.ds(pl.multiple_of(blk * rb, rb), rb), 0], ysems.at[s])

    @pl.when(i == 0)
    def _():
        x_fetch(0, 0).start()

    @pl.when(i + 1 < n_used)
    def _():
        x_fetch(i + 1, 1 - slot).start()

    def fetch(e):
        return [pltpu.make_async_copy(hbm.at[layer, e], stage, sems.at[j]) for j, (hbm, stage, _) in enumerate(staged)]

    @pl.when(i == 0)
    def _():
        for cp in fetch(be_ref[0]):
            cp.start()

    @pl.when(used & (first_ref[i] == 1))
    def _():
        for cp in fetch(be_ref[i]):
            cp.wait()
        for _, stage, dst in staged:
            rows = stage.shape[0]
            step = min(_CAST_ROWS, rows)

            def cast(c, carry, stage=stage, dst=dst, step=step):
                sl = pl.ds(pl.multiple_of(c * step, step), step)
                dst[sl, :] = stage[sl, :].astype(bf16)
                return carry

            lax.fori_loop(0, rows // step, cast, 0)
        nxt = nxt_ref[be_ref[i]]

        @pl.when(nxt >= 0)
        def _():
            for cp in fetch(nxt):
                cp.start()

    @pl.when(used)
    def _():
        x_fetch(i, slot).wait()

    @pl.when(used & (i >= 2))
    def _():
        y_store(i - 2, slot).wait()

    def ffn(rows):
        x = _unpack_rows(xbuf[slot, 0:rows, :]).astype(bf16)
        act = (_silu(_dot(x, wg_b[...])) * _dot(x, wu_b[...])).astype(bf16)
        ybuf[slot, 0:rows, :] = _pack_rows(_dot(act, wd_b[...]))
        if rows < rb:
            ybuf[slot, rows:rb, :] = jnp.zeros((rb - rows, ybuf.shape[2]), ybuf.dtype)

    @pl.when(used & (half_ref[i] == 0))
    def _():
        ffn(rb)

    @pl.when(used & (half_ref[i] == 1))
    def _():
        ffn(rb // 2)

    @pl.when(used)
    def _():
        y_store(i, slot).start()

    @pl.when(i == n_used - 1)
    def _():
        y_store(i, slot).wait()

        @pl.when(i >= 1)
        def _():
            y_store(i - 1, 1 - slot).wait()


_EXPERT_VMEM_LIMIT = 60 * 1024 * 1024


def _experts(xs3, n_used, block_expert, first, nxt, half, wg, wu, wd, layer, rb):
    d, hid = wg.shape[2], wg.shape[3]
    n_blocks = xs3.shape[0] // rb

    anyspace = pl.BlockSpec(memory_space=pl.ANY)
    grid_spec = pltpu.PrefetchScalarGridSpec(
        num_scalar_prefetch=5,
        grid=(n_blocks,),
        in_specs=[anyspace, anyspace, anyspace, anyspace],
        out_specs=anyspace,
        scratch_shapes=[pltpu.VMEM((d, hid), f32), pltpu.VMEM((d, hid), f32), pltpu.VMEM((hid, d), f32),
                        pltpu.VMEM((d, hid), bf16), pltpu.VMEM((d, hid), bf16), pltpu.VMEM((hid, d), bf16),
                        pltpu.VMEM((2, rb, d // 2), jnp.uint32), pltpu.VMEM((2, rb, d // 2), jnp.uint32),
                        pltpu.SemaphoreType.DMA((3,)), pltpu.SemaphoreType.DMA((2,)), pltpu.SemaphoreType.DMA((2,))],
    )
    return pl.pallas_call(
        functools.partial(_expert_kernel, layer=layer),
        grid_spec=grid_spec,
        out_shape=jax.ShapeDtypeStruct(xs3.shape, xs3.dtype),
        compiler_params=pltpu.CompilerParams(dimension_semantics=("arbitrary",), vmem_limit_bytes=_EXPERT_VMEM_LIMIT),
        name="moe_experts",
    )(n_used, block_expert, first, nxt, half, xs3, wg, wu, wd)


def _combine_kernel(dcur_ref, dnxt_ref, y_hbm, x_ref, gate_ref, mod_ref, g_ref, b_ref, o_ref, ybuf, sems, *, alpha):
    i = pl.program_id(0)
    nb = pl.num_programs(0)
    tm = x_ref.shape[0]

    def issue(dref, slot):
        def body(r8, carry):
            base = pl.multiple_of(r8 * 8, 8)
            for j in range(8):
                for k in range(TOP_K):
                    pltpu.make_async_copy(y_hbm.at[dref[0, 0, k * tm + base + j]],
                                          ybuf.at[slot, k, pl.ds(base + j, 1), :], sems.at[slot]).start()
            return carry

        lax.fori_loop(0, tm // 8, body, 0)

    @pl.when(i == 0)
    def _():
        issue(dcur_ref, 0)

    def step(slot):
        @pl.when(i + 1 < nb)
        def _():
            issue(dnxt_ref, 1 - slot)

        for k in range(TOP_K):
            pltpu.make_async_copy(y_hbm.at[pl.ds(0, tm), 0], ybuf.at[slot, k], sems.at[slot]).wait()
        gates = gate_ref[...]
        f = gates[:, 4:5] * _unpack_rows(ybuf[slot, 0]) + gates[:, 5:6] * _unpack_rows(ybuf[slot, 1])
        o_ref[...] = _ln(alpha * x_ref[...] + mod_ref[0, 5:6, :] * f) * g_ref[...] + b_ref[...]

    for slot in range(2):
        pl.when(i % 2 == slot)(functools.partial(step, slot))


def _combine(y3, dest_rows, xs, rout, mod, g, b, n_lat, lat_len, n_batch, alpha, tm):
    n, d = xs.shape
    nb = n // tm
    nlb, bpb = n_lat // tm, lat_len // tm

    def grp(i):
        return jnp.where(i < nlb, i // bpb, n_batch)

    dest3 = dest_rows.reshape(nb, 1, TOP_K * tm)
    const = lambda i: (0, 0)
    return pl.pallas_call(
        functools.partial(_combine_kernel, alpha=alpha),
        grid=(nb,),
        in_specs=[pl.BlockSpec((1, 1, TOP_K * tm), lambda i: (i, 0, 0), memory_space=pltpu.SMEM),
                  pl.BlockSpec((1, 1, TOP_K * tm), lambda i: (jnp.minimum(i + 1, nb - 1), 0, 0),
                               memory_space=pltpu.SMEM),
                  pl.BlockSpec(memory_space=pl.ANY),
                  pl.BlockSpec((tm, d), lambda i: (i, 0)),
                  pl.BlockSpec((tm, LANES), lambda i: (i, 0)),
                  pl.BlockSpec((1, 6, d), lambda i: (grp(i), 0, 0)),
                  pl.BlockSpec((1, d), const), pl.BlockSpec((1, d), const)],
        out_specs=pl.BlockSpec((tm, d), lambda i: (i, 0)),
        out_shape=jax.ShapeDtypeStruct((n, d), f32),
        scratch_shapes=[pltpu.VMEM((2, TOP_K, tm, d // 2), jnp.uint32), pltpu.SemaphoreType.DMA((2,))],
        compiler_params=_cparams("arbitrary"),
        name="moe_combine",
    )(dest3, dest3, y3, xs, rout, mod, g, b)


def _moe(h, rout, routt, cnt, xs, wg, wu, wd, layer, mod, g, b, n_lat, lat_len, n_batch, alpha):
    n, d = xs.shape
    tm = routt.shape[1]
    rb = tm
    fields = routt.reshape(n // tm, 8, tm)
    eid = fields[:, 0:TOP_K, :].astype(i32)
    rank = fields[:, TOP_K:2 * TOP_K, :].astype(i32)
    counts = cnt[0, _ROUTE_LANE0:_ROUTE_LANE0 + N_EXPERTS].astype(i32)
    padded = (counts + rb - 1) // rb * rb
    pend = jnp.cumsum(padded)
    pstart = pend - padded
    dest_rows = rank
    for e in range(N_EXPERTS):
        dest_rows = dest_rows + jnp.where(eid == e, pstart[e], 0)
    dest_rows = dest_rows.reshape(-1)
    n_blocks = TOP_K * n // rb + N_EXPERTS
    n_used = (pend[-1:] // rb).astype(i32)
    ex = jnp.arange(N_EXPERTS, dtype=i32)
    row0 = (jnp.arange(n_blocks, dtype=i32) * rb)[:, None]
    owns = (pstart[None, :] <= row0) & (row0 < pend[None, :])
    used = jnp.any(owns, axis=1)
    last_active = jnp.max(jnp.where(counts > 0, ex, 0))
    block_expert = jnp.where(used, jnp.sum(jnp.where(owns, ex[None, :], 0), axis=1), last_active).astype(i32)
    first = jnp.any(owns & (pstart[None, :] == row0), axis=1).astype(i32)
    valid = jnp.sum(jnp.where(owns, jnp.clip((pstart + counts)[None, :] - row0, 0, rb), 0), axis=1)
    half = (used & (valid <= rb // 2)).astype(i32)
    later_active = (ex[None, :] > ex[:, None]) & (counts > 0)[None, :]
    nxt = jnp.min(jnp.where(later_active, ex[None, :], N_EXPERTS), axis=1)
    nxt = jnp.where(nxt == N_EXPERTS, -1, nxt).astype(i32)
    xs3 = _dispatch(h, dest_rows, counts, pstart, padded, n_blocks * rb, tm)
    y3 = _experts(xs3, n_used, block_expert, first, nxt, half, wg, wu, wd, layer, rb)
    return _combine(y3, dest_rows, xs, rout, mod, g, b, n_lat, lat_len, n_batch, alpha, tm)


def _permute_w_in(w):
    d = w.shape[0]
    o_sq, o_sk, o_sv, o_rq, o_rk, o_rv, o_gf, o_gb, o_mq, o_ckv, o_kr = (
        0, 768, 1024, 1280, 1664, 2048, 2816, 3584, 4352, 5120, 5376)
    mq = w[:, o_mq:o_mq + 768].reshape(d, MLA_HEADS, MLA_NOPE_DIM + MLA_ROPE_DIM)
    parts = [w[:, o_sq:o_sq + 768], w[:, o_rv:o_rv + 768], w[:, o_gf:o_gf + 768], w[:, o_gb:o_gb + 768],
             mq[:, :, :MLA_NOPE_DIM].reshape(d, 512), w[:, o_sk:o_sk + 256], w[:, o_sv:o_sv + 256],
             mq[:, :, MLA_NOPE_DIM:].reshape(d, 256), w[:, o_ckv:o_ckv + 256], w[:, o_rq:o_rq + 384],
             w[:, o_rk:o_rk + 384], w[:, o_kr:o_kr + 64], jnp.zeros((d, _NP - _KR - 64), w.dtype)]
    return jnp.concatenate(parts, axis=1).astype(bf16)


def kernel(x, c, ctx, c_ctx, w_ada, b_ada, w_in, swa_sink, ret_decay, mla_kv_norm, mla_w_uk, mla_w_uv, w_out, ln1_g, ln1_b, ln2_g, ln2_b, moe_w_group, moe_b_group, moe_w_expert, moe_b_expert, moe_w_gate, moe_w_up, moe_w_down):
    n_batch, lat_len, d = x.shape
    ctx_len = ctx.shape[1]
    depth = w_ada.shape[0]
    n_lat, n_ctx = n_batch * lat_len, n_batch * ctx_len
    alpha = (2 * depth) ** 0.25

    cc = jnp.zeros((8, d), f32).at[:n_batch].set(c).at[n_batch].set(c_ctx)
    mod_all = _ada(cc, w_ada, b_ada).reshape(depth, 8, 6, d)
    tab = _rope_tables(lat_len, ctx_len, _inproj_rows(lat_len, n_ctx))
    x_lat, x_ctx = x.reshape(n_lat, d), ctx.reshape(n_ctx, d)

    for l in range(depth):
        ctx_out = l < depth - 1
        mod = mod_all[l]
        p = _inproj(x_lat, x_ctx, mod, tab, _permute_w_in(w_in[l]), n_lat, lat_len, ctx_len, n_batch)
        kx, v = _mla_expand(p, mla_kv_norm[l][None, :], mla_w_uk[l].astype(bf16), mla_w_uv[l].astype(bf16))
        m = _mla_attention(p, kx, v, n_batch, lat_len, ctx_len, ctx_out)
        a = _swa_attention(p, swa_sink[l], n_batch, lat_len, ctx_len, ctx_out)
        lg = jnp.log1p(-jnp.exp2(-ret_decay[l].astype(f32)))
        yf, yb = _retention(p, lg, n_batch, lat_len, ctx_len)
        n_rows = n_lat + n_ctx if ctx_out else n_lat
        rout_w = jnp.zeros((d, LANES), f32).at[:, :N_GROUPS].set(moe_w_group[l]).at[
            :, _ROUTE_LANE0:_ROUTE_LANE0 + N_EXPERTS].set(moe_w_expert[l]).astype(bf16)
        rout_b = jnp.zeros((1, LANES), f32).at[0, :N_GROUPS].set(moe_b_group[l]).at[
            0, _ROUTE_LANE0:_ROUTE_LANE0 + N_EXPERTS].set(moe_b_expert[l])
        xs, h, rout, routt, cnt = _outproj(a, yf, yb, p, m, x_lat, x_ctx, mod, w_out[l].astype(bf16), ln1_g[l][None, :],
                                           ln1_b[l][None, :], rout_w, rout_b, n_rows, n_lat, lat_len, n_batch, alpha)
        xs = _moe(h, rout, routt, cnt, xs, moe_w_gate, moe_w_up, moe_w_down, l, mod, ln2_g[l][None, :],
                  ln2_b[l][None, :], n_lat, lat_len, n_batch, alpha)
        x_lat = x_ctx = xs
    return xs[:n_lat].reshape(n_batch, lat_len, d)
```

```python
import functools

import jax
import jax.numpy as jnp
from jax import lax
from jax.experimental import pallas as pl
from jax.experimental.pallas import tpu as pltpu

f32 = jnp.float32
bf16 = jnp.bfloat16
i32 = jnp.int32

GRID_W = 64
SWA_HEADS, SWA_KV_HEADS, SWA_HEAD_DIM, SWA_WINDOW = 6, 2, 128, 128
RET_HEADS, RET_QK_DIM, RET_V_DIM, RET_CHUNK = 6, 64, 128, 128
MLA_HEADS, MLA_NOPE_DIM, MLA_ROPE_DIM, MLA_V_DIM, MLA_KV_RANK = 4, 128, 64, 128, 256
N_GROUPS, EXPERTS_PER_GROUP, TOP_K = 4, 8, 2
N_EXPERTS = N_GROUPS * EXPERTS_PER_GROUP
ROPE_BASE = 10000.0
NORM_EPS = 1e-6
NEG_INF = -1e30
LANES = 128

_SQ, _RV, _GF, _GB, _MQN, _SK, _SV, _MQR, _CKV, _RQ, _RK, _KR = (
    0, 768, 1536, 2304, 3072, 3584, 3840, 4096, 4352, 4608, 4992, 5376)
_NP = 5504
_LOG2E = 1.4426950408889634
_SWA_SCALE = SWA_HEAD_DIM ** -0.5
_RET_SCALE = RET_QK_DIM ** -0.5
_MLA_SCALE = (MLA_NOPE_DIM + MLA_ROPE_DIM) ** -0.5 * _LOG2E
_SEGMENTS = ((_SQ, 768, 0, _SWA_SCALE), (_RV, 768, None, 1.0), (_GF, 768, None, 1.0), (_GB, 768, None, 1.0),
             (_MQN, 512, None, _MLA_SCALE), (_SK, 256, 0, 1.0), (_SV, 256, None, 1.0), (_MQR, 256, 2, _MLA_SCALE),
             (_CKV, 256, None, 1.0), (_RQ, 384, 1, _RET_SCALE), (_RK, 384, 1, 1.0), (_KR, 128, 2, 1.0))

_VMEM_LIMIT = 48 * 1024 * 1024


def _cparams(*sem):
    return pltpu.CompilerParams(dimension_semantics=sem, vmem_limit_bytes=_VMEM_LIMIT)


def _tile(n, pref):
    t = min(n, pref)
    while n % t:
        t //= 2
    return t


def _ln(x):
    mu = jnp.mean(x, axis=-1, keepdims=True)
    xc = x - mu
    var = jnp.mean(xc * xc, axis=-1, keepdims=True)
    return xc * lax.rsqrt(var + NORM_EPS)


def _silu(x):
    return x / (1.0 + jnp.exp(-x))


def _pack_rows(x):
    half = x.shape[1] // 2
    lo = lax.bitcast_convert_type(x[:, :half].astype(bf16).astype(f32), jnp.uint32)
    hi = lax.bitcast_convert_type(x[:, half:].astype(bf16).astype(f32), jnp.uint32)
    return hi | (lo >> 16)


def _unpack_rows(u):
    lo = lax.bitcast_convert_type(u << 16, f32)
    hi = lax.bitcast_convert_type(u & jnp.uint32(0xFFFF0000), f32)
    return jnp.concatenate([lo, hi], axis=1)


def _dot(a, b):
    return jnp.dot(a, b, preferred_element_type=f32)


def _dot_nt(a, b):
    return lax.dot_general(a, b, (((1,), (1,)), ((), ())), preferred_element_type=f32)


def _dot_tn(a, b):
    return lax.dot_general(a, b, (((0,), (0,)), ((), ())), preferred_element_type=f32)


def _ada_kernel(c_ref, w_ref, b_ref, o_ref):
    s = _silu(c_ref[...]).astype(bf16)
    o_ref[0] = _dot(s, w_ref[0].astype(bf16)) + b_ref[0]


def _ada(cc, w_ada, b_ada):
    depth, d, n = w_ada.shape
    tn = _tile(n, 2048)
    return pl.pallas_call(
        _ada_kernel,
        grid=(depth, n // tn),
        in_specs=[pl.BlockSpec((8, d), lambda l, j: (0, 0)),
                  pl.BlockSpec((1, d, tn), lambda l, j: (l, 0, j)),
                  pl.BlockSpec((1, 1, tn), lambda l, j: (l, 0, j))],
        out_specs=pl.BlockSpec((1, 8, tn), lambda l, j: (l, 0, j)),
        out_shape=jax.ShapeDtypeStruct((depth, 8, n), f32),
        compiler_params=_cparams("parallel", "parallel"),
        name="ada",
    )(cc, w_ada, b_ada.reshape(depth, 1, n))


_INPROJ_CHUNK = 512


def _inproj_kernel(xl_ref, xc_ref, mod_ref, tab_ref, w_ref, o_ref, *, nlb):
    tm = xl_ref.shape[0]
    h = (_ln(_stream_block(xl_ref, xc_ref, nlb)) * (1.0 + mod_ref[0, 1:2, :]) + mod_ref[0, 0:1, :]).astype(bf16)
    lane = lax.broadcasted_iota(i32, (tm, LANES), 1)
    first = {32: (lane & 63) < 32, 16: (lane & 31) < 16}
    slab_kind = {}
    for off, width, typ, scale in _SEGMENTS:
        for k in range(off // LANES, (off + width) // LANES):
            slab_kind[k] = (typ, scale)
    for c0 in range(0, _NP, _INPROJ_CHUNK):
        c1 = min(c0 + _INPROJ_CHUNK, _NP)
        acc = _dot(h, w_ref[:, c0:c1])
        for k in range(c0 // LANES, c1 // LANES):
            typ, scale = slab_kind[k]
            xk = acc[:, k * LANES - c0:(k + 1) * LANES - c0]
            if scale != 1.0:
                xk = xk * scale
            if typ is not None:
                half = 16 if typ == 2 else 32
                partner = jnp.where(first[half], pltpu.roll(xk, LANES - half, 1), pltpu.roll(xk, half, 1))
                xk = xk * tab_ref[typ, 0] + partner * tab_ref[typ, 1]
            o_ref[:, k * LANES:(k + 1) * LANES] = xk.astype(bf16)


_INPROJ_VMEM_LIMIT = 60 * 1024 * 1024


def _inproj_rows(lat_len, n_ctx):
    tm = _tile(lat_len, 512)
    while n_ctx % tm:
        tm //= 2
    return tm


def _inproj(x_lat, x_ctx, mod, tab, w, n_lat, lat_len, ctx_len, n_batch):
    d = x_lat.shape[1]
    t = n_lat + n_batch * ctx_len
    tm = _inproj_rows(lat_len, t - n_lat)
    nlb, bpb = n_lat // tm, lat_len // tm
    ctx_blocks = max(1, ctx_len // tm)

    def grp(i):
        return jnp.where(i < nlb, i // bpb, n_batch)

    def posblk(i):
        return jnp.where(i < nlb, i % bpb, bpb + (i - nlb) % ctx_blocks)

    return pl.pallas_call(
        functools.partial(_inproj_kernel, nlb=None if x_ctx is x_lat else nlb),
        grid=(t // tm,),
        in_specs=_stream_specs(x_lat, x_ctx, tm, n_lat) + [
                  pl.BlockSpec((1, 6, d), lambda i: (grp(i), 0, 0)),
                  pl.BlockSpec((3, 2, tm, LANES), lambda i: (0, 0, posblk(i), 0)),
                  pl.BlockSpec((d, _NP), lambda i: (0, 0), pipeline_mode=pl.Buffered(1))],
        out_specs=pl.BlockSpec((tm, _NP), lambda i: (i, 0)),
        out_shape=jax.ShapeDtypeStruct((t, _NP), bf16),
        compiler_params=pltpu.CompilerParams(dimension_semantics=("parallel",), vmem_limit_bytes=_INPROJ_VMEM_LIMIT),
        name="inproj",
    )(x_lat, x_ctx, mod, tab, w)


def _rope_tables(lat_len, ctx_len, tm):
    lane = jnp.arange(LANES)
    t = jnp.arange(lat_len)
    rows = (t // GRID_W).astype(f32)
    cols = (t % GRID_W).astype(f32)

    def freq(half):
        return ROPE_BASE ** (-jnp.arange(half, dtype=f32) / half)

    def table(pos, inv, sign):
        ang = pos * inv[None, :]
        return jnp.stack([jnp.cos(ang), jnp.sin(ang) * sign[None, :]])

    def ident(n):
        return jnp.stack([jnp.ones((n, LANES), f32), jnp.zeros((n, LANES), f32)])

    sign32 = jnp.where((lane % 64) < 32, -1.0, 1.0).astype(f32)
    sign16 = jnp.where((lane % 32) < 16, -1.0, 1.0).astype(f32)
    inv32 = freq(32)[lane % 32]
    inv16 = freq(16)[lane % 16]
    pos_swa = jnp.where((lane // 64)[None, :] == 0, rows[:, None], cols[:, None])
    pos_mla = jnp.where(((lane % 64) // 32)[None, :] == 0, rows[:, None], cols[:, None])
    rep = max(1, tm // ctx_len)
    nc = rep * ctx_len
    pos_ret_l = jnp.broadcast_to((ctx_len + t).astype(f32)[:, None], (lat_len, LANES))
    pos_ret_c = jnp.broadcast_to(jnp.tile(jnp.arange(ctx_len), rep).astype(f32)[:, None], (nc, LANES))
    t0 = jnp.concatenate([table(pos_swa, inv32, sign32), ident(nc)], axis=1)
    t1 = jnp.concatenate([table(pos_ret_l, inv32, sign32), table(pos_ret_c, inv32, sign32)], axis=1)
    t2 = jnp.concatenate([table(pos_mla, inv16, sign16), ident(nc)], axis=1)
    return jnp.stack([t0, t1, t2])


def _mla_expand_kernel(ckv_ref, kr_ref, g_ref, wuk_ref, wuv_ref, k_ref, v_ref):
    c = ckv_ref[...].astype(f32)
    cn = (c * lax.rsqrt(jnp.mean(c * c, axis=-1, keepdims=True) + NORM_EPS) * g_ref[...]).astype(bf16)
    kn = _dot(cn, wuk_ref[...]).astype(bf16)
    vv = _dot(cn, wuv_ref[...]).astype(bf16)
    kr = kr_ref[...]
    for h in range(MLA_HEADS):
        k_ref[h, :, 0:128] = kn[:, h * 128:(h + 1) * 128]
        k_ref[h, :, 128:256] = kr
        v_ref[h] = vv[:, h * 128:(h + 1) * 128]


def _mla_expand(p, g, wuk, wuv):
    t = p.shape[0]
    tm = _tile(t, 1024)
    return pl.pallas_call(
        _mla_expand_kernel,
        grid=(t // tm,),
        in_specs=[pl.BlockSpec((tm, 256), lambda i: (i, _CKV // 256)),
                  pl.BlockSpec((tm, 128), lambda i: (i, _KR // 128)),
                  pl.BlockSpec((1, 256), lambda i: (0, 0)),
                  pl.BlockSpec((256, 512), lambda i: (0, 0)),
                  pl.BlockSpec((256, 512), lambda i: (0, 0))],
        out_specs=[pl.BlockSpec((MLA_HEADS, tm, 256), lambda i: (0, i, 0)),
                   pl.BlockSpec((MLA_HEADS, tm, 128), lambda i: (0, i, 0))],
        out_shape=[jax.ShapeDtypeStruct((MLA_HEADS, t, 256), bf16), jax.ShapeDtypeStruct((MLA_HEADS, t, 128), bf16)],
        compiler_params=_cparams("parallel"),
        name="mla_expand",
    )(p, p, g, wuk, wuv)


def _mla_kernel(*refs, with_lat):
    if with_lat:
        qn_ref, qr_ref, kc_ref, vc_ref, kl_ref, vl_ref, o_ref, qx_scr = refs
    else:
        qn_ref, qr_ref, kc_ref, vc_ref, _, o_ref, qx_scr = refs
    lane = lax.broadcasted_iota(i32, (qn_ref.shape[0], LANES), 1)
    for h in range(MLA_HEADS):
        slab = qr_ref[:, (h // 2) * 128:(h // 2 + 1) * 128].astype(f32)
        if h % 2:
            slab = pltpu.roll(slab, 64, 1)
        qx_scr[h, :, 0:128] = qn_ref[:, h * 128:(h + 1) * 128]
        qx_scr[h, :, 128:256] = jnp.where(lane < 64, slab, 0.0).astype(bf16)
    for h in range(MLA_HEADS):
        q = qx_scr[h]
        parts = [(_dot_nt(q, kc_ref[h]), vc_ref[h])]
        if with_lat:
            parts.append((_dot_nt(q, kl_ref[h]), vl_ref[h]))
        m = None
        for s, _ in parts:
            mx = jnp.max(s, axis=1, keepdims=True)
            m = mx if m is None else jnp.maximum(m, mx)
        den, acc = None, None
        for s, vv in parts:
            pr = jnp.exp2(s - m)
            sm = jnp.sum(pr, axis=1, keepdims=True)
            pv = _dot(pr.astype(bf16), vv)
            den = sm if den is None else den + sm
            acc = pv if acc is None else acc + pv
        o_ref[:, h * 128:(h + 1) * 128] = (acc / den).astype(bf16)


def _mla_attention(p, kx, v, n_batch, lat_len, ctx_len, compute_ctx):
    t = p.shape[0]
    tq = _tile(lat_len, 256)
    nlq = lat_len // tq
    cbase = n_batch * lat_len // ctx_len
    once = pl.Buffered(1)
    out = pl.pallas_call(
        functools.partial(_mla_kernel, with_lat=True),
        grid=(n_batch, nlq),
        in_specs=[pl.BlockSpec((tq, 512), lambda b, qi: (b * nlq + qi, _MQN // 512)),
                  pl.BlockSpec((tq, 256), lambda b, qi: (b * nlq + qi, _MQR // 256)),
                  pl.BlockSpec((MLA_HEADS, ctx_len, 256), lambda b, qi: (0, cbase + b, 0)),
                  pl.BlockSpec((MLA_HEADS, ctx_len, 128), lambda b, qi: (0, cbase + b, 0)),
                  pl.BlockSpec((MLA_HEADS, lat_len, 256), lambda b, qi: (0, b, 0), pipeline_mode=once),
                  pl.BlockSpec((MLA_HEADS, lat_len, 128), lambda b, qi: (0, b, 0), pipeline_mode=once)],
        out_specs=pl.BlockSpec((tq, 512), lambda b, qi: (b * nlq + qi, 0)),
        out_shape=jax.ShapeDtypeStruct((t, 512), bf16),
        scratch_shapes=[pltpu.VMEM((MLA_HEADS, tq, 256), bf16)],
        compiler_params=_cparams("parallel", "arbitrary"),
        name="mla_attn",
    )(p, p, kx, v, kx, v)
    if compute_ctx:
        out = pl.pallas_call(
            functools.partial(_mla_kernel, with_lat=False),
            grid=(n_batch,),
            in_specs=[pl.BlockSpec((ctx_len, 512), lambda b: (cbase + b, _MQN // 512)),
                      pl.BlockSpec((ctx_len, 256), lambda b: (cbase + b, _MQR // 256)),
                      pl.BlockSpec((MLA_HEADS, ctx_len, 256), lambda b: (0, cbase + b, 0)),
                      pl.BlockSpec((MLA_HEADS, ctx_len, 128), lambda b: (0, cbase + b, 0)),
                      pl.BlockSpec(memory_space=pl.ANY)],
            out_specs=pl.BlockSpec((ctx_len, 512), lambda b: (cbase + b, 0)),
            out_shape=jax.ShapeDtypeStruct((t, 512), bf16),
            scratch_shapes=[pltpu.VMEM((MLA_HEADS, ctx_len, 256), bf16)],
            input_output_aliases={4: 0},
            compiler_params=_cparams("parallel"),
            name="mla_ctx",
        )(p, p, kx, v, out)
    return out


def _swa_softmax_out(parts, sink_col, o_ref, g):
    m = sink_col
    for s, _ in parts:
        m = jnp.maximum(m, jnp.max(s, axis=1, keepdims=True))
    den = jnp.exp(sink_col - m)
    acc = None
    for s, vv in parts:
        pr = jnp.exp(s - m)
        den = den + jnp.sum(pr, axis=1, keepdims=True)
        pv = _dot(pr.astype(bf16), vv)
        acc = pv if acc is None else acc + pv
    o = acc / den
    nq = o.shape[0] // 3
    for j in range(3):
        o_ref[:, (3 * g + j) * 128:(3 * g + j + 1) * 128] = o[j * nq:(j + 1) * nq].astype(bf16)


def _sink_col(sink_ref, g, nq):
    row = lax.broadcasted_iota(i32, (3 * nq, 1), 0)
    return jnp.where(row < nq, sink_ref[3 * g], jnp.where(row < 2 * nq, sink_ref[3 * g + 1], sink_ref[3 * g + 2]))


def _swa_kernel(sink_ref, q_ref, kp_ref, ko_ref, kn_ref, vp_ref, vo_ref, vn_ref, kc_ref, vc_ref, o_ref):
    i = pl.program_id(1)
    nb = pl.num_programs(1)
    blk = q_ref.shape[0]
    q = q_ref[...]
    r = lax.broadcasted_iota(i32, (3 * blk, 3 * blk), 0) & (blk - 1)
    c = lax.broadcasted_iota(i32, (3 * blk, 3 * blk), 1)
    lo = jnp.where(i == 0, blk, 0)
    hi = jnp.where(i == nb - 1, 2 * blk, 3 * blk)
    d = c - r
    valid = (d >= 0) & (d <= 2 * SWA_WINDOW) & (c >= lo) & (c < hi)
    for g in range(SWA_KV_HEADS):
        hs = slice(g * 128, (g + 1) * 128)
        qg = jnp.concatenate([q[:, (3 * g + j) * 128:(3 * g + j + 1) * 128] for j in range(3)], axis=0)
        kloc = jnp.concatenate([kp_ref[:, hs], ko_ref[:, hs], kn_ref[:, hs]], axis=0)
        vloc = jnp.concatenate([vp_ref[:, hs], vo_ref[:, hs], vn_ref[:, hs]], axis=0)
        s_loc = jnp.where(valid, _dot_nt(qg, kloc), NEG_INF)
        s_ctx = _dot_nt(qg, kc_ref[:, hs])
        _swa_softmax_out([(s_ctx, vc_ref[:, hs]), (s_loc, vloc)], _sink_col(sink_ref, g, blk), o_ref, g)


def _swa_ctx_kernel(sink_ref, q_ref, kc_ref, vc_ref, prev_ref, o_ref):
    del prev_ref
    q = q_ref[...]
    nq = q.shape[0]
    for g in range(SWA_KV_HEADS):
        hs = slice(g * 128, (g + 1) * 128)
        qg = jnp.concatenate([q[:, (3 * g + j) * 128:(3 * g + j + 1) * 128] for j in range(3)], axis=0)
        s_ctx = _dot_nt(qg, kc_ref[:, hs])
        _swa_softmax_out([(s_ctx, vc_ref[:, hs])], _sink_col(sink_ref, g, nq), o_ref, g)


def _swa_attention(p, sink, n_batch, lat_len, ctx_len, compute_ctx):
    t = p.shape[0]
    blk = SWA_WINDOW
    nb = lat_len // blk
    cbase = n_batch * lat_len // ctx_len
    smem = pl.BlockSpec(memory_space=pltpu.SMEM)

    def kspec(col, off):
        return pl.BlockSpec((blk, 256), lambda b, i: (b * nb + jnp.clip(i + off, 0, nb - 1), col // 256))

    out = pl.pallas_call(
        _swa_kernel,
        grid=(n_batch, nb),
        in_specs=[smem,
                  pl.BlockSpec((blk, 768), lambda b, i: (b * nb + i, _SQ // 768)),
                  kspec(_SK, -1), kspec(_SK, 0), kspec(_SK, 1),
                  kspec(_SV, -1), kspec(_SV, 0), kspec(_SV, 1),
                  pl.BlockSpec((ctx_len, 256), lambda b, i: (cbase + b, _SK // 256)),
                  pl.BlockSpec((ctx_len, 256), lambda b, i: (cbase + b, _SV // 256))],
        out_specs=pl.BlockSpec((blk, 768), lambda b, i: (b * nb + i, 0)),
        out_shape=jax.ShapeDtypeStruct((t, 768), bf16),
        compiler_params=_cparams("parallel", "parallel"),
        name="swa_attn",
    )(sink, p, p, p, p, p, p, p, p, p)
    if compute_ctx:
        out = pl.pallas_call(
            _swa_ctx_kernel,
            grid=(n_batch,),
            in_specs=[smem,
                      pl.BlockSpec((ctx_len, 768), lambda b: (cbase + b, _SQ // 768)),
                      pl.BlockSpec((ctx_len, 256), lambda b: (cbase + b, _SK // 256)),
                      pl.BlockSpec((ctx_len, 256), lambda b: (cbase + b, _SV // 256)),
                      pl.BlockSpec(memory_space=pl.ANY)],
            out_specs=pl.BlockSpec((ctx_len, 768), lambda b: (cbase + b, 0)),
            out_shape=jax.ShapeDtypeStruct((t, 768), bf16),
            input_output_aliases={4: 0},
            compiler_params=_cparams("parallel"),
            name="swa_ctx",
        )(sink, p, p, p, out)
    return out


def _ret_kernel(lg_ref, qf_ref, kf_ref, vf_ref, qb_ref, kb_ref, vb_ref, of_ref, ob_ref,
                s_scr, d_scr, qd_scr, kd_scr, gc_scr):
    i = pl.program_id(1)
    cc = RET_CHUNK

    @pl.when(i == 0)
    def _():
        s_scr[...] = jnp.zeros(s_scr.shape, f32)
        r = lax.broadcasted_iota(i32, (cc, cc), 0).astype(f32)
        c = lax.broadcasted_iota(i32, (cc, cc), 1).astype(f32)
        for direction in range(2):
            backward = direction == 1
            diff = (c - r) if backward else (r - c)
            for h in range(RET_HEADS):
                lgh = lg_ref[direction, h]
                j = direction * RET_HEADS + h
                d_scr[j] = jnp.where(diff >= 0, jnp.exp(lgh * jnp.maximum(diff, 0.0)), 0.0)
                qd_scr[j] = jnp.exp(lgh * ((cc - r) if backward else (r + 1.0)))
                kd_scr[j] = jnp.exp(lgh * (c if backward else (cc - 1.0 - c)))
                gc_scr[j] = jnp.exp(lgh * cc + jnp.zeros((cc, cc), f32))

    lane = lax.broadcasted_iota(i32, (cc, LANES), 1)
    cps = qf_ref.shape[0] // cc
    for step in range(cps):
        for pair in range(RET_HEADS // 2):
            for direction, (q_ref, k_ref, v_ref, o_ref) in enumerate(
                    ((qf_ref, kf_ref, vf_ref, of_ref), (qb_ref, kb_ref, vb_ref, ob_ref))):
                chunk = step if direction == 0 else cps - 1 - step
                rows = slice(chunk * cc, (chunk + 1) * cc)
                q2 = q_ref[rows, pair * 128:(pair + 1) * 128].astype(f32)
                k2b = k_ref[rows, pair * 128:(pair + 1) * 128]
                k2t = jnp.transpose(k2b.astype(f32))
                ja, jb = direction * RET_HEADS + 2 * pair, direction * RET_HEADS + 2 * pair + 1
                va = v_ref[rows, 2 * pair * 128:(2 * pair + 1) * 128]
                vb = v_ref[rows, (2 * pair + 1) * 128:(2 * pair + 2) * 128]
                qa = jnp.where(lane < 64, q2, 0.0)
                qb = jnp.where(lane >= 64, q2, 0.0)
                s2 = _dot_nt(jnp.concatenate([qa, qb], axis=0).astype(bf16), k2b)
                lhs = jnp.concatenate([s2[0:cc] * d_scr[ja], s2[cc:2 * cc] * d_scr[jb],
                                       qa * qd_scr[ja], qb * qd_scr[jb]], axis=1).astype(bf16)
                zero = jnp.zeros((cc, LANES), bf16)
                rhs = jnp.concatenate([jnp.concatenate([va, zero], axis=1), jnp.concatenate([zero, vb], axis=1),
                                       jnp.concatenate([s_scr[ja].astype(bf16), zero], axis=1),
                                       jnp.concatenate([zero, s_scr[jb].astype(bf16)], axis=1)], axis=0)
                y2 = _dot(lhs, rhs)
                upd = _dot(jnp.concatenate([k2t * kd_scr[ja], k2t * kd_scr[jb]], axis=0).astype(bf16),
                           jnp.concatenate([va, vb], axis=1))
                s_scr[ja] = gc_scr[ja] * s_scr[ja] + upd[0:LANES, 0:LANES]
                s_scr[jb] = gc_scr[jb] * s_scr[jb] + upd[LANES:2 * LANES, LANES:2 * LANES]
                o_ref[rows, 2 * pair * 128:(2 * pair + 1) * 128] = _ln(y2[:, 0:LANES]).astype(bf16)
                o_ref[rows, (2 * pair + 1) * 128:(2 * pair + 2) * 128] = _ln(y2[:, LANES:2 * LANES]).astype(bf16)


def _retention(p, lg, n_batch, lat_len, ctx_len):
    t = p.shape[0]
    cc = RET_CHUNK
    cps = 2 if (ctx_len // cc) % 2 == 0 and (lat_len // cc) % 2 == 0 else 1
    blk = cps * cc
    ncc, nlc = ctx_len // blk, lat_len // blk
    cbase = n_batch * nlc
    smem = pl.BlockSpec(memory_space=pltpu.SMEM)
    scratch = [pltpu.VMEM((2 * RET_HEADS, cc, cc), f32) for _ in range(5)]

    def fwd_row(b, i):
        return jnp.where(i < ncc, cbase + b * ncc + i, b * nlc + (i - ncc))

    def bwd_row(b, i):
        return jnp.where(i < ncc, cbase + b * ncc + (ncc - 1 - i), b * nlc + (nlc - 1 - (i - ncc)))

    def specs(row):
        return [pl.BlockSpec((blk, 384), lambda b, i: (row(b, i), _RQ // 384)),
                pl.BlockSpec((blk, 384), lambda b, i: (row(b, i), _RK // 384)),
                pl.BlockSpec((blk, 768), lambda b, i: (row(b, i), _RV // 768))]

    return pl.pallas_call(
        _ret_kernel,
        grid=(n_batch, ncc + nlc),
        in_specs=[smem] + specs(fwd_row) + specs(bwd_row),
        out_specs=[pl.BlockSpec((blk, 768), lambda b, i: (fwd_row(b, i), 0)),
                   pl.BlockSpec((blk, 768), lambda b, i: (bwd_row(b, i), 0))],
        out_shape=[jax.ShapeDtypeStruct((t, 768), bf16), jax.ShapeDtypeStruct((t, 768), bf16)],
        scratch_shapes=scratch,
        compiler_params=_cparams("parallel", "arbitrary"),
        name="retention",
    )(lg, p, p, p, p, p, p)


def _stream_specs(x_lat, x_ctx, tm, n_lat):
    d = x_lat.shape[1]
    nlb = n_lat // tm
    if x_ctx is x_lat:
        return [pl.BlockSpec((tm, d), lambda i: (i, 0)), pl.BlockSpec((tm, d), lambda i: (0, 0))]
    return [pl.BlockSpec((tm, d), lambda i: (jnp.minimum(i, nlb - 1), 0)),
            pl.BlockSpec((tm, d), lambda i: (jnp.maximum(i - nlb, 0), 0))]


def _stream_block(xl_ref, xc_ref, nlb):
    if nlb is None:
        return xl_ref[...]
    return jnp.where(pl.program_id(0) < nlb, xl_ref[...], xc_ref[...])


def _outproj_kernel(a_ref, yf_ref, yb_ref, gf_ref, gb_ref, m_ref, xl_ref, xc_ref, mod_ref, w_ref, g_ref, b_ref,
                    rw_ref, rb_ref, xo_ref, h_ref, rout_ref, routt_ref, cnt_ref, carry_scr, *, alpha, nlb):
    @pl.when(pl.program_id(0) == 0)
    def _():
        carry_scr[...] = jnp.zeros(carry_scr.shape, f32)

    ret = (_silu(gf_ref[...].astype(f32)) * yf_ref[...].astype(f32)
           + _silu(gb_ref[...].astype(f32)) * yb_ref[...].astype(f32)).astype(bf16)
    y = _dot(jnp.concatenate([a_ref[...], ret, m_ref[...]], axis=1), w_ref[...])
    xn = _ln(alpha * _stream_block(xl_ref, xc_ref, nlb) + mod_ref[0, 2:3, :] * y) * g_ref[...] + b_ref[...]
    xo_ref[...] = xn
    h = _ln(xn) * (1.0 + mod_ref[0, 4:5, :]) + mod_ref[0, 3:4, :]
    h_ref[...] = _pack_rows(h)
    table = _route(_dot(h.astype(bf16), rw_ref[...]) + rb_ref[...], carry_scr)
    rout_ref[...] = table
    routt_ref[...] = jnp.transpose(table)[0:8, :]
    cnt_ref[...] = carry_scr[...]


def _outproj(a, yf, yb, p, m, x_lat, x_ctx, mod, w, g, b, rw, rb, n_rows, n_lat, lat_len, n_batch, alpha):
    d = x_lat.shape[1]
    tm = _tile(lat_len, 256)
    while (n_rows - n_lat) % tm:
        tm //= 2
    nlb, bpb = n_lat // tm, lat_len // tm

    def grp(i):
        return jnp.where(i < nlb, i // bpb, n_batch)

    row = lambda i: (i, 0)
    const = lambda i: (0, 0)
    return pl.pallas_call(
        functools.partial(_outproj_kernel, alpha=alpha, nlb=None if x_ctx is x_lat else nlb),
        grid=(n_rows // tm,),
        in_specs=[pl.BlockSpec((tm, 768), row), pl.BlockSpec((tm, 768), row), pl.BlockSpec((tm, 768), row),
                  pl.BlockSpec((tm, 768), lambda i: (i, _GF // 768)), pl.BlockSpec((tm, 768), lambda i: (i, _GB // 768)),
                  pl.BlockSpec((tm, 512), row)] + _stream_specs(x_lat, x_ctx, tm, n_lat) + [
                  pl.BlockSpec((1, 6, d), lambda i: (grp(i), 0, 0)),
                  pl.BlockSpec(w.shape, const),
                  pl.BlockSpec((1, d), const), pl.BlockSpec((1, d), const),
                  pl.BlockSpec((d, LANES), const), pl.BlockSpec((1, LANES), const)],
        out_specs=[pl.BlockSpec((tm, d), row), pl.BlockSpec((tm, d // 2), row),
                   pl.BlockSpec((tm, LANES), row), pl.BlockSpec((8, tm), row), pl.BlockSpec((8, LANES), const)],
        out_shape=[jax.ShapeDtypeStruct((n_rows, d), f32), jax.ShapeDtypeStruct((n_rows, d // 2), jnp.uint32),
                   jax.ShapeDtypeStruct((n_rows, LANES), f32), jax.ShapeDtypeStruct((n_rows // tm * 8, tm), f32),
                   jax.ShapeDtypeStruct((8, LANES), f32)],
        scratch_shapes=[pltpu.VMEM((8, LANES), f32)],
        compiler_params=_cparams("arbitrary"),
        name="outproj",
    )(a, yf, yb, p, p, m, x_lat, x_ctx, mod, w, g, b, rw, rb)


_ROUTE_LANE0 = N_GROUPS


def _route(logits, carry_scr):
    tm = logits.shape[0]
    lane = lax.broadcasted_iota(i32, (tm, LANES), 1)
    lane_f = lane.astype(f32)
    big = float(2 * LANES)
    gl = jnp.where(lane < N_GROUPS, logits, -jnp.inf)
    gmax = jnp.max(gl, axis=1, keepdims=True)
    gidx = jnp.min(jnp.where(gl == gmax, lane_f, big), axis=1, keepdims=True)
    p_group = 1.0 / jnp.sum(jnp.exp(gl - gmax), axis=1, keepdims=True)
    egroup = ((lane - _ROUTE_LANE0) >> 3).astype(f32)
    in_group = (lane >= _ROUTE_LANE0) & (lane < _ROUTE_LANE0 + N_EXPERTS) & (egroup == gidx)
    ev = jnp.where(in_group, logits, -jnp.inf)
    e1 = jnp.max(ev, axis=1, keepdims=True)
    i1 = jnp.min(jnp.where(ev == e1, lane_f, big), axis=1, keepdims=True)
    ev2 = jnp.where(lane_f == i1, -jnp.inf, ev)
    e2 = jnp.max(ev2, axis=1, keepdims=True)
    i2 = jnp.min(jnp.where(ev2 == e2, lane_f, big), axis=1, keepdims=True)
    tt = jnp.exp(e2 - e1)
    w1 = p_group / (1.0 + tt)
    w2 = p_group * tt / (1.0 + tt)

    hit1 = lane_f == i1
    hit2 = lane_f == i2
    onehot = jnp.where(hit1, 1.0, jnp.where(hit2, 1.0, 0.0))
    rr = lax.broadcasted_iota(i32, (tm, tm), 0)
    cc = lax.broadcasted_iota(i32, (tm, tm), 1)
    lower = jnp.where(cc < rr, 1.0, 0.0).astype(bf16)
    before = _dot(lower, onehot.astype(bf16)) + carry_scr[0:1, :]
    rank1 = jnp.sum(jnp.where(hit1, before, 0.0), axis=1, keepdims=True)
    rank2 = jnp.sum(jnp.where(hit2, before, 0.0), axis=1, keepdims=True)
    carry_scr[0:1, :] = carry_scr[0:1, :] + jnp.sum(onehot, axis=0, keepdims=True)

    return jnp.where(lane == 0, i1 - _ROUTE_LANE0, jnp.where(lane == 1, i2 - _ROUTE_LANE0, jnp.where(
        lane == 2, rank1, jnp.where(lane == 3, rank2, jnp.where(lane == 4, w1, jnp.where(lane == 5, w2, 0.0))))))


def _dispatch_kernel(cnt_ref, pstart_ref, padded_ref, dest_ref, h_ref, o_hbm, sem):
    i = pl.program_id(0)
    nb = pl.num_programs(0) - 1
    tm = dest_ref.shape[2] // TOP_K

    def row_copy(src_row, dst_row):
        return pltpu.make_async_copy(h_ref.at[pl.ds(src_row, 1), :], o_hbm.at[dst_row], sem)

    @pl.when(i < nb)
    def _():
        def issue(r8, carry):
            base = pl.multiple_of(r8 * 8, 8)
            for j in range(8):
                for k in range(TOP_K):
                    row_copy(base + j, dest_ref[0, 0, k * tm + base + j]).start()
            return carry

        lax.fori_loop(0, tm // 8, issue, 0)
        for _ in range(TOP_K):
            pltpu.make_async_copy(h_ref, o_hbm.at[pl.ds(0, tm), 0], sem).wait()

    @pl.when(i == nb)
    def _():
        def per_expert(e, total):
            lo, hi = cnt_ref[e], padded_ref[e]

            def fill(r, carry):
                row_copy(0, pstart_ref[e] + r).start()
                return carry

            lax.fori_loop(lo, hi, fill, 0)
            return total + (hi - lo)

        total = lax.fori_loop(0, N_EXPERTS, per_expert, 0)

        def drain(r, carry):
            row_copy(0, 0).wait()
            return carry

        lax.fori_loop(0, total, drain, 0)


def _dispatch(h, dest_rows, counts, pstart, padded, n_rows, tm):
    n, d = h.shape
    nb = n // tm
    grid_spec = pltpu.PrefetchScalarGridSpec(
        num_scalar_prefetch=3,
        grid=(nb + 1,),
        in_specs=[pl.BlockSpec((1, 1, TOP_K * tm), lambda i, *_: (jnp.minimum(i, nb - 1), 0, 0),
                               memory_space=pltpu.SMEM),
                  pl.BlockSpec((tm, d), lambda i, *_: (jnp.minimum(i, nb - 1), 0))],
        out_specs=pl.BlockSpec(memory_space=pl.ANY),
        scratch_shapes=[pltpu.SemaphoreType.DMA(())],
    )
    return pl.pallas_call(
        _dispatch_kernel,
        grid_spec=grid_spec,
        out_shape=jax.ShapeDtypeStruct((n_rows, 1, d), h.dtype),
        compiler_params=_cparams("arbitrary"),
        name="moe_dispatch",
    )(counts, pstart, padded, dest_rows.reshape(nb, 1, TOP_K * tm), h)


_CAST_ROWS = 256


def _expert_kernel(nused_ref, be_ref, first_ref, nxt_ref, half_ref, x_hbm, wg_hbm, wu_hbm, wd_hbm, y_hbm,
                   wg_f, wu_f, wd_f, wg_b, wu_b, wd_b, xbuf, ybuf, sems, xsems, ysems, *, layer):
    i = pl.program_id(0)
    n_used = nused_ref[0]
    used = i < n_used
    rb = xbuf.shape[1]
    slot = i % 2
    staged = ((wg_hbm, wg_f, wg_b), (wu_hbm, wu_f, wu_b), (wd_hbm, wd_f, wd_b))

    def x_fetch(blk, s):
        return pltpu.make_async_copy(x_hbm.at[pl.ds(pl.multiple_of(blk * rb, rb), rb), 0], xbuf.at[s], xsems.at[s])

    def y_store(blk, s):
        return pltpu.make_async_copy(ybuf.at[s], y_hbm.at[pl.ds(pl.multiple_of(blk * rb, rb), rb), 0], ysems.at[s])

    @pl.when(i == 0)
    def _():
        x_fetch(0, 0).start()

    @pl.when(i + 1 < n_used)
    def _():
        x_fetch(i + 1, 1 - slot).start()

    def fetch(e):
        return [pltpu.make_async_copy(hbm.at[layer, e], stage, sems.at[j]) for j, (hbm, stage, _) in enumerate(staged)]

    @pl.when(i == 0)
    def _():
        for cp in fetch(be_ref[0]):
            cp.start()

    @pl.when(used & (first_ref[i] == 1))
    def _():
        for cp in fetch(be_ref[i]):
            cp.wait()
        for _, stage, dst in staged:
            rows = stage.shape[0]
            step = min(_CAST_ROWS, rows)

            def cast(c, carry, stage=stage, dst=dst, step=step):
                sl = pl.ds(pl.multiple_of(c * step, step), step)
                dst[sl, :] = stage[sl, :].astype(bf16)
                return carry

            lax.fori_loop(0, rows // step, cast, 0)
        nxt = nxt_ref[be_ref[i]]

        @pl.when(nxt >= 0)
        def _():
            for cp in fetch(nxt):
                cp.start()

    @pl.when(used)
    def _():
        x_fetch(i, slot).wait()

    @pl.when(used & (i >= 2))
    def _():
        y_store(i - 2, slot).wait()

    def ffn(rows):
        x = _unpack_rows(xbuf[slot, 0:rows, :]).astype(bf16)
        act = (_silu(_dot(x, wg_b[...])) * _dot(x, wu_b[...])).astype(bf16)
        ybuf[slot, 0:rows, :] = _pack_rows(_dot(act, wd_b[...]))
        if rows < rb:
            ybuf[slot, rows:rb, :] = jnp.zeros((rb - rows, ybuf.shape[2]), ybuf.dtype)

    @pl.when(used & (half_ref[i] == 0))
    def _():
        ffn(rb)

    @pl.when(used & (half_ref[i] == 1))
    def _():
        ffn(rb // 2)

    @pl.when(used)
    def _():
        y_store(i, slot).start()

    @pl.when(i == n_used - 1)
    def _():
        y_store(i, slot).wait()

        @pl.when(i >= 1)
        def _():
            y_store(i - 1, 1 - slot).wait()


_EXPERT_VMEM_LIMIT = 60 * 1024 * 1024


def _experts(xs3, n_used, block_expert, first, nxt, half, wg, wu, wd, layer, rb):
    d, hid = wg.shape[2], wg.shape[3]
    n_blocks = xs3.shape[0] // rb

    anyspace = pl.BlockSpec(memory_space=pl.ANY)
    grid_spec = pltpu.PrefetchScalarGridSpec(
        num_scalar_prefetch=5,
        grid=(n_blocks,),
        in_specs=[anyspace, anyspace, anyspace, anyspace],
        out_specs=anyspace,
        scratch_shapes=[pltpu.VMEM((d, hid), f32), pltpu.VMEM((d, hid), f32), pltpu.VMEM((hid, d), f32),
                        pltpu.VMEM((d, hid), bf16), pltpu.VMEM((d, hid), bf16), pltpu.VMEM((hid, d), bf16),
                        pltpu.VMEM((2, rb, d // 2), jnp.uint32), pltpu.VMEM((2, rb, d // 2), jnp.uint32),
                        pltpu.SemaphoreType.DMA((3,)), pltpu.SemaphoreType.DMA((2,)), pltpu.SemaphoreType.DMA((2,))],
    )
    return pl.pallas_call(
        functools.partial(_expert_kernel, layer=layer),
        grid_spec=grid_spec,
        out_shape=jax.ShapeDtypeStruct(xs3.shape, xs3.dtype),
        compiler_params=pltpu.CompilerParams(dimension_semantics=("arbitrary",), vmem_limit_bytes=_EXPERT_VMEM_LIMIT),
        name="moe_experts",
    )(n_used, block_expert, first, nxt, half, xs3, wg, wu, wd)


def _combine_kernel(dcur_ref, dnxt_ref, y_hbm, x_ref, gate_ref, mod_ref, g_ref, b_ref, o_ref, ybuf, sems, *, alpha):
    i = pl.program_id(0)
    nb = pl.num_programs(0)
    tm = x_ref.shape[0]

    def issue(dref, slot):
        def body(r8, carry):
            base = pl.multiple_of(r8 * 8, 8)
            for j in range(8):
                for k in range(TOP_K):
                    pltpu.make_async_copy(y_hbm.at[dref[0, 0, k * tm + base + j]],
                                          ybuf.at[slot, k, pl.ds(base + j, 1), :], sems.at[slot]).start()
            return carry

        lax.fori_loop(0, tm // 8, body, 0)

    @pl.when(i == 0)
    def _():
        issue(dcur_ref, 0)

    def step(slot):
        @pl.when(i + 1 < nb)
        def _():
            issue(dnxt_ref, 1 - slot)

        for k in range(TOP_K):
            pltpu.make_async_copy(y_hbm.at[pl.ds(0, tm), 0], ybuf.at[slot, k], sems.at[slot]).wait()
        gates = gate_ref[...]
        f = gates[:, 4:5] * _unpack_rows(ybuf[slot, 0]) + gates[:, 5:6] * _unpack_rows(ybuf[slot, 1])
        o_ref[...] = _ln(alpha * x_ref[...] + mod_ref[0, 5:6, :] * f) * g_ref[...] + b_ref[...]

    for slot in range(2):
        pl.when(i % 2 == slot)(functools.partial(step, slot))


def _combine(y3, dest_rows, xs, rout, mod, g, b, n_lat, lat_len, n_batch, alpha, tm):
    n, d = xs.shape
    nb = n // tm
    nlb, bpb = n_lat // tm, lat_len // tm

    def grp(i):
        return jnp.where(i < nlb, i // bpb, n_batch)

    dest3 = dest_rows.reshape(nb, 1, TOP_K * tm)
    const = lambda i: (0, 0)
    return pl.pallas_call(
        functools.partial(_combine_kernel, alpha=alpha),
        grid=(nb,),
        in_specs=[pl.BlockSpec((1, 1, TOP_K * tm), lambda i: (i, 0, 0), memory_space=pltpu.SMEM),
                  pl.BlockSpec((1, 1, TOP_K * tm), lambda i: (jnp.minimum(i + 1, nb - 1), 0, 0),
                               memory_space=pltpu.SMEM),
                  pl.BlockSpec(memory_space=pl.ANY),
                  pl.BlockSpec((tm, d), lambda i: (i, 0)),
                  pl.BlockSpec((tm, LANES), lambda i: (i, 0)),
                  pl.BlockSpec((1, 6, d), lambda i: (grp(i), 0, 0)),
                  pl.BlockSpec((1, d), const), pl.BlockSpec((1, d), const)],
        out_specs=pl.BlockSpec((tm, d), lambda i: (i, 0)),
        out_shape=jax.ShapeDtypeStruct((n, d), f32),
        scratch_shapes=[pltpu.VMEM((2, TOP_K, tm, d // 2), jnp.uint32), pltpu.SemaphoreType.DMA((2,))],
        compiler_params=_cparams("arbitrary"),
        name="moe_combine",
    )(dest3, dest3, y3, xs, rout, mod, g, b)


def _moe(h, rout, routt, cnt, xs, wg, wu, wd, layer, mod, g, b, n_lat, lat_len, n_batch, alpha):
    n, d = xs.shape
    tm = routt.shape[1]
    rb = tm
    fields = routt.reshape(n // tm, 8, tm)
    eid = fields[:, 0:TOP_K, :].astype(i32)
    rank = fields[:, TOP_K:2 * TOP_K, :].astype(i32)
    counts = cnt[0, _ROUTE_LANE0:_ROUTE_LANE0 + N_EXPERTS].astype(i32)
    padded = (counts + rb - 1) // rb * rb
    pend = jnp.cumsum(padded)
    pstart = pend - padded
    dest_rows = rank
    for e in range(N_EXPERTS):
        dest_rows = dest_rows + jnp.where(eid == e, pstart[e], 0)
    dest_rows = dest_rows.reshape(-1)
    n_blocks = TOP_K * n // rb + N_EXPERTS
    n_used = (pend[-1:] // rb).astype(i32)
    ex = jnp.arange(N_EXPERTS, dtype=i32)
    row0 = (jnp.arange(n_blocks, dtype=i32) * rb)[:, None]
    owns = (pstart[None, :] <= row0) & (row0 < pend[None, :])
    used = jnp.any(owns, axis=1)
    last_active = jnp.max(jnp.where(counts > 0, ex, 0))
    block_expert = jnp.where(used, jnp.sum(jnp.where(owns, ex[None, :], 0), axis=1), last_active).astype(i32)
    first = jnp.any(owns & (pstart[None, :] == row0), axis=1).astype(i32)
    valid = jnp.sum(jnp.where(owns, jnp.clip((pstart + counts)[None, :] - row0, 0, rb), 0), axis=1)
    half = (used & (valid <= rb // 2)).astype(i32)
    later_active = (ex[None, :] > ex[:, None]) & (counts > 0)[None, :]
    nxt = jnp.min(jnp.where(later_active, ex[None, :], N_EXPERTS), axis=1)
    nxt = jnp.where(nxt == N_EXPERTS, -1, nxt).astype(i32)
    xs3 = _dispatch(h, dest_rows, counts, pstart, padded, n_blocks * rb, tm)
    y3 = _experts(xs3, n_used, block_expert, first, nxt, half, wg, wu, wd, layer, rb)
    return _combine(y3, dest_rows, xs, rout, mod, g, b, n_lat, lat_len, n_batch, alpha, tm)


def _permute_w_in(w):
    d = w.shape[0]
    o_sq, o_sk, o_sv, o_rq, o_rk, o_rv, o_gf, o_gb, o_mq, o_ckv, o_kr = (
        0, 768, 1024, 1280, 1664, 2048, 2816, 3584, 4352, 5120, 5376)
    mq = w[:, o_mq:o_mq + 768].reshape(d, MLA_HEADS, MLA_NOPE_DIM + MLA_ROPE_DIM)
    parts = [w[:, o_sq:o_sq + 768], w[:, o_rv:o_rv + 768], w[:, o_gf:o_gf + 768], w[:, o_gb:o_gb + 768],
             mq[:, :, :MLA_NOPE_DIM].reshape(d, 512), w[:, o_sk:o_sk + 256], w[:, o_sv:o_sv + 256],
             mq[:, :, MLA_NOPE_DIM:].reshape(d, 256), w[:, o_ckv:o_ckv + 256], w[:, o_rq:o_rq + 384],
             w[:, o_rk:o_rk + 384], w[:, o_kr:o_kr + 64], jnp.zeros((d, _NP - _KR - 64), w.dtype)]
    return jnp.concatenate(parts, axis=1).astype(bf16)


def kernel(x, c, ctx, c_ctx, w_ada, b_ada, w_in, swa_sink, ret_decay, mla_kv_norm, mla_w_uk, mla_w_uv, w_out, ln1_g, ln1_b, ln2_g, ln2_b, moe_w_group, moe_b_group, moe_w_expert, moe_b_expert, moe_w_gate, moe_w_up, moe_w_down):
    n_batch, lat_len, d = x.shape
    ctx_len = ctx.shape[1]
    depth = w_ada.shape[0]
    n_lat, n_ctx = n_batch * lat_len, n_batch * ctx_len
    alpha = (2 * depth) ** 0.25

    cc = jnp.zeros((8, d), f32).at[:n_batch].set(c).at[n_batch].set(c_ctx)
    mod_all = _ada(cc, w_ada, b_ada).reshape(depth, 8, 6, d)
    tab = _rope_tables(lat_len, ctx_len, _inproj_rows(lat_len, n_ctx))
    x_lat, x_ctx = x.reshape(n_lat, d), ctx.reshape(n_ctx, d)

    for l in range(depth):
        ctx_out = l < depth - 1
        mod = mod_all[l]
        p = _inproj(x_lat, x_ctx, mod, tab, _permute_w_in(w_in[l]), n_lat, lat_len, ctx_len, n_batch)
        kx, v = _mla_expand(p, mla_kv_norm[l][None, :], mla_w_uk[l].astype(bf16), mla_w_uv[l].astype(bf16))
        m = _mla_attention(p, kx, v, n_batch, lat_len, ctx_len, ctx_out)
        a = _swa_attention(p, swa_sink[l], n_batch, lat_len, ctx_len, ctx_out)
        lg = jnp.log1p(-jnp.exp2(-ret_decay[l].astype(f32)))
        yf, yb = _retention(p, lg, n_batch, lat_len, ctx_len)
        n_rows = n_lat + n_ctx if ctx_out else n_lat
        rout_w = jnp.zeros((d, LANES), f32).at[:, :N_GROUPS].set(moe_w_group[l]).at[
            :, _ROUTE_LANE0:_ROUTE_LANE0 + N_EXPERTS].set(moe_w_expert[l]).astype(bf16)
        rout_b = jnp.zeros((1, LANES), f32).at[0, :N_GROUPS].set(moe_b_group[l]).at[
            0, _ROUTE_LANE0:_ROUTE_LANE0 + N_EXPERTS].set(moe_b_expert[l])
        xs, h, rout, routt, cnt = _outproj(a, yf, yb, p, m, x_lat, x_ctx, mod, w_out[l].astype(bf16), ln1_g[l][None, :],
                                           ln1_b[l][None, :], rout_w, rout_b, n_rows, n_lat, lat_len, n_batch, alpha)
        xs = _moe(h, rout, routt, cnt, xs, moe_w_gate, moe_w_up, moe_w_down, l, mod, ln2_g[l][None, :],
                  ln2_b[l][None, :], n_lat, lat_len, n_batch, alpha)
        x_lat = x_ctx = xs
    return xs[:n_lat].reshape(n_batch, lat_len, d)
```

```python
import functools

import jax
import jax.numpy as jnp
from jax import lax
from jax.experimental import pallas as pl
from jax.experimental.pallas import tpu as pltpu

f32 = jnp.float32
bf16 = jnp.bfloat16
i32 = jnp.int32

GRID_W = 64
SWA_HEADS, SWA_KV_HEADS, SWA_HEAD_DIM, SWA_WINDOW = 6, 2, 128, 128
RET_HEADS, RET_QK_DIM, RET_V_DIM, RET_CHUNK = 6, 64, 128, 128
MLA_HEADS, MLA_NOPE_DIM, MLA_ROPE_DIM, MLA_V_DIM, MLA_KV_RANK = 4, 128, 64, 128, 256
N_GROUPS, EXPERTS_PER_GROUP, TOP_K = 4, 8, 2
N_EXPERTS = N_GROUPS * EXPERTS_PER_GROUP
ROPE_BASE = 10000.0
NORM_EPS = 1e-6
NEG_INF = -1e30
LANES = 128

_SQ, _RV, _GF, _GB, _MQN, _SK, _SV, _MQR, _CKV, _RQ, _RK, _KR = (
    0, 768, 1536, 2304, 3072, 3584, 3840, 4096, 4352, 4608, 4992, 5376)
_NP = 5504
_LOG2E = 1.4426950408889634
_SWA_SCALE = SWA_HEAD_DIM ** -0.5
_RET_SCALE = RET_QK_DIM ** -0.5
_MLA_SCALE = (MLA_NOPE_DIM + MLA_ROPE_DIM) ** -0.5 * _LOG2E
_SEGMENTS = ((_SQ, 768, 0, _SWA_SCALE), (_RV, 768, None, 1.0), (_GF, 768, None, 1.0), (_GB, 768, None, 1.0),
             (_MQN, 512, None, _MLA_SCALE), (_SK, 256, 0, 1.0), (_SV, 256, None, 1.0), (_MQR, 256, 2, _MLA_SCALE),
             (_CKV, 256, None, 1.0), (_RQ, 384, 1, _RET_SCALE), (_RK, 384, 1, 1.0), (_KR, 128, 2, 1.0))

_VMEM_LIMIT = 48 * 1024 * 1024


def _cparams(*sem):
    return pltpu.CompilerParams(dimension_semantics=sem, vmem_limit_bytes=_VMEM_LIMIT)


def _tile(n, pref):
    t = min(n, pref)
    while n % t:
        t //= 2
    return t


def _ln(x):
    mu = jnp.mean(x, axis=-1, keepdims=True)
    xc = x - mu
    var = jnp.mean(xc * xc, axis=-1, keepdims=True)
    return xc * lax.rsqrt(var + NORM_EPS)


def _silu(x):
    return x / (1.0 + jnp.exp(-x))


def _pack_rows(x):
    half = x.shape[1] // 2
    lo = lax.bitcast_convert_type(x[:, :half].astype(bf16).astype(f32), jnp.uint32)
    hi = lax.bitcast_convert_type(x[:, half:].astype(bf16).astype(f32), jnp.uint32)
    return hi | (lo >> 16)


def _unpack_rows(u):
    lo = lax.bitcast_convert_type(u << 16, f32)
    hi = lax.bitcast_convert_type(u & jnp.uint32(0xFFFF0000), f32)
    return jnp.concatenate([lo, hi], axis=1)


def _dot(a, b):
    return jnp.dot(a, b, preferred_element_type=f32)


def _dot_nt(a, b):
    return lax.dot_general(a, b, (((1,), (1,)), ((), ())), preferred_element_type=f32)


def _dot_tn(a, b):
    return lax.dot_general(a, b, (((0,), (0,)), ((), ())), preferred_element_type=f32)


def _ada_kernel(c_ref, w_ref, b_ref, o_ref):
    s = _silu(c_ref[...]).astype(bf16)
    o_ref[0] = _dot(s, w_ref[0].astype(bf16)) + b_ref[0]


def _ada(cc, w_ada, b_ada):
    depth, d, n = w_ada.shape
    tn = _tile(n, 2048)
    return pl.pallas_call(
        _ada_kernel,
        grid=(depth, n // tn),
        in_specs=[pl.BlockSpec((8, d), lambda l, j: (0, 0)),
                  pl.BlockSpec((1, d, tn), lambda l, j: (l, 0, j)),
                  pl.BlockSpec((1, 1, tn), lambda l, j: (l, 0, j))],
        out_specs=pl.BlockSpec((1, 8, tn), lambda l, j: (l, 0, j)),
        out_shape=jax.ShapeDtypeStruct((depth, 8, n), f32),
        compiler_params=_cparams("parallel", "parallel"),
        name="ada",
    )(cc, w_ada, b_ada.reshape(depth, 1, n))


_INPROJ_CHUNK = 512


def _inproj_kernel(xl_ref, xc_ref, mod_ref, tab_ref, w_ref, o_ref, *, nlb):
    tm = xl_ref.shape[0]
    h = (_ln(_stream_block(xl_ref, xc_ref, nlb)) * (1.0 + mod_ref[0, 1:2, :]) + mod_ref[0, 0:1, :]).astype(bf16)
    lane = lax.broadcasted_iota(i32, (tm, LANES), 1)
    first = {32: (lane & 63) < 32, 16: (lane & 31) < 16}
    slab_kind = {}
    for off, width, typ, scale in _SEGMENTS:
        for k in range(off // LANES, (off + width) // LANES):
            slab_kind[k] = (typ, scale)
    for c0 in range(0, _NP, _INPROJ_CHUNK):
        c1 = min(c0 + _INPROJ_CHUNK, _NP)
        acc = _dot(h, w_ref[:, c0:c1])
        for k in range(c0 // LANES, c1 // LANES):
            typ, scale = slab_kind[k]
            xk = acc[:, k * LANES - c0:(k + 1) * LANES - c0]
            if scale != 1.0:
                xk = xk * scale
            if typ is not None:
                half = 16 if typ == 2 else 32
                partner = jnp.where(first[half], pltpu.roll(xk, LANES - half, 1), pltpu.roll(xk, half, 1))
                xk = xk * tab_ref[typ, 0] + partner * tab_ref[typ, 1]
            o_ref[:, k * LANES:(k + 1) * LANES] = xk.astype(bf16)


_INPROJ_VMEM_LIMIT = 60 * 1024 * 1024


def _inproj_rows(lat_len, n_ctx):
    tm = _tile(lat_len, 512)
    while n_ctx % tm:
        tm //= 2
    return tm


def _inproj(x_lat, x_ctx, mod, tab, w, n_lat, lat_len, ctx_len, n_batch):
    d = x_lat.shape[1]
    t = n_lat + n_batch * ctx_len
    tm = _inproj_rows(lat_len, t - n_lat)
    nlb, bpb = n_lat // tm, lat_len // tm
    ctx_blocks = max(1, ctx_len // tm)

    def grp(i):
        return jnp.where(i < nlb, i // bpb, n_batch)

    def posblk(i):
        return jnp.where(i < nlb, i % bpb, bpb + (i - nlb) % ctx_blocks)

    return pl.pallas_call(
        functools.partial(_inproj_kernel, nlb=None if x_ctx is x_lat else nlb),
        grid=(t // tm,),
        in_specs=_stream_specs(x_lat, x_ctx, tm, n_lat) + [
                  pl.BlockSpec((1, 6, d), lambda i: (grp(i), 0, 0)),
                  pl.BlockSpec((3, 2, tm, LANES), lambda i: (0, 0, posblk(i), 0)),
                  pl.BlockSpec((d, _NP), lambda i: (0, 0), pipeline_mode=pl.Buffered(1))],
        out_specs=pl.BlockSpec((tm, _NP), lambda i: (i, 0)),
        out_shape=jax.ShapeDtypeStruct((t, _NP), bf16),
        compiler_params=pltpu.CompilerParams(dimension_semantics=("parallel",), vmem_limit_bytes=_INPROJ_VMEM_LIMIT),
        name="inproj",
    )(x_lat, x_ctx, mod, tab, w)


def _rope_tables(lat_len, ctx_len, tm):
    lane = jnp.arange(LANES)
    t = jnp.arange(lat_len)
    rows = (t // GRID_W).astype(f32)
    cols = (t % GRID_W).astype(f32)

    def freq(half):
        return ROPE_BASE ** (-jnp.arange(half, dtype=f32) / half)

    def table(pos, inv, sign):
        ang = pos * inv[None, :]
        return jnp.stack([jnp.cos(ang), jnp.sin(ang) * sign[None, :]])

    def ident(n):
        return jnp.stack([jnp.ones((n, LANES), f32), jnp.zeros((n, LANES), f32)])

    sign32 = jnp.where((lane % 64) < 32, -1.0, 1.0).astype(f32)
    sign16 = jnp.where((lane % 32) < 16, -1.0, 1.0).astype(f32)
    inv32 = freq(32)[lane % 32]
    inv16 = freq(16)[lane % 16]
    pos_swa = jnp.where((lane // 64)[None, :] == 0, rows[:, None], cols[:, None])
    pos_mla = jnp.where(((lane % 64) // 32)[None, :] == 0, rows[:, None], cols[:, None])
    rep = max(1, tm // ctx_len)
    nc = rep * ctx_len
    pos_ret_l = jnp.broadcast_to((ctx_len + t).astype(f32)[:, None], (lat_len, LANES))
    pos_ret_c = jnp.broadcast_to(jnp.tile(jnp.arange(ctx_len), rep).astype(f32)[:, None], (nc, LANES))
    t0 = jnp.concatenate([table(pos_swa, inv32, sign32), ident(nc)], axis=1)
    t1 = jnp.concatenate([table(pos_ret_l, inv32, sign32), table(pos_ret_c, inv32, sign32)], axis=1)
    t2 = jnp.concatenate([table(pos_mla, inv16, sign16), ident(nc)], axis=1)
    return jnp.stack([t0, t1, t2])


def _mla_expand_kernel(ckv_ref, kr_ref, g_ref, wuk_ref, wuv_ref, k_ref, v_ref):
    c = ckv_ref[...].astype(f32)
    cn = (c * lax.rsqrt(jnp.mean(c * c, axis=-1, keepdims=True) + NORM_EPS) * g_ref[...]).astype(bf16)
    kn = _dot(cn, wuk_ref[...]).astype(bf16)
    vv = _dot(cn, wuv_ref[...]).astype(bf16)
    kr = kr_ref[...]
    for h in range(MLA_HEADS):
        k_ref[h, :, 0:128] = kn[:, h * 128:(h + 1) * 128]
        k_ref[h, :, 128:256] = kr
        v_ref[h] = vv[:, h * 128:(h + 1) * 128]


def _mla_expand(p, g, wuk, wuv):
    t = p.shape[0]
    tm = _tile(t, 1024)
    return pl.pallas_call(
        _mla_expand_kernel,
        grid=(t // tm,),
        in_specs=[pl.BlockSpec((tm, 256), lambda i: (i, _CKV // 256)),
                  pl.BlockSpec((tm, 128), lambda i: (i, _KR // 128)),
                  pl.BlockSpec((1, 256), lambda i: (0, 0)),
                  pl.BlockSpec((256, 512), lambda i: (0, 0)),
                  pl.BlockSpec((256, 512), lambda i: (0, 0))],
        out_specs=[pl.BlockSpec((MLA_HEADS, tm, 256), lambda i: (0, i, 0)),
                   pl.BlockSpec((MLA_HEADS, tm, 128), lambda i: (0, i, 0))],
        out_shape=[jax.ShapeDtypeStruct((MLA_HEADS, t, 256), bf16), jax.ShapeDtypeStruct((MLA_HEADS, t, 128), bf16)],
        compiler_params=_cparams("parallel"),
        name="mla_expand",
    )(p, p, g, wuk, wuv)


def _mla_kernel(*refs, with_lat):
    if with_lat:
        qn_ref, qr_ref, kc_ref, vc_ref, kl_ref, vl_ref, o_ref, qx_scr = refs
    else:
        qn_ref, qr_ref, kc_ref, vc_ref, _, o_ref, qx_scr = refs
    lane = lax.broadcasted_iota(i32, (qn_ref.shape[0], LANES), 1)
    for h in range(MLA_HEADS):
        slab = qr_ref[:, (h // 2) * 128:(h // 2 + 1) * 128].astype(f32)
        if h % 2:
            slab = pltpu.roll(slab, 64, 1)
        qx_scr[h, :, 0:128] = qn_ref[:, h * 128:(h + 1) * 128]
        qx_scr[h, :, 128:256] = jnp.where(lane < 64, slab, 0.0).astype(bf16)
    def scores(h):
        q = qx_scr[h]
        parts = [(_dot_nt(q, kc_ref[h]), vc_ref[h])]
        if with_lat:
            parts.append((_dot_nt(q, kl_ref[h]), vl_ref[h]))
        return parts

    nxt = scores(0)
    for h in range(MLA_HEADS):
        parts, nxt = nxt, (scores(h + 1) if h + 1 < MLA_HEADS else None)
        m = None
        for s, _ in parts:
            mx = jnp.max(s, axis=1, keepdims=True)
            m = mx if m is None else jnp.maximum(m, mx)
        den, acc = None, None
        for s, vv in parts:
            pr = jnp.exp2(s - m)
            sm = jnp.sum(pr, axis=1, keepdims=True)
            pv = _dot(pr.astype(bf16), vv)
            den = sm if den is None else den + sm
            acc = pv if acc is None else acc + pv
        o_ref[:, h * 128:(h + 1) * 128] = (acc / den).astype(bf16)


def _mla_attention(p, kx, v, n_batch, lat_len, ctx_len, compute_ctx):
    t = p.shape[0]
    tq = _tile(lat_len, 256)
    nlq = lat_len // tq
    cbase = n_batch * lat_len // ctx_len
    once = pl.Buffered(1)
    out = pl.pallas_call(
        functools.partial(_mla_kernel, with_lat=True),
        grid=(n_batch, nlq),
        in_specs=[pl.BlockSpec((tq, 512), lambda b, qi: (b * nlq + qi, _MQN // 512)),
                  pl.BlockSpec((tq, 256), lambda b, qi: (b * nlq + qi, _MQR // 256)),
                  pl.BlockSpec((MLA_HEADS, ctx_len, 256), lambda b, qi: (0, cbase + b, 0)),
                  pl.BlockSpec((MLA_HEADS, ctx_len, 128), lambda b, qi: (0, cbase + b, 0)),
                  pl.BlockSpec((MLA_HEADS, lat_len, 256), lambda b, qi: (0, b, 0), pipeline_mode=once),
                  pl.BlockSpec((MLA_HEADS, lat_len, 128), lambda b, qi: (0, b, 0), pipeline_mode=once)],
        out_specs=pl.BlockSpec((tq, 512), lambda b, qi: (b * nlq + qi, 0)),
        out_shape=jax.ShapeDtypeStruct((t, 512), bf16),
        scratch_shapes=[pltpu.VMEM((MLA_HEADS, tq, 256), bf16)],
        compiler_params=_cparams("parallel", "arbitrary"),
        name="mla_attn",
    )(p, p, kx, v, kx, v)
    if compute_ctx:
        out = pl.pallas_call(
            functools.partial(_mla_kernel, with_lat=False),
            grid=(n_batch,),
            in_specs=[pl.BlockSpec((ctx_len, 512), lambda b: (cbase + b, _MQN // 512)),
                      pl.BlockSpec((ctx_len, 256), lambda b: (cbase + b, _MQR // 256)),
                      pl.BlockSpec((MLA_HEADS, ctx_len, 256), lambda b: (0, cbase + b, 0)),
                      pl.BlockSpec((MLA_HEADS, ctx_len, 128), lambda b: (0, cbase + b, 0)),
                      pl.BlockSpec(memory_space=pl.ANY)],
            out_specs=pl.BlockSpec((ctx_len, 512), lambda b: (cbase + b, 0)),
            out_shape=jax.ShapeDtypeStruct((t, 512), bf16),
            scratch_shapes=[pltpu.VMEM((MLA_HEADS, ctx_len, 256), bf16)],
            input_output_aliases={4: 0},
            compiler_params=_cparams("parallel"),
            name="mla_ctx",
        )(p, p, kx, v, out)
    return out


def _swa_softmax_out(parts, sink_col, o_ref, g):
    m = sink_col
    for s, _ in parts:
        m = jnp.maximum(m, jnp.max(s, axis=1, keepdims=True))
    den = jnp.exp(sink_col - m)
    acc = None
    for s, vv in parts:
        pr = jnp.exp(s - m)
        den = den + jnp.sum(pr, axis=1, keepdims=True)
        pv = _dot(pr.astype(bf16), vv)
        acc = pv if acc is None else acc + pv
    o = acc / den
    nq = o.shape[0] // 3
    for j in range(3):
        o_ref[:, (3 * g + j) * 128:(3 * g + j + 1) * 128] = o[j * nq:(j + 1) * nq].astype(bf16)


def _sink_col(sink_ref, g, nq):
    row = lax.broadcasted_iota(i32, (3 * nq, 1), 0)
    return jnp.where(row < nq, sink_ref[3 * g], jnp.where(row < 2 * nq, sink_ref[3 * g + 1], sink_ref[3 * g + 2]))


def _swa_kernel(sink_ref, q_ref, kp_ref, ko_ref, kn_ref, vp_ref, vo_ref, vn_ref, kc_ref, vc_ref, o_ref):
    i = pl.program_id(1)
    nb = pl.num_programs(1)
    blk = q_ref.shape[0]
    q = q_ref[...]
    r = lax.broadcasted_iota(i32, (3 * blk, 3 * blk), 0) & (blk - 1)
    c = lax.broadcasted_iota(i32, (3 * blk, 3 * blk), 1)
    lo = jnp.where(i == 0, blk, 0)
    hi = jnp.where(i == nb - 1, 2 * blk, 3 * blk)
    d = c - r
    valid = (d >= 0) & (d <= 2 * SWA_WINDOW) & (c >= lo) & (c < hi)
    scores = []
    for g in range(SWA_KV_HEADS):
        hs = slice(g * 128, (g + 1) * 128)
        qg = jnp.concatenate([q[:, (3 * g + j) * 128:(3 * g + j + 1) * 128] for j in range(3)], axis=0)
        kloc = jnp.concatenate([kp_ref[:, hs], ko_ref[:, hs], kn_ref[:, hs]], axis=0)
        scores.append((_dot_nt(qg, kc_ref[:, hs]), jnp.where(valid, _dot_nt(qg, kloc), NEG_INF)))
    for g in range(SWA_KV_HEADS):
        hs = slice(g * 128, (g + 1) * 128)
        vloc = jnp.concatenate([vp_ref[:, hs], vo_ref[:, hs], vn_ref[:, hs]], axis=0)
        s_ctx, s_loc = scores[g]
        _swa_softmax_out([(s_ctx, vc_ref[:, hs]), (s_loc, vloc)], _sink_col(sink_ref, g, blk), o_ref, g)


def _swa_ctx_kernel(sink_ref, q_ref, kc_ref, vc_ref, prev_ref, o_ref):
    del prev_ref
    q = q_ref[...]
    nq = q.shape[0]
    for g in range(SWA_KV_HEADS):
        hs = slice(g * 128, (g + 1) * 128)
        qg = jnp.concatenate([q[:, (3 * g + j) * 128:(3 * g + j + 1) * 128] for j in range(3)], axis=0)
        s_ctx = _dot_nt(qg, kc_ref[:, hs])
        _swa_softmax_out([(s_ctx, vc_ref[:, hs])], _sink_col(sink_ref, g, nq), o_ref, g)


def _swa_attention(p, sink, n_batch, lat_len, ctx_len, compute_ctx):
    t = p.shape[0]
    blk = SWA_WINDOW
    nb = lat_len // blk
    cbase = n_batch * lat_len // ctx_len
    smem = pl.BlockSpec(memory_space=pltpu.SMEM)

    def kspec(col, off):
        return pl.BlockSpec((blk, 256), lambda b, i: (b * nb + jnp.clip(i + off, 0, nb - 1), col // 256))

    out = pl.pallas_call(
        _swa_kernel,
        grid=(n_batch, nb),
        in_specs=[smem,
                  pl.BlockSpec((blk, 768), lambda b, i: (b * nb + i, _SQ // 768)),
                  kspec(_SK, -1), kspec(_SK, 0), kspec(_SK, 1),
                  kspec(_SV, -1), kspec(_SV, 0), kspec(_SV, 1),
                  pl.BlockSpec((ctx_len, 256), lambda b, i: (cbase + b, _SK // 256)),
                  pl.BlockSpec((ctx_len, 256), lambda b, i: (cbase + b, _SV // 256))],
        out_specs=pl.BlockSpec((blk, 768), lambda b, i: (b * nb + i, 0)),
        out_shape=jax.ShapeDtypeStruct((t, 768), bf16),
        compiler_params=_cparams("parallel", "parallel"),
        name="swa_attn",
    )(sink, p, p, p, p, p, p, p, p, p)
    if compute_ctx:
        out = pl.pallas_call(
            _swa_ctx_kernel,
            grid=(n_batch,),
            in_specs=[smem,
                      pl.BlockSpec((ctx_len, 768), lambda b: (cbase + b, _SQ // 768)),
                      pl.BlockSpec((ctx_len, 256), lambda b: (cbase + b, _SK // 256)),
                      pl.BlockSpec((ctx_len, 256), lambda b: (cbase + b, _SV // 256)),
                      pl.BlockSpec(memory_space=pl.ANY)],
            out_specs=pl.BlockSpec((ctx_len, 768), lambda b: (cbase + b, 0)),
            out_shape=jax.ShapeDtypeStruct((t, 768), bf16),
            input_output_aliases={4: 0},
            compiler_params=_cparams("parallel"),
            name="swa_ctx",
        )(sink, p, p, p, out)
    return out


def _ret_kernel(lg_ref, qf_ref, kf_ref, vf_ref, qb_ref, kb_ref, vb_ref, of_ref, ob_ref,
                s_scr, d_scr, qd_scr, kd_scr, gc_scr):
    i = pl.program_id(1)
    cc = RET_CHUNK

    @pl.when(i == 0)
    def _():
        s_scr[...] = jnp.zeros(s_scr.shape, f32)
        r = lax.broadcasted_iota(i32, (cc, cc), 0).astype(f32)
        c = lax.broadcasted_iota(i32, (cc, cc), 1).astype(f32)
        for direction in range(2):
            backward = direction == 1
            diff = (c - r) if backward else (r - c)
            for h in range(RET_HEADS):
                lgh = lg_ref[direction, h]
                j = direction * RET_HEADS + h
                d_scr[j] = jnp.where(diff >= 0, jnp.exp(lgh * jnp.maximum(diff, 0.0)), 0.0)
                qd_scr[j] = jnp.exp(lgh * ((cc - r) if backward else (r + 1.0)))
                kd_scr[j] = jnp.exp(lgh * (c if backward else (cc - 1.0 - c)))
                gc_scr[j] = jnp.exp(lgh * cc + jnp.zeros((cc, cc), f32))

    lane = lax.broadcasted_iota(i32, (cc, LANES), 1)
    cps = qf_ref.shape[0] // cc
    zero = jnp.zeros((cc, LANES), bf16)
    streams = ((qf_ref, kf_ref, vf_ref, of_ref), (qb_ref, kb_ref, vb_ref, ob_ref))
    for step in range(cps):
        work = []
        for pair in range(RET_HEADS // 2):
            for direction, (q_ref, k_ref, v_ref, o_ref) in enumerate(streams):
                chunk = step if direction == 0 else cps - 1 - step
                rows = slice(chunk * cc, (chunk + 1) * cc)
                q2 = q_ref[rows, pair * 128:(pair + 1) * 128].astype(f32)
                k2b = k_ref[rows, pair * 128:(pair + 1) * 128]
                k2t = jnp.transpose(k2b.astype(f32))
                ja, jb = direction * RET_HEADS + 2 * pair, direction * RET_HEADS + 2 * pair + 1
                va = v_ref[rows, 2 * pair * 128:(2 * pair + 1) * 128]
                vb = v_ref[rows, (2 * pair + 1) * 128:(2 * pair + 2) * 128]
                qa = jnp.where(lane < 64, q2, 0.0)
                qb = jnp.where(lane >= 64, q2, 0.0)
                s2 = _dot_nt(jnp.concatenate([qa, qb], axis=0).astype(bf16), k2b)
                upd = _dot(jnp.concatenate([k2t * kd_scr[ja], k2t * kd_scr[jb]], axis=0).astype(bf16),
                           jnp.concatenate([va, vb], axis=1))
                work.append((pair, o_ref, rows, ja, jb, va, vb, qa, qb, s2, upd))
        for pair, o_ref, rows, ja, jb, va, vb, qa, qb, s2, upd in work:
            lhs = jnp.concatenate([s2[0:cc] * d_scr[ja], s2[cc:2 * cc] * d_scr[jb],
                                   qa * qd_scr[ja], qb * qd_scr[jb]], axis=1).astype(bf16)
            rhs = jnp.concatenate([jnp.concatenate([va, zero], axis=1), jnp.concatenate([zero, vb], axis=1),
                                   jnp.concatenate([s_scr[ja].astype(bf16), zero], axis=1),
                                   jnp.concatenate([zero, s_scr[jb].astype(bf16)], axis=1)], axis=0)
            y2 = _dot(lhs, rhs)
            s_scr[ja] = gc_scr[ja] * s_scr[ja] + upd[0:LANES, 0:LANES]
            s_scr[jb] = gc_scr[jb] * s_scr[jb] + upd[LANES:2 * LANES, LANES:2 * LANES]
            o_ref[rows, 2 * pair * 128:(2 * pair + 1) * 128] = _ln(y2[:, 0:LANES]).astype(bf16)
            o_ref[rows, (2 * pair + 1) * 128:(2 * pair + 2) * 128] = _ln(y2[:, LANES:2 * LANES]).astype(bf16)


def _retention(p, lg, n_batch, lat_len, ctx_len):
    t = p.shape[0]
    cc = RET_CHUNK
    cps = 2 if (ctx_len // cc) % 2 == 0 and (lat_len // cc) % 2 == 0 else 1
    blk = cps * cc
    ncc, nlc = ctx_len // blk, lat_len // blk
    cbase = n_batch * nlc
    smem = pl.BlockSpec(memory_space=pltpu.SMEM)
    scratch = [pltpu.VMEM((2 * RET_HEADS, cc, cc), f32) for _ in range(5)]

    def fwd_row(b, i):
        return jnp.where(i < ncc, cbase + b * ncc + i, b * nlc + (i - ncc))

    def bwd_row(b, i):
        return jnp.where(i < ncc, cbase + b * ncc + (ncc - 1 - i), b * nlc + (nlc - 1 - (i - ncc)))

    def specs(row):
        return [pl.BlockSpec((blk, 384), lambda b, i: (row(b, i), _RQ // 384)),
                pl.BlockSpec((blk, 384), lambda b, i: (row(b, i), _RK // 384)),
                pl.BlockSpec((blk, 768), lambda b, i: (row(b, i), _RV // 768))]

    return pl.pallas_call(
        _ret_kernel,
        grid=(n_batch, ncc + nlc),
        in_specs=[smem] + specs(fwd_row) + specs(bwd_row),
        out_specs=[pl.BlockSpec((blk, 768), lambda b, i: (fwd_row(b, i), 0)),
                   pl.BlockSpec((blk, 768), lambda b, i: (bwd_row(b, i), 0))],
        out_shape=[jax.ShapeDtypeStruct((t, 768), bf16), jax.ShapeDtypeStruct((t, 768), bf16)],
        scratch_shapes=scratch,
        compiler_params=_cparams("parallel", "arbitrary"),
        name="retention",
    )(lg, p, p, p, p, p, p)


def _stream_specs(x_lat, x_ctx, tm, n_lat):
    d = x_lat.shape[1]
    nlb = n_lat // tm
    if x_ctx is x_lat:
        return [pl.BlockSpec((tm, d), lambda i: (i, 0)), pl.BlockSpec((8, d), lambda i: (0, 0))]
    return [pl.BlockSpec((tm, d), lambda i: (jnp.minimum(i, nlb - 1), 0)),
            pl.BlockSpec((tm, d), lambda i: (jnp.maximum(i - nlb, 0), 0))]


def _stream_block(xl_ref, xc_ref, nlb, rows=slice(None)):
    if nlb is None:
        return xl_ref[rows, :]
    return jnp.where(pl.program_id(0) < nlb, xl_ref[rows, :], xc_ref[rows, :])


_ROUTE_ROWS = 256


def _outproj_kernel(a_ref, yf_ref, yb_ref, gf_ref, gb_ref, m_ref, xl_ref, xc_ref, mod_ref, w_ref, g_ref, b_ref,
                    rw_ref, rb_ref, xo_ref, h_ref, rout_ref, routt_ref, cnt_ref, carry_scr, *, alpha, nlb, sub):
    @pl.when(pl.program_id(0) == 0)
    def _():
        carry_scr[...] = jnp.zeros(carry_scr.shape, f32)

    nsub = a_ref.shape[0] // sub
    ys = []
    for sb in range(nsub):
        rows = slice(sb * sub, (sb + 1) * sub)
        ret = (_silu(gf_ref[rows, :].astype(f32)) * yf_ref[rows, :].astype(f32)
               + _silu(gb_ref[rows, :].astype(f32)) * yb_ref[rows, :].astype(f32)).astype(bf16)
        ys.append(_dot(jnp.concatenate([a_ref[rows, :], ret, m_ref[rows, :]], axis=1), w_ref[...]))
    for sb in range(nsub):
        rows = slice(sb * sub, (sb + 1) * sub)
        xn = (_ln(alpha * _stream_block(xl_ref, xc_ref, nlb, rows) + mod_ref[0, 2:3, :] * ys[sb]) * g_ref[...]
              + b_ref[...])
        xo_ref[rows, :] = xn
        h = _ln(xn) * (1.0 + mod_ref[0, 4:5, :]) + mod_ref[0, 3:4, :]
        h_ref[rows, :] = _pack_rows(h)
        table = _route(_dot(h.astype(bf16), rw_ref[...]) + rb_ref[...], carry_scr)
        rout_ref[rows, :] = table
        routt_ref[sb * 8:(sb + 1) * 8, :] = jnp.transpose(table)[0:8, :]
    cnt_ref[...] = carry_scr[...]


def _outproj(a, yf, yb, p, m, x_lat, x_ctx, mod, w, g, b, rw, rb, n_rows, n_lat, lat_len, n_batch, alpha):
    d = x_lat.shape[1]
    tm = _tile(lat_len, 2 * _ROUTE_ROWS)
    while (n_rows - n_lat) % tm:
        tm //= 2
    sub = min(tm, _ROUTE_ROWS)
    nlb, bpb = n_lat // tm, lat_len // tm

    def grp(i):
        return jnp.where(i < nlb, i // bpb, n_batch)

    row = lambda i: (i, 0)
    const = lambda i: (0, 0)
    return pl.pallas_call(
        functools.partial(_outproj_kernel, alpha=alpha, nlb=None if x_ctx is x_lat else nlb, sub=sub),
        grid=(n_rows // tm,),
        in_specs=[pl.BlockSpec((tm, 768), row), pl.BlockSpec((tm, 768), row), pl.BlockSpec((tm, 768), row),
                  pl.BlockSpec((tm, 768), lambda i: (i, _GF // 768)), pl.BlockSpec((tm, 768), lambda i: (i, _GB // 768)),
                  pl.BlockSpec((tm, 512), row)] + _stream_specs(x_lat, x_ctx, tm, n_lat) + [
                  pl.BlockSpec((1, 6, d), lambda i: (grp(i), 0, 0)),
                  pl.BlockSpec(w.shape, const, pipeline_mode=pl.Buffered(1)),
                  pl.BlockSpec((1, d), const), pl.BlockSpec((1, d), const),
                  pl.BlockSpec((d, LANES), const), pl.BlockSpec((1, LANES), const)],
        out_specs=[pl.BlockSpec((tm, d), row), pl.BlockSpec((tm, d // 2), row),
                   pl.BlockSpec((tm, LANES), row), pl.BlockSpec((tm // sub * 8, sub), row),
                   pl.BlockSpec((8, LANES), const)],
        out_shape=[jax.ShapeDtypeStruct((n_rows, d), f32), jax.ShapeDtypeStruct((n_rows, d // 2), jnp.uint32),
                   jax.ShapeDtypeStruct((n_rows, LANES), f32), jax.ShapeDtypeStruct((n_rows // sub * 8, sub), f32),
                   jax.ShapeDtypeStruct((8, LANES), f32)],
        scratch_shapes=[pltpu.VMEM((8, LANES), f32)],
        compiler_params=pltpu.CompilerParams(dimension_semantics=("arbitrary",), vmem_limit_bytes=_INPROJ_VMEM_LIMIT),
        name="outproj",
    )(a, yf, yb, p, p, m, x_lat, x_ctx, mod, w, g, b, rw, rb)


_ROUTE_LANE0 = N_GROUPS


def _route(logits, carry_scr):
    tm = logits.shape[0]
    lane = lax.broadcasted_iota(i32, (tm, LANES), 1)
    lane_f = lane.astype(f32)
    big = float(2 * LANES)
    gl = jnp.where(lane < N_GROUPS, logits, -jnp.inf)
    gmax = jnp.max(gl, axis=1, keepdims=True)
    gidx = jnp.min(jnp.where(gl == gmax, lane_f, big), axis=1, keepdims=True)
    p_group = 1.0 / jnp.sum(jnp.exp(gl - gmax), axis=1, keepdims=True)
    egroup = ((lane - _ROUTE_LANE0) >> 3).astype(f32)
    in_group = (lane >= _ROUTE_LANE0) & (lane < _ROUTE_LANE0 + N_EXPERTS) & (egroup == gidx)
    ev = jnp.where(in_group, logits, -jnp.inf)
    e1 = jnp.max(ev, axis=1, keepdims=True)
    i1 = jnp.min(jnp.where(ev == e1, lane_f, big), axis=1, keepdims=True)
    ev2 = jnp.where(lane_f == i1, -jnp.inf, ev)
    e2 = jnp.max(ev2, axis=1, keepdims=True)
    i2 = jnp.min(jnp.where(ev2 == e2, lane_f, big), axis=1, keepdims=True)
    tt = jnp.exp(e2 - e1)
    w1 = p_group / (1.0 + tt)
    w2 = p_group * tt / (1.0 + tt)

    hit1 = lane_f == i1
    hit2 = lane_f == i2
    onehot = jnp.where(hit1, 1.0, jnp.where(hit2, 1.0, 0.0))
    rr = lax.broadcasted_iota(i32, (tm, tm), 0)
    cc = lax.broadcasted_iota(i32, (tm, tm), 1)
    lower = jnp.where(cc < rr, 1.0, 0.0).astype(bf16)
    before = _dot(lower, onehot.astype(bf16)) + carry_scr[0:1, :]
    rank1 = jnp.sum(jnp.where(hit1, before, 0.0), axis=1, keepdims=True)
    rank2 = jnp.sum(jnp.where(hit2, before, 0.0), axis=1, keepdims=True)
    carry_scr[0:1, :] = carry_scr[0:1, :] + jnp.sum(onehot, axis=0, keepdims=True)

    return jnp.where(lane == 0, i1 - _ROUTE_LANE0, jnp.where(lane == 1, i2 - _ROUTE_LANE0, jnp.where(
        lane == 2, rank1, jnp.where(lane == 3, rank2, jnp.where(lane == 4, w1, jnp.where(lane == 5, w2, 0.0))))))


def _dispatch_kernel(cnt_ref, pstart_ref, padded_ref, dest_ref, h_ref, o_hbm, sem):
    i = pl.program_id(0)
    nb = pl.num_programs(0) - 1
    tm = dest_ref.shape[2] // TOP_K

    def row_copy(src_row, dst_row):
        return pltpu.make_async_copy(h_ref.at[pl.ds(src_row, 1), :], o_hbm.at[dst_row], sem)

    @pl.when(i < nb)
    def _():
        def issue(r8, carry):
            base = pl.multiple_of(r8 * 8, 8)
            for j in range(8):
                for k in range(TOP_K):
                    row_copy(base + j, dest_ref[0, 0, k * tm + base + j]).start()
            return carry

        lax.fori_loop(0, tm // 8, issue, 0)
        for _ in range(TOP_K):
            pltpu.make_async_copy(h_ref, o_hbm.at[pl.ds(0, tm), 0], sem).wait()

    @pl.when(i == nb)
    def _():
        def per_expert(e, total):
            lo, hi = cnt_ref[e], padded_ref[e]

            def fill(r, carry):
                row_copy(0, pstart_ref[e] + r).start()
                return carry

            lax.fori_loop(lo, hi, fill, 0)
            return total + (hi - lo)

        total = lax.fori_loop(0, N_EXPERTS, per_expert, 0)

        def drain(r, carry):
            row_copy(0, 0).wait()
            return carry

        lax.fori_loop(0, total, drain, 0)


def _dispatch(h, dest_rows, counts, pstart, padded, n_rows, tm):
    n, d = h.shape
    nb = n // tm
    grid_spec = pltpu.PrefetchScalarGridSpec(
        num_scalar_prefetch=3,
        grid=(nb + 1,),
        in_specs=[pl.BlockSpec((1, 1, TOP_K * tm), lambda i, *_: (jnp.minimum(i, nb - 1), 0, 0),
                               memory_space=pltpu.SMEM),
                  pl.BlockSpec((tm, d), lambda i, *_: (jnp.minimum(i, nb - 1), 0))],
        out_specs=pl.BlockSpec(memory_space=pl.ANY),
        scratch_shapes=[pltpu.SemaphoreType.DMA(())],
    )
    return pl.pallas_call(
        _dispatch_kernel,
        grid_spec=grid_spec,
        out_shape=jax.ShapeDtypeStruct((n_rows, 1, d), h.dtype),
        compiler_params=_cparams("arbitrary"),
        name="moe_dispatch",
    )(counts, pstart, padded, dest_rows.reshape(nb, 1, TOP_K * tm), h)


_CAST_ROWS = 256


def _expert_kernel(nused_ref, be_ref, first_ref, nxt_ref, half_ref, x_hbm, wg_hbm, wu_hbm, wd_hbm, y_hbm,
                   wg_f, wu_f, wd_f, wg_b, wu_b, wd_b, xbuf, ybuf, sems, xsems, ysems, *, layer):
    i = pl.program_id(0)
    n_used = nused_ref[0]
    used = i < n_used
    rb = xbuf.shape[1]
    slot = i % 2
    staged = ((wg_hbm, wg_f, wg_b), (wu_hbm, wu_f, wu_b), (wd_hbm, wd_f, wd_b))

    def x_fetch(blk, s):
        return pltpu.make_async_copy(x_hbm.at[pl.ds(pl.multiple_of(blk * rb, rb), rb), 0], xbuf.at[s], xsems.at[s])

    def y_store(blk, s):
        return pltpu.make_async_copy(ybuf.at[s], y_hbm.at[pl.ds(pl.multiple_of(blk * rb, rb), rb), 0], ysems.at[s])

    @pl.when(i == 0)
    def _():
        x_fetch(0, 0).start()

    @pl.when(i + 1 < n_used)
    def _():
        x_fetch(i + 1, 1 - slot).start()

    def fetch(e):
        return [pltpu.make_async_copy(hbm.at[layer, e], stage, sems.at[j]) for j, (hbm, stage, _) in enumerate(staged)]

    @pl.when(i == 0)
    def _():
        for cp in fetch(be_ref[0]):
            cp.start()

    @pl.when(used & (first_ref[i] == 1))
    def _():
        for cp in fetch(be_ref[i]):
            cp.wait()
        for _, stage, dst in staged:
            rows = stage.shape[0]
            step = min(_CAST_ROWS, rows)

            def cast(c, carry, stage=stage, dst=dst, step=step):
                sl = pl.ds(pl.multiple_of(c * step, step), step)
                dst[sl, :] = stage[sl, :].astype(bf16)
                return carry

            lax.fori_loop(0, rows // step, cast, 0)
        nxt = nxt_ref[be_ref[i]]

        @pl.when(nxt >= 0)
        def _():
            for cp in fetch(nxt):
                cp.start()

    @pl.when(used)
    def _():
        x_fetch(i, slot).wait()

    @pl.when(used & (i >= 2))
    def _():
        y_store(i - 2, slot).wait()

    def ffn(rows):
        x = _unpack_rows(xbuf[slot, 0:rows, :]).astype(bf16)
        act = (_silu(_dot(x, wg_b[...])) * _dot(x, wu_b[...])).astype(bf16)
        ybuf[slot, 0:rows, :] = _pack_rows(_dot(act, wd_b[...]))
        if rows < rb:
            ybuf[slot, rows:rb, :] = jnp.zeros((rb - rows, ybuf.shape[2]), ybuf.dtype)

    @pl.when(used & (half_ref[i] == 0))
    def _():
        ffn(rb)

    @pl.when(used & (half_ref[i] == 1))
    def _():
        ffn(rb // 2)

    @pl.when(used)
    def _():
        y_store(i, slot).start()

    @pl.when(i == n_used - 1)
    def _():
        y_store(i, slot).wait()

        @pl.when(i >= 1)
        def _():
            y_store(i - 1, 1 - slot).wait()


_EXPERT_VMEM_LIMIT = 60 * 1024 * 1024


def _experts(xs3, n_used, block_expert, first, nxt, half, wg, wu, wd, layer, rb):
    d, hid = wg.shape[2], wg.shape[3]
    n_blocks = xs3.shape[0] // rb

    anyspace = pl.BlockSpec(memory_space=pl.ANY)
    grid_spec = pltpu.PrefetchScalarGridSpec(
        num_scalar_prefetch=5,
        grid=(n_blocks,),
        in_specs=[anyspace, anyspace, anyspace, anyspace],
        out_specs=anyspace,
        scratch_shapes=[pltpu.VMEM((d, hid), f32), pltpu.VMEM((d, hid), f32), pltpu.VMEM((hid, d), f32),
                        pltpu.VMEM((d, hid), bf16), pltpu.VMEM((d, hid), bf16), pltpu.VMEM((hid, d), bf16),
                        pltpu.VMEM((2, rb, d // 2), jnp.uint32), pltpu.VMEM((2, rb, d // 2), jnp.uint32),
                        pltpu.SemaphoreType.DMA((3,)), pltpu.SemaphoreType.DMA((2,)), pltpu.SemaphoreType.DMA((2,))],
    )
    return pl.pallas_call(
        functools.partial(_expert_kernel, layer=layer),
        grid_spec=grid_spec,
        out_shape=jax.ShapeDtypeStruct(xs3.shape, xs3.dtype),
        compiler_params=pltpu.CompilerParams(dimension_semantics=("arbitrary",), vmem_limit_bytes=_EXPERT_VMEM_LIMIT),
        name="moe_experts",
    )(n_used, block_expert, first, nxt, half, xs3, wg, wu, wd)


def _combine_kernel(dcur_ref, dnxt_ref, y_hbm, x_ref, gate_ref, mod_ref, g_ref, b_ref, o_ref, ybuf, sems, *, alpha):
    i = pl.program_id(0)
    nb = pl.num_programs(0)
    tm = x_ref.shape[0]

    def issue(dref, slot):
        def body(r8, carry):
            base = pl.multiple_of(r8 * 8, 8)
            for j in range(8):
                for k in range(TOP_K):
                    pltpu.make_async_copy(y_hbm.at[dref[0, 0, k * tm + base + j]],
                                          ybuf.at[slot, k, pl.ds(base + j, 1), :], sems.at[slot]).start()
            return carry

        lax.fori_loop(0, tm // 8, body, 0)

    @pl.when(i == 0)
    def _():
        issue(dcur_ref, 0)

    def step(slot):
        @pl.when(i + 1 < nb)
        def _():
            issue(dnxt_ref, 1 - slot)

        for k in range(TOP_K):
            pltpu.make_async_copy(y_hbm.at[pl.ds(0, tm), 0], ybuf.at[slot, k], sems.at[slot]).wait()
        gates = gate_ref[...]
        f = gates[:, 4:5] * _unpack_rows(ybuf[slot, 0]) + gates[:, 5:6] * _unpack_rows(ybuf[slot, 1])
        o_ref[...] = _ln(alpha * x_ref[...] + mod_ref[0, 5:6, :] * f) * g_ref[...] + b_ref[...]

    for slot in range(2):
        pl.when(i % 2 == slot)(functools.partial(step, slot))


def _combine(y3, dest_rows, xs, rout, mod, g, b, n_lat, lat_len, n_batch, alpha, tm):
    n, d = xs.shape
    nb = n // tm
    nlb, bpb = n_lat // tm, lat_len // tm

    def grp(i):
        return jnp.where(i < nlb, i // bpb, n_batch)

    dest3 = dest_rows.reshape(nb, 1, TOP_K * tm)
    const = lambda i: (0, 0)
    return pl.pallas_call(
        functools.partial(_combine_kernel, alpha=alpha),
        grid=(nb,),
        in_specs=[pl.BlockSpec((1, 1, TOP_K * tm), lambda i: (i, 0, 0), memory_space=pltpu.SMEM),
                  pl.BlockSpec((1, 1, TOP_K * tm), lambda i: (jnp.minimum(i + 1, nb - 1), 0, 0),
                               memory_space=pltpu.SMEM),
                  pl.BlockSpec(memory_space=pl.ANY),
                  pl.BlockSpec((tm, d), lambda i: (i, 0)),
                  pl.BlockSpec((tm, LANES), lambda i: (i, 0)),
                  pl.BlockSpec((1, 6, d), lambda i: (grp(i), 0, 0)),
                  pl.BlockSpec((1, d), const), pl.BlockSpec((1, d), const)],
        out_specs=pl.BlockSpec((tm, d), lambda i: (i, 0)),
        out_shape=jax.ShapeDtypeStruct((n, d), f32),
        scratch_shapes=[pltpu.VMEM((2, TOP_K, tm, d // 2), jnp.uint32), pltpu.SemaphoreType.DMA((2,))],
        compiler_params=_cparams("arbitrary"),
        name="moe_combine",
    )(dest3, dest3, y3, xs, rout, mod, g, b)


def _moe(h, rout, routt, cnt, xs, wg, wu, wd, layer, mod, g, b, n_lat, lat_len, n_batch, alpha):
    n, d = xs.shape
    tm = routt.shape[1]
    rb = tm
    fields = routt.reshape(n // tm, 8, tm)
    eid = fields[:, 0:TOP_K, :].astype(i32)
    rank = fields[:, TOP_K:2 * TOP_K, :].astype(i32)
    counts = cnt[0, _ROUTE_LANE0:_ROUTE_LANE0 + N_EXPERTS].astype(i32)
    padded = (counts + rb - 1) // rb * rb
    pend = jnp.cumsum(padded)
    pstart = pend - padded
    dest_rows = rank
    for e in range(N_EXPERTS):
        dest_rows = dest_rows + jnp.where(eid == e, pstart[e], 0)
    dest_rows = dest_rows.reshape(-1)
    n_blocks = TOP_K * n // rb + N_EXPERTS
    n_used = (pend[-1:] // rb).astype(i32)
    ex = jnp.arange(N_EXPERTS, dtype=i32)
    row0 = (jnp.arange(n_blocks, dtype=i32) * rb)[:, None]
    owns = (pstart[None, :] <= row0) & (row0 < pend[None, :])
    used = jnp.any(owns, axis=1)
    last_active = jnp.max(jnp.where(counts > 0, ex, 0))
    block_expert = jnp.where(used, jnp.sum(jnp.where(owns, ex[None, :], 0), axis=1), last_active).astype(i32)
    first = jnp.any(owns & (pstart[None, :] == row0), axis=1).astype(i32)
    valid = jnp.sum(jnp.where(owns, jnp.clip((pstart + counts)[None, :] - row0, 0, rb), 0), axis=1)
    half = (used & (valid <= rb // 2)).astype(i32)
    later_active = (ex[None, :] > ex[:, None]) & (counts > 0)[None, :]
    nxt = jnp.min(jnp.where(later_active, ex[None, :], N_EXPERTS), axis=1)
    nxt = jnp.where(nxt == N_EXPERTS, -1, nxt).astype(i32)
    xs3 = _dispatch(h, dest_rows, counts, pstart, padded, n_blocks * rb, tm)
    y3 = _experts(xs3, n_used, block_expert, first, nxt, half, wg, wu, wd, layer, rb)
    return _combine(y3, dest_rows, xs, rout, mod, g, b, n_lat, lat_len, n_batch, alpha, tm)


def _permute_w_in(w):
    d = w.shape[0]
    o_sq, o_sk, o_sv, o_rq, o_rk, o_rv, o_gf, o_gb, o_mq, o_ckv, o_kr = (
        0, 768, 1024, 1280, 1664, 2048, 2816, 3584, 4352, 5120, 5376)
    mq = w[:, o_mq:o_mq + 768].reshape(d, MLA_HEADS, MLA_NOPE_DIM + MLA_ROPE_DIM)
    parts = [w[:, o_sq:o_sq + 768], w[:, o_rv:o_rv + 768], w[:, o_gf:o_gf + 768], w[:, o_gb:o_gb + 768],
             mq[:, :, :MLA_NOPE_DIM].reshape(d, 512), w[:, o_sk:o_sk + 256], w[:, o_sv:o_sv + 256],
             mq[:, :, MLA_NOPE_DIM:].reshape(d, 256), w[:, o_ckv:o_ckv + 256], w[:, o_rq:o_rq + 384],
             w[:, o_rk:o_rk + 384], w[:, o_kr:o_kr + 64], jnp.zeros((d, _NP - _KR - 64), w.dtype)]
    return jnp.concatenate(parts, axis=1).astype(bf16)


def kernel(x, c, ctx, c_ctx, w_ada, b_ada, w_in, swa_sink, ret_decay, mla_kv_norm, mla_w_uk, mla_w_uv, w_out, ln1_g, ln1_b, ln2_g, ln2_b, moe_w_group, moe_b_group, moe_w_expert, moe_b_expert, moe_w_gate, moe_w_up, moe_w_down):
    n_batch, lat_len, d = x.shape
    ctx_len = ctx.shape[1]
    depth = w_ada.shape[0]
    n_lat, n_ctx = n_batch * lat_len, n_batch * ctx_len
    alpha = (2 * depth) ** 0.25

    cc = jnp.zeros((8, d), f32).at[:n_batch].set(c).at[n_batch].set(c_ctx)
    mod_all = _ada(cc, w_ada, b_ada).reshape(depth, 8, 6, d)
    tab = _rope_tables(lat_len, ctx_len, _inproj_rows(lat_len, n_ctx))
    x_lat, x_ctx = x.reshape(n_lat, d), ctx.reshape(n_ctx, d)

    for l in range(depth):
        ctx_out = l < depth - 1
        mod = mod_all[l]
        p = _inproj(x_lat, x_ctx, mod, tab, _permute_w_in(w_in[l]), n_lat, lat_len, ctx_len, n_batch)
        kx, v = _mla_expand(p, mla_kv_norm[l][None, :], mla_w_uk[l].astype(bf16), mla_w_uv[l].astype(bf16))
        m = _mla_attention(p, kx, v, n_batch, lat_len, ctx_len, ctx_out)
        a = _swa_attention(p, swa_sink[l], n_batch, lat_len, ctx_len, ctx_out)
        lg = jnp.log1p(-jnp.exp2(-ret_decay[l].astype(f32)))
        yf, yb = _retention(p, lg, n_batch, lat_len, ctx_len)
        n_rows = n_lat + n_ctx if ctx_out else n_lat
        rout_w = jnp.zeros((d, LANES), f32).at[:, :N_GROUPS].set(moe_w_group[l]).at[
            :, _ROUTE_LANE0:_ROUTE_LANE0 + N_EXPERTS].set(moe_w_expert[l]).astype(bf16)
        rout_b = jnp.zeros((1, LANES), f32).at[0, :N_GROUPS].set(moe_b_group[l]).at[
            0, _ROUTE_LANE0:_ROUTE_LANE0 + N_EXPERTS].set(moe_b_expert[l])
        xs, h, rout, routt, cnt = _outproj(a, yf, yb, p, m, x_lat, x_ctx, mod, w_out[l].astype(bf16), ln1_g[l][None, :],
                                           ln1_b[l][None, :], rout_w, rout_b, n_rows, n_lat, lat_len, n_batch, alpha)
        xs = _moe(h, rout, routt, cnt, xs, moe_w_gate, moe_w_up, moe_w_down, l, mod, ln2_g[l][None, :],
                  ln2_b[l][None, :], n_lat, lat_len, n_batch, alpha)
        x_lat = x_ctx = xs
    return xs[:n_lat].reshape(n_batch, lat_len, d)
```

```python
import functools

import jax
import jax.numpy as jnp
from jax import lax
from jax.experimental import pallas as pl
from jax.experimental.pallas import tpu as pltpu

f32 = jnp.float32
bf16 = jnp.bfloat16
i32 = jnp.int32

GRID_W = 64
SWA_HEADS, SWA_KV_HEADS, SWA_HEAD_DIM, SWA_WINDOW = 6, 2, 128, 128
RET_HEADS, RET_QK_DIM, RET_V_DIM, RET_CHUNK = 6, 64, 128, 128
MLA_HEADS, MLA_NOPE_DIM, MLA_ROPE_DIM, MLA_V_DIM, MLA_KV_RANK = 4, 128, 64, 128, 256
N_GROUPS, EXPERTS_PER_GROUP, TOP_K = 4, 8, 2
N_EXPERTS = N_GROUPS * EXPERTS_PER_GROUP
ROPE_BASE = 10000.0
NORM_EPS = 1e-6
NEG_INF = -1e30
LANES = 128

_SQ, _RV, _GF, _GB, _MQN, _SK, _SV, _MQR, _CKV, _RQ, _RK, _KR = (
    0, 768, 1536, 2304, 3072, 3584, 3840, 4096, 4352, 4608, 4992, 5376)
_NP = 5504
_LOG2E = 1.4426950408889634
_SWA_SCALE = SWA_HEAD_DIM ** -0.5
_RET_SCALE = RET_QK_DIM ** -0.5
_MLA_SCALE = (MLA_NOPE_DIM + MLA_ROPE_DIM) ** -0.5 * _LOG2E
_SEGMENTS = ((_SQ, 768, 0, _SWA_SCALE), (_RV, 768, None, 1.0), (_GF, 768, None, 1.0), (_GB, 768, None, 1.0),
             (_MQN, 512, None, _MLA_SCALE), (_SK, 256, 0, 1.0), (_SV, 256, None, 1.0), (_MQR, 256, 2, _MLA_SCALE),
             (_CKV, 256, None, 1.0), (_RQ, 384, 1, _RET_SCALE), (_RK, 384, 1, 1.0), (_KR, 128, 2, 1.0))

_VMEM_LIMIT = 48 * 1024 * 1024


def _cparams(*sem):
    return pltpu.CompilerParams(dimension_semantics=sem, vmem_limit_bytes=_VMEM_LIMIT)


def _tile(n, pref):
    t = min(n, pref)
    while n % t:
        t //= 2
    return t


def _ln(x):
    mu = jnp.mean(x, axis=-1, keepdims=True)
    xc = x - mu
    var = jnp.mean(xc * xc, axis=-1, keepdims=True)
    return xc * lax.rsqrt(var + NORM_EPS)


def _silu(x):
    return x / (1.0 + jnp.exp(-x))


def _pack_rows(x):
    half = x.shape[1] // 2
    lo = lax.bitcast_convert_type(x[:, :half].astype(bf16).astype(f32), jnp.uint32)
    hi = lax.bitcast_convert_type(x[:, half:].astype(bf16).astype(f32), jnp.uint32)
    return hi | (lo >> 16)


def _unpack_rows(u):
    lo = lax.bitcast_convert_type(u << 16, f32)
    hi = lax.bitcast_convert_type(u & jnp.uint32(0xFFFF0000), f32)
    return jnp.concatenate([lo, hi], axis=1)


def _dot(a, b):
    return jnp.dot(a, b, preferred_element_type=f32)


def _dot_nt(a, b):
    return lax.dot_general(a, b, (((1,), (1,)), ((), ())), preferred_element_type=f32)


def _dot_tn(a, b):
    return lax.dot_general(a, b, (((0,), (0,)), ((), ())), preferred_element_type=f32)


def _ada_kernel(c_ref, w_ref, b_ref, o_ref):
    s = _silu(c_ref[...]).astype(bf16)
    o_ref[0] = _dot(s, w_ref[0].astype(bf16)) + b_ref[0]


def _ada(cc, w_ada, b_ada):
    depth, d, n = w_ada.shape
    tn = _tile(n, 2048)
    return pl.pallas_call(
        _ada_kernel,
        grid=(depth, n // tn),
        in_specs=[pl.BlockSpec((8, d), lambda l, j: (0, 0)),
                  pl.BlockSpec((1, d, tn), lambda l, j: (l, 0, j)),
                  pl.BlockSpec((1, 1, tn), lambda l, j: (l, 0, j))],
        out_specs=pl.BlockSpec((1, 8, tn), lambda l, j: (l, 0, j)),
        out_shape=jax.ShapeDtypeStruct((depth, 8, n), f32),
        compiler_params=_cparams("parallel", "parallel"),
        name="ada",
    )(cc, w_ada, b_ada.reshape(depth, 1, n))


_INPROJ_CHUNK = 512


def _inproj_kernel(xl_ref, xc_ref, mod_ref, tab_ref, w_ref, o_ref, *, nlb):
    tm = xl_ref.shape[0]
    h = (_ln(_stream_block(xl_ref, xc_ref, nlb)) * (1.0 + mod_ref[0, 1:2, :]) + mod_ref[0, 0:1, :]).astype(bf16)
    lane = lax.broadcasted_iota(i32, (tm, LANES), 1)
    first = {32: (lane & 63) < 32, 16: (lane & 31) < 16}
    slab_kind = {}
    for off, width, typ, scale in _SEGMENTS:
        for k in range(off // LANES, (off + width) // LANES):
            slab_kind[k] = (typ, scale)
    for c0 in range(0, _NP, _INPROJ_CHUNK):
        c1 = min(c0 + _INPROJ_CHUNK, _NP)
        acc = _dot(h, w_ref[:, c0:c1])
        for k in range(c0 // LANES, c1 // LANES):
            typ, scale = slab_kind[k]
            xk = acc[:, k * LANES - c0:(k + 1) * LANES - c0]
            if scale != 1.0:
                xk = xk * scale
            if typ is not None:
                half = 16 if typ == 2 else 32
                partner = jnp.where(first[half], pltpu.roll(xk, LANES - half, 1), pltpu.roll(xk, half, 1))
                xk = xk * tab_ref[typ, 0] + partner * tab_ref[typ, 1]
            o_ref[:, k * LANES:(k + 1) * LANES] = xk.astype(bf16)


_INPROJ_VMEM_LIMIT = 60 * 1024 * 1024


def _inproj_rows(lat_len, n_ctx):
    tm = _tile(lat_len, 512)
    while n_ctx % tm:
        tm //= 2
    return tm


def _inproj(x_lat, x_ctx, mod, tab, w, n_lat, lat_len, ctx_len, n_batch):
    d = x_lat.shape[1]
    t = n_lat + n_batch * ctx_len
    tm = _inproj_rows(lat_len, t - n_lat)
    nlb, bpb = n_lat // tm, lat_len // tm
    ctx_blocks = max(1, ctx_len // tm)

    def grp(i):
        return jnp.where(i < nlb, i // bpb, n_batch)

    def posblk(i):
        return jnp.where(i < nlb, i % bpb, bpb + (i - nlb) % ctx_blocks)

    return pl.pallas_call(
        functools.partial(_inproj_kernel, nlb=None if x_ctx is x_lat else nlb),
        grid=(t // tm,),
        in_specs=_stream_specs(x_lat, x_ctx, tm, n_lat) + [
                  pl.BlockSpec((1, 6, d), lambda i: (grp(i), 0, 0)),
                  pl.BlockSpec((3, 2, tm, LANES), lambda i: (0, 0, posblk(i), 0)),
                  pl.BlockSpec((d, _NP), lambda i: (0, 0), pipeline_mode=pl.Buffered(1))],
        out_specs=pl.BlockSpec((tm, _NP), lambda i: (i, 0)),
        out_shape=jax.ShapeDtypeStruct((t, _NP), bf16),
        compiler_params=pltpu.CompilerParams(dimension_semantics=("parallel",), vmem_limit_bytes=_INPROJ_VMEM_LIMIT),
        name="inproj",
    )(x_lat, x_ctx, mod, tab, w)


def _rope_tables(lat_len, ctx_len, tm):
    lane = jnp.arange(LANES)
    t = jnp.arange(lat_len)
    rows = (t // GRID_W).astype(f32)
    cols = (t % GRID_W).astype(f32)

    def freq(half):
        return ROPE_BASE ** (-jnp.arange(half, dtype=f32) / half)

    def table(pos, inv, sign):
        ang = pos * inv[None, :]
        return jnp.stack([jnp.cos(ang), jnp.sin(ang) * sign[None, :]])

    def ident(n):
        return jnp.stack([jnp.ones((n, LANES), f32), jnp.zeros((n, LANES), f32)])

    sign32 = jnp.where((lane % 64) < 32, -1.0, 1.0).astype(f32)
    sign16 = jnp.where((lane % 32) < 16, -1.0, 1.0).astype(f32)
    inv32 = freq(32)[lane % 32]
    inv16 = freq(16)[lane % 16]
    pos_swa = jnp.where((lane // 64)[None, :] == 0, rows[:, None], cols[:, None])
    pos_mla = jnp.where(((lane % 64) // 32)[None, :] == 0, rows[:, None], cols[:, None])
    rep = max(1, tm // ctx_len)
    nc = rep * ctx_len
    pos_ret_l = jnp.broadcast_to((ctx_len + t).astype(f32)[:, None], (lat_len, LANES))
    pos_ret_c = jnp.broadcast_to(jnp.tile(jnp.arange(ctx_len), rep).astype(f32)[:, None], (nc, LANES))
    t0 = jnp.concatenate([table(pos_swa, inv32, sign32), ident(nc)], axis=1)
    t1 = jnp.concatenate([table(pos_ret_l, inv32, sign32), table(pos_ret_c, inv32, sign32)], axis=1)
    t2 = jnp.concatenate([table(pos_mla, inv16, sign16), ident(nc)], axis=1)
    return jnp.stack([t0, t1, t2])


def _mla_expand_kernel(ckv_ref, kr_ref, g_ref, wuk_ref, wuv_ref, k_ref, v_ref):
    c = ckv_ref[...].astype(f32)
    cn = (c * lax.rsqrt(jnp.mean(c * c, axis=-1, keepdims=True) + NORM_EPS) * g_ref[...]).astype(bf16)
    kn = _dot(cn, wuk_ref[...]).astype(bf16)
    vv = _dot(cn, wuv_ref[...]).astype(bf16)
    kr = kr_ref[...]
    for h in range(MLA_HEADS):
        k_ref[h, :, 0:128] = kn[:, h * 128:(h + 1) * 128]
        k_ref[h, :, 128:256] = kr
        v_ref[h] = vv[:, h * 128:(h + 1) * 128]


def _mla_expand(p, g, wuk, wuv):
    t = p.shape[0]
    tm = _tile(t, 1024)
    return pl.pallas_call(
        _mla_expand_kernel,
        grid=(t // tm,),
        in_specs=[pl.BlockSpec((tm, 256), lambda i: (i, _CKV // 256)),
                  pl.BlockSpec((tm, 128), lambda i: (i, _KR // 128)),
                  pl.BlockSpec((1, 256), lambda i: (0, 0)),
                  pl.BlockSpec((256, 512), lambda i: (0, 0)),
                  pl.BlockSpec((256, 512), lambda i: (0, 0))],
        out_specs=[pl.BlockSpec((MLA_HEADS, tm, 256), lambda i: (0, i, 0)),
                   pl.BlockSpec((MLA_HEADS, tm, 128), lambda i: (0, i, 0))],
        out_shape=[jax.ShapeDtypeStruct((MLA_HEADS, t, 256), bf16), jax.ShapeDtypeStruct((MLA_HEADS, t, 128), bf16)],
        compiler_params=_cparams("parallel"),
        name="mla_expand",
    )(p, p, g, wuk, wuv)


def _mla_kernel(*refs, with_lat):
    if with_lat:
        qn_ref, qr_ref, kc_ref, vc_ref, kl_ref, vl_ref, o_ref, qx_scr = refs
    else:
        qn_ref, qr_ref, kc_ref, vc_ref, _, o_ref, qx_scr = refs
    lane = lax.broadcasted_iota(i32, (qn_ref.shape[0], LANES), 1)
    for h in range(MLA_HEADS):
        slab = qr_ref[:, (h // 2) * 128:(h // 2 + 1) * 128].astype(f32)
        if h % 2:
            slab = pltpu.roll(slab, 64, 1)
        qx_scr[h, :, 0:128] = qn_ref[:, h * 128:(h + 1) * 128]
        qx_scr[h, :, 128:256] = jnp.where(lane < 64, slab, 0.0).astype(bf16)
    def scores(h):
        q = qx_scr[h]
        parts = [(_dot_nt(q, kc_ref[h]), vc_ref[h])]
        if with_lat:
            parts.append((_dot_nt(q, kl_ref[h]), vl_ref[h]))
        return parts

    nxt = scores(0)
    for h in range(MLA_HEADS):
        parts, nxt = nxt, (scores(h + 1) if h + 1 < MLA_HEADS else None)
        m = None
        for s, _ in parts:
            mx = jnp.max(s, axis=1, keepdims=True)
            m = mx if m is None else jnp.maximum(m, mx)
        den, acc = None, None
        for s, vv in parts:
            pr = jnp.exp2(s - m)
            sm = jnp.sum(pr, axis=1, keepdims=True)
            pv = _dot(pr.astype(bf16), vv)
            den = sm if den is None else den + sm
            acc = pv if acc is None else acc + pv
        o_ref[:, h * 128:(h + 1) * 128] = (acc / den).astype(bf16)


def _mla_attention(p, kx, v, n_batch, lat_len, ctx_len, compute_ctx):
    t = p.shape[0]
    tq = _tile(lat_len, 256)
    nlq = lat_len // tq
    cbase = n_batch * lat_len // ctx_len
    once = pl.Buffered(1)
    out = pl.pallas_call(
        functools.partial(_mla_kernel, with_lat=True),
        grid=(n_batch, nlq),
        in_specs=[pl.BlockSpec((tq, 512), lambda b, qi: (b * nlq + qi, _MQN // 512)),
                  pl.BlockSpec((tq, 256), lambda b, qi: (b * nlq + qi, _MQR // 256)),
                  pl.BlockSpec((MLA_HEADS, ctx_len, 256), lambda b, qi: (0, cbase + b, 0)),
                  pl.BlockSpec((MLA_HEADS, ctx_len, 128), lambda b, qi: (0, cbase + b, 0)),
                  pl.BlockSpec((MLA_HEADS, lat_len, 256), lambda b, qi: (0, b, 0), pipeline_mode=once),
                  pl.BlockSpec((MLA_HEADS, lat_len, 128), lambda b, qi: (0, b, 0), pipeline_mode=once)],
        out_specs=pl.BlockSpec((tq, 512), lambda b, qi: (b * nlq + qi, 0)),
        out_shape=jax.ShapeDtypeStruct((t, 512), bf16),
        scratch_shapes=[pltpu.VMEM((MLA_HEADS, tq, 256), bf16)],
        compiler_params=_cparams("parallel", "arbitrary"),
        name="mla_attn",
    )(p, p, kx, v, kx, v)
    if compute_ctx:
        out = pl.pallas_call(
            functools.partial(_mla_kernel, with_lat=False),
            grid=(n_batch,),
            in_specs=[pl.BlockSpec((ctx_len, 512), lambda b: (cbase + b, _MQN // 512)),
                      pl.BlockSpec((ctx_len, 256), lambda b: (cbase + b, _MQR // 256)),
                      pl.BlockSpec((MLA_HEADS, ctx_len, 256), lambda b: (0, cbase + b, 0)),
                      pl.BlockSpec((MLA_HEADS, ctx_len, 128), lambda b: (0, cbase + b, 0)),
                      pl.BlockSpec(memory_space=pl.ANY)],
            out_specs=pl.BlockSpec((ctx_len, 512), lambda b: (cbase + b, 0)),
            out_shape=jax.ShapeDtypeStruct((t, 512), bf16),
            scratch_shapes=[pltpu.VMEM((MLA_HEADS, ctx_len, 256), bf16)],
            input_output_aliases={4: 0},
            compiler_params=_cparams("parallel"),
            name="mla_ctx",
        )(p, p, kx, v, out)
    return out


def _swa_softmax_out(parts, sink_col, o_ref, g):
    m = sink_col
    for s, _ in parts:
        m = jnp.maximum(m, jnp.max(s, axis=1, keepdims=True))
    den = jnp.exp(sink_col - m)
    acc = None
    for s, vv in parts:
        pr = jnp.exp(s - m)
        den = den + jnp.sum(pr, axis=1, keepdims=True)
        pv = _dot(pr.astype(bf16), vv)
        acc = pv if acc is None else acc + pv
    o = acc / den
    nq = o.shape[0] // 3
    for j in range(3):
        o_ref[:, (3 * g + j) * 128:(3 * g + j + 1) * 128] = o[j * nq:(j + 1) * nq].astype(bf16)


def _sink_col(sink_ref, g, nq):
    row = lax.broadcasted_iota(i32, (3 * nq, 1), 0)
    return jnp.where(row < nq, sink_ref[3 * g], jnp.where(row < 2 * nq, sink_ref[3 * g + 1], sink_ref[3 * g + 2]))


def _swa_kernel(sink_ref, q_ref, kp_ref, ko_ref, kn_ref, vp_ref, vo_ref, vn_ref, kc_ref, vc_ref, o_ref):
    i = pl.program_id(1)
    nb = pl.num_programs(1)
    blk = q_ref.shape[0]
    q = q_ref[...]
    r = lax.broadcasted_iota(i32, (3 * blk, 3 * blk), 0) & (blk - 1)
    c = lax.broadcasted_iota(i32, (3 * blk, 3 * blk), 1)
    lo = jnp.where(i == 0, blk, 0)
    hi = jnp.where(i == nb - 1, 2 * blk, 3 * blk)
    d = c - r
    valid = (d >= 0) & (d <= 2 * SWA_WINDOW) & (c >= lo) & (c < hi)
    scores = []
    for g in range(SWA_KV_HEADS):
        hs = slice(g * 128, (g + 1) * 128)
        qg = jnp.concatenate([q[:, (3 * g + j) * 128:(3 * g + j + 1) * 128] for j in range(3)], axis=0)
        kloc = jnp.concatenate([kp_ref[:, hs], ko_ref[:, hs], kn_ref[:, hs]], axis=0)
        scores.append((_dot_nt(qg, kc_ref[:, hs]), jnp.where(valid, _dot_nt(qg, kloc), NEG_INF)))
    for g in range(SWA_KV_HEADS):
        hs = slice(g * 128, (g + 1) * 128)
        vloc = jnp.concatenate([vp_ref[:, hs], vo_ref[:, hs], vn_ref[:, hs]], axis=0)
        s_ctx, s_loc = scores[g]
        _swa_softmax_out([(s_ctx, vc_ref[:, hs]), (s_loc, vloc)], _sink_col(sink_ref, g, blk), o_ref, g)


def _swa_ctx_kernel(sink_ref, q_ref, kc_ref, vc_ref, prev_ref, o_ref):
    del prev_ref
    q = q_ref[...]
    nq = q.shape[0]
    for g in range(SWA_KV_HEADS):
        hs = slice(g * 128, (g + 1) * 128)
        qg = jnp.concatenate([q[:, (3 * g + j) * 128:(3 * g + j + 1) * 128] for j in range(3)], axis=0)
        s_ctx = _dot_nt(qg, kc_ref[:, hs])
        _swa_softmax_out([(s_ctx, vc_ref[:, hs])], _sink_col(sink_ref, g, nq), o_ref, g)


def _swa_attention(p, sink, n_batch, lat_len, ctx_len, compute_ctx):
    t = p.shape[0]
    blk = SWA_WINDOW
    nb = lat_len // blk
    cbase = n_batch * lat_len // ctx_len
    smem = pl.BlockSpec(memory_space=pltpu.SMEM)

    def kspec(col, off):
        return pl.BlockSpec((blk, 256), lambda b, i: (b * nb + jnp.clip(i + off, 0, nb - 1), col // 256))

    out = pl.pallas_call(
        _swa_kernel,
        grid=(n_batch, nb),
        in_specs=[smem,
                  pl.BlockSpec((blk, 768), lambda b, i: (b * nb + i, _SQ // 768)),
                  kspec(_SK, -1), kspec(_SK, 0), kspec(_SK, 1),
                  kspec(_SV, -1), kspec(_SV, 0), kspec(_SV, 1),
                  pl.BlockSpec((ctx_len, 256), lambda b, i: (cbase + b, _SK // 256)),
                  pl.BlockSpec((ctx_len, 256), lambda b, i: (cbase + b, _SV // 256))],
        out_specs=pl.BlockSpec((blk, 768), lambda b, i: (b * nb + i, 0)),
        out_shape=jax.ShapeDtypeStruct((t, 768), bf16),
        compiler_params=_cparams("parallel", "parallel"),
        name="swa_attn",
    )(sink, p, p, p, p, p, p, p, p, p)
    if compute_ctx:
        out = pl.pallas_call(
            _swa_ctx_kernel,
            grid=(n_batch,),
            in_specs=[smem,
                      pl.BlockSpec((ctx_len, 768), lambda b: (cbase + b, _SQ // 768)),
                      pl.BlockSpec((ctx_len, 256), lambda b: (cbase + b, _SK // 256)),
                      pl.BlockSpec((ctx_len, 256), lambda b: (cbase + b, _SV // 256)),
                      pl.BlockSpec(memory_space=pl.ANY)],
            out_specs=pl.BlockSpec((ctx_len, 768), lambda b: (cbase + b, 0)),
            out_shape=jax.ShapeDtypeStruct((t, 768), bf16),
            input_output_aliases={4: 0},
            compiler_params=_cparams("parallel"),
            name="swa_ctx",
        )(sink, p, p, p, out)
    return out


def _ret_kernel(lg_ref, qf_ref, kf_ref, vf_ref, qb_ref, kb_ref, vb_ref, of_ref, ob_ref,
                s_scr, d_scr, qd_scr, kd_scr, gc_scr):
    i = pl.program_id(1)
    cc = RET_CHUNK

    @pl.when(i == 0)
    def _():
        s_scr[...] = jnp.zeros(s_scr.shape, f32)
        r = lax.broadcasted_iota(i32, (cc, cc), 0).astype(f32)
        c = lax.broadcasted_iota(i32, (cc, cc), 1).astype(f32)
        for direction in range(2):
            backward = direction == 1
            diff = (c - r) if backward else (r - c)
            for h in range(RET_HEADS):
                lgh = lg_ref[direction, h]
                j = direction * RET_HEADS + h
                d_scr[j] = jnp.where(diff >= 0, jnp.exp(lgh * jnp.maximum(diff, 0.0)), 0.0)
                qd_scr[j] = jnp.exp(lgh * ((cc - r) if backward else (r + 1.0)))
                kd_scr[j] = jnp.exp(lgh * (c if backward else (cc - 1.0 - c)))
                gc_scr[j] = jnp.exp(lgh * cc + jnp.zeros((cc, cc), f32))

    lane = lax.broadcasted_iota(i32, (cc, LANES), 1)
    cps = qf_ref.shape[0] // cc
    zero = jnp.zeros((cc, LANES), bf16)
    streams = ((qf_ref, kf_ref, vf_ref, of_ref), (qb_ref, kb_ref, vb_ref, ob_ref))
    for step in range(cps):
        work = []
        for pair in range(RET_HEADS // 2):
            for direction, (q_ref, k_ref, v_ref, o_ref) in enumerate(streams):
                chunk = step if direction == 0 else cps - 1 - step
                rows = slice(chunk * cc, (chunk + 1) * cc)
                q2 = q_ref[rows, pair * 128:(pair + 1) * 128].astype(f32)
                k2b = k_ref[rows, pair * 128:(pair + 1) * 128]
                k2t = jnp.transpose(k2b.astype(f32))
                ja, jb = direction * RET_HEADS + 2 * pair, direction * RET_HEADS + 2 * pair + 1
                va = v_ref[rows, 2 * pair * 128:(2 * pair + 1) * 128]
                vb = v_ref[rows, (2 * pair + 1) * 128:(2 * pair + 2) * 128]
                qa = jnp.where(lane < 64, q2, 0.0)
                qb = jnp.where(lane >= 64, q2, 0.0)
                s2 = _dot_nt(jnp.concatenate([qa, qb], axis=0).astype(bf16), k2b)
                upd = _dot(jnp.concatenate([k2t * kd_scr[ja], k2t * kd_scr[jb]], axis=0).astype(bf16),
                           jnp.concatenate([va, vb], axis=1))
                work.append((pair, o_ref, rows, ja, jb, va, vb, qa, qb, s2, upd))
        for pair, o_ref, rows, ja, jb, va, vb, qa, qb, s2, upd in work:
            lhs = jnp.concatenate([s2[0:cc] * d_scr[ja], s2[cc:2 * cc] * d_scr[jb],
                                   qa * qd_scr[ja], qb * qd_scr[jb]], axis=1).astype(bf16)
            rhs = jnp.concatenate([jnp.concatenate([va, zero], axis=1), jnp.concatenate([zero, vb], axis=1),
                                   jnp.concatenate([s_scr[ja].astype(bf16), zero], axis=1),
                                   jnp.concatenate([zero, s_scr[jb].astype(bf16)], axis=1)], axis=0)
            y2 = _dot(lhs, rhs)
            s_scr[ja] = gc_scr[ja] * s_scr[ja] + upd[0:LANES, 0:LANES]
            s_scr[jb] = gc_scr[jb] * s_scr[jb] + upd[LANES:2 * LANES, LANES:2 * LANES]
            o_ref[rows, 2 * pair * 128:(2 * pair + 1) * 128] = _ln(y2[:, 0:LANES]).astype(bf16)
            o_ref[rows, (2 * pair + 1) * 128:(2 * pair + 2) * 128] = _ln(y2[:, LANES:2 * LANES]).astype(bf16)


def _retention(p, lg, n_batch, lat_len, ctx_len):
    t = p.shape[0]
    cc = RET_CHUNK
    cps = 2 if (ctx_len // cc) % 2 == 0 and (lat_len // cc) % 2 == 0 else 1
    blk = cps * cc
    ncc, nlc = ctx_len // blk, lat_len // blk
    cbase = n_batch * nlc
    smem = pl.BlockSpec(memory_space=pltpu.SMEM)
    scratch = [pltpu.VMEM((2 * RET_HEADS, cc, cc), f32) for _ in range(5)]

    def fwd_row(b, i):
        return jnp.where(i < ncc, cbase + b * ncc + i, b * nlc + (i - ncc))

    def bwd_row(b, i):
        return jnp.where(i < ncc, cbase + b * ncc + (ncc - 1 - i), b * nlc + (nlc - 1 - (i - ncc)))

    def specs(row):
        return [pl.BlockSpec((blk, 384), lambda b, i: (row(b, i), _RQ // 384)),
                pl.BlockSpec((blk, 384), lambda b, i: (row(b, i), _RK // 384)),
                pl.BlockSpec((blk, 768), lambda b, i: (row(b, i), _RV // 768))]

    return pl.pallas_call(
        _ret_kernel,
        grid=(n_batch, ncc + nlc),
        in_specs=[smem] + specs(fwd_row) + specs(bwd_row),
        out_specs=[pl.BlockSpec((blk, 768), lambda b, i: (fwd_row(b, i), 0)),
                   pl.BlockSpec((blk, 768), lambda b, i: (bwd_row(b, i), 0))],
        out_shape=[jax.ShapeDtypeStruct((t, 768), bf16), jax.ShapeDtypeStruct((t, 768), bf16)],
        scratch_shapes=scratch,
        compiler_params=_cparams("parallel", "arbitrary"),
        name="retention",
    )(lg, p, p, p, p, p, p)


def _stream_specs(x_lat, x_ctx, tm, n_lat):
    d = x_lat.shape[1]
    nlb = n_lat // tm
    if x_ctx is x_lat:
        return [pl.BlockSpec((tm, d), lambda i: (i, 0)), pl.BlockSpec((8, d), lambda i: (0, 0))]
    return [pl.BlockSpec((tm, d), lambda i: (jnp.minimum(i, nlb - 1), 0)),
            pl.BlockSpec((tm, d), lambda i: (jnp.maximum(i - nlb, 0), 0))]


def _stream_block(xl_ref, xc_ref, nlb, rows=slice(None)):
    if nlb is None:
        return xl_ref[rows, :]
    return jnp.where(pl.program_id(0) < nlb, xl_ref[rows, :], xc_ref[rows, :])


_ROUTE_ROWS = 256


def _outproj_kernel(a_ref, yf_ref, yb_ref, gf_ref, gb_ref, m_ref, xl_ref, xc_ref, mod_ref, w_ref, g_ref, b_ref,
                    rw_ref, rb_ref, xo_ref, h_ref, rout_ref, routt_ref, cnt_ref, carry_scr, *, alpha, nlb, sub):
    @pl.when(pl.program_id(0) == 0)
    def _():
        carry_scr[...] = jnp.zeros(carry_scr.shape, f32)

    nsub = a_ref.shape[0] // sub
    ys = []
    for sb in range(nsub):
        rows = slice(sb * sub, (sb + 1) * sub)
        ret = (_silu(gf_ref[rows, :].astype(f32)) * yf_ref[rows, :].astype(f32)
               + _silu(gb_ref[rows, :].astype(f32)) * yb_ref[rows, :].astype(f32)).astype(bf16)
        ys.append(_dot(jnp.concatenate([a_ref[rows, :], ret, m_ref[rows, :]], axis=1), w_ref[...]))
    logits = []
    for sb in range(nsub):
        rows = slice(sb * sub, (sb + 1) * sub)
        xn = (_ln(alpha * _stream_block(xl_ref, xc_ref, nlb, rows) + mod_ref[0, 2:3, :] * ys[sb]) * g_ref[...]
              + b_ref[...])
        xo_ref[rows, :] = xn
        h = _ln(xn) * (1.0 + mod_ref[0, 4:5, :]) + mod_ref[0, 3:4, :]
        h_ref[rows, :] = _pack_rows(h)
        logits.append(_dot(h.astype(bf16), rw_ref[...]) + rb_ref[...])
    for sb in range(nsub):
        rows = slice(sb * sub, (sb + 1) * sub)
        table = _route(logits[sb], carry_scr)
        rout_ref[rows, :] = table
        routt_ref[sb * 8:(sb + 1) * 8, :] = jnp.transpose(table)[0:8, :]
    cnt_ref[...] = carry_scr[...]


def _outproj(a, yf, yb, p, m, x_lat, x_ctx, mod, w, g, b, rw, rb, n_rows, n_lat, lat_len, n_batch, alpha):
    d = x_lat.shape[1]
    tm = _tile(lat_len, 2 * _ROUTE_ROWS)
    while (n_rows - n_lat) % tm:
        tm //= 2
    sub = min(tm, _ROUTE_ROWS)
    nlb, bpb = n_lat // tm, lat_len // tm

    def grp(i):
        return jnp.where(i < nlb, i // bpb, n_batch)

    row = lambda i: (i, 0)
    const = lambda i: (0, 0)
    return pl.pallas_call(
        functools.partial(_outproj_kernel, alpha=alpha, nlb=None if x_ctx is x_lat else nlb, sub=sub),
        grid=(n_rows // tm,),
        in_specs=[pl.BlockSpec((tm, 768), row), pl.BlockSpec((tm, 768), row), pl.BlockSpec((tm, 768), row),
                  pl.BlockSpec((tm, 768), lambda i: (i, _GF // 768)), pl.BlockSpec((tm, 768), lambda i: (i, _GB // 768)),
                  pl.BlockSpec((tm, 512), row)] + _stream_specs(x_lat, x_ctx, tm, n_lat) + [
                  pl.BlockSpec((1, 6, d), lambda i: (grp(i), 0, 0)),
                  pl.BlockSpec(w.shape, const, pipeline_mode=pl.Buffered(1)),
                  pl.BlockSpec((1, d), const), pl.BlockSpec((1, d), const),
                  pl.BlockSpec((d, LANES), const), pl.BlockSpec((1, LANES), const)],
        out_specs=[pl.BlockSpec((tm, d), row), pl.BlockSpec((tm, d // 2), row),
                   pl.BlockSpec((tm, LANES), row), pl.BlockSpec((tm // sub * 8, sub), row),
                   pl.BlockSpec((8, LANES), const)],
        out_shape=[jax.ShapeDtypeStruct((n_rows, d), f32), jax.ShapeDtypeStruct((n_rows, d // 2), jnp.uint32),
                   jax.ShapeDtypeStruct((n_rows, LANES), f32), jax.ShapeDtypeStruct((n_rows // sub * 8, sub), f32),
                   jax.ShapeDtypeStruct((8, LANES), f32)],
        scratch_shapes=[pltpu.VMEM((8, LANES), f32)],
        compiler_params=pltpu.CompilerParams(dimension_semantics=("arbitrary",), vmem_limit_bytes=_INPROJ_VMEM_LIMIT),
        name="outproj",
    )(a, yf, yb, p, p, m, x_lat, x_ctx, mod, w, g, b, rw, rb)


_ROUTE_LANE0 = N_GROUPS


def _route(logits, carry_scr):
    tm = logits.shape[0]
    lane = lax.broadcasted_iota(i32, (tm, LANES), 1)
    lane_f = lane.astype(f32)
    big = float(2 * LANES)
    gl = jnp.where(lane < N_GROUPS, logits, -jnp.inf)
    gmax = jnp.max(gl, axis=1, keepdims=True)
    gidx = jnp.min(jnp.where(gl == gmax, lane_f, big), axis=1, keepdims=True)
    p_group = 1.0 / jnp.sum(jnp.exp(gl - gmax), axis=1, keepdims=True)
    egroup = ((lane - _ROUTE_LANE0) >> 3).astype(f32)
    in_group = (lane >= _ROUTE_LANE0) & (lane < _ROUTE_LANE0 + N_EXPERTS) & (egroup == gidx)
    ev = jnp.where(in_group, logits, -jnp.inf)
    e1 = jnp.max(ev, axis=1, keepdims=True)
    i1 = jnp.min(jnp.where(ev == e1, lane_f, big), axis=1, keepdims=True)
    ev2 = jnp.where(lane_f == i1, -jnp.inf, ev)
    e2 = jnp.max(ev2, axis=1, keepdims=True)
    i2 = jnp.min(jnp.where(ev2 == e2, lane_f, big), axis=1, keepdims=True)
    tt = jnp.exp(e2 - e1)
    w1 = p_group / (1.0 + tt)
    w2 = p_group * tt / (1.0 + tt)

    hit1 = lane_f == i1
    hit2 = lane_f == i2
    onehot = jnp.where(hit1, 1.0, jnp.where(hit2, 1.0, 0.0))
    rr = lax.broadcasted_iota(i32, (tm, tm), 0)
    cc = lax.broadcasted_iota(i32, (tm, tm), 1)
    lower = jnp.where(cc < rr, 1.0, 0.0).astype(bf16)
    before = _dot(lower, onehot.astype(bf16)) + carry_scr[0:1, :]
    rank1 = jnp.sum(jnp.where(hit1, before, 0.0), axis=1, keepdims=True)
    rank2 = jnp.sum(jnp.where(hit2, before, 0.0), axis=1, keepdims=True)
    carry_scr[0:1, :] = carry_scr[0:1, :] + jnp.sum(onehot, axis=0, keepdims=True)

    return jnp.where(lane == 0, i1 - _ROUTE_LANE0, jnp.where(lane == 1, i2 - _ROUTE_LANE0, jnp.where(
        lane == 2, rank1, jnp.where(lane == 3, rank2, jnp.where(lane == 4, w1, jnp.where(lane == 5, w2, 0.0))))))


def _dispatch_kernel(cnt_ref, pstart_ref, padded_ref, dest_ref, h_ref, o_hbm, sem):
    i = pl.program_id(0)
    nb = pl.num_programs(0) - 1
    tm = dest_ref.shape[2] // TOP_K

    def row_copy(src_row, dst_row):
        return pltpu.make_async_copy(h_ref.at[pl.ds(src_row, 1), :], o_hbm.at[dst_row], sem)

    @pl.when(i < nb)
    def _():
        def issue(r8, carry):
            base = pl.multiple_of(r8 * 8, 8)
            for j in range(8):
                for k in range(TOP_K):
                    row_copy(base + j, dest_ref[0, 0, k * tm + base + j]).start()
            return carry

        lax.fori_loop(0, tm // 8, issue, 0)
        for _ in range(TOP_K):
            pltpu.make_async_copy(h_ref, o_hbm.at[pl.ds(0, tm), 0], sem).wait()

    @pl.when(i == nb)
    def _():
        def per_expert(e, total):
            lo, hi = cnt_ref[e], padded_ref[e]

            def fill(r, carry):
                row_copy(0, pstart_ref[e] + r).start()
                return carry

            lax.fori_loop(lo, hi, fill, 0)
            return total + (hi - lo)

        total = lax.fori_loop(0, N_EXPERTS, per_expert, 0)

        def drain(r, carry):
            row_copy(0, 0).wait()
            return carry

        lax.fori_loop(0, total, drain, 0)


def _dispatch(h, dest_rows, counts, pstart, padded, n_rows, tm):
    n, d = h.shape
    nb = n // tm
    grid_spec = pltpu.PrefetchScalarGridSpec(
        num_scalar_prefetch=3,
        grid=(nb + 1,),
        in_specs=[pl.BlockSpec((1, 1, TOP_K * tm), lambda i, *_: (jnp.minimum(i, nb - 1), 0, 0),
                               memory_space=pltpu.SMEM),
                  pl.BlockSpec((tm, d), lambda i, *_: (jnp.minimum(i, nb - 1), 0))],
        out_specs=pl.BlockSpec(memory_space=pl.ANY),
        scratch_shapes=[pltpu.SemaphoreType.DMA(())],
    )
    return pl.pallas_call(
        _dispatch_kernel,
        grid_spec=grid_spec,
        out_shape=jax.ShapeDtypeStruct((n_rows, 1, d), h.dtype),
        compiler_params=_cparams("arbitrary"),
        name="moe_dispatch",
    )(counts, pstart, padded, dest_rows.reshape(nb, 1, TOP_K * tm), h)


_CAST_ROWS = 256
_WEIGHT_FETCH_CHUNKS = 4


def _expert_kernel(nused_ref, be_ref, first_ref, nxt_ref, half_ref, x_hbm, wg_hbm, wu_hbm, wd_hbm, y_hbm,
                   wg_f, wu_f, wd_f, wg_b, wu_b, wd_b, xbuf, ybuf, sems, xsems, ysems, *, layer):
    i = pl.program_id(0)
    n_used = nused_ref[0]
    used = i < n_used
    rb = xbuf.shape[1]
    slot = i % 2
    staged = ((wg_hbm, wg_f, wg_b), (wu_hbm, wu_f, wu_b), (wd_hbm, wd_f, wd_b))

    def x_fetch(blk, s):
        return pltpu.make_async_copy(x_hbm.at[pl.ds(pl.multiple_of(blk * rb, rb), rb), 0], xbuf.at[s], xsems.at[s])

    def y_store(blk, s):
        return pltpu.make_async_copy(ybuf.at[s], y_hbm.at[pl.ds(pl.multiple_of(blk * rb, rb), rb), 0], ysems.at[s])

    @pl.when(i == 0)
    def _():
        x_fetch(0, 0).start()

    @pl.when(i + 1 < n_used)
    def _():
        x_fetch(i + 1, 1 - slot).start()

    def fetch(e):
        copies = []
        for j, (hbm, stage, _) in enumerate(staged):
            rows = stage.shape[0] // _WEIGHT_FETCH_CHUNKS
            for c in range(_WEIGHT_FETCH_CHUNKS):
                sl = pl.ds(c * rows, rows)
                copies.append(pltpu.make_async_copy(hbm.at[layer, e, sl, :], stage.at[sl, :], sems.at[j]))
        return copies

    @pl.when(i == 0)
    def _():
        for cp in fetch(be_ref[0]):
            cp.start()

    @pl.when(used & (first_ref[i] == 1))
    def _():
        for cp in fetch(be_ref[i]):
            cp.wait()
        for _, stage, dst in staged:
            rows = stage.shape[0]
            step = min(_CAST_ROWS, rows)

            def cast(c, carry, stage=stage, dst=dst, step=step):
                sl = pl.ds(pl.multiple_of(c * step, step), step)
                dst[sl, :] = stage[sl, :].astype(bf16)
                return carry

            lax.fori_loop(0, rows // step, cast, 0)
        nxt = nxt_ref[be_ref[i]]

        @pl.when(nxt >= 0)
        def _():
            for cp in fetch(nxt):
                cp.start()

    @pl.when(used)
    def _():
        x_fetch(i, slot).wait()

    @pl.when(used & (i >= 2))
    def _():
        y_store(i - 2, slot).wait()

    def ffn(rows):
        x = _unpack_rows(xbuf[slot, 0:rows, :]).astype(bf16)
        act = (_silu(_dot(x, wg_b[...])) * _dot(x, wu_b[...])).astype(bf16)
        ybuf[slot, 0:rows, :] = _pack_rows(_dot(act, wd_b[...]))
        if rows < rb:
            ybuf[slot, rows:rb, :] = jnp.zeros((rb - rows, ybuf.shape[2]), ybuf.dtype)

    @pl.when(used & (half_ref[i] == 0))
    def _():
        ffn(rb)

    @pl.when(used & (half_ref[i] == 1))
    def _():
        ffn(rb // 2)

    @pl.when(used)
    def _():
        y_store(i, slot).start()

    @pl.when(i == n_used - 1)
    def _():
        y_store(i, slot).wait()

        @pl.when(i >= 1)
        def _():
            y_store(i - 1, 1 - slot).wait()


_EXPERT_VMEM_LIMIT = 60 * 1024 * 1024


def _experts(xs3, n_used, block_expert, first, nxt, half, wg, wu, wd, layer, rb):
    d, hid = wg.shape[2], wg.shape[3]
    n_blocks = xs3.shape[0] // rb

    anyspace = pl.BlockSpec(memory_space=pl.ANY)
    grid_spec = pltpu.PrefetchScalarGridSpec(
        num_scalar_prefetch=5,
        grid=(n_blocks,),
        in_specs=[anyspace, anyspace, anyspace, anyspace],
        out_specs=anyspace,
        scratch_shapes=[pltpu.VMEM((d, hid), f32), pltpu.VMEM((d, hid), f32), pltpu.VMEM((hid, d), f32),
                        pltpu.VMEM((d, hid), bf16), pltpu.VMEM((d, hid), bf16), pltpu.VMEM((hid, d), bf16),
                        pltpu.VMEM((2, rb, d // 2), jnp.uint32), pltpu.VMEM((2, rb, d // 2), jnp.uint32),
                        pltpu.SemaphoreType.DMA((3,)), pltpu.SemaphoreType.DMA((2,)), pltpu.SemaphoreType.DMA((2,))],
    )
    return pl.pallas_call(
        functools.partial(_expert_kernel, layer=layer),
        grid_spec=grid_spec,
        out_shape=jax.ShapeDtypeStruct(xs3.shape, xs3.dtype),
        compiler_params=pltpu.CompilerParams(dimension_semantics=("arbitrary",), vmem_limit_bytes=_EXPERT_VMEM_LIMIT),
        name="moe_experts",
    )(n_used, block_expert, first, nxt, half, xs3, wg, wu, wd)


def _combine_kernel(dcur_ref, dnxt_ref, y_hbm, x_ref, gate_ref, mod_ref, g_ref, b_ref, o_ref, ybuf, sems, *, alpha):
    i = pl.program_id(0)
    nb = pl.num_programs(0)
    tm = x_ref.shape[0]

    def issue(dref, slot):
        def body(r8, carry):
            base = pl.multiple_of(r8 * 8, 8)
            for j in range(8):
                for k in range(TOP_K):
                    pltpu.make_async_copy(y_hbm.at[dref[0, 0, k * tm + base + j]],
                                          ybuf.at[slot, k, pl.ds(base + j, 1), :], sems.at[slot]).start()
            return carry

        lax.fori_loop(0, tm // 8, body, 0)

    @pl.when(i == 0)
    def _():
        issue(dcur_ref, 0)

    def step(slot):
        @pl.when(i + 1 < nb)
        def _():
            issue(dnxt_ref, 1 - slot)

        for k in range(TOP_K):
            pltpu.make_async_copy(y_hbm.at[pl.ds(0, tm), 0], ybuf.at[slot, k], sems.at[slot]).wait()
        gates = gate_ref[...]
        f = gates[:, 4:5] * _unpack_rows(ybuf[slot, 0]) + gates[:, 5:6] * _unpack_rows(ybuf[slot, 1])
        o_ref[...] = _ln(alpha * x_ref[...] + mod_ref[0, 5:6, :] * f) * g_ref[...] + b_ref[...]

    for slot in range(2):
        pl.when(i % 2 == slot)(functools.partial(step, slot))


def _combine(y3, dest_rows, xs, rout, mod, g, b, n_lat, lat_len, n_batch, alpha, tm):
    n, d = xs.shape
    nb = n // tm
    nlb, bpb = n_lat // tm, lat_len // tm

    def grp(i):
        return jnp.where(i < nlb, i // bpb, n_batch)

    dest3 = dest_rows.reshape(nb, 1, TOP_K * tm)
    const = lambda i: (0, 0)
    return pl.pallas_call(
        functools.partial(_combine_kernel, alpha=alpha),
        grid=(nb,),
        in_specs=[pl.BlockSpec((1, 1, TOP_K * tm), lambda i: (i, 0, 0), memory_space=pltpu.SMEM),
                  pl.BlockSpec((1, 1, TOP_K * tm), lambda i: (jnp.minimum(i + 1, nb - 1), 0, 0),
                               memory_space=pltpu.SMEM),
                  pl.BlockSpec(memory_space=pl.ANY),
                  pl.BlockSpec((tm, d), lambda i: (i, 0)),
                  pl.BlockSpec((tm, LANES), lambda i: (i, 0)),
                  pl.BlockSpec((1, 6, d), lambda i: (grp(i), 0, 0)),
                  pl.BlockSpec((1, d), const), pl.BlockSpec((1, d), const)],
        out_specs=pl.BlockSpec((tm, d), lambda i: (i, 0)),
        out_shape=jax.ShapeDtypeStruct((n, d), f32),
        scratch_shapes=[pltpu.VMEM((2, TOP_K, tm, d // 2), jnp.uint32), pltpu.SemaphoreType.DMA((2,))],
        compiler_params=_cparams("arbitrary"),
        name="moe_combine",
    )(dest3, dest3, y3, xs, rout, mod, g, b)


def _moe(h, rout, routt, cnt, xs, wg, wu, wd, layer, mod, g, b, n_lat, lat_len, n_batch, alpha):
    n, d = xs.shape
    tm = routt.shape[1]
    rb = tm
    fields = routt.reshape(n // tm, 8, tm)
    eid = fields[:, 0:TOP_K, :].astype(i32)
    rank = fields[:, TOP_K:2 * TOP_K, :].astype(i32)
    counts = cnt[0, _ROUTE_LANE0:_ROUTE_LANE0 + N_EXPERTS].astype(i32)
    padded = (counts + rb - 1) // rb * rb
    pend = jnp.cumsum(padded)
    pstart = pend - padded
    dest_rows = rank
    for e in range(N_EXPERTS):
        dest_rows = dest_rows + jnp.where(eid == e, pstart[e], 0)
    dest_rows = dest_rows.reshape(-1)
    n_blocks = TOP_K * n // rb + N_EXPERTS
    n_used = (pend[-1:] // rb).astype(i32)
    ex = jnp.arange(N_EXPERTS, dtype=i32)
    row0 = (jnp.arange(n_blocks, dtype=i32) * rb)[:, None]
    owns = (pstart[None, :] <= row0) & (row0 < pend[None, :])
    used = jnp.any(owns, axis=1)
    last_active = jnp.max(jnp.where(counts > 0, ex, 0))
    block_expert = jnp.where(used, jnp.sum(jnp.where(owns, ex[None, :], 0), axis=1), last_active).astype(i32)
    first = jnp.any(owns & (pstart[None, :] == row0), axis=1).astype(i32)
    valid = jnp.sum(jnp.where(owns, jnp.clip((pstart + counts)[None, :] - row0, 0, rb), 0), axis=1)
    half = (used & (valid <= rb // 2)).astype(i32)
    later_active = (ex[None, :] > ex[:, None]) & (counts > 0)[None, :]
    nxt = jnp.min(jnp.where(later_active, ex[None, :], N_EXPERTS), axis=1)
    nxt = jnp.where(nxt == N_EXPERTS, -1, nxt).astype(i32)
    xs3 = _dispatch(h, dest_rows, counts, pstart, padded, n_blocks * rb, tm)
    y3 = _experts(xs3, n_used, block_expert, first, nxt, half, wg, wu, wd, layer, rb)
    return _combine(y3, dest_rows, xs, rout, mod, g, b, n_lat, lat_len, n_batch, alpha, tm)


def _permute_w_in(w):
    d = w.shape[0]
    o_sq, o_sk, o_sv, o_rq, o_rk, o_rv, o_gf, o_gb, o_mq, o_ckv, o_kr = (
        0, 768, 1024, 1280, 1664, 2048, 2816, 3584, 4352, 5120, 5376)
    mq = w[:, o_mq:o_mq + 768].reshape(d, MLA_HEADS, MLA_NOPE_DIM + MLA_ROPE_DIM)
    parts = [w[:, o_sq:o_sq + 768], w[:, o_rv:o_rv + 768], w[:, o_gf:o_gf + 768], w[:, o_gb:o_gb + 768],
             mq[:, :, :MLA_NOPE_DIM].reshape(d, 512), w[:, o_sk:o_sk + 256], w[:, o_sv:o_sv + 256],
             mq[:, :, MLA_NOPE_DIM:].reshape(d, 256), w[:, o_ckv:o_ckv + 256], w[:, o_rq:o_rq + 384],
             w[:, o_rk:o_rk + 384], w[:, o_kr:o_kr + 64], jnp.zeros((d, _NP - _KR - 64), w.dtype)]
    return jnp.concatenate(parts, axis=1).astype(bf16)


def kernel(x, c, ctx, c_ctx, w_ada, b_ada, w_in, swa_sink, ret_decay, mla_kv_norm, mla_w_uk, mla_w_uv, w_out, ln1_g, ln1_b, ln2_g, ln2_b, moe_w_group, moe_b_group, moe_w_expert, moe_b_expert, moe_w_gate, moe_w_up, moe_w_down):
    n_batch, lat_len, d = x.shape
    ctx_len = ctx.shape[1]
    depth = w_ada.shape[0]
    n_lat, n_ctx = n_batch * lat_len, n_batch * ctx_len
    alpha = (2 * depth) ** 0.25

    cc = jnp.zeros((8, d), f32).at[:n_batch].set(c).at[n_batch].set(c_ctx)
    mod_all = _ada(cc, w_ada, b_ada).reshape(depth, 8, 6, d)
    tab = _rope_tables(lat_len, ctx_len, _inproj_rows(lat_len, n_ctx))
    x_lat, x_ctx = x.reshape(n_lat, d), ctx.reshape(n_ctx, d)

    for l in range(depth):
        ctx_out = l < depth - 1
        mod = mod_all[l]
        p = _inproj(x_lat, x_ctx, mod, tab, _permute_w_in(w_in[l]), n_lat, lat_len, ctx_len, n_batch)
        kx, v = _mla_expand(p, mla_kv_norm[l][None, :], mla_w_uk[l].astype(bf16), mla_w_uv[l].astype(bf16))
        m = _mla_attention(p, kx, v, n_batch, lat_len, ctx_len, ctx_out)
        a = _swa_attention(p, swa_sink[l], n_batch, lat_len, ctx_len, ctx_out)
        lg = jnp.log1p(-jnp.exp2(-ret_decay[l].astype(f32)))
        yf, yb = _retention(p, lg, n_batch, lat_len, ctx_len)
        n_rows = n_lat + n_ctx if ctx_out else n_lat
        rout_w = jnp.zeros((d, LANES), f32).at[:, :N_GROUPS].set(moe_w_group[l]).at[
            :, _ROUTE_LANE0:_ROUTE_LANE0 + N_EXPERTS].set(moe_w_expert[l]).astype(bf16)
        rout_b = jnp.zeros((1, LANES), f32).at[0, :N_GROUPS].set(moe_b_group[l]).at[
            0, _ROUTE_LANE0:_ROUTE_LANE0 + N_EXPERTS].set(moe_b_expert[l])
        xs, h, rout, routt, cnt = _outproj(a, yf, yb, p, m, x_lat, x_ctx, mod, w_out[l].astype(bf16), ln1_g[l][None, :],
                                           ln1_b[l][None, :], rout_w, rout_b, n_rows, n_lat, lat_len, n_batch, alpha)
        xs = _moe(h, rout, routt, cnt, xs, moe_w_gate, moe_w_up, moe_w_down, l, mod, ln2_g[l][None, :],
                  ln2_b[l][None, :], n_lat, lat_len, n_batch, alpha)
        x_lat = x_ctx = xs
    return xs[:n_lat].reshape(n_batch, lat_len, d)
```

```python
import functools

import jax
import jax.numpy as jnp
from jax import lax
from jax.experimental import pallas as pl
from jax.experimental.pallas import tpu as pltpu

f32 = jnp.float32
bf16 = jnp.bfloat16
i32 = jnp.int32

GRID_W = 64
SWA_HEADS, SWA_KV_HEADS, SWA_HEAD_DIM, SWA_WINDOW = 6, 2, 128, 128
RET_HEADS, RET_QK_DIM, RET_V_DIM, RET_CHUNK = 6, 64, 128, 128
MLA_HEADS, MLA_NOPE_DIM, MLA_ROPE_DIM, MLA_V_DIM, MLA_KV_RANK = 4, 128, 64, 128, 256
N_GROUPS, EXPERTS_PER_GROUP, TOP_K = 4, 8, 2
N_EXPERTS = N_GROUPS * EXPERTS_PER_GROUP
ROPE_BASE = 10000.0
NORM_EPS = 1e-6
NEG_INF = -1e30
LANES = 128

_SQ, _RV, _GF, _GB, _MQN, _SK, _SV, _MQR, _CKV, _RQ, _RK, _KR = (
    0, 768, 1536, 2304, 3072, 3584, 3840, 4096, 4352, 4608, 4992, 5376)
_NP = 5504
_LOG2E = 1.4426950408889634
_SWA_SCALE = SWA_HEAD_DIM ** -0.5
_RET_SCALE = RET_QK_DIM ** -0.5
_MLA_SCALE = (MLA_NOPE_DIM + MLA_ROPE_DIM) ** -0.5 * _LOG2E
_SEGMENTS = ((_SQ, 768, 0, _SWA_SCALE), (_RV, 768, None, 1.0), (_GF, 768, None, 1.0), (_GB, 768, None, 1.0),
             (_MQN, 512, None, _MLA_SCALE), (_SK, 256, 0, 1.0), (_SV, 256, None, 1.0), (_MQR, 256, 2, _MLA_SCALE),
             (_CKV, 256, None, 1.0), (_RQ, 384, 1, _RET_SCALE), (_RK, 384, 1, 1.0), (_KR, 128, 2, 1.0))

_VMEM_LIMIT = 48 * 1024 * 1024


def _cparams(*sem):
    return pltpu.CompilerParams(dimension_semantics=sem, vmem_limit_bytes=_VMEM_LIMIT)


def _tile(n, pref):
    t = min(n, pref)
    while n % t:
        t //= 2
    return t


def _ln(x):
    mu = jnp.mean(x, axis=-1, keepdims=True)
    xc = x - mu
    var = jnp.mean(xc * xc, axis=-1, keepdims=True)
    return xc * lax.rsqrt(var + NORM_EPS)


def _silu(x):
    return x / (1.0 + jnp.exp(-x))


def _pack_rows(x):
    half = x.shape[1] // 2
    lo = lax.bitcast_convert_type(x[:, :half].astype(bf16).astype(f32), jnp.uint32)
    hi = lax.bitcast_convert_type(x[:, half:].astype(bf16).astype(f32), jnp.uint32)
    return hi | (lo >> 16)


def _unpack_rows(u):
    lo = lax.bitcast_convert_type(u << 16, f32)
    hi = lax.bitcast_convert_type(u & jnp.uint32(0xFFFF0000), f32)
    return jnp.concatenate([lo, hi], axis=1)


def _dot(a, b):
    return jnp.dot(a, b, preferred_element_type=f32)


def _dot_nt(a, b):
    return lax.dot_general(a, b, (((1,), (1,)), ((), ())), preferred_element_type=f32)


def _dot_tn(a, b):
    return lax.dot_general(a, b, (((0,), (0,)), ((), ())), preferred_element_type=f32)


def _ada_kernel(c_ref, w_ref, b_ref, o_ref):
    s = _silu(c_ref[...]).astype(bf16)
    o_ref[0] = _dot(s, w_ref[0].astype(bf16)) + b_ref[0]


def _ada(cc, w_ada, b_ada):
    depth, d, n = w_ada.shape
    tn = _tile(n, 1024)
    return pl.pallas_call(
        _ada_kernel,
        grid=(depth, n // tn),
        in_specs=[pl.BlockSpec((8, d), lambda l, j: (0, 0)),
                  pl.BlockSpec((1, d, tn), lambda l, j: (l, 0, j)),
                  pl.BlockSpec((1, 1, tn), lambda l, j: (l, 0, j))],
        out_specs=pl.BlockSpec((1, 8, tn), lambda l, j: (l, 0, j)),
        out_shape=jax.ShapeDtypeStruct((depth, 8, n), f32),
        compiler_params=_cparams("parallel", "parallel"),
        name="ada",
    )(cc, w_ada, b_ada.reshape(depth, 1, n))


_INPROJ_CHUNK = 512


def _inproj_kernel(xl_ref, xc_ref, mod_ref, tab_ref, w_ref, o_ref, *, nlb):
    tm = xl_ref.shape[0]
    h = (_ln(_stream_block(xl_ref, xc_ref, nlb)) * (1.0 + mod_ref[0, 1:2, :]) + mod_ref[0, 0:1, :]).astype(bf16)
    lane = lax.broadcasted_iota(i32, (tm, LANES), 1)
    first = {32: (lane & 63) < 32, 16: (lane & 31) < 16}
    slab_kind = {}
    for off, width, typ, scale in _SEGMENTS:
        for k in range(off // LANES, (off + width) // LANES):
            slab_kind[k] = (typ, scale)
    for c0 in range(0, _NP, _INPROJ_CHUNK):
        c1 = min(c0 + _INPROJ_CHUNK, _NP)
        acc = _dot(h, w_ref[:, c0:c1])
        for k in range(c0 // LANES, c1 // LANES):
            typ, scale = slab_kind[k]
            xk = acc[:, k * LANES - c0:(k + 1) * LANES - c0]
            if scale != 1.0:
                xk = xk * scale
            if typ is not None:
                half = 16 if typ == 2 else 32
                partner = jnp.where(first[half], pltpu.roll(xk, LANES - half, 1), pltpu.roll(xk, half, 1))
                xk = xk * tab_ref[typ, 0] + partner * tab_ref[typ, 1]
            o_ref[:, k * LANES:(k + 1) * LANES] = xk.astype(bf16)


_INPROJ_VMEM_LIMIT = 60 * 1024 * 1024


def _inproj_rows(lat_len, n_ctx):
    tm = _tile(lat_len, 512)
    while n_ctx % tm:
        tm //= 2
    return tm


def _inproj(x_lat, x_ctx, mod, tab, w, n_lat, lat_len, ctx_len, n_batch):
    d = x_lat.shape[1]
    t = n_lat + n_batch * ctx_len
    tm = _inproj_rows(lat_len, t - n_lat)
    nlb, bpb = n_lat // tm, lat_len // tm
    ctx_blocks = max(1, ctx_len // tm)

    def grp(i):
        return jnp.where(i < nlb, i // bpb, n_batch)

    def posblk(i):
        return jnp.where(i < nlb, i % bpb, bpb + (i - nlb) % ctx_blocks)

    return pl.pallas_call(
        functools.partial(_inproj_kernel, nlb=None if x_ctx is x_lat else nlb),
        grid=(t // tm,),
        in_specs=_stream_specs(x_lat, x_ctx, tm, n_lat) + [
                  pl.BlockSpec((1, 6, d), lambda i: (grp(i), 0, 0)),
                  pl.BlockSpec((3, 2, tm, LANES), lambda i: (0, 0, posblk(i), 0)),
                  pl.BlockSpec((d, _NP), lambda i: (0, 0), pipeline_mode=pl.Buffered(1))],
        out_specs=pl.BlockSpec((tm, _NP), lambda i: (i, 0)),
        out_shape=jax.ShapeDtypeStruct((t, _NP), bf16),
        compiler_params=pltpu.CompilerParams(dimension_semantics=("parallel",), vmem_limit_bytes=_INPROJ_VMEM_LIMIT),
        name="inproj",
    )(x_lat, x_ctx, mod, tab, w)


def _rope_tables(lat_len, ctx_len, tm):
    lane = jnp.arange(LANES)
    t = jnp.arange(lat_len)
    rows = (t // GRID_W).astype(f32)
    cols = (t % GRID_W).astype(f32)

    def freq(half):
        return ROPE_BASE ** (-jnp.arange(half, dtype=f32) / half)

    def table(pos, inv, sign):
        ang = pos * inv[None, :]
        return jnp.stack([jnp.cos(ang), jnp.sin(ang) * sign[None, :]])

    def ident(n):
        return jnp.stack([jnp.ones((n, LANES), f32), jnp.zeros((n, LANES), f32)])

    sign32 = jnp.where((lane % 64) < 32, -1.0, 1.0).astype(f32)
    sign16 = jnp.where((lane % 32) < 16, -1.0, 1.0).astype(f32)
    inv32 = freq(32)[lane % 32]
    inv16 = freq(16)[lane % 16]
    pos_swa = jnp.where((lane // 64)[None, :] == 0, rows[:, None], cols[:, None])
    pos_mla = jnp.where(((lane % 64) // 32)[None, :] == 0, rows[:, None], cols[:, None])
    rep = max(1, tm // ctx_len)
    nc = rep * ctx_len
    pos_ret_l = jnp.broadcast_to((ctx_len + t).astype(f32)[:, None], (lat_len, LANES))
    pos_ret_c = jnp.broadcast_to(jnp.tile(jnp.arange(ctx_len), rep).astype(f32)[:, None], (nc, LANES))
    t0 = jnp.concatenate([table(pos_swa, inv32, sign32), ident(nc)], axis=1)
    t1 = jnp.concatenate([table(pos_ret_l, inv32, sign32), table(pos_ret_c, inv32, sign32)], axis=1)
    t2 = jnp.concatenate([table(pos_mla, inv16, sign16), ident(nc)], axis=1)
    return jnp.stack([t0, t1, t2])


def _mla_expand_kernel(ckv_ref, kr_ref, g_ref, wuk_ref, wuv_ref, k_ref, v_ref):
    c = ckv_ref[...].astype(f32)
    cn = (c * lax.rsqrt(jnp.mean(c * c, axis=-1, keepdims=True) + NORM_EPS) * g_ref[...]).astype(bf16)
    kn = _dot(cn, wuk_ref[...]).astype(bf16)
    vv = _dot(cn, wuv_ref[...]).astype(bf16)
    kr = kr_ref[...]
    for h in range(MLA_HEADS):
        k_ref[h, :, 0:128] = kn[:, h * 128:(h + 1) * 128]
        k_ref[h, :, 128:256] = kr
        v_ref[h] = vv[:, h * 128:(h + 1) * 128]


def _mla_expand(p, g, wuk, wuv):
    t = p.shape[0]
    tm = _tile(t, 1024)
    return pl.pallas_call(
        _mla_expand_kernel,
        grid=(t // tm,),
        in_specs=[pl.BlockSpec((tm, 256), lambda i: (i, _CKV // 256)),
                  pl.BlockSpec((tm, 128), lambda i: (i, _KR // 128)),
                  pl.BlockSpec((1, 256), lambda i: (0, 0)),
                  pl.BlockSpec((256, 512), lambda i: (0, 0)),
                  pl.BlockSpec((256, 512), lambda i: (0, 0))],
        out_specs=[pl.BlockSpec((MLA_HEADS, tm, 256), lambda i: (0, i, 0)),
                   pl.BlockSpec((MLA_HEADS, tm, 128), lambda i: (0, i, 0))],
        out_shape=[jax.ShapeDtypeStruct((MLA_HEADS, t, 256), bf16), jax.ShapeDtypeStruct((MLA_HEADS, t, 128), bf16)],
        compiler_params=_cparams("parallel"),
        name="mla_expand",
    )(p, p, g, wuk, wuv)


def _mla_kernel(*refs, with_lat):
    if with_lat:
        qn_ref, qr_ref, kc_ref, vc_ref, kl_ref, vl_ref, o_ref, qx_scr = refs
    else:
        qn_ref, qr_ref, kc_ref, vc_ref, _, o_ref, qx_scr = refs
    lane = lax.broadcasted_iota(i32, (qn_ref.shape[0], LANES), 1)
    for h in range(MLA_HEADS):
        slab = qr_ref[:, (h // 2) * 128:(h // 2 + 1) * 128].astype(f32)
        if h % 2:
            slab = pltpu.roll(slab, 64, 1)
        qx_scr[h, :, 0:128] = qn_ref[:, h * 128:(h + 1) * 128]
        qx_scr[h, :, 128:256] = jnp.where(lane < 64, slab, 0.0).astype(bf16)
    def scores(h):
        q = qx_scr[h]
        parts = [(_dot_nt(q, kc_ref[h]), vc_ref[h])]
        if with_lat:
            parts.append((_dot_nt(q, kl_ref[h]), vl_ref[h]))
        return parts

    nxt = scores(0)
    for h in range(MLA_HEADS):
        parts, nxt = nxt, (scores(h + 1) if h + 1 < MLA_HEADS else None)
        m = None
        for s, _ in parts:
            mx = jnp.max(s, axis=1, keepdims=True)
            m = mx if m is None else jnp.maximum(m, mx)
        den, acc = None, None
        for s, vv in parts:
            pr = jnp.exp2(s - m)
            sm = jnp.sum(pr, axis=1, keepdims=True)
            pv = _dot(pr.astype(bf16), vv)
            den = sm if den is None else den + sm
            acc = pv if acc is None else acc + pv
        o_ref[:, h * 128:(h + 1) * 128] = (acc / den).astype(bf16)


def _mla_attention(p, kx, v, n_batch, lat_len, ctx_len, compute_ctx):
    t = p.shape[0]
    tq = _tile(lat_len, 256)
    nlq = lat_len // tq
    cbase = n_batch * lat_len // ctx_len
    once = pl.Buffered(1)
    out = pl.pallas_call(
        functools.partial(_mla_kernel, with_lat=True),
        grid=(n_batch, nlq),
        in_specs=[pl.BlockSpec((tq, 512), lambda b, qi: (b * nlq + qi, _MQN // 512)),
                  pl.BlockSpec((tq, 256), lambda b, qi: (b * nlq + qi, _MQR // 256)),
                  pl.BlockSpec((MLA_HEADS, ctx_len, 256), lambda b, qi: (0, cbase + b, 0)),
                  pl.BlockSpec((MLA_HEADS, ctx_len, 128), lambda b, qi: (0, cbase + b, 0)),
                  pl.BlockSpec((MLA_HEADS, lat_len, 256), lambda b, qi: (0, b, 0), pipeline_mode=once),
                  pl.BlockSpec((MLA_HEADS, lat_len, 128), lambda b, qi: (0, b, 0), pipeline_mode=once)],
        out_specs=pl.BlockSpec((tq, 512), lambda b, qi: (b * nlq + qi, 0)),
        out_shape=jax.ShapeDtypeStruct((t, 512), bf16),
        scratch_shapes=[pltpu.VMEM((MLA_HEADS, tq, 256), bf16)],
        compiler_params=_cparams("parallel", "arbitrary"),
        name="mla_attn",
    )(p, p, kx, v, kx, v)
    if compute_ctx:
        out = pl.pallas_call(
            functools.partial(_mla_kernel, with_lat=False),
            grid=(n_batch,),
            in_specs=[pl.BlockSpec((ctx_len, 512), lambda b: (cbase + b, _MQN // 512)),
                      pl.BlockSpec((ctx_len, 256), lambda b: (cbase + b, _MQR // 256)),
                      pl.BlockSpec((MLA_HEADS, ctx_len, 256), lambda b: (0, cbase + b, 0)),
                      pl.BlockSpec((MLA_HEADS, ctx_len, 128), lambda b: (0, cbase + b, 0)),
                      pl.BlockSpec(memory_space=pl.ANY)],
            out_specs=pl.BlockSpec((ctx_len, 512), lambda b: (cbase + b, 0)),
            out_shape=jax.ShapeDtypeStruct((t, 512), bf16),
            scratch_shapes=[pltpu.VMEM((MLA_HEADS, ctx_len, 256), bf16)],
            input_output_aliases={4: 0},
            compiler_params=_cparams("parallel"),
            name="mla_ctx",
        )(p, p, kx, v, out)
    return out


def _swa_softmax_out(parts, sink_col, o_ref, g):
    m = sink_col
    for s, _ in parts:
        m = jnp.maximum(m, jnp.max(s, axis=1, keepdims=True))
    den = jnp.exp(sink_col - m)
    acc = None
    for s, vv in parts:
        pr = jnp.exp(s - m)
        den = den + jnp.sum(pr, axis=1, keepdims=True)
        pv = _dot(pr.astype(bf16), vv)
        acc = pv if acc is None else acc + pv
    o = acc / den
    nq = o.shape[0] // 3
    for j in range(3):
        o_ref[:, (3 * g + j) * 128:(3 * g + j + 1) * 128] = o[j * nq:(j + 1) * nq].astype(bf16)


def _sink_col(sink_ref, g, nq):
    row = lax.broadcasted_iota(i32, (3 * nq, 1), 0)
    return jnp.where(row < nq, sink_ref[3 * g], jnp.where(row < 2 * nq, sink_ref[3 * g + 1], sink_ref[3 * g + 2]))


def _swa_band_bias(blk):
    r = jnp.arange(3 * blk)[:, None] % blk
    c = jnp.arange(3 * blk)[None, :]
    d = c - r
    band = (d >= 0) & (d <= 2 * SWA_WINDOW)
    variants = []
    for v in range(4):
        lo = blk if v & 1 else 0
        hi = 2 * blk if v & 2 else 3 * blk
        variants.append(jnp.where(band & (c >= lo) & (c < hi), 0.0, NEG_INF).astype(f32))
    return jnp.stack(variants)


def _swa_kernel(sink_ref, bias_ref, q_ref, kp_ref, ko_ref, kn_ref, vp_ref, vo_ref, vn_ref, kc_ref, vc_ref, o_ref):
    blk = q_ref.shape[0]
    q = q_ref[...]
    bias = bias_ref[0]
    scores = []
    for g in range(SWA_KV_HEADS):
        hs = slice(g * 128, (g + 1) * 128)
        qg = jnp.concatenate([q[:, (3 * g + j) * 128:(3 * g + j + 1) * 128] for j in range(3)], axis=0)
        kloc = jnp.concatenate([kp_ref[:, hs], ko_ref[:, hs], kn_ref[:, hs]], axis=0)
        scores.append((_dot_nt(qg, kc_ref[:, hs]), _dot_nt(qg, kloc) + bias))
    for g in range(SWA_KV_HEADS):
        hs = slice(g * 128, (g + 1) * 128)
        vloc = jnp.concatenate([vp_ref[:, hs], vo_ref[:, hs], vn_ref[:, hs]], axis=0)
        s_ctx, s_loc = scores[g]
        _swa_softmax_out([(s_ctx, vc_ref[:, hs]), (s_loc, vloc)], _sink_col(sink_ref, g, blk), o_ref, g)


def _swa_ctx_kernel(sink_ref, q_ref, kc_ref, vc_ref, prev_ref, o_ref):
    del prev_ref
    q = q_ref[...]
    nq = q.shape[0]
    for g in range(SWA_KV_HEADS):
        hs = slice(g * 128, (g + 1) * 128)
        qg = jnp.concatenate([q[:, (3 * g + j) * 128:(3 * g + j + 1) * 128] for j in range(3)], axis=0)
        s_ctx = _dot_nt(qg, kc_ref[:, hs])
        _swa_softmax_out([(s_ctx, vc_ref[:, hs])], _sink_col(sink_ref, g, nq), o_ref, g)


def _swa_attention(p, sink, n_batch, lat_len, ctx_len, compute_ctx):
    t = p.shape[0]
    blk = SWA_WINDOW
    nb = lat_len // blk
    cbase = n_batch * lat_len // ctx_len
    smem = pl.BlockSpec(memory_space=pltpu.SMEM)

    def kspec(col, off):
        return pl.BlockSpec((blk, 256), lambda b, i: (b * nb + jnp.clip(i + off, 0, nb - 1), col // 256))

    out = pl.pallas_call(
        _swa_kernel,
        grid=(n_batch, nb),
        in_specs=[smem,
                  pl.BlockSpec((1, 3 * blk, 3 * blk),
                               lambda b, i: ((i == 0).astype(i32) + 2 * (i == nb - 1).astype(i32), 0, 0)),
                  pl.BlockSpec((blk, 768), lambda b, i: (b * nb + i, _SQ // 768)),
                  kspec(_SK, -1), kspec(_SK, 0), kspec(_SK, 1),
                  kspec(_SV, -1), kspec(_SV, 0), kspec(_SV, 1),
                  pl.BlockSpec((ctx_len, 256), lambda b, i: (cbase + b, _SK // 256)),
                  pl.BlockSpec((ctx_len, 256), lambda b, i: (cbase + b, _SV // 256))],
        out_specs=pl.BlockSpec((blk, 768), lambda b, i: (b * nb + i, 0)),
        out_shape=jax.ShapeDtypeStruct((t, 768), bf16),
        compiler_params=_cparams("parallel", "parallel"),
        name="swa_attn",
    )(sink, _swa_band_bias(blk), p, p, p, p, p, p, p, p, p)
    if compute_ctx:
        out = pl.pallas_call(
            _swa_ctx_kernel,
            grid=(n_batch,),
            in_specs=[smem,
                      pl.BlockSpec((ctx_len, 768), lambda b: (cbase + b, _SQ // 768)),
                      pl.BlockSpec((ctx_len, 256), lambda b: (cbase + b, _SK // 256)),
                      pl.BlockSpec((ctx_len, 256), lambda b: (cbase + b, _SV // 256)),
                      pl.BlockSpec(memory_space=pl.ANY)],
            out_specs=pl.BlockSpec((ctx_len, 768), lambda b: (cbase + b, 0)),
            out_shape=jax.ShapeDtypeStruct((t, 768), bf16),
            input_output_aliases={4: 0},
            compiler_params=_cparams("parallel"),
            name="swa_ctx",
        )(sink, p, p, p, out)
    return out


def _ret_kernel(lg_ref, qf_ref, kf_ref, vf_ref, qb_ref, kb_ref, vb_ref, of_ref, ob_ref,
                s_scr, d_scr, qd_scr, kd_scr, gc_scr):
    i = pl.program_id(1)
    cc = RET_CHUNK

    @pl.when(i == 0)
    def _():
        s_scr[...] = jnp.zeros(s_scr.shape, f32)
        r = lax.broadcasted_iota(i32, (cc, cc), 0).astype(f32)
        c = lax.broadcasted_iota(i32, (cc, cc), 1).astype(f32)
        for direction in range(2):
            backward = direction == 1
            diff = (c - r) if backward else (r - c)
            for h in range(RET_HEADS):
                lgh = lg_ref[direction, h]
                j = direction * RET_HEADS + h
                d_scr[j] = jnp.where(diff >= 0, jnp.exp(lgh * jnp.maximum(diff, 0.0)), 0.0)
                qd_scr[j] = jnp.exp(lgh * ((cc - r) if backward else (r + 1.0)))
                kd_scr[j] = jnp.exp(lgh * (c if backward else (cc - 1.0 - c)))
                gc_scr[j] = jnp.exp(lgh * cc + jnp.zeros((cc, cc), f32))

    lane = lax.broadcasted_iota(i32, (cc, LANES), 1)
    cps = qf_ref.shape[0] // cc
    zero = jnp.zeros((cc, LANES), bf16)
    streams = ((qf_ref, kf_ref, vf_ref, of_ref), (qb_ref, kb_ref, vb_ref, ob_ref))
    for step in range(cps):
        work = []
        for pair in range(RET_HEADS // 2):
            for direction, (q_ref, k_ref, v_ref, o_ref) in enumerate(streams):
                chunk = step if direction == 0 else cps - 1 - step
                rows = slice(chunk * cc, (chunk + 1) * cc)
                q2 = q_ref[rows, pair * 128:(pair + 1) * 128].astype(f32)
                k2b = k_ref[rows, pair * 128:(pair + 1) * 128]
                k2t = jnp.transpose(k2b.astype(f32))
                ja, jb = direction * RET_HEADS + 2 * pair, direction * RET_HEADS + 2 * pair + 1
                va = v_ref[rows, 2 * pair * 128:(2 * pair + 1) * 128]
                vb = v_ref[rows, (2 * pair + 1) * 128:(2 * pair + 2) * 128]
                qa = jnp.where(lane < 64, q2, 0.0)
                qb = jnp.where(lane >= 64, q2, 0.0)
                s2 = _dot_nt(jnp.concatenate([qa, qb], axis=0).astype(bf16), k2b)
                upd = _dot(jnp.concatenate([k2t * kd_scr[ja], k2t * kd_scr[jb]], axis=0).astype(bf16),
                           jnp.concatenate([va, vb], axis=1))
                work.append((pair, o_ref, rows, ja, jb, va, vb, qa, qb, s2, upd))
        for pair, o_ref, rows, ja, jb, va, vb, qa, qb, s2, upd in work:
            lhs = jnp.concatenate([s2[0:cc] * d_scr[ja], s2[cc:2 * cc] * d_scr[jb],
                                   qa * qd_scr[ja], qb * qd_scr[jb]], axis=1).astype(bf16)
            rhs = jnp.concatenate([jnp.concatenate([va, zero], axis=1), jnp.concatenate([zero, vb], axis=1),
                                   jnp.concatenate([s_scr[ja].astype(bf16), zero], axis=1),
                                   jnp.concatenate([zero, s_scr[jb].astype(bf16)], axis=1)], axis=0)
            y2 = _dot(lhs, rhs)
            s_scr[ja] = gc_scr[ja] * s_scr[ja] + upd[0:LANES, 0:LANES]
            s_scr[jb] = gc_scr[jb] * s_scr[jb] + upd[LANES:2 * LANES, LANES:2 * LANES]
            o_ref[rows, 2 * pair * 128:(2 * pair + 1) * 128] = _ln(y2[:, 0:LANES]).astype(bf16)
            o_ref[rows, (2 * pair + 1) * 128:(2 * pair + 2) * 128] = _ln(y2[:, LANES:2 * LANES]).astype(bf16)


def _retention(p, lg, n_batch, lat_len, ctx_len):
    t = p.shape[0]
    cc = RET_CHUNK
    cps = 2 if (ctx_len // cc) % 2 == 0 and (lat_len // cc) % 2 == 0 else 1
    blk = cps * cc
    ncc, nlc = ctx_len // blk, lat_len // blk
    cbase = n_batch * nlc
    smem = pl.BlockSpec(memory_space=pltpu.SMEM)
    scratch = [pltpu.VMEM((2 * RET_HEADS, cc, cc), f32) for _ in range(5)]

    def fwd_row(b, i):
        return jnp.where(i < ncc, cbase + b * ncc + i, b * nlc + (i - ncc))

    def bwd_row(b, i):
        return jnp.where(i < ncc, cbase + b * ncc + (ncc - 1 - i), b * nlc + (nlc - 1 - (i - ncc)))

    def specs(row):
        return [pl.BlockSpec((blk, 384), lambda b, i: (row(b, i), _RQ // 384)),
                pl.BlockSpec((blk, 384), lambda b, i: (row(b, i), _RK // 384)),
                pl.BlockSpec((blk, 768), lambda b, i: (row(b, i), _RV // 768))]

    return pl.pallas_call(
        _ret_kernel,
        grid=(n_batch, ncc + nlc),
        in_specs=[smem] + specs(fwd_row) + specs(bwd_row),
        out_specs=[pl.BlockSpec((blk, 768), lambda b, i: (fwd_row(b, i), 0)),
                   pl.BlockSpec((blk, 768), lambda b, i: (bwd_row(b, i), 0))],
        out_shape=[jax.ShapeDtypeStruct((t, 768), bf16), jax.ShapeDtypeStruct((t, 768), bf16)],
        scratch_shapes=scratch,
        compiler_params=_cparams("parallel", "arbitrary"),
        name="retention",
    )(lg, p, p, p, p, p, p)


def _stream_specs(x_lat, x_ctx, tm, n_lat):
    d = x_lat.shape[1]
    nlb = n_lat // tm
    if x_ctx is x_lat:
        return [pl.BlockSpec((tm, d), lambda i: (i, 0)), pl.BlockSpec((8, d), lambda i: (0, 0))]
    return [pl.BlockSpec((tm, d), lambda i: (jnp.minimum(i, nlb - 1), 0)),
            pl.BlockSpec((tm, d), lambda i: (jnp.maximum(i - nlb, 0), 0))]


def _stream_block(xl_ref, xc_ref, nlb, rows=slice(None)):
    if nlb is None:
        return xl_ref[rows, :]
    return jnp.where(pl.program_id(0) < nlb, xl_ref[rows, :], xc_ref[rows, :])


_ROUTE_ROWS = 256


def _outproj_kernel(a_ref, yf_ref, yb_ref, gf_ref, gb_ref, m_ref, xl_ref, xc_ref, mod_ref, w_ref, g_ref, b_ref,
                    rw_ref, rb_ref, xo_ref, h_ref, rout_ref, routt_ref, cnt_ref, carry_scr, *, alpha, nlb, sub):
    @pl.when(pl.program_id(0) == 0)
    def _():
        carry_scr[...] = jnp.zeros(carry_scr.shape, f32)

    nsub = a_ref.shape[0] // sub
    ys = []
    for sb in range(nsub):
        rows = slice(sb * sub, (sb + 1) * sub)
        ret = (_silu(gf_ref[rows, :].astype(f32)) * yf_ref[rows, :].astype(f32)
               + _silu(gb_ref[rows, :].astype(f32)) * yb_ref[rows, :].astype(f32)).astype(bf16)
        ys.append(_dot(jnp.concatenate([a_ref[rows, :], ret, m_ref[rows, :]], axis=1), w_ref[...]))
    logits = []
    for sb in range(nsub):
        rows = slice(sb * sub, (sb + 1) * sub)
        xn = (_ln(alpha * _stream_block(xl_ref, xc_ref, nlb, rows) + mod_ref[0, 2:3, :] * ys[sb]) * g_ref[...]
              + b_ref[...])
        xo_ref[rows, :] = xn
        h = _ln(xn) * (1.0 + mod_ref[0, 4:5, :]) + mod_ref[0, 3:4, :]
        h_ref[rows, :] = _pack_rows(h)
        logits.append(_dot(h.astype(bf16), rw_ref[...]) + rb_ref[...])
    for sb in range(nsub):
        rows = slice(sb * sub, (sb + 1) * sub)
        table = _route(logits[sb], carry_scr)
        rout_ref[rows, :] = table
        routt_ref[sb * 8:(sb + 1) * 8, :] = jnp.transpose(table)[0:8, :]
    cnt_ref[...] = carry_scr[...]


def _outproj(a, yf, yb, p, m, x_lat, x_ctx, mod, w, g, b, rw, rb, n_rows, n_lat, lat_len, n_batch, alpha):
    d = x_lat.shape[1]
    tm = _tile(lat_len, 2 * _ROUTE_ROWS)
    while (n_rows - n_lat) % tm:
        tm //= 2
    sub = min(tm, _ROUTE_ROWS)
    nlb, bpb = n_lat // tm, lat_len // tm

    def grp(i):
        return jnp.where(i < nlb, i // bpb, n_batch)

    row = lambda i: (i, 0)
    const = lambda i: (0, 0)
    return pl.pallas_call(
        functools.partial(_outproj_kernel, alpha=alpha, nlb=None if x_ctx is x_lat else nlb, sub=sub),
        grid=(n_rows // tm,),
        in_specs=[pl.BlockSpec((tm, 768), row), pl.BlockSpec((tm, 768), row), pl.BlockSpec((tm, 768), row),
                  pl.BlockSpec((tm, 768), lambda i: (i, _GF // 768)), pl.BlockSpec((tm, 768), lambda i: (i, _GB // 768)),
                  pl.BlockSpec((tm, 512), row)] + _stream_specs(x_lat, x_ctx, tm, n_lat) + [
                  pl.BlockSpec((1, 6, d), lambda i: (grp(i), 0, 0)),
                  pl.BlockSpec(w.shape, const, pipeline_mode=pl.Buffered(1)),
                  pl.BlockSpec((1, d), const), pl.BlockSpec((1, d), const),
                  pl.BlockSpec((d, LANES), const), pl.BlockSpec((1, LANES), const)],
        out_specs=[pl.BlockSpec((tm, d), row), pl.BlockSpec((tm, d // 2), row),
                   pl.BlockSpec((tm, LANES), row), pl.BlockSpec((tm // sub * 8, sub), row),
                   pl.BlockSpec((8, LANES), const)],
        out_shape=[jax.ShapeDtypeStruct((n_rows, d), f32), jax.ShapeDtypeStruct((n_rows, d // 2), jnp.uint32),
                   jax.ShapeDtypeStruct((n_rows, LANES), f32), jax.ShapeDtypeStruct((n_rows // sub * 8, sub), f32),
                   jax.ShapeDtypeStruct((8, LANES), f32)],
        scratch_shapes=[pltpu.VMEM((8, LANES), f32)],
        compiler_params=pltpu.CompilerParams(dimension_semantics=("arbitrary",), vmem_limit_bytes=_INPROJ_VMEM_LIMIT),
        name="outproj",
    )(a, yf, yb, p, p, m, x_lat, x_ctx, mod, w, g, b, rw, rb)


_ROUTE_LANE0 = N_GROUPS


def _route(logits, carry_scr):
    tm = logits.shape[0]
    lane = lax.broadcasted_iota(i32, (tm, LANES), 1)
    lane_f = lane.astype(f32)
    big = float(2 * LANES)
    gl = jnp.where(lane < N_GROUPS, logits, -jnp.inf)
    gmax = jnp.max(gl, axis=1, keepdims=True)
    gidx = jnp.min(jnp.where(gl == gmax, lane_f, big), axis=1, keepdims=True)
    p_group = 1.0 / jnp.sum(jnp.exp(gl - gmax), axis=1, keepdims=True)
    egroup = ((lane - _ROUTE_LANE0) >> 3).astype(f32)
    in_group = (lane >= _ROUTE_LANE0) & (lane < _ROUTE_LANE0 + N_EXPERTS) & (egroup == gidx)
    ev = jnp.where(in_group, logits, -jnp.inf)
    e1 = jnp.max(ev, axis=1, keepdims=True)
    i1 = jnp.min(jnp.where(ev == e1, lane_f, big), axis=1, keepdims=True)
    ev2 = jnp.where(lane_f == i1, -jnp.inf, ev)
    e2 = jnp.max(ev2, axis=1, keepdims=True)
    i2 = jnp.min(jnp.where(ev2 == e2, lane_f, big), axis=1, keepdims=True)
    tt = jnp.exp(e2 - e1)
    w1 = p_group / (1.0 + tt)
    w2 = p_group * tt / (1.0 + tt)

    hit1 = lane_f == i1
    hit2 = lane_f == i2
    onehot = jnp.where(hit1, 1.0, jnp.where(hit2, 1.0, 0.0))
    rr = lax.broadcasted_iota(i32, (tm, tm), 0)
    cc = lax.broadcasted_iota(i32, (tm, tm), 1)
    lower = jnp.where(cc < rr, 1.0, 0.0).astype(bf16)
    before = _dot(lower, onehot.astype(bf16)) + carry_scr[0:1, :]
    rank1 = jnp.sum(jnp.where(hit1, before, 0.0), axis=1, keepdims=True)
    rank2 = jnp.sum(jnp.where(hit2, before, 0.0), axis=1, keepdims=True)
    carry_scr[0:1, :] = carry_scr[0:1, :] + jnp.sum(onehot, axis=0, keepdims=True)

    return jnp.where(lane == 0, i1 - _ROUTE_LANE0, jnp.where(lane == 1, i2 - _ROUTE_LANE0, jnp.where(
        lane == 2, rank1, jnp.where(lane == 3, rank2, jnp.where(lane == 4, w1, jnp.where(lane == 5, w2, 0.0))))))


def _dispatch_kernel(cnt_ref, pstart_ref, padded_ref, dest_ref, h_ref, o_hbm, sem):
    i = pl.program_id(0)
    nb = pl.num_programs(0) - 1
    tm = dest_ref.shape[2] // TOP_K

    def row_copy(src_row, dst_row):
        return pltpu.make_async_copy(h_ref.at[pl.ds(src_row, 1), :], o_hbm.at[dst_row], sem)

    @pl.when(i < nb)
    def _():
        def issue(r8, carry):
            base = pl.multiple_of(r8 * 8, 8)
            for j in range(8):
                for k in range(TOP_K):
                    row_copy(base + j, dest_ref[0, 0, k * tm + base + j]).start()
            return carry

        lax.fori_loop(0, tm // 8, issue, 0)
        for _ in range(TOP_K):
            pltpu.make_async_copy(h_ref, o_hbm.at[pl.ds(0, tm), 0], sem).wait()

    @pl.when(i == nb)
    def _():
        def per_expert(e, total):
            lo, hi = cnt_ref[e], padded_ref[e]

            def fill(r, carry):
                row_copy(0, pstart_ref[e] + r).start()
                return carry

            lax.fori_loop(lo, hi, fill, 0)
            return total + (hi - lo)

        total = lax.fori_loop(0, N_EXPERTS, per_expert, 0)

        def drain(r, carry):
            row_copy(0, 0).wait()
            return carry

        lax.fori_loop(0, total, drain, 0)


def _dispatch(h, dest_rows, counts, pstart, padded, n_rows, tm):
    n, d = h.shape
    nb = n // tm
    grid_spec = pltpu.PrefetchScalarGridSpec(
        num_scalar_prefetch=3,
        grid=(nb + 1,),
        in_specs=[pl.BlockSpec((1, 1, TOP_K * tm), lambda i, *_: (jnp.minimum(i, nb - 1), 0, 0),
                               memory_space=pltpu.SMEM),
                  pl.BlockSpec((tm, d), lambda i, *_: (jnp.minimum(i, nb - 1), 0))],
        out_specs=pl.BlockSpec(memory_space=pl.ANY),
        scratch_shapes=[pltpu.SemaphoreType.DMA(())],
    )
    return pl.pallas_call(
        _dispatch_kernel,
        grid_spec=grid_spec,
        out_shape=jax.ShapeDtypeStruct((n_rows, 1, d), h.dtype),
        compiler_params=_cparams("arbitrary"),
        name="moe_dispatch",
    )(counts, pstart, padded, dest_rows.reshape(nb, 1, TOP_K * tm), h)


_CAST_ROWS = 256


def _expert_kernel(nused_ref, be_ref, first_ref, nxt_ref, half_ref, x_hbm, wg_hbm, wu_hbm, wd_hbm, y_hbm,
                   wg_f, wu_f, wd_f, wg_b, wu_b, wd_b, xbuf, ybuf, sems, xsems, ysems, *, layer):
    i = pl.program_id(0)
    n_used = nused_ref[0]
    used = i < n_used
    rb = xbuf.shape[1]
    slot = i % 2
    staged = ((wg_hbm, wg_f, wg_b), (wu_hbm, wu_f, wu_b), (wd_hbm, wd_f, wd_b))

    def x_fetch(blk, s):
        return pltpu.make_async_copy(x_hbm.at[pl.ds(pl.multiple_of(blk * rb, rb), rb), 0], xbuf.at[s], xsems.at[s])

    def y_store(blk, s):
        return pltpu.make_async_copy(ybuf.at[s], y_hbm.at[pl.ds(pl.multiple_of(blk * rb, rb), rb), 0], ysems.at[s])

    @pl.when(i == 0)
    def _():
        x_fetch(0, 0).start()

    @pl.when(i + 1 < n_used)
    def _():
        x_fetch(i + 1, 1 - slot).start()

    def fetch(e):
        return [pltpu.make_async_copy(hbm.at[layer, e], stage, sems.at[j]) for j, (hbm, stage, _) in enumerate(staged)]

    @pl.when(i == 0)
    def _():
        for cp in fetch(be_ref[0]):
            cp.start()

    @pl.when(used & (first_ref[i] == 1))
    def _():
        for cp in fetch(be_ref[i]):
            cp.wait()
        for _, stage, dst in staged:
            rows = stage.shape[0]
            step = min(_CAST_ROWS, rows)

            def cast(c, carry, stage=stage, dst=dst, step=step):
                sl = pl.ds(pl.multiple_of(c * step, step), step)
                dst[sl, :] = stage[sl, :].astype(bf16)
                return carry

            lax.fori_loop(0, rows // step, cast, 0)
        nxt = nxt_ref[be_ref[i]]

        @pl.when(nxt >= 0)
        def _():
            for cp in fetch(nxt):
                cp.start()

    @pl.when(used)
    def _():
        x_fetch(i, slot).wait()

    @pl.when(used & (i >= 2))
    def _():
        y_store(i - 2, slot).wait()

    def ffn(rows):
        x = _unpack_rows(xbuf[slot, 0:rows, :]).astype(bf16)
        act = (_silu(_dot(x, wg_b[...])) * _dot(x, wu_b[...])).astype(bf16)
        ybuf[slot, 0:rows, :] = _pack_rows(_dot(act, wd_b[...]))
        if rows < rb:
            ybuf[slot, rows:rb, :] = jnp.zeros((rb - rows, ybuf.shape[2]), ybuf.dtype)

    @pl.when(used & (half_ref[i] == 0))
    def _():
        ffn(rb)

    @pl.when(used & (half_ref[i] == 1))
    def _():
        ffn(rb // 2)

    @pl.when(used)
    def _():
        y_store(i, slot).start()

    @pl.when(i == n_used - 1)
    def _():
        y_store(i, slot).wait()

        @pl.when(i >= 1)
        def _():
            y_store(i - 1, 1 - slot).wait()


_EXPERT_VMEM_LIMIT = 60 * 1024 * 1024


def _experts(xs3, n_used, block_expert, first, nxt, half, wg, wu, wd, layer, rb):
    d, hid = wg.shape[2], wg.shape[3]
    n_blocks = xs3.shape[0] // rb

    anyspace = pl.BlockSpec(memory_space=pl.ANY)
    grid_spec = pltpu.PrefetchScalarGridSpec(
        num_scalar_prefetch=5,
        grid=(n_blocks,),
        in_specs=[anyspace, anyspace, anyspace, anyspace],
        out_specs=anyspace,
        scratch_shapes=[pltpu.VMEM((d, hid), f32), pltpu.VMEM((d, hid), f32), pltpu.VMEM((hid, d), f32),
                        pltpu.VMEM((d, hid), bf16), pltpu.VMEM((d, hid), bf16), pltpu.VMEM((hid, d), bf16),
                        pltpu.VMEM((2, rb, d // 2), jnp.uint32), pltpu.VMEM((2, rb, d // 2), jnp.uint32),
                        pltpu.SemaphoreType.DMA((3,)), pltpu.SemaphoreType.DMA((2,)), pltpu.SemaphoreType.DMA((2,))],
    )
    return pl.pallas_call(
        functools.partial(_expert_kernel, layer=layer),
        grid_spec=grid_spec,
        out_shape=jax.ShapeDtypeStruct(xs3.shape, xs3.dtype),
        compiler_params=pltpu.CompilerParams(dimension_semantics=("arbitrary",), vmem_limit_bytes=_EXPERT_VMEM_LIMIT),
        name="moe_experts",
    )(n_used, block_expert, first, nxt, half, xs3, wg, wu, wd)


def _combine_kernel(dcur_ref, dnxt_ref, y_hbm, x_ref, gate_ref, mod_ref, g_ref, b_ref, o_ref, ybuf, sems, *, alpha):
    i = pl.program_id(0)
    nb = pl.num_programs(0)
    tm = x_ref.shape[0]

    def issue(dref, slot):
        def body(r8, carry):
            base = pl.multiple_of(r8 * 8, 8)
            for j in range(8):
                for k in range(TOP_K):
                    pltpu.make_async_copy(y_hbm.at[dref[0, 0, k * tm + base + j]],
                                          ybuf.at[slot, k, pl.ds(base + j, 1), :], sems.at[slot]).start()
            return carry

        lax.fori_loop(0, tm // 8, body, 0)

    @pl.when(i == 0)
    def _():
        issue(dcur_ref, 0)

    def step(slot):
        @pl.when(i + 1 < nb)
        def _():
            issue(dnxt_ref, 1 - slot)

        for k in range(TOP_K):
            pltpu.make_async_copy(y_hbm.at[pl.ds(0, tm), 0], ybuf.at[slot, k], sems.at[slot]).wait()
        gates = gate_ref[...]
        f = gates[:, 4:5] * _unpack_rows(ybuf[slot, 0]) + gates[:, 5:6] * _unpack_rows(ybuf[slot, 1])
        o_ref[...] = _ln(alpha * x_ref[...] + mod_ref[0, 5:6, :] * f) * g_ref[...] + b_ref[...]

    for slot in range(2):
        pl.when(i % 2 == slot)(functools.partial(step, slot))


def _combine(y3, dest_rows, xs, rout, mod, g, b, n_lat, lat_len, n_batch, alpha, tm):
    n, d = xs.shape
    nb = n // tm
    nlb, bpb = n_lat // tm, lat_len // tm

    def grp(i):
        return jnp.where(i < nlb, i // bpb, n_batch)

    dest3 = dest_rows.reshape(nb, 1, TOP_K * tm)
    const = lambda i: (0, 0)
    return pl.pallas_call(
        functools.partial(_combine_kernel, alpha=alpha),
        grid=(nb,),
        in_specs=[pl.BlockSpec((1, 1, TOP_K * tm), lambda i: (i, 0, 0), memory_space=pltpu.SMEM),
                  pl.BlockSpec((1, 1, TOP_K * tm), lambda i: (jnp.minimum(i + 1, nb - 1), 0, 0),
                               memory_space=pltpu.SMEM),
                  pl.BlockSpec(memory_space=pl.ANY),
                  pl.BlockSpec((tm, d), lambda i: (i, 0)),
                  pl.BlockSpec((tm, LANES), lambda i: (i, 0)),
                  pl.BlockSpec((1, 6, d), lambda i: (grp(i), 0, 0)),
                  pl.BlockSpec((1, d), const), pl.BlockSpec((1, d), const)],
        out_specs=pl.BlockSpec((tm, d), lambda i: (i, 0)),
        out_shape=jax.ShapeDtypeStruct((n, d), f32),
        scratch_shapes=[pltpu.VMEM((2, TOP_K, tm, d // 2), jnp.uint32), pltpu.SemaphoreType.DMA((2,))],
        compiler_params=_cparams("arbitrary"),
        name="moe_combine",
    )(dest3, dest3, y3, xs, rout, mod, g, b)


def _moe(h, rout, routt, cnt, xs, wg, wu, wd, layer, mod, g, b, n_lat, lat_len, n_batch, alpha):
    n, d = xs.shape
    tm = routt.shape[1]
    rb = tm
    fields = routt.reshape(n // tm, 8, tm)
    eid = fields[:, 0:TOP_K, :].astype(i32)
    rank = fields[:, TOP_K:2 * TOP_K, :].astype(i32)
    counts = cnt[0, _ROUTE_LANE0:_ROUTE_LANE0 + N_EXPERTS].astype(i32)
    padded = (counts + rb - 1) // rb * rb
    pend = jnp.cumsum(padded)
    pstart = pend - padded
    dest_rows = rank
    for e in range(N_EXPERTS):
        dest_rows = dest_rows + jnp.where(eid == e, pstart[e], 0)
    dest_rows = dest_rows.reshape(-1)
    n_blocks = TOP_K * n // rb + N_EXPERTS
    n_used = (pend[-1:] // rb).astype(i32)
    ex = jnp.arange(N_EXPERTS, dtype=i32)
    row0 = (jnp.arange(n_blocks, dtype=i32) * rb)[:, None]
    owns = (pstart[None, :] <= row0) & (row0 < pend[None, :])
    used = jnp.any(owns, axis=1)
    last_active = jnp.max(jnp.where(counts > 0, ex, 0))
    block_expert = jnp.where(used, jnp.sum(jnp.where(owns, ex[None, :], 0), axis=1), last_active).astype(i32)
    first = jnp.any(owns & (pstart[None, :] == row0), axis=1).astype(i32)
    valid = jnp.sum(jnp.where(owns, jnp.clip((pstart + counts)[None, :] - row0, 0, rb), 0), axis=1)
    half = (used & (valid <= rb // 2)).astype(i32)
    later_active = (ex[None, :] > ex[:, None]) & (counts > 0)[None, :]
    nxt = jnp.min(jnp.where(later_active, ex[None, :], N_EXPERTS), axis=1)
    nxt = jnp.where(nxt == N_EXPERTS, -1, nxt).astype(i32)
    xs3 = _dispatch(h, dest_rows, counts, pstart, padded, n_blocks * rb, tm)
    y3 = _experts(xs3, n_used, block_expert, first, nxt, half, wg, wu, wd, layer, rb)
    return _combine(y3, dest_rows, xs, rout, mod, g, b, n_lat, lat_len, n_batch, alpha, tm)


def _permute_w_in(w):
    d = w.shape[0]
    o_sq, o_sk, o_sv, o_rq, o_rk, o_rv, o_gf, o_gb, o_mq, o_ckv, o_kr = (
        0, 768, 1024, 1280, 1664, 2048, 2816, 3584, 4352, 5120, 5376)
    mq = w[:, o_mq:o_mq + 768].reshape(d, MLA_HEADS, MLA_NOPE_DIM + MLA_ROPE_DIM)
    parts = [w[:, o_sq:o_sq + 768], w[:, o_rv:o_rv + 768], w[:, o_gf:o_gf + 768], w[:, o_gb:o_gb + 768],
             mq[:, :, :MLA_NOPE_DIM].reshape(d, 512), w[:, o_sk:o_sk + 256], w[:, o_sv:o_sv + 256],
             mq[:, :, MLA_NOPE_DIM:].reshape(d, 256), w[:, o_ckv:o_ckv + 256], w[:, o_rq:o_rq + 384],
             w[:, o_rk:o_rk + 384], w[:, o_kr:o_kr + 64], jnp.zeros((d, _NP - _KR - 64), w.dtype)]
    return jnp.concatenate(parts, axis=1).astype(bf16)


def kernel(x, c, ctx, c_ctx, w_ada, b_ada, w_in, swa_sink, ret_decay, mla_kv_norm, mla_w_uk, mla_w_uv, w_out, ln1_g, ln1_b, ln2_g, ln2_b, moe_w_group, moe_b_group, moe_w_expert, moe_b_expert, moe_w_gate, moe_w_up, moe_w_down):
    n_batch, lat_len, d = x.shape
    ctx_len = ctx.shape[1]
    depth = w_ada.shape[0]
    n_lat, n_ctx = n_batch * lat_len, n_batch * ctx_len
    alpha = (2 * depth) ** 0.25

    cc = jnp.zeros((8, d), f32).at[:n_batch].set(c).at[n_batch].set(c_ctx)
    mod_all = _ada(cc, w_ada, b_ada).reshape(depth, 8, 6, d)
    tab = _rope_tables(lat_len, ctx_len, _inproj_rows(lat_len, n_ctx))
    x_lat, x_ctx = x.reshape(n_lat, d), ctx.reshape(n_ctx, d)

    for l in range(depth):
        ctx_out = l < depth - 1
        mod = mod_all[l]
        p = _inproj(x_lat, x_ctx, mod, tab, _permute_w_in(w_in[l]), n_lat, lat_len, ctx_len, n_batch)
        kx, v = _mla_expand(p, mla_kv_norm[l][None, :], mla_w_uk[l].astype(bf16), mla_w_uv[l].astype(bf16))
        m = _mla_attention(p, kx, v, n_batch, lat_len, ctx_len, ctx_out)
        a = _swa_attention(p, swa_sink[l], n_batch, lat_len, ctx_len, ctx_out)
        lg = jnp.log1p(-jnp.exp2(-ret_decay[l].astype(f32)))
        yf, yb = _retention(p, lg, n_batch, lat_len, ctx_len)
        n_rows = n_lat + n_ctx if ctx_out else n_lat
        rout_w = jnp.zeros((d, LANES), f32).at[:, :N_GROUPS].set(moe_w_group[l]).at[
            :, _ROUTE_LANE0:_ROUTE_LANE0 + N_EXPERTS].set(moe_w_expert[l]).astype(bf16)
        rout_b = jnp.zeros((1, LANES), f32).at[0, :N_GROUPS].set(moe_b_group[l]).at[
            0, _ROUTE_LANE0:_ROUTE_LANE0 + N_EXPERTS].set(moe_b_expert[l])
        xs, h, rout, routt, cnt = _outproj(a, yf, yb, p, m, x_lat, x_ctx, mod, w_out[l].astype(bf16), ln1_g[l][None, :],
                                           ln1_b[l][None, :], rout_w, rout_b, n_rows, n_lat, lat_len, n_batch, alpha)
        xs = _moe(h, rout, routt, cnt, xs, moe_w_gate, moe_w_up, moe_w_down, l, mod, ln2_g[l][None, :],
                  ln2_b[l][None, :], n_lat, lat_len, n_batch, alpha)
        x_lat = x_ctx = xs
    return xs[:n_lat].reshape(n_batch, lat_len, d)
```

```python
import functools

import jax
import jax.numpy as jnp
from jax import lax
from jax.experimental import pallas as pl
from jax.experimental.pallas import tpu as pltpu

f32 = jnp.float32
bf16 = jnp.bfloat16
i32 = jnp.int32

GRID_W = 64
SWA_HEADS, SWA_KV_HEADS, SWA_HEAD_DIM, SWA_WINDOW = 6, 2, 128, 128
RET_HEADS, RET_QK_DIM, RET_V_DIM, RET_CHUNK = 6, 64, 128, 128
MLA_HEADS, MLA_NOPE_DIM, MLA_ROPE_DIM, MLA_V_DIM, MLA_KV_RANK = 4, 128, 64, 128, 256
N_GROUPS, EXPERTS_PER_GROUP, TOP_K = 4, 8, 2
N_EXPERTS = N_GROUPS * EXPERTS_PER_GROUP
ROPE_BASE = 10000.0
NORM_EPS = 1e-6
NEG_INF = -1e30
LANES = 128

_SQ, _RV, _GF, _GB, _MQN, _SK, _SV, _MQR, _CKV, _RQ, _RK, _KR = (
    0, 768, 1536, 2304, 3072, 3584, 3840, 4096, 4352, 4608, 4992, 5376)
_NP = 5504
_LOG2E = 1.4426950408889634
_SWA_SCALE = SWA_HEAD_DIM ** -0.5
_RET_SCALE = RET_QK_DIM ** -0.5
_MLA_SCALE = (MLA_NOPE_DIM + MLA_ROPE_DIM) ** -0.5 * _LOG2E
_SEGMENTS = ((_SQ, 768, 0, _SWA_SCALE), (_RV, 768, None, 1.0), (_GF, 768, None, 1.0), (_GB, 768, None, 1.0),
             (_MQN, 512, None, _MLA_SCALE), (_SK, 256, 0, 1.0), (_SV, 256, None, 1.0), (_MQR, 256, 2, _MLA_SCALE),
             (_CKV, 256, None, 1.0), (_RQ, 384, 1, _RET_SCALE), (_RK, 384, 1, 1.0), (_KR, 128, 2, 1.0))

_VMEM_LIMIT = 48 * 1024 * 1024


def _cparams(*sem):
    return pltpu.CompilerParams(dimension_semantics=sem, vmem_limit_bytes=_VMEM_LIMIT)


def _tile(n, pref):
    t = min(n, pref)
    while n % t:
        t //= 2
    return t


def _ln(x):
    mu = jnp.mean(x, axis=-1, keepdims=True)
    xc = x - mu
    var = jnp.mean(xc * xc, axis=-1, keepdims=True)
    return xc * lax.rsqrt(var + NORM_EPS)


def _silu(x):
    return x / (1.0 + jnp.exp(-x))


def _pack_rows(x):
    half = x.shape[1] // 2
    lo = lax.bitcast_convert_type(x[:, :half].astype(bf16).astype(f32), jnp.uint32)
    hi = lax.bitcast_convert_type(x[:, half:].astype(bf16).astype(f32), jnp.uint32)
    return hi | (lo >> 16)


def _unpack_rows(u):
    lo = lax.bitcast_convert_type(u << 16, f32)
    hi = lax.bitcast_convert_type(u & jnp.uint32(0xFFFF0000), f32)
    return jnp.concatenate([lo, hi], axis=1)


def _dot(a, b):
    return jnp.dot(a, b, preferred_element_type=f32)


def _dot_nt(a, b):
    return lax.dot_general(a, b, (((1,), (1,)), ((), ())), preferred_element_type=f32)


def _dot_tn(a, b):
    return lax.dot_general(a, b, (((0,), (0,)), ((), ())), preferred_element_type=f32)


def _ada_kernel(c_ref, w_ref, b_ref, o_ref):
    s = _silu(c_ref[...]).astype(bf16)
    o_ref[0] = _dot(s, w_ref[0].astype(bf16)) + b_ref[0]


def _ada(cc, w_ada, b_ada):
    depth, d, n = w_ada.shape
    tn = _tile(n, 2048)
    return pl.pallas_call(
        _ada_kernel,
        grid=(depth, n // tn),
        in_specs=[pl.BlockSpec((8, d), lambda l, j: (0, 0)),
                  pl.BlockSpec((1, d, tn), lambda l, j: (l, 0, j)),
                  pl.BlockSpec((1, 1, tn), lambda l, j: (l, 0, j))],
        out_specs=pl.BlockSpec((1, 8, tn), lambda l, j: (l, 0, j)),
        out_shape=jax.ShapeDtypeStruct((depth, 8, n), f32),
        compiler_params=_cparams("parallel", "parallel"),
        name="ada",
    )(cc, w_ada, b_ada.reshape(depth, 1, n))


_INPROJ_CHUNK = 512


def _inproj_kernel(xl_ref, xc_ref, mod_ref, tab_ref, w_ref, o_ref, *, nlb):
    tm = xl_ref.shape[0]
    h = (_ln(_stream_block(xl_ref, xc_ref, nlb)) * (1.0 + mod_ref[0, 1:2, :]) + mod_ref[0, 0:1, :]).astype(bf16)
    lane = lax.broadcasted_iota(i32, (tm, LANES), 1)
    first = {32: (lane & 63) < 32, 16: (lane & 31) < 16}
    slab_kind = {}
    for off, width, typ, scale in _SEGMENTS:
        for k in range(off // LANES, (off + width) // LANES):
            slab_kind[k] = (typ, scale)
    for c0 in range(0, _NP, _INPROJ_CHUNK):
        c1 = min(c0 + _INPROJ_CHUNK, _NP)
        acc = _dot(h, w_ref[:, c0:c1])
        for k in range(c0 // LANES, c1 // LANES):
            typ, scale = slab_kind[k]
            xk = acc[:, k * LANES - c0:(k + 1) * LANES - c0]
            if scale != 1.0:
                xk = xk * scale
            if typ is not None:
                half = 16 if typ == 2 else 32
                partner = jnp.where(first[half], pltpu.roll(xk, LANES - half, 1), pltpu.roll(xk, half, 1))
                xk = xk * tab_ref[typ, 0] + partner * tab_ref[typ, 1]
            o_ref[:, k * LANES:(k + 1) * LANES] = xk.astype(bf16)


_INPROJ_VMEM_LIMIT = 60 * 1024 * 1024


def _inproj_rows(lat_len, n_ctx):
    tm = _tile(lat_len, 512)
    while n_ctx % tm:
        tm //= 2
    return tm


def _inproj(x_lat, x_ctx, mod, tab, w, layer, n_lat, lat_len, ctx_len, n_batch):
    d = x_lat.shape[1]
    t = n_lat + n_batch * ctx_len
    tm = _inproj_rows(lat_len, t - n_lat)
    nlb, bpb = n_lat // tm, lat_len // tm
    ctx_blocks = max(1, ctx_len // tm)

    def grp(i):
        return jnp.where(i < nlb, i // bpb, n_batch)

    def posblk(i):
        return jnp.where(i < nlb, i % bpb, bpb + (i - nlb) % ctx_blocks)

    return pl.pallas_call(
        functools.partial(_inproj_kernel, nlb=None if x_ctx is x_lat else nlb),
        grid=(t // tm,),
        in_specs=_stream_specs(x_lat, x_ctx, tm, n_lat) + [
                  pl.BlockSpec((1, 6, d), lambda i: (grp(i), 0, 0)),
                  pl.BlockSpec((3, 2, tm, LANES), lambda i: (0, 0, posblk(i), 0)),
                  pl.BlockSpec((None, d, _NP), lambda i: (layer, 0, 0), pipeline_mode=pl.Buffered(1))],
        out_specs=pl.BlockSpec((tm, _NP), lambda i: (i, 0)),
        out_shape=jax.ShapeDtypeStruct((t, _NP), bf16),
        compiler_params=pltpu.CompilerParams(dimension_semantics=("parallel",), vmem_limit_bytes=_INPROJ_VMEM_LIMIT),
        name="inproj",
    )(x_lat, x_ctx, mod, tab, w)


def _rope_tables(lat_len, ctx_len, tm):
    lane = jnp.arange(LANES)
    t = jnp.arange(lat_len)
    rows = (t // GRID_W).astype(f32)
    cols = (t % GRID_W).astype(f32)

    def freq(half):
        return ROPE_BASE ** (-jnp.arange(half, dtype=f32) / half)

    def table(pos, inv, sign):
        ang = pos * inv[None, :]
        return jnp.stack([jnp.cos(ang), jnp.sin(ang) * sign[None, :]])

    def ident(n):
        return jnp.stack([jnp.ones((n, LANES), f32), jnp.zeros((n, LANES), f32)])

    sign32 = jnp.where((lane % 64) < 32, -1.0, 1.0).astype(f32)
    sign16 = jnp.where((lane % 32) < 16, -1.0, 1.0).astype(f32)
    inv32 = freq(32)[lane % 32]
    inv16 = freq(16)[lane % 16]
    pos_swa = jnp.where((lane // 64)[None, :] == 0, rows[:, None], cols[:, None])
    pos_mla = jnp.where(((lane % 64) // 32)[None, :] == 0, rows[:, None], cols[:, None])
    rep = max(1, tm // ctx_len)
    nc = rep * ctx_len
    pos_ret_l = jnp.broadcast_to((ctx_len + t).astype(f32)[:, None], (lat_len, LANES))
    pos_ret_c = jnp.broadcast_to(jnp.tile(jnp.arange(ctx_len), rep).astype(f32)[:, None], (nc, LANES))
    t0 = jnp.concatenate([table(pos_swa, inv32, sign32), ident(nc)], axis=1)
    t1 = jnp.concatenate([table(pos_ret_l, inv32, sign32), table(pos_ret_c, inv32, sign32)], axis=1)
    t2 = jnp.concatenate([table(pos_mla, inv16, sign16), ident(nc)], axis=1)
    return jnp.stack([t0, t1, t2])


def _mla_expand_kernel(ckv_ref, kr_ref, g_ref, wuk_ref, wuv_ref, k_ref, v_ref):
    c = ckv_ref[...].astype(f32)
    cn = (c * lax.rsqrt(jnp.mean(c * c, axis=-1, keepdims=True) + NORM_EPS) * g_ref[...]).astype(bf16)
    kn = _dot(cn, wuk_ref[...]).astype(bf16)
    vv = _dot(cn, wuv_ref[...]).astype(bf16)
    kr = kr_ref[...]
    for h in range(MLA_HEADS):
        k_ref[h, :, 0:128] = kn[:, h * 128:(h + 1) * 128]
        k_ref[h, :, 128:256] = kr
        v_ref[h] = vv[:, h * 128:(h + 1) * 128]


def _mla_expand(p, g, wuk, wuv):
    t = p.shape[0]
    tm = _tile(t, 1024)
    return pl.pallas_call(
        _mla_expand_kernel,
        grid=(t // tm,),
        in_specs=[pl.BlockSpec((tm, 256), lambda i: (i, _CKV // 256)),
                  pl.BlockSpec((tm, 128), lambda i: (i, _KR // 128)),
                  pl.BlockSpec((1, 256), lambda i: (0, 0)),
                  pl.BlockSpec((256, 512), lambda i: (0, 0)),
                  pl.BlockSpec((256, 512), lambda i: (0, 0))],
        out_specs=[pl.BlockSpec((MLA_HEADS, tm, 256), lambda i: (0, i, 0)),
                   pl.BlockSpec((MLA_HEADS, tm, 128), lambda i: (0, i, 0))],
        out_shape=[jax.ShapeDtypeStruct((MLA_HEADS, t, 256), bf16), jax.ShapeDtypeStruct((MLA_HEADS, t, 128), bf16)],
        compiler_params=_cparams("parallel"),
        name="mla_expand",
    )(p, p, g, wuk, wuv)


def _mla_kernel(*refs, with_lat):
    if with_lat:
        qn_ref, qr_ref, kc_ref, vc_ref, kl_ref, vl_ref, o_ref, qx_scr = refs
    else:
        qn_ref, qr_ref, kc_ref, vc_ref, _, o_ref, qx_scr = refs
    lane = lax.broadcasted_iota(i32, (qn_ref.shape[0], LANES), 1)
    for h in range(MLA_HEADS):
        slab = qr_ref[:, (h // 2) * 128:(h // 2 + 1) * 128].astype(f32)
        if h % 2:
            slab = pltpu.roll(slab, 64, 1)
        qx_scr[h, :, 0:128] = qn_ref[:, h * 128:(h + 1) * 128]
        qx_scr[h, :, 128:256] = jnp.where(lane < 64, slab, 0.0).astype(bf16)
    def scores(h):
        q = qx_scr[h]
        parts = [(_dot_nt(q, kc_ref[h]), vc_ref[h])]
        if with_lat:
            parts.append((_dot_nt(q, kl_ref[h]), vl_ref[h]))
        return parts

    nxt = scores(0)
    for h in range(MLA_HEADS):
        parts, nxt = nxt, (scores(h + 1) if h + 1 < MLA_HEADS else None)
        m = None
        for s, _ in parts:
            mx = jnp.max(s, axis=1, keepdims=True)
            m = mx if m is None else jnp.maximum(m, mx)
        den, acc = None, None
        for s, vv in parts:
            pr = jnp.exp2(s - m)
            sm = jnp.sum(pr, axis=1, keepdims=True)
            pv = _dot(pr.astype(bf16), vv)
            den = sm if den is None else den + sm
            acc = pv if acc is None else acc + pv
        o_ref[:, h * 128:(h + 1) * 128] = (acc / den).astype(bf16)


def _mla_attention(p, kx, v, n_batch, lat_len, ctx_len, compute_ctx):
    t = p.shape[0]
    tq = _tile(lat_len, 256)
    nlq = lat_len // tq
    cbase = n_batch * lat_len // ctx_len
    once = pl.Buffered(1)
    out = pl.pallas_call(
        functools.partial(_mla_kernel, with_lat=True),
        grid=(n_batch, nlq),
        in_specs=[pl.BlockSpec((tq, 512), lambda b, qi: (b * nlq + qi, _MQN // 512)),
                  pl.BlockSpec((tq, 256), lambda b, qi: (b * nlq + qi, _MQR // 256)),
                  pl.BlockSpec((MLA_HEADS, ctx_len, 256), lambda b, qi: (0, cbase + b, 0)),
                  pl.BlockSpec((MLA_HEADS, ctx_len, 128), lambda b, qi: (0, cbase + b, 0)),
                  pl.BlockSpec((MLA_HEADS, lat_len, 256), lambda b, qi: (0, b, 0), pipeline_mode=once),
                  pl.BlockSpec((MLA_HEADS, lat_len, 128), lambda b, qi: (0, b, 0), pipeline_mode=once)],
        out_specs=pl.BlockSpec((tq, 512), lambda b, qi: (b * nlq + qi, 0)),
        out_shape=jax.ShapeDtypeStruct((t, 512), bf16),
        scratch_shapes=[pltpu.VMEM((MLA_HEADS, tq, 256), bf16)],
        compiler_params=_cparams("parallel", "arbitrary"),
        name="mla_attn",
    )(p, p, kx, v, kx, v)
    if compute_ctx:
        out = pl.pallas_call(
            functools.partial(_mla_kernel, with_lat=False),
            grid=(n_batch,),
            in_specs=[pl.BlockSpec((ctx_len, 512), lambda b: (cbase + b, _MQN // 512)),
                      pl.BlockSpec((ctx_len, 256), lambda b: (cbase + b, _MQR // 256)),
                      pl.BlockSpec((MLA_HEADS, ctx_len, 256), lambda b: (0, cbase + b, 0)),
                      pl.BlockSpec((MLA_HEADS, ctx_len, 128), lambda b: (0, cbase + b, 0)),
                      pl.BlockSpec(memory_space=pl.ANY)],
            out_specs=pl.BlockSpec((ctx_len, 512), lambda b: (cbase + b, 0)),
            out_shape=jax.ShapeDtypeStruct((t, 512), bf16),
            scratch_shapes=[pltpu.VMEM((MLA_HEADS, ctx_len, 256), bf16)],
            input_output_aliases={4: 0},
            compiler_params=_cparams("parallel"),
            name="mla_ctx",
        )(p, p, kx, v, out)
    return out


def _swa_softmax_out(parts, sink_col, o_ref, g):
    m = sink_col
    for s, _ in parts:
        m = jnp.maximum(m, jnp.max(s, axis=1, keepdims=True))
    den = jnp.exp(sink_col - m)
    acc = None
    for s, vv in parts:
        pr = jnp.exp(s - m)
        den = den + jnp.sum(pr, axis=1, keepdims=True)
        pv = _dot(pr.astype(bf16), vv)
        acc = pv if acc is None else acc + pv
    o = acc / den
    nq = o.shape[0] // 3
    for j in range(3):
        o_ref[:, (3 * g + j) * 128:(3 * g + j + 1) * 128] = o[j * nq:(j + 1) * nq].astype(bf16)


def _sink_col(sink_ref, g, nq):
    row = lax.broadcasted_iota(i32, (3 * nq, 1), 0)
    return jnp.where(row < nq, sink_ref[3 * g], jnp.where(row < 2 * nq, sink_ref[3 * g + 1], sink_ref[3 * g + 2]))


def _swa_kernel(sink_ref, q_ref, kvp_ref, kvo_ref, kvn_ref, kvc_ref, o_ref):
    nkv = SWA_KV_HEADS * SWA_HEAD_DIM
    kp_ref, ko_ref, kn_ref, kc_ref = (r.at[:, 0:nkv] for r in (kvp_ref, kvo_ref, kvn_ref, kvc_ref))
    vp_ref, vo_ref, vn_ref, vc_ref = (r.at[:, nkv:2 * nkv] for r in (kvp_ref, kvo_ref, kvn_ref, kvc_ref))
    i = pl.program_id(1)
    nb = pl.num_programs(1)
    blk = q_ref.shape[0]
    q = q_ref[...]
    r = lax.broadcasted_iota(i32, (3 * blk, 3 * blk), 0) & (blk - 1)
    c = lax.broadcasted_iota(i32, (3 * blk, 3 * blk), 1)
    lo = jnp.where(i == 0, blk, 0)
    hi = jnp.where(i == nb - 1, 2 * blk, 3 * blk)
    d = c - r
    valid = (d >= 0) & (d <= 2 * SWA_WINDOW) & (c >= lo) & (c < hi)
    scores = []
    for g in range(SWA_KV_HEADS):
        hs = slice(g * 128, (g + 1) * 128)
        qg = jnp.concatenate([q[:, (3 * g + j) * 128:(3 * g + j + 1) * 128] for j in range(3)], axis=0)
        kloc = jnp.concatenate([kp_ref[:, hs], ko_ref[:, hs], kn_ref[:, hs]], axis=0)
        scores.append((_dot_nt(qg, kc_ref[:, hs]), jnp.where(valid, _dot_nt(qg, kloc), NEG_INF)))
    for g in range(SWA_KV_HEADS):
        hs = slice(g * 128, (g + 1) * 128)
        vloc = jnp.concatenate([vp_ref[:, hs], vo_ref[:, hs], vn_ref[:, hs]], axis=0)
        s_ctx, s_loc = scores[g]
        _swa_softmax_out([(s_ctx, vc_ref[:, hs]), (s_loc, vloc)], _sink_col(sink_ref, g, blk), o_ref, g)


def _swa_ctx_kernel(sink_ref, q_ref, kc_ref, vc_ref, prev_ref, o_ref):
    del prev_ref
    q = q_ref[...]
    nq = q.shape[0]
    for g in range(SWA_KV_HEADS):
        hs = slice(g * 128, (g + 1) * 128)
        qg = jnp.concatenate([q[:, (3 * g + j) * 128:(3 * g + j + 1) * 128] for j in range(3)], axis=0)
        s_ctx = _dot_nt(qg, kc_ref[:, hs])
        _swa_softmax_out([(s_ctx, vc_ref[:, hs])], _sink_col(sink_ref, g, nq), o_ref, g)


def _swa_attention(p, sink, n_batch, lat_len, ctx_len, compute_ctx):
    t = p.shape[0]
    blk = SWA_WINDOW
    nb = lat_len // blk
    cbase = n_batch * lat_len // ctx_len
    smem = pl.BlockSpec(memory_space=pltpu.SMEM)

    assert _SV == _SK + 256 and _SK % 512 == 0

    def kvspec(off):
        return pl.BlockSpec((blk, 512), lambda b, i: (b * nb + jnp.clip(i + off, 0, nb - 1), _SK // 512))

    out = pl.pallas_call(
        _swa_kernel,
        grid=(n_batch, nb),
        in_specs=[smem,
                  pl.BlockSpec((blk, 768), lambda b, i: (b * nb + i, _SQ // 768)),
                  kvspec(-1), kvspec(0), kvspec(1),
                  pl.BlockSpec((ctx_len, 512), lambda b, i: (cbase + b, _SK // 512))],
        out_specs=pl.BlockSpec((blk, 768), lambda b, i: (b * nb + i, 0)),
        out_shape=jax.ShapeDtypeStruct((t, 768), bf16),
        compiler_params=_cparams("parallel", "parallel"),
        name="swa_attn",
    )(sink, p, p, p, p, p)
    if compute_ctx:
        out = pl.pallas_call(
            _swa_ctx_kernel,
            grid=(n_batch,),
            in_specs=[smem,
                      pl.BlockSpec((ctx_len, 768), lambda b: (cbase + b, _SQ // 768)),
                      pl.BlockSpec((ctx_len, 256), lambda b: (cbase + b, _SK // 256)),
                      pl.BlockSpec((ctx_len, 256), lambda b: (cbase + b, _SV // 256)),
                      pl.BlockSpec(memory_space=pl.ANY)],
            out_specs=pl.BlockSpec((ctx_len, 768), lambda b: (cbase + b, 0)),
            out_shape=jax.ShapeDtypeStruct((t, 768), bf16),
            input_output_aliases={4: 0},
            compiler_params=_cparams("parallel"),
            name="swa_ctx",
        )(sink, p, p, p, out)
    return out


def _ret_kernel(lg_ref, qkf_ref, vf_ref, qkb_ref, vb_ref, of_ref, ob_ref,
                s_scr, d_scr, qd_scr, kd_scr, gc_scr):
    nqk = RET_HEADS * RET_QK_DIM
    qf_ref, qb_ref = qkf_ref.at[:, 0:nqk], qkb_ref.at[:, 0:nqk]
    kf_ref, kb_ref = qkf_ref.at[:, nqk:2 * nqk], qkb_ref.at[:, nqk:2 * nqk]
    i = pl.program_id(1)
    cc = RET_CHUNK

    @pl.when(i == 0)
    def _():
        s_scr[...] = jnp.zeros(s_scr.shape, f32)
        r = lax.broadcasted_iota(i32, (cc, cc), 0).astype(f32)
        c = lax.broadcasted_iota(i32, (cc, cc), 1).astype(f32)
        for direction in range(2):
            backward = direction == 1
            diff = (c - r) if backward else (r - c)
            for h in range(RET_HEADS):
                lgh = lg_ref[direction, h]
                j = direction * RET_HEADS + h
                d_scr[j] = jnp.where(diff >= 0, jnp.exp(lgh * jnp.maximum(diff, 0.0)), 0.0)
                qd_scr[j] = jnp.exp(lgh * ((cc - r) if backward else (r + 1.0)))
                kd_scr[j] = jnp.exp(lgh * (c if backward else (cc - 1.0 - c)))
                gc_scr[j] = jnp.exp(lgh * cc + jnp.zeros((cc, cc), f32))

    lane = lax.broadcasted_iota(i32, (cc, LANES), 1)
    cps = qf_ref.shape[0] // cc
    zero = jnp.zeros((cc, LANES), bf16)
    streams = ((qf_ref, kf_ref, vf_ref, of_ref), (qb_ref, kb_ref, vb_ref, ob_ref))
    for step in range(cps):
        work = []
        for pair in range(RET_HEADS // 2):
            for direction, (q_ref, k_ref, v_ref, o_ref) in enumerate(streams):
                chunk = step if direction == 0 else cps - 1 - step
                rows = slice(chunk * cc, (chunk + 1) * cc)
                q2 = q_ref[rows, pair * 128:(pair + 1) * 128].astype(f32)
                k2b = k_ref[rows, pair * 128:(pair + 1) * 128]
                k2t = jnp.transpose(k2b.astype(f32))
                ja, jb = direction * RET_HEADS + 2 * pair, direction * RET_HEADS + 2 * pair + 1
                va = v_ref[rows, 2 * pair * 128:(2 * pair + 1) * 128]
                vb = v_ref[rows, (2 * pair + 1) * 128:(2 * pair + 2) * 128]
                qa = jnp.where(lane < 64, q2, 0.0)
                qb = jnp.where(lane >= 64, q2, 0.0)
                s2 = _dot_nt(jnp.concatenate([qa, qb], axis=0).astype(bf16), k2b)
                upd = _dot(jnp.concatenate([k2t * kd_scr[ja], k2t * kd_scr[jb]], axis=0).astype(bf16),
                           jnp.concatenate([va, vb], axis=1))
                work.append((pair, o_ref, rows, ja, jb, va, vb, qa, qb, s2, upd))
        for pair, o_ref, rows, ja, jb, va, vb, qa, qb, s2, upd in work:
            lhs = jnp.concatenate([s2[0:cc] * d_scr[ja], s2[cc:2 * cc] * d_scr[jb],
                                   qa * qd_scr[ja], qb * qd_scr[jb]], axis=1).astype(bf16)
            rhs = jnp.concatenate([jnp.concatenate([va, zero], axis=1), jnp.concatenate([zero, vb], axis=1),
                                   jnp.concatenate([s_scr[ja].astype(bf16), zero], axis=1),
                                   jnp.concatenate([zero, s_scr[jb].astype(bf16)], axis=1)], axis=0)
            y2 = _dot(lhs, rhs)
            s_scr[ja] = gc_scr[ja] * s_scr[ja] + upd[0:LANES, 0:LANES]
            s_scr[jb] = gc_scr[jb] * s_scr[jb] + upd[LANES:2 * LANES, LANES:2 * LANES]
            o_ref[rows, 2 * pair * 128:(2 * pair + 1) * 128] = _ln(y2[:, 0:LANES]).astype(bf16)
            o_ref[rows, (2 * pair + 1) * 128:(2 * pair + 2) * 128] = _ln(y2[:, LANES:2 * LANES]).astype(bf16)


def _retention(p, lg, n_batch, lat_len, ctx_len):
    t = p.shape[0]
    cc = RET_CHUNK
    cps = 2 if (ctx_len // cc) % 2 == 0 and (lat_len // cc) % 2 == 0 else 1
    blk = cps * cc
    ncc, nlc = ctx_len // blk, lat_len // blk
    cbase = n_batch * nlc
    smem = pl.BlockSpec(memory_space=pltpu.SMEM)
    scratch = [pltpu.VMEM((2 * RET_HEADS, cc, cc), f32) for _ in range(5)]

    def fwd_row(b, i):
        return jnp.where(i < ncc, cbase + b * ncc + i, b * nlc + (i - ncc))

    def bwd_row(b, i):
        return jnp.where(i < ncc, cbase + b * ncc + (ncc - 1 - i), b * nlc + (nlc - 1 - (i - ncc)))

    assert _RK == _RQ + 384 and _RQ % 768 == 0

    def specs(row):
        return [pl.BlockSpec((blk, 768), lambda b, i: (row(b, i), _RQ // 768)),
                pl.BlockSpec((blk, 768), lambda b, i: (row(b, i), _RV // 768))]

    return pl.pallas_call(
        _ret_kernel,
        grid=(n_batch, ncc + nlc),
        in_specs=[smem] + specs(fwd_row) + specs(bwd_row),
        out_specs=[pl.BlockSpec((blk, 768), lambda b, i: (fwd_row(b, i), 0)),
                   pl.BlockSpec((blk, 768), lambda b, i: (bwd_row(b, i), 0))],
        out_shape=[jax.ShapeDtypeStruct((t, 768), bf16), jax.ShapeDtypeStruct((t, 768), bf16)],
        scratch_shapes=scratch,
        compiler_params=_cparams("parallel", "arbitrary"),
        name="retention",
    )(lg, p, p, p, p)


def _stream_specs(x_lat, x_ctx, tm, n_lat):
    d = x_lat.shape[1]
    nlb = n_lat // tm
    if x_ctx is x_lat:
        return [pl.BlockSpec((tm, d), lambda i: (i, 0)), pl.BlockSpec((8, d), lambda i: (0, 0))]
    return [pl.BlockSpec((tm, d), lambda i: (jnp.minimum(i, nlb - 1), 0)),
            pl.BlockSpec((tm, d), lambda i: (jnp.maximum(i - nlb, 0), 0))]


def _stream_block(xl_ref, xc_ref, nlb, rows=slice(None)):
    if nlb is None:
        return xl_ref[rows, :]
    return jnp.where(pl.program_id(0) < nlb, xl_ref[rows, :], xc_ref[rows, :])


_ROUTE_ROWS = 256


def _outproj_kernel(a_ref, yf_ref, yb_ref, gf_ref, gb_ref, m_ref, xl_ref, xc_ref, mod_ref, w_ref, g_ref, b_ref,
                    rw_ref, rb_ref, xo_ref, h_ref, rout_ref, routt_ref, cnt_ref, carry_scr, *, alpha, nlb, sub):
    @pl.when(pl.program_id(0) == 0)
    def _():
        carry_scr[...] = jnp.zeros(carry_scr.shape, f32)

    nsub = a_ref.shape[0] // sub
    ys = []
    for sb in range(nsub):
        rows = slice(sb * sub, (sb + 1) * sub)
        ret = (_silu(gf_ref[rows, :].astype(f32)) * yf_ref[rows, :].astype(f32)
               + _silu(gb_ref[rows, :].astype(f32)) * yb_ref[rows, :].astype(f32)).astype(bf16)
        ys.append(_dot(jnp.concatenate([a_ref[rows, :], ret, m_ref[rows, :]], axis=1), w_ref[...]))
    for sb in range(nsub):
        rows = slice(sb * sub, (sb + 1) * sub)
        xn = (_ln(alpha * _stream_block(xl_ref, xc_ref, nlb, rows) + mod_ref[0, 2:3, :] * ys[sb]) * g_ref[...]
              + b_ref[...])
        xo_ref[rows, :] = xn
        h = _ln(xn) * (1.0 + mod_ref[0, 4:5, :]) + mod_ref[0, 3:4, :]
        h_ref[rows, :] = _pack_rows(h)
        table = _route(_dot(h.astype(bf16), rw_ref[...]) + rb_ref[...], carry_scr)
        rout_ref[rows, :] = table
        routt_ref[sb * 8:(sb + 1) * 8, :] = jnp.transpose(table)[0:8, :]
    cnt_ref[...] = carry_scr[...]


def _outproj(a, yf, yb, p, m, x_lat, x_ctx, mod, w, g, b, rw, rb, n_rows, n_lat, lat_len, n_batch, alpha):
    d = x_lat.shape[1]
    tm = _tile(lat_len, 2 * _ROUTE_ROWS)
    while (n_rows - n_lat) % tm:
        tm //= 2
    sub = min(tm, _ROUTE_ROWS)
    nlb, bpb = n_lat // tm, lat_len // tm

    def grp(i):
        return jnp.where(i < nlb, i // bpb, n_batch)

    row = lambda i: (i, 0)
    const = lambda i: (0, 0)
    return pl.pallas_call(
        functools.partial(_outproj_kernel, alpha=alpha, nlb=None if x_ctx is x_lat else nlb, sub=sub),
        grid=(n_rows // tm,),
        in_specs=[pl.BlockSpec((tm, 768), row), pl.BlockSpec((tm, 768), row), pl.BlockSpec((tm, 768), row),
                  pl.BlockSpec((tm, 768), lambda i: (i, _GF // 768)), pl.BlockSpec((tm, 768), lambda i: (i, _GB // 768)),
                  pl.BlockSpec((tm, 512), row)] + _stream_specs(x_lat, x_ctx, tm, n_lat) + [
                  pl.BlockSpec((1, 6, d), lambda i: (grp(i), 0, 0)),
                  pl.BlockSpec(w.shape, const, pipeline_mode=pl.Buffered(1)),
                  pl.BlockSpec((1, d), const), pl.BlockSpec((1, d), const),
                  pl.BlockSpec((d, LANES), const), pl.BlockSpec((1, LANES), const)],
        out_specs=[pl.BlockSpec((tm, d), row), pl.BlockSpec((tm, d // 2), row),
                   pl.BlockSpec((tm, LANES), row), pl.BlockSpec((tm // sub * 8, sub), row),
                   pl.BlockSpec((8, LANES), const)],
        out_shape=[jax.ShapeDtypeStruct((n_rows, d), f32), jax.ShapeDtypeStruct((n_rows, d // 2), jnp.uint32),
                   jax.ShapeDtypeStruct((n_rows, LANES), f32), jax.ShapeDtypeStruct((n_rows // sub * 8, sub), f32),
                   jax.ShapeDtypeStruct((8, LANES), f32)],
        scratch_shapes=[pltpu.VMEM((8, LANES), f32)],
        compiler_params=pltpu.CompilerParams(dimension_semantics=("arbitrary",), vmem_limit_bytes=_INPROJ_VMEM_LIMIT),
        name="outproj",
    )(a, yf, yb, p, p, m, x_lat, x_ctx, mod, w, g, b, rw, rb)


_ROUTE_LANE0 = N_GROUPS


def _route(logits, carry_scr):
    tm = logits.shape[0]
    lane = lax.broadcasted_iota(i32, (tm, LANES), 1)
    lane_f = lane.astype(f32)
    big = float(2 * LANES)
    gl = jnp.where(lane < N_GROUPS, logits, -jnp.inf)
    gmax = jnp.max(gl, axis=1, keepdims=True)
    gidx = jnp.min(jnp.where(gl == gmax, lane_f, big), axis=1, keepdims=True)
    p_group = 1.0 / jnp.sum(jnp.exp(gl - gmax), axis=1, keepdims=True)
    egroup = ((lane - _ROUTE_LANE0) >> 3).astype(f32)
    in_group = (lane >= _ROUTE_LANE0) & (lane < _ROUTE_LANE0 + N_EXPERTS) & (egroup == gidx)
    ev = jnp.where(in_group, logits, -jnp.inf)
    e1 = jnp.max(ev, axis=1, keepdims=True)
    i1 = jnp.min(jnp.where(ev == e1, lane_f, big), axis=1, keepdims=True)
    ev2 = jnp.where(lane_f == i1, -jnp.inf, ev)
    e2 = jnp.max(ev2, axis=1, keepdims=True)
    i2 = jnp.min(jnp.where(ev2 == e2, lane_f, big), axis=1, keepdims=True)
    tt = jnp.exp(e2 - e1)
    w1 = p_group / (1.0 + tt)
    w2 = p_group * tt / (1.0 + tt)

    hit1 = lane_f == i1
    hit2 = lane_f == i2
    onehot = jnp.where(hit1, 1.0, jnp.where(hit2, 1.0, 0.0))
    rr = lax.broadcasted_iota(i32, (tm, tm), 0)
    cc = lax.broadcasted_iota(i32, (tm, tm), 1)
    lower = jnp.where(cc < rr, 1.0, 0.0).astype(bf16)
    before = _dot(lower, onehot.astype(bf16)) + carry_scr[0:1, :]
    rank1 = jnp.sum(jnp.where(hit1, before, 0.0), axis=1, keepdims=True)
    rank2 = jnp.sum(jnp.where(hit2, before, 0.0), axis=1, keepdims=True)
    carry_scr[0:1, :] = carry_scr[0:1, :] + jnp.sum(onehot, axis=0, keepdims=True)

    return jnp.where(lane == 0, i1 - _ROUTE_LANE0, jnp.where(lane == 1, i2 - _ROUTE_LANE0, jnp.where(
        lane == 2, rank1, jnp.where(lane == 3, rank2, jnp.where(lane == 4, w1, jnp.where(lane == 5, w2, 0.0))))))


def _dispatch_kernel(cnt_ref, pstart_ref, padded_ref, dest_ref, h_ref, o_hbm, sem):
    i = pl.program_id(0)
    nb = pl.num_programs(0) - 1
    tm = dest_ref.shape[2] // TOP_K

    def row_copy(src_row, dst_row):
        return pltpu.make_async_copy(h_ref.at[pl.ds(src_row, 1), :], o_hbm.at[dst_row], sem)

    @pl.when(i < nb)
    def _():
        def issue(r8, carry):
            base = pl.multiple_of(r8 * 8, 8)
            for j in range(8):
                for k in range(TOP_K):
                    row_copy(base + j, dest_ref[0, 0, k * tm + base + j]).start()
            return carry

        lax.fori_loop(0, tm // 8, issue, 0)
        for _ in range(TOP_K):
            pltpu.make_async_copy(h_ref, o_hbm.at[pl.ds(0, tm), 0], sem).wait()

    @pl.when(i == nb)
    def _():
        def per_expert(e, total):
            lo, hi = cnt_ref[e], padded_ref[e]

            def fill(r, carry):
                row_copy(0, pstart_ref[e] + r).start()
                return carry

            lax.fori_loop(lo, hi, fill, 0)
            return total + (hi - lo)

        total = lax.fori_loop(0, N_EXPERTS, per_expert, 0)

        def drain(r, carry):
            row_copy(0, 0).wait()
            return carry

        lax.fori_loop(0, total, drain, 0)


def _dispatch(h, dest_rows, counts, pstart, padded, n_rows, tm):
    n, d = h.shape
    nb = n // tm
    grid_spec = pltpu.PrefetchScalarGridSpec(
        num_scalar_prefetch=3,
        grid=(nb + 1,),
        in_specs=[pl.BlockSpec((1, 1, TOP_K * tm), lambda i, *_: (jnp.minimum(i, nb - 1), 0, 0),
                               memory_space=pltpu.SMEM),
                  pl.BlockSpec((tm, d), lambda i, *_: (jnp.minimum(i, nb - 1), 0))],
        out_specs=pl.BlockSpec(memory_space=pl.ANY),
        scratch_shapes=[pltpu.SemaphoreType.DMA(())],
    )
    return pl.pallas_call(
        _dispatch_kernel,
        grid_spec=grid_spec,
        out_shape=jax.ShapeDtypeStruct((n_rows, 1, d), h.dtype),
        compiler_params=_cparams("arbitrary"),
        name="moe_dispatch",
    )(counts, pstart, padded, dest_rows.reshape(nb, 1, TOP_K * tm), h)


_CAST_ROWS = 256


def _expert_kernel(nused_ref, be_ref, first_ref, nxt_ref, half_ref, x_hbm, wg_hbm, wu_hbm, wd_hbm, y_hbm,
                   wg_f, wu_f, wd_f, wg_b, wu_b, wd_b, xbuf, ybuf, sems, xsems, ysems, *, layer):
    i = pl.program_id(0)
    n_used = nused_ref[0]
    used = i < n_used
    rb = xbuf.shape[1]
    slot = i % 2
    staged = ((wg_hbm, wg_f, wg_b), (wu_hbm, wu_f, wu_b), (wd_hbm, wd_f, wd_b))

    def x_fetch(blk, s):
        return pltpu.make_async_copy(x_hbm.at[pl.ds(pl.multiple_of(blk * rb, rb), rb), 0], xbuf.at[s], xsems.at[s])

    def y_store(blk, s):
        return pltpu.make_async_copy(ybuf.at[s], y_hbm.at[pl.ds(pl.multiple_of(blk * rb, rb), rb), 0], ysems.at[s])

    @pl.when(i == 0)
    def _():
        x_fetch(0, 0).start()

    @pl.when(i + 1 < n_used)
    def _():
        x_fetch(i + 1, 1 - slot).start()

    def fetch(e):
        return [pltpu.make_async_copy(hbm.at[layer, e], stage, sems.at[j]) for j, (hbm, stage, _) in enumerate(staged)]

    @pl.when(i == 0)
    def _():
        for cp in fetch(be_ref[0]):
            cp.start()

    @pl.when(used & (first_ref[i] == 1))
    def _():
        for cp in fetch(be_ref[i]):
            cp.wait()
        for _, stage, dst in staged:
            rows = stage.shape[0]
            step = min(_CAST_ROWS, rows)

            def cast(c, carry, stage=stage, dst=dst, step=step):
                sl = pl.ds(pl.multiple_of(c * step, step), step)
                dst[sl, :] = stage[sl, :].astype(bf16)
                return carry

            lax.fori_loop(0, rows // step, cast, 0)
        nxt = nxt_ref[be_ref[i]]

        @pl.when(nxt >= 0)
        def _():
            for cp in fetch(nxt):
                cp.start()

    @pl.when(used)
    def _():
        x_fetch(i, slot).wait()

    @pl.when(used & (i >= 2))
    def _():
        y_store(i - 2, slot).wait()

    def ffn(rows):
        x = _unpack_rows(xbuf[slot, 0:rows, :]).astype(bf16)
        act = (_silu(_dot(x, wg_b[...])) * _dot(x, wu_b[...])).astype(bf16)
        ybuf[slot, 0:rows, :] = _pack_rows(_dot(act, wd_b[...]))
        if rows < rb:
            ybuf[slot, rows:rb, :] = jnp.zeros((rb - rows, ybuf.shape[2]), ybuf.dtype)

    @pl.when(used & (half_ref[i] == 0))
    def _():
        ffn(rb)

    @pl.when(used & (half_ref[i] == 1))
    def _():
        ffn(rb // 2)

    @pl.when(used)
    def _():
        y_store(i, slot).start()

    @pl.when(i == n_used - 1)
    def _():
        y_store(i, slot).wait()

        @pl.when(i >= 1)
        def _():
            y_store(i - 1, 1 - slot).wait()


_EXPERT_VMEM_LIMIT = 60 * 1024 * 1024


def _experts(xs3, n_used, block_expert, first, nxt, half, wg, wu, wd, layer, rb):
    d, hid = wg.shape[2], wg.shape[3]
    n_blocks = xs3.shape[0] // rb

    anyspace = pl.BlockSpec(memory_space=pl.ANY)
    grid_spec = pltpu.PrefetchScalarGridSpec(
        num_scalar_prefetch=5,
        grid=(n_blocks,),
        in_specs=[anyspace, anyspace, anyspace, anyspace],
        out_specs=anyspace,
        scratch_shapes=[pltpu.VMEM((d, hid), f32), pltpu.VMEM((d, hid), f32), pltpu.VMEM((hid, d), f32),
                        pltpu.VMEM((d, hid), bf16), pltpu.VMEM((d, hid), bf16), pltpu.VMEM((hid, d), bf16),
                        pltpu.VMEM((2, rb, d // 2), jnp.uint32), pltpu.VMEM((2, rb, d // 2), jnp.uint32),
                        pltpu.SemaphoreType.DMA((3,)), pltpu.SemaphoreType.DMA((2,)), pltpu.SemaphoreType.DMA((2,))],
    )
    return pl.pallas_call(
        functools.partial(_expert_kernel, layer=layer),
        grid_spec=grid_spec,
        out_shape=jax.ShapeDtypeStruct(xs3.shape, xs3.dtype),
        compiler_params=pltpu.CompilerParams(dimension_semantics=("arbitrary",), vmem_limit_bytes=_EXPERT_VMEM_LIMIT),
        name="moe_experts",
    )(n_used, block_expert, first, nxt, half, xs3, wg, wu, wd)


def _combine_kernel(dcur_ref, dnxt_ref, y_hbm, x_ref, gate_ref, mod_ref, g_ref, b_ref, o_ref, ybuf, sems, *, alpha):
    i = pl.program_id(0)
    nb = pl.num_programs(0)
    tm = x_ref.shape[0]

    def issue(dref, slot):
        def body(r8, carry):
            base = pl.multiple_of(r8 * 8, 8)
            for j in range(8):
                for k in range(TOP_K):
                    pltpu.make_async_copy(y_hbm.at[dref[0, 0, k * tm + base + j]],
                                          ybuf.at[slot, k, pl.ds(base + j, 1), :], sems.at[slot]).start()
            return carry

        lax.fori_loop(0, tm // 8, body, 0)

    @pl.when(i == 0)
    def _():
        issue(dcur_ref, 0)

    def step(slot):
        @pl.when(i + 1 < nb)
        def _():
            issue(dnxt_ref, 1 - slot)

        for k in range(TOP_K):
            pltpu.make_async_copy(y_hbm.at[pl.ds(0, tm), 0], ybuf.at[slot, k], sems.at[slot]).wait()
        gates = gate_ref[...]
        f = gates[:, 4:5] * _unpack_rows(ybuf[slot, 0]) + gates[:, 5:6] * _unpack_rows(ybuf[slot, 1])
        o_ref[...] = _ln(alpha * x_ref[...] + mod_ref[0, 5:6, :] * f) * g_ref[...] + b_ref[...]

    for slot in range(2):
        pl.when(i % 2 == slot)(functools.partial(step, slot))


def _combine(y3, dest_rows, xs, rout, mod, g, b, n_lat, lat_len, n_batch, alpha, tm):
    n, d = xs.shape
    nb = n // tm
    nlb, bpb = n_lat // tm, lat_len // tm

    def grp(i):
        return jnp.where(i < nlb, i // bpb, n_batch)

    dest3 = dest_rows.reshape(nb, 1, TOP_K * tm)
    const = lambda i: (0, 0)
    return pl.pallas_call(
        functools.partial(_combine_kernel, alpha=alpha),
        grid=(nb,),
        in_specs=[pl.BlockSpec((1, 1, TOP_K * tm), lambda i: (i, 0, 0), memory_space=pltpu.SMEM),
                  pl.BlockSpec((1, 1, TOP_K * tm), lambda i: (jnp.minimum(i + 1, nb - 1), 0, 0),
                               memory_space=pltpu.SMEM),
                  pl.BlockSpec(memory_space=pl.ANY),
                  pl.BlockSpec((tm, d), lambda i: (i, 0)),
                  pl.BlockSpec((tm, LANES), lambda i: (i, 0)),
                  pl.BlockSpec((1, 6, d), lambda i: (grp(i), 0, 0)),
                  pl.BlockSpec((1, d), const), pl.BlockSpec((1, d), const)],
        out_specs=pl.BlockSpec((tm, d), lambda i: (i, 0)),
        out_shape=jax.ShapeDtypeStruct((n, d), f32),
        scratch_shapes=[pltpu.VMEM((2, TOP_K, tm, d // 2), jnp.uint32), pltpu.SemaphoreType.DMA((2,))],
        compiler_params=_cparams("arbitrary"),
        name="moe_combine",
    )(dest3, dest3, y3, xs, rout, mod, g, b)


def _moe(h, rout, routt, cnt, xs, wg, wu, wd, layer, mod, g, b, n_lat, lat_len, n_batch, alpha):
    n, d = xs.shape
    tm = routt.shape[1]
    rb = tm
    fields = routt.reshape(n // tm, 8, tm)
    eid = fields[:, 0:TOP_K, :].astype(i32)
    rank = fields[:, TOP_K:2 * TOP_K, :].astype(i32)
    counts = cnt[0, _ROUTE_LANE0:_ROUTE_LANE0 + N_EXPERTS].astype(i32)
    padded = (counts + rb - 1) // rb * rb
    pend = jnp.cumsum(padded)
    pstart = pend - padded
    dest_rows = rank
    for e in range(N_EXPERTS):
        dest_rows = dest_rows + jnp.where(eid == e, pstart[e], 0)
    dest_rows = dest_rows.reshape(-1)
    n_blocks = TOP_K * n // rb + N_EXPERTS
    n_used = (pend[-1:] // rb).astype(i32)
    ex = jnp.arange(N_EXPERTS, dtype=i32)
    row0 = (jnp.arange(n_blocks, dtype=i32) * rb)[:, None]
    owns = (pstart[None, :] <= row0) & (row0 < pend[None, :])
    used = jnp.any(owns, axis=1)
    last_active = jnp.max(jnp.where(counts > 0, ex, 0))
    block_expert = jnp.where(used, jnp.sum(jnp.where(owns, ex[None, :], 0), axis=1), last_active).astype(i32)
    first = jnp.any(owns & (pstart[None, :] == row0), axis=1).astype(i32)
    valid = jnp.sum(jnp.where(owns, jnp.clip((pstart + counts)[None, :] - row0, 0, rb), 0), axis=1)
    half = (used & (valid <= rb // 2)).astype(i32)
    later_active = (ex[None, :] > ex[:, None]) & (counts > 0)[None, :]
    nxt = jnp.min(jnp.where(later_active, ex[None, :], N_EXPERTS), axis=1)
    nxt = jnp.where(nxt == N_EXPERTS, -1, nxt).astype(i32)
    xs3 = _dispatch(h, dest_rows, counts, pstart, padded, n_blocks * rb, tm)
    y3 = _experts(xs3, n_used, block_expert, first, nxt, half, wg, wu, wd, layer, rb)
    return _combine(y3, dest_rows, xs, rout, mod, g, b, n_lat, lat_len, n_batch, alpha, tm)


def _permute_w_in(w):
    depth, d = w.shape[0], w.shape[1]
    o_sq, o_sk, o_sv, o_rq, o_rk, o_rv, o_gf, o_gb, o_mq, o_ckv, o_kr = (
        0, 768, 1024, 1280, 1664, 2048, 2816, 3584, 4352, 5120, 5376)
    mq = w[..., o_mq:o_mq + 768].reshape(depth, d, MLA_HEADS, MLA_NOPE_DIM + MLA_ROPE_DIM)
    parts = [w[..., o_sq:o_sq + 768], w[..., o_rv:o_rv + 768], w[..., o_gf:o_gf + 768], w[..., o_gb:o_gb + 768],
             mq[..., :MLA_NOPE_DIM].reshape(depth, d, 512), w[..., o_sk:o_sk + 256], w[..., o_sv:o_sv + 256],
             mq[..., MLA_NOPE_DIM:].reshape(depth, d, 256), w[..., o_ckv:o_ckv + 256], w[..., o_rq:o_rq + 384],
             w[..., o_rk:o_rk + 384], w[..., o_kr:o_kr + 64], jnp.zeros((depth, d, _NP - _KR - 64), w.dtype)]
    return jnp.concatenate(parts, axis=-1).astype(bf16)


def kernel(x, c, ctx, c_ctx, w_ada, b_ada, w_in, swa_sink, ret_decay, mla_kv_norm, mla_w_uk, mla_w_uv, w_out, ln1_g, ln1_b, ln2_g, ln2_b, moe_w_group, moe_b_group, moe_w_expert, moe_b_expert, moe_w_gate, moe_w_up, moe_w_down):
    n_batch, lat_len, d = x.shape
    ctx_len = ctx.shape[1]
    depth = w_ada.shape[0]
    n_lat, n_ctx = n_batch * lat_len, n_batch * ctx_len
    alpha = (2 * depth) ** 0.25

    cc = jnp.zeros((8, d), f32).at[:n_batch].set(c).at[n_batch].set(c_ctx)
    mod_all = _ada(cc, w_ada, b_ada).reshape(depth, 8, 6, d)
    tab = _rope_tables(lat_len, ctx_len, _inproj_rows(lat_len, n_ctx))
    x_lat, x_ctx = x.reshape(n_lat, d), ctx.reshape(n_ctx, d)
    w_in_perm = _permute_w_in(w_in)

    for l in range(depth):
        ctx_out = l < depth - 1
        mod = mod_all[l]
        p = _inproj(x_lat, x_ctx, mod, tab, w_in_perm, l, n_lat, lat_len, ctx_len, n_batch)
        kx, v = _mla_expand(p, mla_kv_norm[l][None, :], mla_w_uk[l].astype(bf16), mla_w_uv[l].astype(bf16))
        m = _mla_attention(p, kx, v, n_batch, lat_len, ctx_len, ctx_out)
        a = _swa_attention(p, swa_sink[l], n_batch, lat_len, ctx_len, ctx_out)
        lg = jnp.log1p(-jnp.exp2(-ret_decay[l].astype(f32)))
        yf, yb = _retention(p, lg, n_batch, lat_len, ctx_len)
        n_rows = n_lat + n_ctx if ctx_out else n_lat
        rout_w = jnp.zeros((d, LANES), f32).at[:, :N_GROUPS].set(moe_w_group[l]).at[
            :, _ROUTE_LANE0:_ROUTE_LANE0 + N_EXPERTS].set(moe_w_expert[l]).astype(bf16)
        rout_b = jnp.zeros((1, LANES), f32).at[0, :N_GROUPS].set(moe_b_group[l]).at[
            0, _ROUTE_LANE0:_ROUTE_LANE0 + N_EXPERTS].set(moe_b_expert[l])
        xs, h, rout, routt, cnt = _outproj(a, yf, yb, p, m, x_lat, x_ctx, mod, w_out[l].astype(bf16), ln1_g[l][None, :],
                                           ln1_b[l][None, :], rout_w, rout_b, n_rows, n_lat, lat_len, n_batch, alpha)
        xs = _moe(h, rout, routt, cnt, xs, moe_w_gate, moe_w_up, moe_w_down, l, mod, ln2_g[l][None, :],
                  ln2_b[l][None, :], n_lat, lat_len, n_batch, alpha)
        x_lat = x_ctx = xs
    return xs[:n_lat].reshape(n_batch, lat_len, d)
```

```python
import functools

import jax
import jax.numpy as jnp
from jax import lax
from jax.experimental import pallas as pl
from jax.experimental.pallas import tpu as pltpu

f32 = jnp.float32
bf16 = jnp.bfloat16
i32 = jnp.int32

GRID_W = 64
SWA_HEADS, SWA_KV_HEADS, SWA_HEAD_DIM, SWA_WINDOW = 6, 2, 128, 128
RET_HEADS, RET_QK_DIM, RET_V_DIM, RET_CHUNK = 6, 64, 128, 128
MLA_HEADS, MLA_NOPE_DIM, MLA_ROPE_DIM, MLA_V_DIM, MLA_KV_RANK = 4, 128, 64, 128, 256
N_GROUPS, EXPERTS_PER_GROUP, TOP_K = 4, 8, 2
N_EXPERTS = N_GROUPS * EXPERTS_PER_GROUP
ROPE_BASE = 10000.0
NORM_EPS = 1e-6
NEG_INF = -1e30
LANES = 128

_SQ, _RV, _GF, _GB, _MQN, _SK, _SV, _MQR, _CKV, _RQ, _RK, _KR = (
    0, 768, 1536, 2304, 3072, 3584, 3840, 4096, 4352, 4608, 4992, 5376)
_NP = 5504
_LOG2E = 1.4426950408889634
_SWA_SCALE = SWA_HEAD_DIM ** -0.5
_RET_SCALE = RET_QK_DIM ** -0.5
_MLA_SCALE = (MLA_NOPE_DIM + MLA_ROPE_DIM) ** -0.5 * _LOG2E
_SEGMENTS = ((_SQ, 768, 0, _SWA_SCALE), (_RV, 768, None, 1.0), (_GF, 768, None, 1.0), (_GB, 768, None, 1.0),
             (_MQN, 512, None, _MLA_SCALE), (_SK, 256, 0, 1.0), (_SV, 256, None, 1.0), (_MQR, 256, 2, _MLA_SCALE),
             (_CKV, 256, None, 1.0), (_RQ, 384, 1, _RET_SCALE), (_RK, 384, 1, 1.0), (_KR, 128, 2, 1.0))

_VMEM_LIMIT = 48 * 1024 * 1024


def _cparams(*sem):
    return pltpu.CompilerParams(dimension_semantics=sem, vmem_limit_bytes=_VMEM_LIMIT)


def _tile(n, pref):
    t = min(n, pref)
    while n % t:
        t //= 2
    return t


def _ln(x):
    mu = jnp.mean(x, axis=-1, keepdims=True)
    xc = x - mu
    var = jnp.mean(xc * xc, axis=-1, keepdims=True)
    return xc * lax.rsqrt(var + NORM_EPS)


def _silu(x):
    return x / (1.0 + jnp.exp(-x))


def _pack_rows(x):
    half = x.shape[1] // 2
    lo = lax.bitcast_convert_type(x[:, :half].astype(bf16).astype(f32), jnp.uint32)
    hi = lax.bitcast_convert_type(x[:, half:].astype(bf16).astype(f32), jnp.uint32)
    return hi | (lo >> 16)


def _unpack_rows(u):
    lo = lax.bitcast_convert_type(u << 16, f32)
    hi = lax.bitcast_convert_type(u & jnp.uint32(0xFFFF0000), f32)
    return jnp.concatenate([lo, hi], axis=1)


def _dot(a, b):
    return jnp.dot(a, b, preferred_element_type=f32)


def _dot_nt(a, b):
    return lax.dot_general(a, b, (((1,), (1,)), ((), ())), preferred_element_type=f32)


def _dot_tn(a, b):
    return lax.dot_general(a, b, (((0,), (0,)), ((), ())), preferred_element_type=f32)


def _ada_kernel(c_ref, w_ref, b_ref, o_ref):
    s = _silu(c_ref[...]).astype(bf16)
    o_ref[0] = _dot(s, w_ref[0].astype(bf16)) + b_ref[0]


def _ada(cc, w_ada, b_ada):
    depth, d, n = w_ada.shape
    tn = _tile(n, 2048)
    return pl.pallas_call(
        _ada_kernel,
        grid=(depth, n // tn),
        in_specs=[pl.BlockSpec((8, d), lambda l, j: (0, 0)),
                  pl.BlockSpec((1, d, tn), lambda l, j: (l, 0, j)),
                  pl.BlockSpec((1, 1, tn), lambda l, j: (l, 0, j))],
        out_specs=pl.BlockSpec((1, 8, tn), lambda l, j: (l, 0, j)),
        out_shape=jax.ShapeDtypeStruct((depth, 8, n), f32),
        compiler_params=_cparams("parallel", "parallel"),
        name="ada",
    )(cc, w_ada, b_ada.reshape(depth, 1, n))


_INPROJ_CHUNK = 512


def _inproj_kernel(xl_ref, xc_ref, mod_ref, tab_ref, w_ref, o_ref, *, nlb):
    tm = xl_ref.shape[0]
    h = (_ln(_stream_block(xl_ref, xc_ref, nlb)) * (1.0 + mod_ref[0, 1:2, :]) + mod_ref[0, 0:1, :]).astype(bf16)
    lane = lax.broadcasted_iota(i32, (tm, LANES), 1)
    first = {32: (lane & 63) < 32, 16: (lane & 31) < 16}
    slab_kind = {}
    for off, width, typ, scale in _SEGMENTS:
        for k in range(off // LANES, (off + width) // LANES):
            slab_kind[k] = (typ, scale)
    for c0 in range(0, _NP, _INPROJ_CHUNK):
        c1 = min(c0 + _INPROJ_CHUNK, _NP)
        acc = _dot(h, w_ref[:, c0:c1])
        for k in range(c0 // LANES, c1 // LANES):
            typ, scale = slab_kind[k]
            xk = acc[:, k * LANES - c0:(k + 1) * LANES - c0]
            if scale != 1.0:
                xk = xk * scale
            if typ is not None:
                half = 16 if typ == 2 else 32
                partner = jnp.where(first[half], pltpu.roll(xk, LANES - half, 1), pltpu.roll(xk, half, 1))
                xk = xk * tab_ref[typ, 0] + partner * tab_ref[typ, 1]
            o_ref[:, k * LANES:(k + 1) * LANES] = xk.astype(bf16)


_INPROJ_VMEM_LIMIT = 60 * 1024 * 1024


def _inproj_rows(lat_len, n_ctx):
    tm = _tile(lat_len, 512)
    while n_ctx % tm:
        tm //= 2
    return tm


def _inproj(x_lat, x_ctx, mod, tab, w, layer, n_lat, lat_len, ctx_len, n_batch):
    d = x_lat.shape[1]
    t = n_lat + n_batch * ctx_len
    tm = _inproj_rows(lat_len, t - n_lat)
    nlb, bpb = n_lat // tm, lat_len // tm
    ctx_blocks = max(1, ctx_len // tm)

    def grp(i):
        return jnp.where(i < nlb, i // bpb, n_batch)

    def posblk(i):
        return jnp.where(i < nlb, i % bpb, bpb + (i - nlb) % ctx_blocks)

    return pl.pallas_call(
        functools.partial(_inproj_kernel, nlb=None if x_ctx is x_lat else nlb),
        grid=(t // tm,),
        in_specs=_stream_specs(x_lat, x_ctx, tm, n_lat) + [
                  pl.BlockSpec((1, 6, d), lambda i: (grp(i), 0, 0)),
                  pl.BlockSpec((3, 2, tm, LANES), lambda i: (0, 0, posblk(i), 0)),
                  pl.BlockSpec((None, d, _NP), lambda i: (layer, 0, 0), pipeline_mode=pl.Buffered(1))],
        out_specs=pl.BlockSpec((tm, _NP), lambda i: (i, 0)),
        out_shape=jax.ShapeDtypeStruct((t, _NP), bf16),
        compiler_params=pltpu.CompilerParams(dimension_semantics=("parallel",), vmem_limit_bytes=_INPROJ_VMEM_LIMIT),
        name="inproj",
    )(x_lat, x_ctx, mod, tab, w)


def _rope_tables(lat_len, ctx_len, tm):
    lane = jnp.arange(LANES)
    t = jnp.arange(lat_len)
    rows = (t // GRID_W).astype(f32)
    cols = (t % GRID_W).astype(f32)

    def freq(half):
        return ROPE_BASE ** (-jnp.arange(half, dtype=f32) / half)

    def table(pos, inv, sign):
        ang = pos * inv[None, :]
        return jnp.stack([jnp.cos(ang), jnp.sin(ang) * sign[None, :]])

    def ident(n):
        return jnp.stack([jnp.ones((n, LANES), f32), jnp.zeros((n, LANES), f32)])

    sign32 = jnp.where((lane % 64) < 32, -1.0, 1.0).astype(f32)
    sign16 = jnp.where((lane % 32) < 16, -1.0, 1.0).astype(f32)
    inv32 = freq(32)[lane % 32]
    inv16 = freq(16)[lane % 16]
    pos_swa = jnp.where((lane // 64)[None, :] == 0, rows[:, None], cols[:, None])
    pos_mla = jnp.where(((lane % 64) // 32)[None, :] == 0, rows[:, None], cols[:, None])
    rep = max(1, tm // ctx_len)
    nc = rep * ctx_len
    pos_ret_l = jnp.broadcast_to((ctx_len + t).astype(f32)[:, None], (lat_len, LANES))
    pos_ret_c = jnp.broadcast_to(jnp.tile(jnp.arange(ctx_len), rep).astype(f32)[:, None], (nc, LANES))
    t0 = jnp.concatenate([table(pos_swa, inv32, sign32), ident(nc)], axis=1)
    t1 = jnp.concatenate([table(pos_ret_l, inv32, sign32), table(pos_ret_c, inv32, sign32)], axis=1)
    t2 = jnp.concatenate([table(pos_mla, inv16, sign16), ident(nc)], axis=1)
    return jnp.stack([t0, t1, t2])


def _mla_expand_kernel(ckv_ref, kr_ref, g_ref, wuk_ref, wuv_ref, k_ref, v_ref):
    c = ckv_ref[...].astype(f32)
    cn = (c * lax.rsqrt(jnp.mean(c * c, axis=-1, keepdims=True) + NORM_EPS) * g_ref[...]).astype(bf16)
    kn = _dot(cn, wuk_ref[...]).astype(bf16)
    vv = _dot(cn, wuv_ref[...]).astype(bf16)
    kr = kr_ref[...]
    for h in range(MLA_HEADS):
        k_ref[h, :, 0:128] = kn[:, h * 128:(h + 1) * 128]
        k_ref[h, :, 128:256] = kr
        v_ref[h] = vv[:, h * 128:(h + 1) * 128]


def _mla_expand(p, g, wuk, wuv):
    t = p.shape[0]
    tm = _tile(t, 1024)
    return pl.pallas_call(
        _mla_expand_kernel,
        grid=(t // tm,),
        in_specs=[pl.BlockSpec((tm, 256), lambda i: (i, _CKV // 256)),
                  pl.BlockSpec((tm, 128), lambda i: (i, _KR // 128)),
                  pl.BlockSpec((1, 256), lambda i: (0, 0)),
                  pl.BlockSpec((256, 512), lambda i: (0, 0)),
                  pl.BlockSpec((256, 512), lambda i: (0, 0))],
        out_specs=[pl.BlockSpec((MLA_HEADS, tm, 256), lambda i: (0, i, 0)),
                   pl.BlockSpec((MLA_HEADS, tm, 128), lambda i: (0, i, 0))],
        out_shape=[jax.ShapeDtypeStruct((MLA_HEADS, t, 256), bf16), jax.ShapeDtypeStruct((MLA_HEADS, t, 128), bf16)],
        compiler_params=_cparams("parallel"),
        name="mla_expand",
    )(p, p, g, wuk, wuv)


def _mla_kernel(*refs, with_lat):
    if with_lat:
        qn_ref, qr_ref, kc_ref, vc_ref, kl_ref, vl_ref, o_ref, qx_scr = refs
    else:
        qn_ref, qr_ref, kc_ref, vc_ref, _, o_ref, qx_scr = refs
    lane = lax.broadcasted_iota(i32, (qn_ref.shape[0], LANES), 1)
    for h in range(MLA_HEADS):
        slab = qr_ref[:, (h // 2) * 128:(h // 2 + 1) * 128].astype(f32)
        if h % 2:
            slab = pltpu.roll(slab, 64, 1)
        qx_scr[h, :, 0:128] = qn_ref[:, h * 128:(h + 1) * 128]
        qx_scr[h, :, 128:256] = jnp.where(lane < 64, slab, 0.0).astype(bf16)
    def scores(h):
        q = qx_scr[h]
        parts = [(_dot_nt(q, kc_ref[h]), vc_ref[h])]
        if with_lat:
            parts.append((_dot_nt(q, kl_ref[h]), vl_ref[h]))
        return parts

    nxt = scores(0)
    for h in range(MLA_HEADS):
        parts, nxt = nxt, (scores(h + 1) if h + 1 < MLA_HEADS else None)
        m = None
        for s, _ in parts:
            mx = jnp.max(s, axis=1, keepdims=True)
            m = mx if m is None else jnp.maximum(m, mx)
        den, acc = None, None
        for s, vv in parts:
            pr = jnp.exp2(s - m)
            sm = jnp.sum(pr, axis=1, keepdims=True)
            pv = _dot(pr.astype(bf16), vv)
            den = sm if den is None else den + sm
            acc = pv if acc is None else acc + pv
        o_ref[:, h * 128:(h + 1) * 128] = (acc / den).astype(bf16)


def _mla_attention(p, kx, v, n_batch, lat_len, ctx_len, compute_ctx):
    t = p.shape[0]
    tq = _tile(lat_len, 256)
    nlq = lat_len // tq
    cbase = n_batch * lat_len // ctx_len
    once = pl.Buffered(1)
    out = pl.pallas_call(
        functools.partial(_mla_kernel, with_lat=True),
        grid=(n_batch, nlq),
        in_specs=[pl.BlockSpec((tq, 512), lambda b, qi: (b * nlq + qi, _MQN // 512)),
                  pl.BlockSpec((tq, 256), lambda b, qi: (b * nlq + qi, _MQR // 256)),
                  pl.BlockSpec((MLA_HEADS, ctx_len, 256), lambda b, qi: (0, cbase + b, 0)),
                  pl.BlockSpec((MLA_HEADS, ctx_len, 128), lambda b, qi: (0, cbase + b, 0)),
                  pl.BlockSpec((MLA_HEADS, lat_len, 256), lambda b, qi: (0, b, 0), pipeline_mode=once),
                  pl.BlockSpec((MLA_HEADS, lat_len, 128), lambda b, qi: (0, b, 0), pipeline_mode=once)],
        out_specs=pl.BlockSpec((tq, 512), lambda b, qi: (b * nlq + qi, 0)),
        out_shape=jax.ShapeDtypeStruct((t, 512), bf16),
        scratch_shapes=[pltpu.VMEM((MLA_HEADS, tq, 256), bf16)],
        compiler_params=_cparams("parallel", "arbitrary"),
        name="mla_attn",
    )(p, p, kx, v, kx, v)
    if compute_ctx:
        out = pl.pallas_call(
            functools.partial(_mla_kernel, with_lat=False),
            grid=(n_batch,),
            in_specs=[pl.BlockSpec((ctx_len, 512), lambda b: (cbase + b, _MQN // 512)),
                      pl.BlockSpec((ctx_len, 256), lambda b: (cbase + b, _MQR // 256)),
                      pl.BlockSpec((MLA_HEADS, ctx_len, 256), lambda b: (0, cbase + b, 0)),
                      pl.BlockSpec((MLA_HEADS, ctx_len, 128), lambda b: (0, cbase + b, 0)),
                      pl.BlockSpec(memory_space=pl.ANY)],
            out_specs=pl.BlockSpec((ctx_len, 512), lambda b: (cbase + b, 0)),
            out_shape=jax.ShapeDtypeStruct((t, 512), bf16),
            scratch_shapes=[pltpu.VMEM((MLA_HEADS, ctx_len, 256), bf16)],
            input_output_aliases={4: 0},
            compiler_params=_cparams("parallel"),
            name="mla_ctx",
        )(p, p, kx, v, out)
    return out


def _swa_softmax_out(parts, sink_col, o_ref, g):
    m = sink_col
    for s, _ in parts:
        m = jnp.maximum(m, jnp.max(s, axis=1, keepdims=True))
    den = jnp.exp(sink_col - m)
    acc = None
    for s, vv in parts:
        pr = jnp.exp(s - m)
        den = den + jnp.sum(pr, axis=1, keepdims=True)
        pv = _dot(pr.astype(bf16), vv)
        acc = pv if acc is None else acc + pv
    o = acc / den
    nq = o.shape[0] // 3
    for j in range(3):
        o_ref[:, (3 * g + j) * 128:(3 * g + j + 1) * 128] = o[j * nq:(j + 1) * nq].astype(bf16)


def _sink_col(sink_ref, g, nq):
    row = lax.broadcasted_iota(i32, (3 * nq, 1), 0)
    return jnp.where(row < nq, sink_ref[3 * g], jnp.where(row < 2 * nq, sink_ref[3 * g + 1], sink_ref[3 * g + 2]))


def _swa_kernel(sink_ref, q_ref, kvp_ref, kvo_ref, kvn_ref, kvc_ref, o_ref):
    nkv = SWA_KV_HEADS * SWA_HEAD_DIM
    kp_ref, ko_ref, kn_ref, kc_ref = (r.at[:, 0:nkv] for r in (kvp_ref, kvo_ref, kvn_ref, kvc_ref))
    vp_ref, vo_ref, vn_ref, vc_ref = (r.at[:, nkv:2 * nkv] for r in (kvp_ref, kvo_ref, kvn_ref, kvc_ref))
    i = pl.program_id(1)
    nb = pl.num_programs(1)
    blk = q_ref.shape[0]
    q = q_ref[...]
    r = lax.broadcasted_iota(i32, (3 * blk, 3 * blk), 0) & (blk - 1)
    c = lax.broadcasted_iota(i32, (3 * blk, 3 * blk), 1)
    lo = jnp.where(i == 0, blk, 0)
    hi = jnp.where(i == nb - 1, 2 * blk, 3 * blk)
    d = c - r
    valid = (d >= 0) & (d <= 2 * SWA_WINDOW) & (c >= lo) & (c < hi)
    scores = []
    for g in range(SWA_KV_HEADS):
        hs = slice(g * 128, (g + 1) * 128)
        qg = jnp.concatenate([q[:, (3 * g + j) * 128:(3 * g + j + 1) * 128] for j in range(3)], axis=0)
        kloc = jnp.concatenate([kp_ref[:, hs], ko_ref[:, hs], kn_ref[:, hs]], axis=0)
        scores.append((_dot_nt(qg, kc_ref[:, hs]), jnp.where(valid, _dot_nt(qg, kloc), NEG_INF)))
    for g in range(SWA_KV_HEADS):
        hs = slice(g * 128, (g + 1) * 128)
        vloc = jnp.concatenate([vp_ref[:, hs], vo_ref[:, hs], vn_ref[:, hs]], axis=0)
        s_ctx, s_loc = scores[g]
        _swa_softmax_out([(s_ctx, vc_ref[:, hs]), (s_loc, vloc)], _sink_col(sink_ref, g, blk), o_ref, g)


def _swa_ctx_kernel(sink_ref, q_ref, kc_ref, vc_ref, prev_ref, o_ref):
    del prev_ref
    q = q_ref[...]
    nq = q.shape[0]
    for g in range(SWA_KV_HEADS):
        hs = slice(g * 128, (g + 1) * 128)
        qg = jnp.concatenate([q[:, (3 * g + j) * 128:(3 * g + j + 1) * 128] for j in range(3)], axis=0)
        s_ctx = _dot_nt(qg, kc_ref[:, hs])
        _swa_softmax_out([(s_ctx, vc_ref[:, hs])], _sink_col(sink_ref, g, nq), o_ref, g)


def _swa_attention(p, sink, n_batch, lat_len, ctx_len, compute_ctx):
    t = p.shape[0]
    blk = SWA_WINDOW
    nb = lat_len // blk
    cbase = n_batch * lat_len // ctx_len
    smem = pl.BlockSpec(memory_space=pltpu.SMEM)

    assert _SV == _SK + 256 and _SK % 512 == 0

    def kvspec(off):
        return pl.BlockSpec((blk, 512), lambda b, i: (b * nb + jnp.clip(i + off, 0, nb - 1), _SK // 512))

    out = pl.pallas_call(
        _swa_kernel,
        grid=(n_batch, nb),
        in_specs=[smem,
                  pl.BlockSpec((blk, 768), lambda b, i: (b * nb + i, _SQ // 768)),
                  kvspec(-1), kvspec(0), kvspec(1),
                  pl.BlockSpec((ctx_len, 512), lambda b, i: (cbase + b, _SK // 512))],
        out_specs=pl.BlockSpec((blk, 768), lambda b, i: (b * nb + i, 0)),
        out_shape=jax.ShapeDtypeStruct((t, 768), bf16),
        compiler_params=_cparams("parallel", "parallel"),
        name="swa_attn",
    )(sink, p, p, p, p, p)
    if compute_ctx:
        out = pl.pallas_call(
            _swa_ctx_kernel,
            grid=(n_batch,),
            in_specs=[smem,
                      pl.BlockSpec((ctx_len, 768), lambda b: (cbase + b, _SQ // 768)),
                      pl.BlockSpec((ctx_len, 256), lambda b: (cbase + b, _SK // 256)),
                      pl.BlockSpec((ctx_len, 256), lambda b: (cbase + b, _SV // 256)),
                      pl.BlockSpec(memory_space=pl.ANY)],
            out_specs=pl.BlockSpec((ctx_len, 768), lambda b: (cbase + b, 0)),
            out_shape=jax.ShapeDtypeStruct((t, 768), bf16),
            input_output_aliases={4: 0},
            compiler_params=_cparams("parallel"),
            name="swa_ctx",
        )(sink, p, p, p, out)
    return out


def _ret_kernel(lg_ref, qkf_ref, vf_ref, qkb_ref, vb_ref, of_ref, ob_ref,
                s_scr, d_scr, qd_scr, kd_scr, gc_scr):
    nqk = RET_HEADS * RET_QK_DIM
    qf_ref, qb_ref = qkf_ref.at[:, 0:nqk], qkb_ref.at[:, 0:nqk]
    kf_ref, kb_ref = qkf_ref.at[:, nqk:2 * nqk], qkb_ref.at[:, nqk:2 * nqk]
    i = pl.program_id(1)
    cc = RET_CHUNK

    @pl.when(i == 0)
    def _():
        s_scr[...] = jnp.zeros(s_scr.shape, f32)
        r = lax.broadcasted_iota(i32, (cc, cc), 0).astype(f32)
        c = lax.broadcasted_iota(i32, (cc, cc), 1).astype(f32)
        for direction in range(2):
            backward = direction == 1
            diff = (c - r) if backward else (r - c)
            for h in range(RET_HEADS):
                lgh = lg_ref[direction, h]
                j = direction * RET_HEADS + h
                d_scr[j] = jnp.where(diff >= 0, jnp.exp(lgh * jnp.maximum(diff, 0.0)), 0.0)
                qd_scr[j] = jnp.exp(lgh * ((cc - r) if backward else (r + 1.0)))
                kd_scr[j] = jnp.exp(lgh * (c if backward else (cc - 1.0 - c)))
                gc_scr[j] = jnp.exp(lgh * cc + jnp.zeros((cc, cc), f32))

    lane = lax.broadcasted_iota(i32, (cc, LANES), 1)
    cps = qf_ref.shape[0] // cc
    zero = jnp.zeros((cc, LANES), bf16)
    streams = ((qf_ref, kf_ref, vf_ref, of_ref), (qb_ref, kb_ref, vb_ref, ob_ref))
    for step in range(cps):
        work = []
        for pair in range(RET_HEADS // 2):
            for direction, (q_ref, k_ref, v_ref, o_ref) in enumerate(streams):
                chunk = step if direction == 0 else cps - 1 - step
                rows = slice(chunk * cc, (chunk + 1) * cc)
                q2 = q_ref[rows, pair * 128:(pair + 1) * 128].astype(f32)
                k2b = k_ref[rows, pair * 128:(pair + 1) * 128]
                k2t = jnp.transpose(k2b.astype(f32))
                ja, jb = direction * RET_HEADS + 2 * pair, direction * RET_HEADS + 2 * pair + 1
                va = v_ref[rows, 2 * pair * 128:(2 * pair + 1) * 128]
                vb = v_ref[rows, (2 * pair + 1) * 128:(2 * pair + 2) * 128]
                qa = jnp.where(lane < 64, q2, 0.0)
                qb = jnp.where(lane >= 64, q2, 0.0)
                s2 = _dot_nt(jnp.concatenate([qa, qb], axis=0).astype(bf16), k2b)
                upd = _dot(jnp.concatenate([k2t * kd_scr[ja], k2t * kd_scr[jb]], axis=0).astype(bf16),
                           jnp.concatenate([va, vb], axis=1))
                work.append((pair, o_ref, rows, ja, jb, va, vb, qa, qb, s2, upd))
        for pair, o_ref, rows, ja, jb, va, vb, qa, qb, s2, upd in work:
            lhs = jnp.concatenate([s2[0:cc] * d_scr[ja], s2[cc:2 * cc] * d_scr[jb],
                                   qa * qd_scr[ja], qb * qd_scr[jb]], axis=1).astype(bf16)
            rhs = jnp.concatenate([jnp.concatenate([va, zero], axis=1), jnp.concatenate([zero, vb], axis=1),
                                   jnp.concatenate([s_scr[ja].astype(bf16), zero], axis=1),
                                   jnp.concatenate([zero, s_scr[jb].astype(bf16)], axis=1)], axis=0)
            y2 = _dot(lhs, rhs)
            s_scr[ja] = gc_scr[ja] * s_scr[ja] + upd[0:LANES, 0:LANES]
            s_scr[jb] = gc_scr[jb] * s_scr[jb] + upd[LANES:2 * LANES, LANES:2 * LANES]
            o_ref[rows, 2 * pair * 128:(2 * pair + 1) * 128] = _ln(y2[:, 0:LANES]).astype(bf16)
            o_ref[rows, (2 * pair + 1) * 128:(2 * pair + 2) * 128] = _ln(y2[:, LANES:2 * LANES]).astype(bf16)


def _retention(p, lg, n_batch, lat_len, ctx_len):
    t = p.shape[0]
    cc = RET_CHUNK
    cps = 2 if (ctx_len // cc) % 2 == 0 and (lat_len // cc) % 2 == 0 else 1
    blk = cps * cc
    ncc, nlc = ctx_len // blk, lat_len // blk
    cbase = n_batch * nlc
    smem = pl.BlockSpec(memory_space=pltpu.SMEM)
    scratch = [pltpu.VMEM((2 * RET_HEADS, cc, cc), f32) for _ in range(5)]

    def fwd_row(b, i):
        return jnp.where(i < ncc, cbase + b * ncc + i, b * nlc + (i - ncc))

    def bwd_row(b, i):
        return jnp.where(i < ncc, cbase + b * ncc + (ncc - 1 - i), b * nlc + (nlc - 1 - (i - ncc)))

    assert _RK == _RQ + 384 and _RQ % 768 == 0

    def specs(row):
        return [pl.BlockSpec((blk, 768), lambda b, i: (row(b, i), _RQ // 768)),
                pl.BlockSpec((blk, 768), lambda b, i: (row(b, i), _RV // 768))]

    return pl.pallas_call(
        _ret_kernel,
        grid=(n_batch, ncc + nlc),
        in_specs=[smem] + specs(fwd_row) + specs(bwd_row),
        out_specs=[pl.BlockSpec((blk, 768), lambda b, i: (fwd_row(b, i), 0)),
                   pl.BlockSpec((blk, 768), lambda b, i: (bwd_row(b, i), 0))],
        out_shape=[jax.ShapeDtypeStruct((t, 768), bf16), jax.ShapeDtypeStruct((t, 768), bf16)],
        scratch_shapes=scratch,
        compiler_params=_cparams("parallel", "arbitrary"),
        name="retention",
    )(lg, p, p, p, p)


def _stream_specs(x_lat, x_ctx, tm, n_lat):
    d = x_lat.shape[1]
    nlb = n_lat // tm
    if x_ctx is x_lat:
        return [pl.BlockSpec((tm, d), lambda i: (i, 0)), pl.BlockSpec((8, d), lambda i: (0, 0))]
    return [pl.BlockSpec((tm, d), lambda i: (jnp.minimum(i, nlb - 1), 0)),
            pl.BlockSpec((tm, d), lambda i: (jnp.maximum(i - nlb, 0), 0))]


def _stream_block(xl_ref, xc_ref, nlb, rows=slice(None)):
    if nlb is None:
        return xl_ref[rows, :]
    return jnp.where(pl.program_id(0) < nlb, xl_ref[rows, :], xc_ref[rows, :])


_ROUTE_ROWS = 256


def _outproj_kernel(a_ref, yf_ref, yb_ref, gf_ref, gb_ref, m_ref, xl_ref, xc_ref, mod_ref, w_ref, g_ref, b_ref,
                    rw_ref, rb_ref, xo_ref, h_ref, rout_ref, routt_ref, cnt_ref, carry_scr, *, alpha, nlb, sub):
    @pl.when(pl.program_id(0) == 0)
    def _():
        carry_scr[...] = jnp.zeros(carry_scr.shape, f32)

    nsub = a_ref.shape[0] // sub
    ys = []
    for sb in range(nsub):
        rows = slice(sb * sub, (sb + 1) * sub)
        ret = (_silu(gf_ref[rows, :].astype(f32)) * yf_ref[rows, :].astype(f32)
               + _silu(gb_ref[rows, :].astype(f32)) * yb_ref[rows, :].astype(f32)).astype(bf16)
        ys.append(_dot(jnp.concatenate([a_ref[rows, :], ret, m_ref[rows, :]], axis=1), w_ref[...]))
    for sb in range(nsub):
        rows = slice(sb * sub, (sb + 1) * sub)
        xn = (_ln(alpha * _stream_block(xl_ref, xc_ref, nlb, rows) + mod_ref[0, 2:3, :] * ys[sb]) * g_ref[...]
              + b_ref[...])
        xo_ref[rows, :] = xn
        h = _ln(xn) * (1.0 + mod_ref[0, 4:5, :]) + mod_ref[0, 3:4, :]
        h_ref[rows, :] = _pack_rows(h)
        table = _route(_dot(h.astype(bf16), rw_ref[...]) + rb_ref[...], carry_scr)
        rout_ref[rows, :] = table
        routt_ref[sb * 8:(sb + 1) * 8, :] = jnp.transpose(table)[0:8, :]
    cnt_ref[...] = carry_scr[...]


def _outproj(a, yf, yb, p, m, x_lat, x_ctx, mod, w, g, b, rw, rb, n_rows, n_lat, lat_len, n_batch, alpha):
    d = x_lat.shape[1]
    tm = _tile(lat_len, 2 * _ROUTE_ROWS)
    while (n_rows - n_lat) % tm:
        tm //= 2
    sub = min(tm, _ROUTE_ROWS)
    nlb, bpb = n_lat // tm, lat_len // tm

    def grp(i):
        return jnp.where(i < nlb, i // bpb, n_batch)

    row = lambda i: (i, 0)
    const = lambda i: (0, 0)
    return pl.pallas_call(
        functools.partial(_outproj_kernel, alpha=alpha, nlb=None if x_ctx is x_lat else nlb, sub=sub),
        grid=(n_rows // tm,),
        in_specs=[pl.BlockSpec((tm, 768), row), pl.BlockSpec((tm, 768), row), pl.BlockSpec((tm, 768), row),
                  pl.BlockSpec((tm, 768), lambda i: (i, _GF // 768)), pl.BlockSpec((tm, 768), lambda i: (i, _GB // 768)),
                  pl.BlockSpec((tm, 512), row)] + _stream_specs(x_lat, x_ctx, tm, n_lat) + [
                  pl.BlockSpec((1, 6, d), lambda i: (grp(i), 0, 0)),
                  pl.BlockSpec(w.shape, const, pipeline_mode=pl.Buffered(1)),
                  pl.BlockSpec((1, d), const), pl.BlockSpec((1, d), const),
                  pl.BlockSpec((d, LANES), const), pl.BlockSpec((1, LANES), const)],
        out_specs=[pl.BlockSpec((tm, d), row), pl.BlockSpec((tm, d // 2), row),
                   pl.BlockSpec((tm, LANES), row), pl.BlockSpec((tm // sub * 8, sub), row),
                   pl.BlockSpec((8, LANES), const)],
        out_shape=[jax.ShapeDtypeStruct((n_rows, d), f32), jax.ShapeDtypeStruct((n_rows, d // 2), jnp.uint32),
                   jax.ShapeDtypeStruct((n_rows, LANES), f32), jax.ShapeDtypeStruct((n_rows // sub * 8, sub), f32),
                   jax.ShapeDtypeStruct((8, LANES), f32)],
        scratch_shapes=[pltpu.VMEM((8, LANES), f32)],
        compiler_params=pltpu.CompilerParams(dimension_semantics=("arbitrary",), vmem_limit_bytes=_INPROJ_VMEM_LIMIT),
        name="outproj",
    )(a, yf, yb, p, p, m, x_lat, x_ctx, mod, w, g, b, rw, rb)


_ROUTE_LANE0 = N_GROUPS


def _route(logits, carry_scr):
    tm = logits.shape[0]
    lane = lax.broadcasted_iota(i32, (tm, LANES), 1)
    lane_f = lane.astype(f32)
    big = float(2 * LANES)
    gl = jnp.where(lane < N_GROUPS, logits, -jnp.inf)
    gmax = jnp.max(gl, axis=1, keepdims=True)
    gidx = jnp.min(jnp.where(gl == gmax, lane_f, big), axis=1, keepdims=True)
    p_group = 1.0 / jnp.sum(jnp.exp(gl - gmax), axis=1, keepdims=True)
    egroup = ((lane - _ROUTE_LANE0) >> 3).astype(f32)
    in_group = (lane >= _ROUTE_LANE0) & (lane < _ROUTE_LANE0 + N_EXPERTS) & (egroup == gidx)
    ev = jnp.where(in_group, logits, -jnp.inf)
    e1 = jnp.max(ev, axis=1, keepdims=True)
    i1 = jnp.min(jnp.where(ev == e1, lane_f, big), axis=1, keepdims=True)
    ev2 = jnp.where(lane_f == i1, -jnp.inf, ev)
    e2 = jnp.max(ev2, axis=1, keepdims=True)
    i2 = jnp.min(jnp.where(ev2 == e2, lane_f, big), axis=1, keepdims=True)
    tt = jnp.exp(e2 - e1)
    w1 = p_group / (1.0 + tt)
    w2 = p_group * tt / (1.0 + tt)

    hit1 = lane_f == i1
    hit2 = lane_f == i2
    onehot = jnp.where(hit1, 1.0, jnp.where(hit2, 1.0, 0.0))
    rr = lax.broadcasted_iota(i32, (tm, tm), 0)
    cc = lax.broadcasted_iota(i32, (tm, tm), 1)
    lower = jnp.where(cc < rr, 1.0, 0.0).astype(bf16)
    before = _dot(lower, onehot.astype(bf16)) + carry_scr[0:1, :]
    rank1 = jnp.sum(jnp.where(hit1, before, 0.0), axis=1, keepdims=True)
    rank2 = jnp.sum(jnp.where(hit2, before, 0.0), axis=1, keepdims=True)
    carry_scr[0:1, :] = carry_scr[0:1, :] + jnp.sum(onehot, axis=0, keepdims=True)

    return jnp.where(lane == 0, i1 - _ROUTE_LANE0, jnp.where(lane == 1, i2 - _ROUTE_LANE0, jnp.where(
        lane == 2, rank1, jnp.where(lane == 3, rank2, jnp.where(lane == 4, w1, jnp.where(lane == 5, w2, 0.0))))))


def _dispatch_kernel(cnt_ref, pstart_ref, padded_ref, dest_ref, h_ref, o_hbm, sem):
    i = pl.program_id(0)
    nb = pl.num_programs(0) - 1
    tm = dest_ref.shape[2] // TOP_K

    def row_copy(src_row, dst_row):
        return pltpu.make_async_copy(h_ref.at[pl.ds(src_row, 1), :], o_hbm.at[dst_row], sem)

    @pl.when(i < nb)
    def _():
        def issue(r8, carry):
            base = pl.multiple_of(r8 * 8, 8)
            for j in range(8):
                for k in range(TOP_K):
                    row_copy(base + j, dest_ref[0, 0, k * tm + base + j]).start(priority=(j * TOP_K + k) % 2)
            return carry

        lax.fori_loop(0, tm // 8, issue, 0)
        for _ in range(TOP_K):
            pltpu.make_async_copy(h_ref, o_hbm.at[pl.ds(0, tm), 0], sem).wait()

    @pl.when(i == nb)
    def _():
        def per_expert(e, total):
            lo, hi = cnt_ref[e], padded_ref[e]

            def fill(r, carry):
                row_copy(0, pstart_ref[e] + r).start()
                return carry

            lax.fori_loop(lo, hi, fill, 0)
            return total + (hi - lo)

        total = lax.fori_loop(0, N_EXPERTS, per_expert, 0)

        def drain(r, carry):
            row_copy(0, 0).wait()
            return carry

        lax.fori_loop(0, total, drain, 0)


def _dispatch(h, dest_rows, counts, pstart, padded, n_rows, tm):
    n, d = h.shape
    nb = n // tm
    grid_spec = pltpu.PrefetchScalarGridSpec(
        num_scalar_prefetch=3,
        grid=(nb + 1,),
        in_specs=[pl.BlockSpec((1, 1, TOP_K * tm), lambda i, *_: (jnp.minimum(i, nb - 1), 0, 0),
                               memory_space=pltpu.SMEM),
                  pl.BlockSpec((tm, d), lambda i, *_: (jnp.minimum(i, nb - 1), 0))],
        out_specs=pl.BlockSpec(memory_space=pl.ANY),
        scratch_shapes=[pltpu.SemaphoreType.DMA(())],
    )
    return pl.pallas_call(
        _dispatch_kernel,
        grid_spec=grid_spec,
        out_shape=jax.ShapeDtypeStruct((n_rows, 1, d), h.dtype),
        compiler_params=_cparams("arbitrary"),
        name="moe_dispatch",
    )(counts, pstart, padded, dest_rows.reshape(nb, 1, TOP_K * tm), h)


_CAST_ROWS = 256


def _expert_kernel(nused_ref, be_ref, first_ref, nxt_ref, half_ref, x_hbm, wg_hbm, wu_hbm, wd_hbm, y_hbm,
                   wg_f, wu_f, wd_f, wg_b, wu_b, wd_b, xbuf, ybuf, sems, xsems, ysems, *, layer):
    i = pl.program_id(0)
    n_used = nused_ref[0]
    used = i < n_used
    rb = xbuf.shape[1]
    slot = i % 2
    staged = ((wg_hbm, wg_f, wg_b), (wu_hbm, wu_f, wu_b), (wd_hbm, wd_f, wd_b))

    def x_fetch(blk, s):
        return pltpu.make_async_copy(x_hbm.at[pl.ds(pl.multiple_of(blk * rb, rb), rb), 0], xbuf.at[s], xsems.at[s])

    def y_store(blk, s):
        return pltpu.make_async_copy(ybuf.at[s], y_hbm.at[pl.ds(pl.multiple_of(blk * rb, rb), rb), 0], ysems.at[s])

    @pl.when(i == 0)
    def _():
        x_fetch(0, 0).start()

    @pl.when(i + 1 < n_used)
    def _():
        x_fetch(i + 1, 1 - slot).start()

    def fetch(e):
        return [pltpu.make_async_copy(hbm.at[layer, e], stage, sems.at[j]) for j, (hbm, stage, _) in enumerate(staged)]

    @pl.when(i == 0)
    def _():
        for cp in fetch(be_ref[0]):
            cp.start()

    @pl.when(used & (first_ref[i] == 1))
    def _():
        for cp in fetch(be_ref[i]):
            cp.wait()
        for _, stage, dst in staged:
            rows = stage.shape[0]
            step = min(_CAST_ROWS, rows)

            def cast(c, carry, stage=stage, dst=dst, step=step):
                sl = pl.ds(pl.multiple_of(c * step, step), step)
                dst[sl, :] = stage[sl, :].astype(bf16)
                return carry

            lax.fori_loop(0, rows // step, cast, 0)
        nxt = nxt_ref[be_ref[i]]

        @pl.when(nxt >= 0)
        def _():
            for cp in fetch(nxt):
                cp.start()

    @pl.when(used)
    def _():
        x_fetch(i, slot).wait()

    @pl.when(used & (i >= 2))
    def _():
        y_store(i - 2, slot).wait()

    def ffn(rows):
        x = _unpack_rows(xbuf[slot, 0:rows, :]).astype(bf16)
        act = (_silu(_dot(x, wg_b[...])) * _dot(x, wu_b[...])).astype(bf16)
        ybuf[slot, 0:rows, :] = _pack_rows(_dot(act, wd_b[...]))
        if rows < rb:
            ybuf[slot, rows:rb, :] = jnp.zeros((rb - rows, ybuf.shape[2]), ybuf.dtype)

    @pl.when(used & (half_ref[i] == 0))
    def _():
        ffn(rb)

    @pl.when(used & (half_ref[i] == 1))
    def _():
        ffn(rb // 2)

    @pl.when(used)
    def _():
        y_store(i, slot).start()

    @pl.when(i == n_used - 1)
    def _():
        y_store(i, slot).wait()

        @pl.when(i >= 1)
        def _():
            y_store(i - 1, 1 - slot).wait()


_EXPERT_VMEM_LIMIT = 60 * 1024 * 1024


def _experts(xs3, n_used, block_expert, first, nxt, half, wg, wu, wd, layer, rb):
    d, hid = wg.shape[2], wg.shape[3]
    n_blocks = xs3.shape[0] // rb

    anyspace = pl.BlockSpec(memory_space=pl.ANY)
    grid_spec = pltpu.PrefetchScalarGridSpec(
        num_scalar_prefetch=5,
        grid=(n_blocks,),
        in_specs=[anyspace, anyspace, anyspace, anyspace],
        out_specs=anyspace,
        scratch_shapes=[pltpu.VMEM((d, hid), f32), pltpu.VMEM((d, hid), f32), pltpu.VMEM((hid, d), f32),
                        pltpu.VMEM((d, hid), bf16), pltpu.VMEM((d, hid), bf16), pltpu.VMEM((hid, d), bf16),
                        pltpu.VMEM((2, rb, d // 2), jnp.uint32), pltpu.VMEM((2, rb, d // 2), jnp.uint32),
                        pltpu.SemaphoreType.DMA((3,)), pltpu.SemaphoreType.DMA((2,)), pltpu.SemaphoreType.DMA((2,))],
    )
    return pl.pallas_call(
        functools.partial(_expert_kernel, layer=layer),
        grid_spec=grid_spec,
        out_shape=jax.ShapeDtypeStruct(xs3.shape, xs3.dtype),
        compiler_params=pltpu.CompilerParams(dimension_semantics=("arbitrary",), vmem_limit_bytes=_EXPERT_VMEM_LIMIT),
        name="moe_experts",
    )(n_used, block_expert, first, nxt, half, xs3, wg, wu, wd)


def _combine_kernel(dcur_ref, dnxt_ref, y_hbm, x_ref, gate_ref, mod_ref, g_ref, b_ref, o_ref, ybuf, sems, *, alpha):
    i = pl.program_id(0)
    nb = pl.num_programs(0)
    tm = x_ref.shape[0]

    def issue(dref, slot):
        def body(r8, carry):
            base = pl.multiple_of(r8 * 8, 8)
            for j in range(8):
                for k in range(TOP_K):
                    pltpu.make_async_copy(y_hbm.at[dref[0, 0, k * tm + base + j]],
                                          ybuf.at[slot, k, pl.ds(base + j, 1), :], sems.at[slot]
                                          ).start(priority=(j * TOP_K + k) % 2)
            return carry

        lax.fori_loop(0, tm // 8, body, 0)

    @pl.when(i == 0)
    def _():
        issue(dcur_ref, 0)

    def step(slot):
        @pl.when(i + 1 < nb)
        def _():
            issue(dnxt_ref, 1 - slot)

        for k in range(TOP_K):
            pltpu.make_async_copy(y_hbm.at[pl.ds(0, tm), 0], ybuf.at[slot, k], sems.at[slot]).wait()
        gates = gate_ref[...]
        f = gates[:, 4:5] * _unpack_rows(ybuf[slot, 0]) + gates[:, 5:6] * _unpack_rows(ybuf[slot, 1])
        o_ref[...] = _ln(alpha * x_ref[...] + mod_ref[0, 5:6, :] * f) * g_ref[...] + b_ref[...]

    for slot in range(2):
        pl.when(i % 2 == slot)(functools.partial(step, slot))


def _combine(y3, dest_rows, xs, rout, mod, g, b, n_lat, lat_len, n_batch, alpha, tm):
    n, d = xs.shape
    nb = n // tm
    nlb, bpb = n_lat // tm, lat_len // tm

    def grp(i):
        return jnp.where(i < nlb, i // bpb, n_batch)

    dest3 = dest_rows.reshape(nb, 1, TOP_K * tm)
    const = lambda i: (0, 0)
    return pl.pallas_call(
        functools.partial(_combine_kernel, alpha=alpha),
        grid=(nb,),
        in_specs=[pl.BlockSpec((1, 1, TOP_K * tm), lambda i: (i, 0, 0), memory_space=pltpu.SMEM),
                  pl.BlockSpec((1, 1, TOP_K * tm), lambda i: (jnp.minimum(i + 1, nb - 1), 0, 0),
                               memory_space=pltpu.SMEM),
                  pl.BlockSpec(memory_space=pl.ANY),
                  pl.BlockSpec((tm, d), lambda i: (i, 0)),
                  pl.BlockSpec((tm, LANES), lambda i: (i, 0)),
                  pl.BlockSpec((1, 6, d), lambda i: (grp(i), 0, 0)),
                  pl.BlockSpec((1, d), const), pl.BlockSpec((1, d), const)],
        out_specs=pl.BlockSpec((tm, d), lambda i: (i, 0)),
        out_shape=jax.ShapeDtypeStruct((n, d), f32),
        scratch_shapes=[pltpu.VMEM((2, TOP_K, tm, d // 2), jnp.uint32), pltpu.SemaphoreType.DMA((2,))],
        compiler_params=_cparams("arbitrary"),
        name="moe_combine",
    )(dest3, dest3, y3, xs, rout, mod, g, b)


def _moe(h, rout, routt, cnt, xs, wg, wu, wd, layer, mod, g, b, n_lat, lat_len, n_batch, alpha):
    n, d = xs.shape
    tm = routt.shape[1]
    rb = tm
    fields = routt.reshape(n // tm, 8, tm)
    eid = fields[:, 0:TOP_K, :].astype(i32)
    rank = fields[:, TOP_K:2 * TOP_K, :].astype(i32)
    counts = cnt[0, _ROUTE_LANE0:_ROUTE_LANE0 + N_EXPERTS].astype(i32)
    padded = (counts + rb - 1) // rb * rb
    pend = jnp.cumsum(padded)
    pstart = pend - padded
    dest_rows = rank
    for e in range(N_EXPERTS):
        dest_rows = dest_rows + jnp.where(eid == e, pstart[e], 0)
    dest_rows = dest_rows.reshape(-1)
    n_blocks = TOP_K * n // rb + N_EXPERTS
    n_used = (pend[-1:] // rb).astype(i32)
    ex = jnp.arange(N_EXPERTS, dtype=i32)
    row0 = (jnp.arange(n_blocks, dtype=i32) * rb)[:, None]
    owns = (pstart[None, :] <= row0) & (row0 < pend[None, :])
    used = jnp.any(owns, axis=1)
    last_active = jnp.max(jnp.where(counts > 0, ex, 0))
    block_expert = jnp.where(used, jnp.sum(jnp.where(owns, ex[None, :], 0), axis=1), last_active).astype(i32)
    first = jnp.any(owns & (pstart[None, :] == row0), axis=1).astype(i32)
    valid = jnp.sum(jnp.where(owns, jnp.clip((pstart + counts)[None, :] - row0, 0, rb), 0), axis=1)
    half = (used & (valid <= rb // 2)).astype(i32)
    later_active = (ex[None, :] > ex[:, None]) & (counts > 0)[None, :]
    nxt = jnp.min(jnp.where(later_active, ex[None, :], N_EXPERTS), axis=1)
    nxt = jnp.where(nxt == N_EXPERTS, -1, nxt).astype(i32)
    xs3 = _dispatch(h, dest_rows, counts, pstart, padded, n_blocks * rb, tm)
    y3 = _experts(xs3, n_used, block_expert, first, nxt, half, wg, wu, wd, layer, rb)
    return _combine(y3, dest_rows, xs, rout, mod, g, b, n_lat, lat_len, n_batch, alpha, tm)


def _permute_w_in(w):
    depth, d = w.shape[0], w.shape[1]
    o_sq, o_sk, o_sv, o_rq, o_rk, o_rv, o_gf, o_gb, o_mq, o_ckv, o_kr = (
        0, 768, 1024, 1280, 1664, 2048, 2816, 3584, 4352, 5120, 5376)
    mq = w[..., o_mq:o_mq + 768].reshape(depth, d, MLA_HEADS, MLA_NOPE_DIM + MLA_ROPE_DIM)
    parts = [w[..., o_sq:o_sq + 768], w[..., o_rv:o_rv + 768], w[..., o_gf:o_gf + 768], w[..., o_gb:o_gb + 768],
             mq[..., :MLA_NOPE_DIM].reshape(depth, d, 512), w[..., o_sk:o_sk + 256], w[..., o_sv:o_sv + 256],
             mq[..., MLA_NOPE_DIM:].reshape(depth, d, 256), w[..., o_ckv:o_ckv + 256], w[..., o_rq:o_rq + 384],
             w[..., o_rk:o_rk + 384], w[..., o_kr:o_kr + 64], jnp.zeros((depth, d, _NP - _KR - 64), w.dtype)]
    return jnp.concatenate(parts, axis=-1).astype(bf16)


def kernel(x, c, ctx, c_ctx, w_ada, b_ada, w_in, swa_sink, ret_decay, mla_kv_norm, mla_w_uk, mla_w_uv, w_out, ln1_g, ln1_b, ln2_g, ln2_b, moe_w_group, moe_b_group, moe_w_expert, moe_b_expert, moe_w_gate, moe_w_up, moe_w_down):
    n_batch, lat_len, d = x.shape
    ctx_len = ctx.shape[1]
    depth = w_ada.shape[0]
    n_lat, n_ctx = n_batch * lat_len, n_batch * ctx_len
    alpha = (2 * depth) ** 0.25

    cc = jnp.zeros((8, d), f32).at[:n_batch].set(c).at[n_batch].set(c_ctx)
    mod_all = _ada(cc, w_ada, b_ada).reshape(depth, 8, 6, d)
    tab = _rope_tables(lat_len, ctx_len, _inproj_rows(lat_len, n_ctx))
    x_lat, x_ctx = x.reshape(n_lat, d), ctx.reshape(n_ctx, d)
    w_in_perm = _permute_w_in(w_in)

    for l in range(depth):
        ctx_out = l < depth - 1
        mod = mod_all[l]
        p = _inproj(x_lat, x_ctx, mod, tab, w_in_perm, l, n_lat, lat_len, ctx_len, n_batch)
        kx, v = _mla_expand(p, mla_kv_norm[l][None, :], mla_w_uk[l].astype(bf16), mla_w_uv[l].astype(bf16))
        m = _mla_attention(p, kx, v, n_batch, lat_len, ctx_len, ctx_out)
        a = _swa_attention(p, swa_sink[l], n_batch, lat_len, ctx_len, ctx_out)
        lg = jnp.log1p(-jnp.exp2(-ret_decay[l].astype(f32)))
        yf, yb = _retention(p, lg, n_batch, lat_len, ctx_len)
        n_rows = n_lat + n_ctx if ctx_out else n_lat
        rout_w = jnp.zeros((d, LANES), f32).at[:, :N_GROUPS].set(moe_w_group[l]).at[
            :, _ROUTE_LANE0:_ROUTE_LANE0 + N_EXPERTS].set(moe_w_expert[l]).astype(bf16)
        rout_b = jnp.zeros((1, LANES), f32).at[0, :N_GROUPS].set(moe_b_group[l]).at[
            0, _ROUTE_LANE0:_ROUTE_LANE0 + N_EXPERTS].set(moe_b_expert[l])
        xs, h, rout, routt, cnt = _outproj(a, yf, yb, p, m, x_lat, x_ctx, mod, w_out[l].astype(bf16), ln1_g[l][None, :],
                                           ln1_b[l][None, :], rout_w, rout_b, n_rows, n_lat, lat_len, n_batch, alpha)
        xs = _moe(h, rout, routt, cnt, xs, moe_w_gate, moe_w_up, moe_w_down, l, mod, ln2_g[l][None, :],
                  ln2_b[l][None, :], n_lat, lat_len, n_batch, alpha)
        x_lat = x_ctx = xs
    return xs[:n_lat].reshape(n_batch, lat_len, d)
```
